```python
import math
import jax, jax.numpy as jnp
from jax import lax
import numpy as np

D_MODEL = 1024
BATCH = 16
SEQ = 256
DEPTH = 2
DEC_BATCH = 8
DEC_SEQ = 1024
PAST_LEN = 512

F32 = jnp.float32
GRID_W = 64
N_HEADS = 8
N_KV_HEADS = 2
GQA_GROUP = N_HEADS // N_KV_HEADS
HEAD_DIM = 128
ATTN_W = N_HEADS * HEAD_DIM
KV_W = N_KV_HEADS * HEAD_DIM
WINDOW = 128
Q_BLOCK = 128
BAND = Q_BLOCK + 2 * WINDOW
ROPE_BASE = 10000.0
ROPE_PAIRS_PER_AXIS = HEAD_DIM // 4
D_RNN = 1024
LRU_BLOCKS = 16
LRU_BW = D_RNN // LRU_BLOCKS
LRU_C = 8.0
CONV_W = 4
CONV_LEFT = 2
N_EXPERTS = 16
N_GROUPS = 4
EXPERTS_PER_GROUP = N_EXPERTS // N_GROUPS
TOP_K = 2
D_EXPERT = 512
N_ADA = 6
EPS = 1e-6
IN_SPLITS = [ATTN_W, ATTN_W + KV_W, ATTN_W + 2 * KV_W, ATTN_W + 2 * KV_W + D_RNN,
             ATTN_W + 2 * KV_W + 2 * D_RNN, ATTN_W + 2 * KV_W + 2 * D_RNN + D_MODEL]
IN_COLS = ATTN_W + 2 * KV_W + 2 * D_RNN + 2 * D_MODEL

kernel_name = "hybrid_dit_prefix_swa_rglru_moe_step"


def _rmsnorm(x, g):
    xf = x.astype(F32)
    y = xf * lax.rsqrt(jnp.mean(xf * xf, axis=-1, keepdims=True) + EPS)
    return (y * g.astype(F32)).astype(x.dtype)


def _modulation(cond, w, b):
    m = jax.nn.silu(cond) @ w + b
    return tuple(mi[:, None, :] for mi in jnp.split(m, N_ADA, axis=-1))


def _rope_2d_tables(n_tokens):
    rows = n_tokens // GRID_W
    row = jnp.repeat(jnp.arange(rows, dtype=F32), GRID_W)
    col = jnp.tile(jnp.arange(GRID_W, dtype=F32), rows)
    inv = ROPE_BASE ** (-jnp.arange(ROPE_PAIRS_PER_AXIS, dtype=F32) / ROPE_PAIRS_PER_AXIS)
    ang = jnp.concatenate([row[:, None] * inv, col[:, None] * inv], axis=-1)
    return jnp.cos(ang), jnp.sin(ang)


def _apply_rope(x, cos, sin):
    xf = x.astype(F32)
    half = HEAD_DIM // 2
    x1, x2 = xf[..., :half], xf[..., half:]
    c = cos[None, :, None, :]
    s = sin[None, :, None, :]
    return jnp.concatenate([x1 * c - x2 * s, x2 * c + x1 * s], axis=-1).astype(x.dtype)


def _attend(q, k, v, mask, sink):
    s = jnp.einsum('bqgrd,bkgd->bgrqk', q, k).astype(F32) * (HEAD_DIM ** -0.5)
    if mask is not None:
        s = jnp.where(mask, s, -jnp.inf)
    sink_col = jnp.broadcast_to(sink.astype(F32)[None, :, :, None, None], s.shape[:-1] + (1,))
    p = jax.nn.softmax(jnp.concatenate([s, sink_col], axis=-1), axis=-1)[..., :-1]
    return jnp.einsum('bgrqk,bkgd->bqgrd', p.astype(v.dtype), v)


def _context_attention(q, k, v, sink):
    B, S = q.shape[:2]
    nb = S // Q_BLOCK
    qb = jnp.moveaxis(q.reshape(B, nb, Q_BLOCK, N_KV_HEADS, GQA_GROUP, HEAD_DIM), 1, 0)
    o = lax.map(lambda q_i: _attend(q_i, k, v, None, sink), qb)
    return jnp.moveaxis(o, 0, 1).reshape(B, S, ATTN_W)


def _latent_attention(q, k, v, k_ctx, v_ctx, sink):
    B, S = q.shape[:2]
    nb = S // Q_BLOCK
    pad = ((0, 0), (WINDOW, WINDOW), (0, 0), (0, 0))
    kp, vp = jnp.pad(k, pad), jnp.pad(v, pad)
    starts = jnp.arange(nb) * Q_BLOCK
    idx = starts[:, None] + jnp.arange(BAND)[None, :]
    k_loc = jnp.moveaxis(kp[:, idx], 1, 0)
    v_loc = jnp.moveaxis(vp[:, idx], 1, 0)
    qpos = starts[:, None] + jnp.arange(Q_BLOCK)[None, :]
    kpos = idx - WINDOW
    mask = ((kpos[:, None, :] >= 0) & (kpos[:, None, :] < S)
            & (jnp.abs(qpos[:, :, None] - kpos[:, None, :]) <= WINDOW))
    ctx_mask = jnp.ones((Q_BLOCK, k_ctx.shape[1]), dtype=bool)
    qb = jnp.moveaxis(q.reshape(B, nb, Q_BLOCK, N_KV_HEADS, GQA_GROUP, HEAD_DIM), 1, 0)

    def block(args):
        q_i, k_i, v_i, m_i = args
        kk = jnp.concatenate([k_i, k_ctx.astype(k_i.dtype)], axis=1)
        vv = jnp.concatenate([v_i, v_ctx.astype(v_i.dtype)], axis=1)
        mm = jnp.concatenate([m_i, ctx_mask], axis=1)
        return _attend(q_i, kk, vv, mm, sink)

    o = lax.map(block, (qb, k_loc, v_loc, mask))
    return jnp.moveaxis(o, 0, 1).reshape(B, S, ATTN_W)


def _conv_centred(x, w, b):
    y = lax.conv_general_dilated(x, w.astype(x.dtype)[:, None, :], window_strides=(1,),
                                 padding=[(CONV_LEFT, CONV_W - 1 - CONV_LEFT)],
                                 dimension_numbers=('NWC', 'WIO', 'NWC'),
                                 feature_group_count=x.shape[-1])
    return y + b


def _block_diag(x, w, b):
    xb = x.reshape(x.shape[:-1] + (LRU_BLOCKS, LRU_BW))
    return jnp.einsum('bsnk,nkj->bsnj', xb, w).reshape(x.shape) + b


def _lru_scan(a, u, h0, reverse):
    def step(h, au):
        h = au[0] * h + au[1]
        return h, h
    h_last, hs = lax.scan(step, h0, (jnp.swapaxes(a, 0, 1), jnp.swapaxes(u, 0, 1)), reverse=reverse)
    return h_last, jnp.swapaxes(hs, 0, 1)


def _rglru_bidir(xr, lp, h0):
    xc = _conv_centred(xr, lp['conv_w'], lp['conv_b']).astype(F32)
    ys, hs = [], []
    for d in range(2):
        r = jax.nn.sigmoid(_block_diag(xc, lp['lru_wa'][d].astype(F32), lp['lru_ba'][d].astype(F32)))
        i = jax.nn.sigmoid(_block_diag(xc, lp['lru_wx'][d].astype(F32), lp['lru_bx'][d].astype(F32)))
        log_a = -LRU_C * r * jax.nn.softplus(-lp['lru_lambda'][d].astype(F32))
        u = jnp.sqrt(-jnp.expm1(2.0 * log_a)) * (i * xc)
        h_last, y = _lru_scan(jnp.exp(log_a), u, h0[:, d], reverse=(d == 1))
        ys.append(y)
        hs.append(h_last)
    return (ys[0] + ys[1]).astype(xr.dtype), jnp.stack(hs, axis=1)


def _moe(h, router_w, router_b, w_gate_up, w_down):
    B, S, D = h.shape
    t = h.reshape(B * S, D)
    scores = jax.nn.sigmoid(t.astype(F32) @ router_w.astype(F32))
    sel = scores + router_b.astype(F32)
    grp_score = lax.top_k(sel.reshape(-1, N_GROUPS, EXPERTS_PER_GROUP), TOP_K)[0].sum(-1)
    best_grp = jnp.argmax(grp_score, axis=-1)
    in_grp = jnp.repeat(best_grp[:, None] == jnp.arange(N_GROUPS)[None, :], EXPERTS_PER_GROUP, axis=-1)
    _, e_idx = lax.top_k(jnp.where(in_grp, sel, -jnp.inf), TOP_K)
    w_sel = jnp.take_along_axis(scores, e_idx, axis=-1)
    w_sel = w_sel / jnp.sum(w_sel, axis=-1, keepdims=True)
    gates = jnp.sum(jax.nn.one_hot(e_idx, N_EXPERTS, dtype=F32) * w_sel[..., None], axis=1)
    gu = jnp.einsum('td,edf->tef', t, w_gate_up)
    g_, u_ = jnp.split(gu, 2, axis=-1)
    act = jax.nn.silu(g_) * u_ * gates[..., None].astype(t.dtype)
    return jnp.einsum('tef,efd->td', act, w_down).reshape(B, S, D)


def _layer(x, mod, lp, router_w, router_b, rope, ctx_kv, h0):
    sh1, sc1, g1, sh2, sc2, g2 = mod
    B, S, _ = x.shape
    h = _rmsnorm(x, lp['norm1']) * (1 + sc1) + sh1
    q, k, v, xr, gr, ga, gb = jnp.split(h @ lp['w_in'], IN_SPLITS, axis=-1)
    q = q.reshape(B, S, N_HEADS, HEAD_DIM)
    k = k.reshape(B, S, N_KV_HEADS, HEAD_DIM)
    v = v.reshape(B, S, N_KV_HEADS, HEAD_DIM)
    sink = lp['sink'].reshape(N_KV_HEADS, GQA_GROUP)
    if ctx_kv is None:
        att = _context_attention(q, k, v, sink)
    else:
        cos, sin = rope
        att = _latent_attention(_apply_rope(q, cos, sin), _apply_rope(k, cos, sin), v,
                                ctx_kv[0], ctx_kv[1], sink)
    rnn, h_last = _rglru_bidir(xr, lp, h0)
    merged = (jax.nn.sigmoid(ga) * (att @ lp['w_attn_proj'])
              + jax.nn.sigmoid(gb) * ((jax.nn.gelu(gr) * rnn) @ lp['w_rnn_proj']))
    x = x + g1 * (merged @ lp['w_out'])
    h2 = _rmsnorm(x, lp['norm2']) * (1 + sc2) + sh2
    x = x + g2 * _moe(h2, router_w, router_b, lp['w_gate_up'], lp['w_down'])
    return x, k, v, h_last


def setup_inputs(seed: int = 0) -> dict:
    key = jax.random.key(seed)
    ks = jax.random.split(key, 32)

    def nrm(k, shape, scale):
        return scale * jax.random.normal(k, shape, F32)

    a0 = jax.random.uniform(ks[18], (DEPTH, 2, D_RNN), F32, 0.9, 0.999)
    s0 = a0 ** (1.0 / LRU_C)
    return {
        'x_prompt': nrm(ks[0], (BATCH, SEQ, D_MODEL), 1.0),
        'x_sample': nrm(ks[1], (DEC_BATCH, DEC_SEQ, D_MODEL), 1.0),
        'cache_k': nrm(ks[2], (DEC_BATCH, DEPTH, PAST_LEN, N_KV_HEADS, HEAD_DIM), 1.0),
        'cache_v': nrm(ks[3], (DEC_BATCH, DEPTH, PAST_LEN, N_KV_HEADS, HEAD_DIM), 1.0),
        'state_h': nrm(ks[4], (DEC_BATCH, DEPTH, 2, D_RNN), 0.5),
        'c': nrm(ks[5], (DEC_BATCH, D_MODEL), 1.0),
        'c_ctx': nrm(ks[6], (D_MODEL,), 1.0),
        'norm1_g': 1.0 + nrm(ks[7], (DEPTH, D_MODEL), 0.02),
        'norm2_g': 1.0 + nrm(ks[8], (DEPTH, D_MODEL), 0.02),
        'w_ada': nrm(ks[9], (DEPTH, D_MODEL, N_ADA * D_MODEL), 0.5 * D_MODEL ** -0.5),
        'b_ada': nrm(ks[10], (DEPTH, N_ADA * D_MODEL), 0.02),
        'w_in': nrm(ks[11], (DEPTH, D_MODEL, IN_COLS), D_MODEL ** -0.5),
        'conv_w': nrm(ks[12], (DEPTH, CONV_W, D_RNN), CONV_W ** -0.5),
        'conv_b': nrm(ks[13], (DEPTH, D_RNN), 0.02),
        'lru_wa': nrm(ks[14], (DEPTH, 2, LRU_BLOCKS, LRU_BW, LRU_BW), LRU_BW ** -0.5),
        'lru_ba': nrm(ks[15], (DEPTH, 2, D_RNN), 0.02),
        'lru_wx': nrm(ks[16], (DEPTH, 2, LRU_BLOCKS, LRU_BW, LRU_BW), LRU_BW ** -0.5),
        'lru_bx': nrm(ks[17], (DEPTH, 2, D_RNN), 0.02),
        'lru_lambda': jnp.log(s0) - jnp.log1p(-s0),
        'attn_sink': nrm(ks[19], (DEPTH, N_HEADS), 0.5),
        'w_attn_proj': nrm(ks[20], (DEPTH, ATTN_W, D_MODEL), ATTN_W ** -0.5),
        'w_rnn_proj': nrm(ks[21], (DEPTH, D_RNN, D_MODEL), D_RNN ** -0.5),
        'w_out': nrm(ks[22], (DEPTH, D_MODEL, D_MODEL), D_MODEL ** -0.5),
        'router_w': nrm(ks[23], (D_MODEL, N_EXPERTS), D_MODEL ** -0.5),
        'router_b': nrm(ks[24], (N_EXPERTS,), 0.01),
        'w_gate_up': nrm(ks[25], (DEPTH, N_EXPERTS, D_MODEL, 2 * D_EXPERT), D_MODEL ** -0.5),
        'w_down': nrm(ks[26], (DEPTH, N_EXPERTS, D_EXPERT, D_MODEL), D_EXPERT ** -0.5),
        'final_norm_g': 1.0 + nrm(ks[27], (D_MODEL,), 0.02),
    }


def reference(x_prompt, x_sample, cache_k, cache_v, state_h, c, c_ctx, norm1_g, norm2_g,
              w_ada, b_ada, w_in, conv_w, conv_b, lru_wa, lru_ba, lru_wx, lru_bx, lru_lambda,
              attn_sink, w_attn_proj, w_rnn_proj, w_out, router_w, router_b, w_gate_up, w_down,
              final_norm_g):
    b_prompt = x_prompt.shape[0]
    rope = _rope_2d_tables(x_sample.shape[1])
    h0_ctx = jnp.zeros((b_prompt, 2, D_RNN), F32)
    xp, xs = x_prompt, x_sample
    ks, vs, hs = [], [], []
    for l in range(DEPTH):
        lp = {
            'norm1': norm1_g[l], 'norm2': norm2_g[l], 'w_in': w_in[l],
            'conv_w': conv_w[l], 'conv_b': conv_b[l],
            'lru_wa': lru_wa[l], 'lru_ba': lru_ba[l], 'lru_wx': lru_wx[l], 'lru_bx': lru_bx[l],
            'lru_lambda': lru_lambda[l], 'sink': attn_sink[l],
            'w_attn_proj': w_attn_proj[l], 'w_rnn_proj': w_rnn_proj[l], 'w_out': w_out[l],
            'w_gate_up': w_gate_up[l], 'w_down': w_down[l],
        }
        mod_ctx = _modulation(c_ctx[None, :], w_ada[l], b_ada[l])
        xp, k_l, v_l, h_l = _layer(xp, mod_ctx, lp, router_w, router_b, None, None, h0_ctx)
        ks.append(k_l)
        vs.append(v_l)
        hs.append(h_l)
        mod_lat = _modulation(c, w_ada[l], b_ada[l])
        xs, _, _, _ = _layer(xs, mod_lat, lp, router_w, router_b, rope,
                             (cache_k[:, l], cache_v[:, l]), state_h[:, l].astype(F32))
    y_prompt = _rmsnorm(xp, final_norm_g)
    y_sample = _rmsnorm(xs, final_norm_g)
    new_cache_k = jnp.stack(ks, axis=1)
    new_cache_v = jnp.stack(vs, axis=1)
    new_state_h = jnp.stack(hs, axis=1).astype(x_prompt.dtype)
    return (y_prompt, y_sample, new_cache_k, new_cache_v, new_state_h)
```

```python
import functools

import jax
import jax.numpy as jnp
from jax import lax
from jax.experimental import pallas as pl
from jax.experimental.pallas import tpu as pltpu

F32 = jnp.float32
BF16 = jnp.bfloat16

D_MODEL = 1024
BATCH = 16
SEQ = 256
DEPTH = 2
DEC_BATCH = 8
DEC_SEQ = 1024
PAST_LEN = 512
GRID_W = 64
N_HEADS = 8
N_KV_HEADS = 2
GQA_GROUP = N_HEADS // N_KV_HEADS
HEAD_DIM = 128
ATTN_W = N_HEADS * HEAD_DIM
KV_W = N_KV_HEADS * HEAD_DIM
WINDOW = 128
Q_BLOCK = 128
ROPE_BASE = 10000.0
ROPE_PAIRS_PER_AXIS = HEAD_DIM // 4
D_RNN = 1024
LRU_BLOCKS = 16
LRU_BW = D_RNN // LRU_BLOCKS
LRU_C = 8.0
CONV_W = 4
CONV_LEFT = 2
N_EXPERTS = 16
N_GROUPS = 4
EXPERTS_PER_GROUP = N_EXPERTS // N_GROUPS
D_EXPERT = 512
N_ADA = 6
EPS = 1e-6
IN_COLS = ATTN_W + 2 * KV_W + 2 * D_RNN + 2 * D_MODEL

T_LAT = DEC_BATCH * DEC_SEQ
T_CTX = BATCH * SEQ
T_ALL = T_LAT + T_CTX

SUBLANES = 8
LANES = 128
VMEM_LIMIT = 56 * 1024 * 1024

TOK_TILE = 256
N_TOK_TILES = T_ALL // TOK_TILE
N_LAT_TILES = T_LAT // TOK_TILE
LAT_TILES_PER_BATCH = DEC_SEQ // TOK_TILE
COND_ROWS = 16
ROPE_ID_BLOCK = DEC_SEQ // TOK_TILE

RNN_CB = 256
RNN_CHUNK = 64
MOE_TILE = 512


def _params(*sem):
    return pltpu.CompilerParams(dimension_semantics=sem, vmem_limit_bytes=VMEM_LIMIT)


def _sigmoid(x):
    return 0.5 * jnp.tanh(0.5 * x) + 0.5


def _mod_row(i):
    return jnp.where(i < N_LAT_TILES, 1 + i // LAT_TILES_PER_BATCH, 0)


def _ada_kernel(cond_ref, w_ref, b_ref, o_ref):
    s = jax.nn.silu(cond_ref[...]).astype(BF16)
    o_ref[...] = jnp.dot(s, w_ref[...].astype(BF16), preferred_element_type=F32) + b_ref[...]


def _ada(cond, w_ada, b_ada):
    cols = N_ADA * D_MODEL
    tn = 1536
    return pl.pallas_call(
        _ada_kernel,
        grid=(DEPTH, cols // tn),
        in_specs=[
            pl.BlockSpec((COND_ROWS, D_MODEL), lambda l, j: (0, 0)),
            pl.BlockSpec((None, D_MODEL, tn), lambda l, j: (l, 0, j)),
            pl.BlockSpec((None, 1, tn), lambda l, j: (l, 0, j)),
        ],
        out_specs=pl.BlockSpec((None, COND_ROWS, tn), lambda l, j: (l, 0, j)),
        out_shape=jax.ShapeDtypeStruct((DEPTH, COND_ROWS, cols), F32),
        compiler_params=_params("arbitrary", "arbitrary"),
        name="ada",
    )(cond, w_ada, b_ada.reshape(DEPTH, 1, cols))


def _inproj_kernel(x_ref, g_ref, sh_ref, sc_ref, cos_ref, sin_ref, w_ref,
                   q_ref, k_ref, v_ref, xr_ref, gr_ref, ga_ref, gb_ref):
    x = x_ref[...]
    y = x * lax.rsqrt(jnp.mean(x * x, axis=-1, keepdims=True) + EPS) * g_ref[...]
    h = (y * (1.0 + sc_ref[...]) + sh_ref[...]).astype(BF16)
    cos = cos_ref[...]
    sin = sin_ref[...]

    def proj(lo, width):
        return jnp.dot(h, w_ref[:, lo:lo + width], preferred_element_type=F32)

    def rope(t):
        return t * cos + pltpu.roll(t, HEAD_DIM // 2, 1) * sin

    scale = HEAD_DIM ** -0.5
    for hd in range(N_HEADS):
        q = proj(hd * HEAD_DIM, HEAD_DIM)
        q_ref[:, hd * HEAD_DIM:(hd + 1) * HEAD_DIM] = (rope(q) * scale).astype(BF16)
    for g in range(N_KV_HEADS):
        k = proj(ATTN_W + g * HEAD_DIM, HEAD_DIM)
        k_ref[:, g * HEAD_DIM:(g + 1) * HEAD_DIM] = rope(k)
    v_ref[...] = proj(ATTN_W + KV_W, KV_W)
    base = ATTN_W + 2 * KV_W
    xr_ref[...] = proj(base, D_RNN).astype(BF16)
    gr_ref[...] = proj(base + D_RNN, D_RNN).astype(BF16)
    ga_ref[...] = proj(base + 2 * D_RNN, D_MODEL).astype(BF16)
    gb_ref[...] = proj(base + 2 * D_RNN + D_MODEL, D_MODEL).astype(BF16)


def _inproj(x, norm_g, mods, rope_cos, rope_sin, w_in, layer):
    row = lambda i: layer * COND_ROWS + _mod_row(i)
    tok = lambda i: (i, 0)
    rope_blk = lambda i: (jnp.where(i < N_LAT_TILES, i % LAT_TILES_PER_BATCH, ROPE_ID_BLOCK), 0)
    wide = pl.BlockSpec((TOK_TILE, D_MODEL), tok)
    kv = pl.BlockSpec((TOK_TILE, KV_W), tok)
    return pl.pallas_call(
        _inproj_kernel,
        grid=(N_TOK_TILES,),
        in_specs=[
            wide,
            pl.BlockSpec((None, 1, D_MODEL), lambda i: (layer, 0, 0)),
            pl.BlockSpec((None, 1, D_MODEL), lambda i: (row(i), 0, 0)),
            pl.BlockSpec((None, 1, D_MODEL), lambda i: (row(i), 0, 1)),
            pl.BlockSpec((TOK_TILE, HEAD_DIM), rope_blk),
            pl.BlockSpec((TOK_TILE, HEAD_DIM), rope_blk),
            pl.BlockSpec((None, D_MODEL, IN_COLS), lambda i: (layer, 0, 0)),
        ],
        out_specs=[wide, kv, kv, wide, wide, wide, wide],
        out_shape=[
            jax.ShapeDtypeStruct((T_ALL, ATTN_W), BF16),
            jax.ShapeDtypeStruct((T_ALL, KV_W), F32),
            jax.ShapeDtypeStruct((T_ALL, KV_W), F32),
            jax.ShapeDtypeStruct((T_ALL, D_RNN), BF16),
            jax.ShapeDtypeStruct((T_ALL, D_RNN), BF16),
            jax.ShapeDtypeStruct((T_ALL, D_MODEL), BF16),
            jax.ShapeDtypeStruct((T_ALL, D_MODEL), BF16),
        ],
        compiler_params=_params("arbitrary"),
        name="inproj",
    )(x, norm_g.reshape(DEPTH, 1, D_MODEL), mods, mods, rope_cos, rope_sin, w_in)


def _stack_heads(q_ref, g):
    lo = g * GQA_GROUP
    return jnp.concatenate(
        [q_ref[:, (lo + r) * HEAD_DIM:(lo + r + 1) * HEAD_DIM] for r in range(GQA_GROUP)], axis=0)


def _sink_column(sink_ref, g, rows):
    return jnp.concatenate(
        [jnp.full((rows, 1), sink_ref[g * GQA_GROUP + r], F32) for r in range(GQA_GROUP)], axis=0)


def _qk(q, k):
    return lax.dot_general(q, k, (((1,), (1,)), ((), ())), preferred_element_type=F32)


def _ctx_attn_kernel(sink_ref, q_ref, k_ref, v_ref, o_ref):
    for g in range(N_KV_HEADS):
        q = _stack_heads(q_ref, g)
        k = k_ref[:, g * HEAD_DIM:(g + 1) * HEAD_DIM].astype(BF16)
        v = v_ref[:, g * HEAD_DIM:(g + 1) * HEAD_DIM].astype(BF16)
        sink = _sink_column(sink_ref, g, SEQ)
        s = _qk(q, k)
        m = jnp.maximum(jnp.max(s, axis=-1, keepdims=True), sink)
        p = jnp.exp(s - m)
        denom = jnp.sum(p, axis=-1, keepdims=True) + jnp.exp(sink - m)
        o = jnp.dot(p.astype(BF16), v, preferred_element_type=F32) / denom
        for r in range(GQA_GROUP):
            hd = g * GQA_GROUP + r
            o_ref[:, hd * HEAD_DIM:(hd + 1) * HEAD_DIM] = o[r * SEQ:(r + 1) * SEQ].astype(BF16)


def _ctx_attn(q, k, v, sink):
    first = T_LAT // SEQ
    blk = lambda b: (first + b, 0)
    return pl.pallas_call(
        _ctx_attn_kernel,
        grid=(BATCH,),
        in_specs=[
            pl.BlockSpec(memory_space=pltpu.SMEM),
            pl.BlockSpec((SEQ, ATTN_W), blk),
            pl.BlockSpec((SEQ, KV_W), blk),
            pl.BlockSpec((SEQ, KV_W), blk),
        ],
        out_specs=pl.BlockSpec((SEQ, ATTN_W), lambda b: (b, 0)),
        out_shape=jax.ShapeDtypeStruct((T_CTX, ATTN_W), BF16),
        compiler_params=_params("arbitrary"),
        name="ctx_attn",
    )(sink, q, k, v)


def _lat_attn_kernel(sink_ref, q_ref, kp_ref, kc_ref, kn_ref, vp_ref, vc_ref, vn_ref,
                     ck_ref, cv_ref, o_ref):
    j = pl.program_id(1)
    rows = GQA_GROUP * Q_BLOCK
    band = Q_BLOCK + 2 * WINDOW
    qpos = j * Q_BLOCK + lax.broadcasted_iota(jnp.int32, (rows, band), 0) % Q_BLOCK
    kpos = j * Q_BLOCK - WINDOW + lax.broadcasted_iota(jnp.int32, (rows, band), 1)
    diff = qpos - kpos
    valid = (kpos >= 0) & (kpos < DEC_SEQ) & (diff <= WINDOW) & (diff >= -WINDOW)
    for g in range(N_KV_HEADS):
        cols = slice(g * HEAD_DIM, (g + 1) * HEAD_DIM)
        q = _stack_heads(q_ref, g)
        k_loc = jnp.concatenate([kp_ref[:, cols], kc_ref[:, cols], kn_ref[:, cols]], axis=0).astype(BF16)
        v_loc = jnp.concatenate([vp_ref[:, cols], vc_ref[:, cols], vn_ref[:, cols]], axis=0).astype(BF16)
        k_ctx = ck_ref[:, cols].astype(BF16)
        v_ctx = cv_ref[:, cols].astype(BF16)
        sink = _sink_column(sink_ref, g, Q_BLOCK)
        s_loc = jnp.where(valid, _qk(q, k_loc), -jnp.inf)
        s_ctx = _qk(q, k_ctx)
        m = jnp.maximum(jnp.maximum(jnp.max(s_loc, axis=-1, keepdims=True),
                                    jnp.max(s_ctx, axis=-1, keepdims=True)), sink)
        p_loc = jnp.exp(s_loc - m)
        p_ctx = jnp.exp(s_ctx - m)
        denom = (jnp.sum(p_loc, axis=-1, keepdims=True) + jnp.sum(p_ctx, axis=-1, keepdims=True)
                 + jnp.exp(sink - m))
        o = (jnp.dot(p_loc.astype(BF16), v_loc, preferred_element_type=F32)
             + jnp.dot(p_ctx.astype(BF16), v_ctx, preferred_element_type=F32)) / denom
        for r in range(GQA_GROUP):
            hd = g * GQA_GROUP + r
            o_ref[:, hd * HEAD_DIM:(hd + 1) * HEAD_DIM] = o[r * Q_BLOCK:(r + 1) * Q_BLOCK].astype(BF16)


def _lat_attn(q, k, v, cache_k, cache_v, sink):
    nb = DEC_SEQ // Q_BLOCK
    cur = lambda b, j: (b * nb + j, 0)
    prev = lambda b, j: (b * nb + jnp.maximum(j - 1, 0), 0)
    nxt = lambda b, j: (b * nb + jnp.minimum(j + 1, nb - 1), 0)
    kvb = lambda im: pl.BlockSpec((Q_BLOCK, KV_W), im)
    cache = pl.BlockSpec((None, PAST_LEN, KV_W), lambda b, j: (b, 0, 0))
    return pl.pallas_call(
        _lat_attn_kernel,
        grid=(DEC_BATCH, nb),
        in_specs=[
            pl.BlockSpec(memory_space=pltpu.SMEM),
            pl.BlockSpec((Q_BLOCK, ATTN_W), cur),
            kvb(prev), kvb(cur), kvb(nxt), kvb(prev), kvb(cur), kvb(nxt),
            cache, cache,
        ],
        out_specs=pl.BlockSpec((Q_BLOCK, ATTN_W), cur),
        out_shape=jax.ShapeDtypeStruct((T_LAT, ATTN_W), BF16),
        compiler_params=_params("arbitrary", "arbitrary"),
        name="lat_attn",
    )(sink, q, k, k, k, v, v, v, cache_k, cache_v)


def _rnn_kernel(seq, xr_ref, cw_ref, cb_ref, w_ref, b_ref, lam_ref, h0_ref,
                y_ref, hl_ref, xt_ref, yt_ref, a_ref, u_ref):
    n_chunks = seq // RNN_CHUNK
    rows = RNN_CHUNK * SUBLANES
    halo = jnp.zeros((CONV_LEFT, SUBLANES, RNN_CB), F32)
    xt_ref[0:CONV_LEFT] = halo
    xt_ref[seq + CONV_LEFT:seq + 2 * CONV_LEFT] = halo

    def load_chunk(c, carry):
        t0 = pl.multiple_of(c * RNN_CHUNK, RNN_CHUNK)
        x = xr_ref[:, pl.ds(t0, RNN_CHUNK), :].astype(F32)
        xt_ref[pl.ds(t0 + CONV_LEFT, RNN_CHUNK)] = pltpu.einshape("btc->tbc", x)
        return carry

    lax.fori_loop(0, n_chunks, load_chunk, 0)

    def conv_chunk(t0):
        acc = cb_ref[...].reshape(1, 1, RNN_CB)
        for tap in range(CONV_W):
            acc = acc + xt_ref[pl.ds(t0 + tap, RNN_CHUNK)] * cw_ref[tap:tap + 1, :].reshape(1, 1, RNN_CB)
        return acc

    for d in range(2):
        softplus = jax.nn.softplus(-lam_ref[d:d + 1, :])
        bias = b_ref[d:d + 1, :]

        def chunk(ci, h, d=d, softplus=softplus, bias=bias):
            c = ci if d == 0 else n_chunks - 1 - ci
            t0 = pl.multiple_of(c * RNN_CHUNK, RNN_CHUNK)
            xc = conv_chunk(t0).reshape(rows, RNN_CB)
            z = jnp.dot(xc.astype(BF16), w_ref[d], preferred_element_type=F32) + bias
            r = _sigmoid(z[:, :RNN_CB])
            i = _sigmoid(z[:, RNN_CB:])
            log_a = (-LRU_C) * r * softplus
            a = jnp.exp(log_a)
            mult = jnp.sqrt(-jnp.tanh(log_a) * (a * a + 1.0))
            a_ref[...] = a.reshape(RNN_CHUNK, SUBLANES, RNN_CB)
            u_ref[...] = (mult * (i * xc)).reshape(RNN_CHUNK, SUBLANES, RNN_CB)

            def step(s, h):
                t = s if d == 0 else RNN_CHUNK - 1 - s
                h = a_ref[t] * h + u_ref[t]
                if d == 0:
                    yt_ref[t0 + t] = h
                else:
                    yt_ref[t0 + t] = yt_ref[t0 + t] + h
                return h

            return lax.fori_loop(0, RNN_CHUNK, step, h, unroll=8)

        hl_ref[d] = lax.fori_loop(0, n_chunks, chunk, h0_ref[d])

    def store_chunk(c, carry):
        t0 = pl.multiple_of(c * RNN_CHUNK, RNN_CHUNK)
        y = pltpu.einshape("tbc->btc", yt_ref[pl.ds(t0, RNN_CHUNK)])
        y_ref[:, pl.ds(t0, RNN_CHUNK), :] = y.astype(BF16)
        return carry

    lax.fori_loop(0, n_chunks, store_chunk, 0)


def _rnn(xr, conv_w, conv_b, w_blk, b_blk, lam, h0, layer, seq, n_groups, first_group):
    n_cb = D_RNN // RNN_CB
    return pl.pallas_call(
        functools.partial(_rnn_kernel, seq),
        grid=(n_groups, n_cb),
        in_specs=[
            pl.BlockSpec((SUBLANES, seq, RNN_CB), lambda g, j: (first_group + g, 0, j)),
            pl.BlockSpec((None, CONV_W, RNN_CB), lambda g, j: (layer, 0, j)),
            pl.BlockSpec((None, 1, RNN_CB), lambda g, j: (layer, 0, j)),
            pl.BlockSpec((None, None, 2, RNN_CB, 2 * RNN_CB), lambda g, j: (layer, j, 0, 0, 0)),
            pl.BlockSpec((None, None, 2, 2 * RNN_CB), lambda g, j: (layer, j, 0, 0)),
            pl.BlockSpec((None, 2, RNN_CB), lambda g, j: (layer, 0, j)),
            pl.BlockSpec((None, 2, SUBLANES, RNN_CB), lambda g, j: (g, 0, 0, j)),
        ],
        out_specs=[
            pl.BlockSpec((SUBLANES, seq, RNN_CB), lambda g, j: (g, 0, j)),
            pl.BlockSpec((None, 2, SUBLANES, RNN_CB), lambda g, j: (g, 0, 0, j)),
        ],
        out_shape=[
            jax.ShapeDtypeStruct((n_groups * SUBLANES, seq, D_RNN), BF16),
            jax.ShapeDtypeStruct((n_groups, 2, SUBLANES, D_RNN), F32),
        ],
        scratch_shapes=[
            pltpu.VMEM((seq + 2 * CONV_LEFT, SUBLANES, RNN_CB), F32),
            pltpu.VMEM((seq, SUBLANES, RNN_CB), F32),
            pltpu.VMEM((RNN_CHUNK, SUBLANES, RNN_CB), F32),
            pltpu.VMEM((RNN_CHUNK, SUBLANES, RNN_CB), F32),
        ],
        compiler_params=_params("arbitrary", "arbitrary"),
        name=f"rnn_s{seq}",
    )(xr, conv_w, conv_b.reshape(DEPTH, 1, D_RNN), w_blk, b_blk, lam, h0)


def _lru_block_weights(lru_wa, lru_wx, lru_ba, lru_bx):
    n_cb = D_RNN // RNN_CB
    per = RNN_CB // LRU_BW

    def dense(w):
        w = w.reshape(DEPTH, 2, n_cb, per, LRU_BW, LRU_BW)
        eye = jnp.eye(per, dtype=w.dtype)
        full = jnp.einsum("ldcpkj,pq->ldcpkqj", w, eye)
        return full.reshape(DEPTH, 2, n_cb, RNN_CB, RNN_CB)

    w = jnp.concatenate([dense(lru_wa), dense(lru_wx)], axis=-1)
    w = jnp.transpose(w, (0, 2, 1, 3, 4)).astype(BF16)
    b = jnp.concatenate([lru_ba.reshape(DEPTH, 2, n_cb, RNN_CB),
                         lru_bx.reshape(DEPTH, 2, n_cb, RNN_CB)], axis=-1)
    return w, jnp.transpose(b, (0, 2, 1, 3))


def _route(scores, sel):
    grp_score = []
    for g in range(N_GROUPS):
        a, b, c, d = sel[g * EXPERTS_PER_GROUP:(g + 1) * EXPERTS_PER_GROUP]
        hi1, lo1 = jnp.maximum(a, b), jnp.minimum(a, b)
        hi2, lo2 = jnp.maximum(c, d), jnp.minimum(c, d)
        grp_score.append(jnp.maximum(hi1, hi2) + jnp.maximum(jnp.minimum(hi1, hi2), jnp.maximum(lo1, lo2)))
    best = jnp.zeros_like(grp_score[0], dtype=jnp.int32)
    best_val = grp_score[0]
    for g in range(1, N_GROUPS):
        better = grp_score[g] > best_val
        best = jnp.where(better, g, best)
        best_val = jnp.where(better, grp_score[g], best_val)
    chosen = []
    for e in range(N_EXPERTS):
        g = e // EXPERTS_PER_GROUP
        rank = jnp.zeros_like(best)
        for o in range(g * EXPERTS_PER_GROUP, (g + 1) * EXPERTS_PER_GROUP):
            if o == e:
                continue
            ahead = (sel[o] >= sel[e]) if o < e else (sel[o] > sel[e])
            rank = rank + jnp.where(ahead, 1, 0)
        chosen.append(jnp.where(best == g, rank, 2) < 2)
    picked = [jnp.where(chosen[e], scores[e], 0.0) for e in range(N_EXPERTS)]
    total = picked[0]
    for e in range(1, N_EXPERTS):
        total = total + picked[e]
    return [p / total for p in picked]


def _merge_kernel(x_ref, att_ref, rnn_ref, gr_ref, ga_ref, gb_ref, g1_ref, sh2_ref, sc2_ref, n2_ref,
                  wap_ref, wrp_ref, wo_ref, rw_ref, rb_ref,
                  xo_ref, h2_ref, gates_ref):
    att = jnp.dot(att_ref[...], wap_ref[...], preferred_element_type=F32)
    gated = (jax.nn.gelu(gr_ref[...].astype(F32)) * rnn_ref[...].astype(F32)).astype(BF16)
    rnn = jnp.dot(gated, wrp_ref[...], preferred_element_type=F32)
    merged = _sigmoid(ga_ref[...].astype(F32)) * att + _sigmoid(gb_ref[...].astype(F32)) * rnn
    x = x_ref[...] + g1_ref[...] * jnp.dot(merged.astype(BF16), wo_ref[...], preferred_element_type=F32)
    xo_ref[...] = x
    y = x * lax.rsqrt(jnp.mean(x * x, axis=-1, keepdims=True) + EPS) * n2_ref[...]
    h2 = y * (1.0 + sc2_ref[...]) + sh2_ref[...]
    h2_ref[...] = h2.astype(BF16)
    h_hi = h2.astype(BF16)
    h_lo = (h2 - h_hi.astype(F32)).astype(BF16)
    rw = rw_ref[...]
    w_hi = rw.astype(BF16)
    w_lo = (rw - w_hi.astype(F32)).astype(BF16)
    logits = _qk(w_hi, h_hi) + (_qk(w_hi, h_lo) + _qk(w_lo, h_hi))
    score = _sigmoid(logits)
    sel = score + rb_ref[...]
    gates = _route([score[e:e + 1, :] for e in range(N_EXPERTS)], [sel[e:e + 1, :] for e in range(N_EXPERTS)])
    pad = jnp.zeros((LANES - N_EXPERTS, TOK_TILE), F32)
    gates_ref[...] = jnp.concatenate(gates + [pad], axis=0).T


def _merge(x, att, rnn, gr, ga, gb, mods, norm_g, wap, wrp, wo, router_wt, router_b, layer):
    tok = lambda i: (i, 0)
    wide = pl.BlockSpec((TOK_TILE, D_MODEL), tok)
    mod = lambda col: pl.BlockSpec((None, 1, D_MODEL), lambda i: (layer * COND_ROWS + _mod_row(i), 0, col))
    mat = pl.BlockSpec((None, D_MODEL, D_MODEL), lambda i: (layer, 0, 0))
    return pl.pallas_call(
        _merge_kernel,
        grid=(N_TOK_TILES,),
        in_specs=[
            wide, wide, wide, wide, wide, wide,
            mod(2), mod(3), mod(4),
            pl.BlockSpec((None, 1, D_MODEL), lambda i: (layer, 0, 0)),
            mat, mat, mat,
            pl.BlockSpec((N_EXPERTS, D_MODEL), lambda i: (0, 0)),
            pl.BlockSpec((N_EXPERTS, 1), lambda i: (0, 0)),
        ],
        out_specs=[wide, wide, pl.BlockSpec((TOK_TILE, LANES), tok)],
        out_shape=[
            jax.ShapeDtypeStruct((T_ALL, D_MODEL), F32),
            jax.ShapeDtypeStruct((T_ALL, D_MODEL), BF16),
            jax.ShapeDtypeStruct((T_ALL, LANES), F32),
        ],
        compiler_params=_params("arbitrary"),
        name="merge",
    )(x, att, rnn, gr, ga, gb, mods, mods, mods, norm_g.reshape(DEPTH, 1, D_MODEL),
      wap, wrp, wo, router_wt, router_b.reshape(N_EXPERTS, 1))


def _moe_kernel(x_ref, h_ref, gates_ref, g2_ref, wgu_ref, wd_ref, o_ref, acc_ref):
    e = pl.program_id(1)

    @pl.when(e == 0)
    def _():
        acc_ref[...] = jnp.zeros_like(acc_ref)

    lane = lax.broadcasted_iota(jnp.int32, (MOE_TILE, LANES), 1)
    gate = jnp.sum(jnp.where(lane == e, gates_ref[...], 0.0), axis=-1, keepdims=True)
    gu = jnp.dot(h_ref[...], wgu_ref[...], preferred_element_type=F32)
    act = jax.nn.silu(gu[:, :D_EXPERT]) * gu[:, D_EXPERT:] * gate
    acc_ref[...] += jnp.dot(act.astype(BF16), wd_ref[...], preferred_element_type=F32)

    @pl.when(e == N_EXPERTS - 1)
    def _():
        o_ref[...] = x_ref[...] + g2_ref[...] * acc_ref[...]


def _moe(x, h2, gates, mods, wgu, wd, layer):
    tok = lambda i, e: (i, 0)
    per = MOE_TILE // TOK_TILE
    return pl.pallas_call(
        _moe_kernel,
        grid=(T_ALL // MOE_TILE, N_EXPERTS),
        in_specs=[
            pl.BlockSpec((MOE_TILE, D_MODEL), tok),
            pl.BlockSpec((MOE_TILE, D_MODEL), tok),
            pl.BlockSpec((MOE_TILE, LANES), tok),
            pl.BlockSpec((None, 1, D_MODEL), lambda i, e: (layer * COND_ROWS + _mod_row(i * per), 0, 5)),
            pl.BlockSpec((None, None, D_MODEL, 2 * D_EXPERT), lambda i, e: (layer, e, 0, 0)),
            pl.BlockSpec((None, None, D_EXPERT, D_MODEL), lambda i, e: (layer, e, 0, 0)),
        ],
        out_specs=pl.BlockSpec((MOE_TILE, D_MODEL), tok),
        out_shape=jax.ShapeDtypeStruct((T_ALL, D_MODEL), F32),
        scratch_shapes=[pltpu.VMEM((MOE_TILE, D_MODEL), F32)],
        compiler_params=_params("arbitrary", "arbitrary"),
        name="moe",
    )(x, h2, gates, mods, wgu, wd)


def _final_norm_kernel(x_ref, g_ref, o_ref):
    x = x_ref[...]
    o_ref[...] = x * lax.rsqrt(jnp.mean(x * x, axis=-1, keepdims=True) + EPS) * g_ref[...]


def _final_norm(x, g, first_tile, n_tiles):
    return pl.pallas_call(
        _final_norm_kernel,
        grid=(n_tiles,),
        in_specs=[
            pl.BlockSpec((TOK_TILE, D_MODEL), lambda i: (first_tile + i, 0)),
            pl.BlockSpec((1, D_MODEL), lambda i: (0, 0)),
        ],
        out_specs=pl.BlockSpec((TOK_TILE, D_MODEL), lambda i: (i, 0)),
        out_shape=jax.ShapeDtypeStruct((n_tiles * TOK_TILE, D_MODEL), F32),
        compiler_params=_params("arbitrary"),
        name="final_norm",
    )(x, g.reshape(1, D_MODEL))


def _rope_tables():
    t = jnp.arange(DEC_SEQ, dtype=jnp.int32)
    row = (t // GRID_W).astype(F32)
    col = (t % GRID_W).astype(F32)
    inv = ROPE_BASE ** (-jnp.arange(ROPE_PAIRS_PER_AXIS, dtype=F32) / ROPE_PAIRS_PER_AXIS)
    ang = jnp.concatenate([row[:, None] * inv, col[:, None] * inv], axis=-1)
    cos, sin = jnp.cos(ang), jnp.sin(ang)
    cos = jnp.concatenate([cos, cos], axis=-1)
    sin = jnp.concatenate([-sin, sin], axis=-1)
    cos = jnp.concatenate([cos, jnp.ones((TOK_TILE, HEAD_DIM), F32)], axis=0)
    sin = jnp.concatenate([sin, jnp.zeros((TOK_TILE, HEAD_DIM), F32)], axis=0)
    return cos, sin


def kernel(x_prompt, x_sample, cache_k, cache_v, state_h, c, c_ctx, norm1_g, norm2_g, w_ada, b_ada, w_in, conv_w, conv_b, lru_wa, lru_ba, lru_wx, lru_bx, lru_lambda, attn_sink, w_attn_proj, w_rnn_proj, w_out, router_w, router_b, w_gate_up, w_down, final_norm_g):
    x = jnp.concatenate([x_sample.reshape(T_LAT, D_MODEL), x_prompt.reshape(T_CTX, D_MODEL)], axis=0)
    cond = jnp.concatenate([c_ctx[None, :], c, jnp.zeros((COND_ROWS - 1 - DEC_BATCH, D_MODEL), F32)], axis=0)
    mods = _ada(cond, w_ada, b_ada).reshape(DEPTH * COND_ROWS, N_ADA, D_MODEL)
    mods = mods.reshape(DEPTH * COND_ROWS, 1, N_ADA * D_MODEL)
    rope_cos, rope_sin = _rope_tables()

    w_in_b = w_in.astype(BF16)
    wap_b = w_attn_proj.astype(BF16)
    wrp_b = w_rnn_proj.astype(BF16)
    wo_b = w_out.astype(BF16)
    wgu_b = w_gate_up.astype(BF16)
    wd_b = w_down.astype(BF16)
    lru_w, lru_b = _lru_block_weights(lru_wa, lru_wx, lru_ba, lru_bx)
    router_wt = router_w.T

    h0_lat = jnp.transpose(state_h.astype(F32), (1, 2, 0, 3)).reshape(DEPTH, 1, 2, DEC_BATCH, D_RNN)
    h0_ctx = jnp.zeros((BATCH // SUBLANES, 2, SUBLANES, D_RNN), F32)
    ck = jnp.transpose(cache_k, (1, 0, 2, 3, 4)).reshape(DEPTH, DEC_BATCH, PAST_LEN, KV_W)
    cv = jnp.transpose(cache_v, (1, 0, 2, 3, 4)).reshape(DEPTH, DEC_BATCH, PAST_LEN, KV_W)

    new_k, new_v, new_h = [], [], []
    for l in range(DEPTH):
        q, k, v, xr, gr, ga, gb = _inproj(x, norm1_g, mods, rope_cos, rope_sin, w_in_b, l)
        att_lat = _lat_attn(q, k, v, ck[l], cv[l], attn_sink[l])
        att_ctx = _ctx_attn(q, k, v, attn_sink[l])
        y_lat, _ = _rnn(xr.reshape(T_ALL // DEC_SEQ, DEC_SEQ, D_RNN), conv_w, conv_b, lru_w, lru_b,
                        lru_lambda, h0_lat[l], l, DEC_SEQ, DEC_BATCH // SUBLANES, 0)
        y_ctx, h_ctx = _rnn(xr.reshape(T_ALL // SEQ, SEQ, D_RNN), conv_w, conv_b, lru_w, lru_b,
                            lru_lambda, h0_ctx, l, SEQ, BATCH // SUBLANES, T_LAT // SEQ // SUBLANES)
        att = jnp.concatenate([att_lat, att_ctx], axis=0)
        rnn = jnp.concatenate([y_lat.reshape(T_LAT, D_RNN), y_ctx.reshape(T_CTX, D_RNN)], axis=0)
        x, h2, gates = _merge(x, att, rnn, gr, ga, gb, mods, norm2_g, wap_b, wrp_b, wo_b,
                              router_wt, router_b, l)
        x = _moe(x, h2, gates, mods, wgu_b, wd_b, l)
        new_k.append(k[T_LAT:].reshape(BATCH, SEQ, N_KV_HEADS, HEAD_DIM))
        new_v.append(v[T_LAT:].reshape(BATCH, SEQ, N_KV_HEADS, HEAD_DIM))
        new_h.append(jnp.transpose(h_ctx, (0, 2, 1, 3)).reshape(BATCH, 2, D_RNN))

    y_sample = _final_norm(x, final_norm_g, 0, N_LAT_TILES).reshape(DEC_BATCH, DEC_SEQ, D_MODEL)
    y_prompt = _final_norm(x, final_norm_g, N_LAT_TILES, N_TOK_TILES - N_LAT_TILES).reshape(BATCH, SEQ, D_MODEL)
    return (y_prompt, y_sample, jnp.stack(new_k, axis=1), jnp.stack(new_v, axis=1),
            jnp.stack(new_h, axis=1).astype(x_prompt.dtype))
```

```python
import functools

import jax
import jax.numpy as jnp
from jax import lax
from jax.experimental import pallas as pl
from jax.experimental.pallas import tpu as pltpu

F32 = jnp.float32
BF16 = jnp.bfloat16

D_MODEL = 1024
BATCH = 16
SEQ = 256
DEPTH = 2
DEC_BATCH = 8
DEC_SEQ = 1024
PAST_LEN = 512
GRID_W = 64
N_HEADS = 8
N_KV_HEADS = 2
GQA_GROUP = N_HEADS // N_KV_HEADS
HEAD_DIM = 128
ATTN_W = N_HEADS * HEAD_DIM
KV_W = N_KV_HEADS * HEAD_DIM
WINDOW = 128
Q_BLOCK = 128
ROPE_BASE = 10000.0
ROPE_PAIRS_PER_AXIS = HEAD_DIM // 4
D_RNN = 1024
LRU_BLOCKS = 16
LRU_BW = D_RNN // LRU_BLOCKS
LRU_C = 8.0
CONV_W = 4
CONV_LEFT = 2
N_EXPERTS = 16
N_GROUPS = 4
EXPERTS_PER_GROUP = N_EXPERTS // N_GROUPS
D_EXPERT = 512
N_ADA = 6
EPS = 1e-6
IN_COLS = ATTN_W + 2 * KV_W + 2 * D_RNN + 2 * D_MODEL

T_LAT = DEC_BATCH * DEC_SEQ
T_CTX = BATCH * SEQ
T_ALL = T_LAT + T_CTX

SUBLANES = 8
LANES = 128
VMEM_LIMIT = 56 * 1024 * 1024

TOK_TILE = 256
N_TOK_TILES = T_ALL // TOK_TILE
N_LAT_TILES = T_LAT // TOK_TILE
LAT_TILES_PER_BATCH = DEC_SEQ // TOK_TILE
COND_ROWS = 16
ROPE_ID_BLOCK = DEC_SEQ // TOK_TILE

RNN_CB = 256
RNN_CHUNK = 64
PAIRS_PER_GROUP = EXPERTS_PER_GROUP * (EXPERTS_PER_GROUP - 1) // 2
N_BUCKETS = N_GROUPS * PAIRS_PER_GROUP
ROW_W = D_MODEL + LANES
MOE_TM = 256
N_MOE_TILES = (T_ALL + N_BUCKETS * (MOE_TM - 1) + MOE_TM - 1) // MOE_TM
MOE_ROWS = N_MOE_TILES * MOE_TM


def _params(*sem):
    return pltpu.CompilerParams(dimension_semantics=sem, vmem_limit_bytes=VMEM_LIMIT)


def _sigmoid(x):
    return 0.5 * jnp.tanh(0.5 * x) + 0.5


def _mod_row(i):
    return jnp.where(i < N_LAT_TILES, 1 + i // LAT_TILES_PER_BATCH, 0)


def _ada_kernel(cond_ref, w_ref, b_ref, o_ref):
    s = jax.nn.silu(cond_ref[...]).astype(BF16)
    o_ref[...] = jnp.dot(s, w_ref[...].astype(BF16), preferred_element_type=F32) + b_ref[...]


def _ada(cond, w_ada, b_ada):
    cols = N_ADA * D_MODEL
    tn = 1536
    return pl.pallas_call(
        _ada_kernel,
        grid=(DEPTH, cols // tn),
        in_specs=[
            pl.BlockSpec((COND_ROWS, D_MODEL), lambda l, j: (0, 0)),
            pl.BlockSpec((None, D_MODEL, tn), lambda l, j: (l, 0, j)),
            pl.BlockSpec((None, 1, tn), lambda l, j: (l, 0, j)),
        ],
        out_specs=pl.BlockSpec((None, COND_ROWS, tn), lambda l, j: (l, 0, j)),
        out_shape=jax.ShapeDtypeStruct((DEPTH, COND_ROWS, cols), F32),
        compiler_params=_params("arbitrary", "arbitrary"),
        name="ada",
    )(cond, w_ada, b_ada.reshape(DEPTH, 1, cols))


def _inproj_kernel(x_ref, g_ref, sh_ref, sc_ref, cos_ref, sin_ref, w_ref,
                   q_ref, k_ref, v_ref, xr_ref, gr_ref, ga_ref, gb_ref):
    x = x_ref[...]
    y = x * lax.rsqrt(jnp.mean(x * x, axis=-1, keepdims=True) + EPS) * g_ref[...]
    h = (y * (1.0 + sc_ref[...]) + sh_ref[...]).astype(BF16)
    cos = cos_ref[...]
    sin = sin_ref[...]

    def proj(lo, width):
        return jnp.dot(h, w_ref[:, lo:lo + width], preferred_element_type=F32)

    def rope(t):
        return t * cos + pltpu.roll(t, HEAD_DIM // 2, 1) * sin

    scale = HEAD_DIM ** -0.5
    for hd in range(N_HEADS):
        q = proj(hd * HEAD_DIM, HEAD_DIM)
        q_ref[:, hd * HEAD_DIM:(hd + 1) * HEAD_DIM] = (rope(q) * scale).astype(BF16)
    for g in range(N_KV_HEADS):
        k = proj(ATTN_W + g * HEAD_DIM, HEAD_DIM)
        k_ref[:, g * HEAD_DIM:(g + 1) * HEAD_DIM] = rope(k)
    v_ref[...] = proj(ATTN_W + KV_W, KV_W)
    base = ATTN_W + 2 * KV_W
    xr_ref[...] = proj(base, D_RNN).astype(BF16)
    gr_ref[...] = proj(base + D_RNN, D_RNN).astype(BF16)
    ga_ref[...] = proj(base + 2 * D_RNN, D_MODEL).astype(BF16)
    gb_ref[...] = proj(base + 2 * D_RNN + D_MODEL, D_MODEL).astype(BF16)


def _inproj(x, norm_g, mods, rope_cos, rope_sin, w_in, layer):
    row = lambda i: layer * COND_ROWS + _mod_row(i)
    tok = lambda i: (i, 0)
    rope_blk = lambda i: (jnp.where(i < N_LAT_TILES, i % LAT_TILES_PER_BATCH, ROPE_ID_BLOCK), 0)
    wide = pl.BlockSpec((TOK_TILE, D_MODEL), tok)
    kv = pl.BlockSpec((TOK_TILE, KV_W), tok)
    return pl.pallas_call(
        _inproj_kernel,
        grid=(N_TOK_TILES,),
        in_specs=[
            wide,
            pl.BlockSpec((None, 1, D_MODEL), lambda i: (layer, 0, 0)),
            pl.BlockSpec((None, 1, D_MODEL), lambda i: (row(i), 0, 0)),
            pl.BlockSpec((None, 1, D_MODEL), lambda i: (row(i), 0, 1)),
            pl.BlockSpec((TOK_TILE, HEAD_DIM), rope_blk),
            pl.BlockSpec((TOK_TILE, HEAD_DIM), rope_blk),
            pl.BlockSpec((None, D_MODEL, IN_COLS), lambda i: (layer, 0, 0)),
        ],
        out_specs=[wide, kv, kv, wide, wide, wide, wide],
        out_shape=[
            jax.ShapeDtypeStruct((T_ALL, ATTN_W), BF16),
            jax.ShapeDtypeStruct((T_ALL, KV_W), F32),
            jax.ShapeDtypeStruct((T_ALL, KV_W), F32),
            jax.ShapeDtypeStruct((T_ALL, D_RNN), BF16),
            jax.ShapeDtypeStruct((T_ALL, D_RNN), BF16),
            jax.ShapeDtypeStruct((T_ALL, D_MODEL), BF16),
            jax.ShapeDtypeStruct((T_ALL, D_MODEL), BF16),
        ],
        compiler_params=_params("arbitrary"),
        name="inproj",
    )(x, norm_g.reshape(DEPTH, 1, D_MODEL), mods, mods, rope_cos, rope_sin, w_in)


def _stack_heads(q_ref, g):
    lo = g * GQA_GROUP
    return jnp.concatenate(
        [q_ref[:, (lo + r) * HEAD_DIM:(lo + r + 1) * HEAD_DIM] for r in range(GQA_GROUP)], axis=0)


def _sink_column(sink_ref, g, rows):
    return jnp.concatenate(
        [jnp.full((rows, 1), sink_ref[g * GQA_GROUP + r], F32) for r in range(GQA_GROUP)], axis=0)


def _qk(q, k):
    return lax.dot_general(q, k, (((1,), (1,)), ((), ())), preferred_element_type=F32)


def _ctx_attn_kernel(sink_ref, q_ref, k_ref, v_ref, o_ref):
    for g in range(N_KV_HEADS):
        q = _stack_heads(q_ref, g)
        k = k_ref[:, g * HEAD_DIM:(g + 1) * HEAD_DIM].astype(BF16)
        v = v_ref[:, g * HEAD_DIM:(g + 1) * HEAD_DIM].astype(BF16)
        sink = _sink_column(sink_ref, g, SEQ)
        s = _qk(q, k)
        m = jnp.maximum(jnp.max(s, axis=-1, keepdims=True), sink)
        p = jnp.exp(s - m)
        denom = jnp.sum(p, axis=-1, keepdims=True) + jnp.exp(sink - m)
        o = jnp.dot(p.astype(BF16), v, preferred_element_type=F32) / denom
        for r in range(GQA_GROUP):
            hd = g * GQA_GROUP + r
            o_ref[:, hd * HEAD_DIM:(hd + 1) * HEAD_DIM] = o[r * SEQ:(r + 1) * SEQ].astype(BF16)


def _ctx_attn(q, k, v, sink):
    first = T_LAT // SEQ
    blk = lambda b: (first + b, 0)
    return pl.pallas_call(
        _ctx_attn_kernel,
        grid=(BATCH,),
        in_specs=[
            pl.BlockSpec(memory_space=pltpu.SMEM),
            pl.BlockSpec((SEQ, ATTN_W), blk),
            pl.BlockSpec((SEQ, KV_W), blk),
            pl.BlockSpec((SEQ, KV_W), blk),
        ],
        out_specs=pl.BlockSpec((SEQ, ATTN_W), lambda b: (b, 0)),
        out_shape=jax.ShapeDtypeStruct((T_CTX, ATTN_W), BF16),
        compiler_params=_params("arbitrary"),
        name="ctx_attn",
    )(sink, q, k, v)


def _lat_attn_kernel(sink_ref, q_ref, kp_ref, kc_ref, kn_ref, vp_ref, vc_ref, vn_ref,
                     ck_ref, cv_ref, o_ref):
    j = pl.program_id(1)
    rows = GQA_GROUP * Q_BLOCK
    band = Q_BLOCK + 2 * WINDOW
    qpos = j * Q_BLOCK + lax.broadcasted_iota(jnp.int32, (rows, band), 0) % Q_BLOCK
    kpos = j * Q_BLOCK - WINDOW + lax.broadcasted_iota(jnp.int32, (rows, band), 1)
    diff = qpos - kpos
    valid = (kpos >= 0) & (kpos < DEC_SEQ) & (diff <= WINDOW) & (diff >= -WINDOW)
    for g in range(N_KV_HEADS):
        cols = slice(g * HEAD_DIM, (g + 1) * HEAD_DIM)
        q = _stack_heads(q_ref, g)
        k_loc = jnp.concatenate([kp_ref[:, cols], kc_ref[:, cols], kn_ref[:, cols]], axis=0).astype(BF16)
        v_loc = jnp.concatenate([vp_ref[:, cols], vc_ref[:, cols], vn_ref[:, cols]], axis=0).astype(BF16)
        k_ctx = ck_ref[:, cols].astype(BF16)
        v_ctx = cv_ref[:, cols].astype(BF16)
        sink = _sink_column(sink_ref, g, Q_BLOCK)
        s_loc = jnp.where(valid, _qk(q, k_loc), -jnp.inf)
        s_ctx = _qk(q, k_ctx)
        m = jnp.maximum(jnp.maximum(jnp.max(s_loc, axis=-1, keepdims=True),
                                    jnp.max(s_ctx, axis=-1, keepdims=True)), sink)
        p_loc = jnp.exp(s_loc - m)
        p_ctx = jnp.exp(s_ctx - m)
        denom = (jnp.sum(p_loc, axis=-1, keepdims=True) + jnp.sum(p_ctx, axis=-1, keepdims=True)
                 + jnp.exp(sink - m))
        o = (jnp.dot(p_loc.astype(BF16), v_loc, preferred_element_type=F32)
             + jnp.dot(p_ctx.astype(BF16), v_ctx, preferred_element_type=F32)) / denom
        for r in range(GQA_GROUP):
            hd = g * GQA_GROUP + r
            o_ref[:, hd * HEAD_DIM:(hd + 1) * HEAD_DIM] = o[r * Q_BLOCK:(r + 1) * Q_BLOCK].astype(BF16)


def _lat_attn(q, k, v, cache_k, cache_v, sink):
    nb = DEC_SEQ // Q_BLOCK
    cur = lambda b, j: (b * nb + j, 0)
    prev = lambda b, j: (b * nb + jnp.maximum(j - 1, 0), 0)
    nxt = lambda b, j: (b * nb + jnp.minimum(j + 1, nb - 1), 0)
    kvb = lambda im: pl.BlockSpec((Q_BLOCK, KV_W), im)
    cache = pl.BlockSpec((None, PAST_LEN, KV_W), lambda b, j: (b, 0, 0))
    return pl.pallas_call(
        _lat_attn_kernel,
        grid=(DEC_BATCH, nb),
        in_specs=[
            pl.BlockSpec(memory_space=pltpu.SMEM),
            pl.BlockSpec((Q_BLOCK, ATTN_W), cur),
            kvb(prev), kvb(cur), kvb(nxt), kvb(prev), kvb(cur), kvb(nxt),
            cache, cache,
        ],
        out_specs=pl.BlockSpec((Q_BLOCK, ATTN_W), cur),
        out_shape=jax.ShapeDtypeStruct((T_LAT, ATTN_W), BF16),
        compiler_params=_params("arbitrary", "arbitrary"),
        name="lat_attn",
    )(sink, q, k, k, k, v, v, v, cache_k, cache_v)


def _rnn_kernel(seq, xr_ref, cw_ref, cb_ref, w_ref, b_ref, lam_ref, h0_ref,
                y_ref, hl_ref, xt_ref, yt_ref, a_ref, u_ref):
    n_chunks = seq // RNN_CHUNK
    rows = RNN_CHUNK * SUBLANES
    halo = jnp.zeros((CONV_LEFT, SUBLANES, RNN_CB), F32)
    xt_ref[0:CONV_LEFT] = halo
    xt_ref[seq + CONV_LEFT:seq + 2 * CONV_LEFT] = halo

    def load_chunk(c, carry):
        t0 = pl.multiple_of(c * RNN_CHUNK, RNN_CHUNK)
        x = xr_ref[:, pl.ds(t0, RNN_CHUNK), :].astype(F32)
        xt_ref[pl.ds(t0 + CONV_LEFT, RNN_CHUNK)] = jnp.swapaxes(x, 0, 1)
        return carry

    lax.fori_loop(0, n_chunks, load_chunk, 0)

    def conv_chunk(t0):
        acc = cb_ref[...].reshape(1, 1, RNN_CB)
        for tap in range(CONV_W):
            acc = acc + xt_ref[pl.ds(t0 + tap, RNN_CHUNK)] * cw_ref[tap:tap + 1, :].reshape(1, 1, RNN_CB)
        return acc

    for d in range(2):
        softplus = jax.nn.softplus(-lam_ref[d:d + 1, :])
        bias = b_ref[d:d + 1, :]

        def chunk(ci, h, d=d, softplus=softplus, bias=bias):
            c = ci if d == 0 else n_chunks - 1 - ci
            t0 = pl.multiple_of(c * RNN_CHUNK, RNN_CHUNK)
            xc = conv_chunk(t0).reshape(rows, RNN_CB)
            z = jnp.dot(xc.astype(BF16), w_ref[d], preferred_element_type=F32) + bias
            r = _sigmoid(z[:, :RNN_CB])
            i = _sigmoid(z[:, RNN_CB:])
            log_a = (-LRU_C) * r * softplus
            a = jnp.exp(log_a)
            mult = jnp.sqrt(-jnp.tanh(log_a) * (a * a + 1.0))
            a_ref[...] = a.reshape(RNN_CHUNK, SUBLANES, RNN_CB)
            u_ref[...] = (mult * (i * xc)).reshape(RNN_CHUNK, SUBLANES, RNN_CB)

            def step(s, h):
                t = s if d == 0 else RNN_CHUNK - 1 - s
                h = a_ref[t] * h + u_ref[t]
                if d == 0:
                    yt_ref[t0 + t] = h
                else:
                    yt_ref[t0 + t] = yt_ref[t0 + t] + h
                return h

            return lax.fori_loop(0, RNN_CHUNK, step, h, unroll=8)

        hl_ref[d] = lax.fori_loop(0, n_chunks, chunk, h0_ref[d])

    def store_chunk(c, carry):
        t0 = pl.multiple_of(c * RNN_CHUNK, RNN_CHUNK)
        y = jnp.swapaxes(yt_ref[pl.ds(t0, RNN_CHUNK)], 0, 1)
        y_ref[:, pl.ds(t0, RNN_CHUNK), :] = y.astype(BF16)
        return carry

    lax.fori_loop(0, n_chunks, store_chunk, 0)


def _rnn(xr, conv_w, conv_b, w_blk, b_blk, lam, h0, layer, seq, n_groups, first_group):
    n_cb = D_RNN // RNN_CB
    return pl.pallas_call(
        functools.partial(_rnn_kernel, seq),
        grid=(n_groups, n_cb),
        in_specs=[
            pl.BlockSpec((SUBLANES, seq, RNN_CB), lambda g, j: (first_group + g, 0, j)),
            pl.BlockSpec((None, CONV_W, RNN_CB), lambda g, j: (layer, 0, j)),
            pl.BlockSpec((None, 1, RNN_CB), lambda g, j: (layer, 0, j)),
            pl.BlockSpec((None, None, 2, RNN_CB, 2 * RNN_CB), lambda g, j: (layer, j, 0, 0, 0)),
            pl.BlockSpec((None, None, 2, 2 * RNN_CB), lambda g, j: (layer, j, 0, 0)),
            pl.BlockSpec((None, 2, RNN_CB), lambda g, j: (layer, 0, j)),
            pl.BlockSpec((None, 2, SUBLANES, RNN_CB), lambda g, j: (g, 0, 0, j)),
        ],
        out_specs=[
            pl.BlockSpec((SUBLANES, seq, RNN_CB), lambda g, j: (g, 0, j)),
            pl.BlockSpec((None, 2, SUBLANES, RNN_CB), lambda g, j: (g, 0, 0, j)),
        ],
        out_shape=[
            jax.ShapeDtypeStruct((n_groups * SUBLANES, seq, D_RNN), BF16),
            jax.ShapeDtypeStruct((n_groups, 2, SUBLANES, D_RNN), F32),
        ],
        scratch_shapes=[
            pltpu.VMEM((seq + 2 * CONV_LEFT, SUBLANES, RNN_CB), F32),
            pltpu.VMEM((seq, SUBLANES, RNN_CB), F32),
            pltpu.VMEM((RNN_CHUNK, SUBLANES, RNN_CB), F32),
            pltpu.VMEM((RNN_CHUNK, SUBLANES, RNN_CB), F32),
        ],
        compiler_params=_params("arbitrary", "arbitrary"),
        name=f"rnn_s{seq}",
    )(xr, conv_w, conv_b.reshape(DEPTH, 1, D_RNN), w_blk, b_blk, lam, h0)


def _lru_block_weights(lru_wa, lru_wx, lru_ba, lru_bx):
    n_cb = D_RNN // RNN_CB
    per = RNN_CB // LRU_BW

    def dense(w):
        w = w.reshape(DEPTH, 2, n_cb, per, LRU_BW, LRU_BW)
        eye = jnp.eye(per, dtype=w.dtype)
        full = jnp.einsum("ldcpkj,pq->ldcpkqj", w, eye)
        return full.reshape(DEPTH, 2, n_cb, RNN_CB, RNN_CB)

    w = jnp.concatenate([dense(lru_wa), dense(lru_wx)], axis=-1)
    w = jnp.transpose(w, (0, 2, 1, 3, 4)).astype(BF16)
    b = jnp.concatenate([lru_ba.reshape(DEPTH, 2, n_cb, RNN_CB),
                         lru_bx.reshape(DEPTH, 2, n_cb, RNN_CB)], axis=-1)
    return w, jnp.transpose(b, (0, 2, 1, 3))


def _route(scores, sel):
    grp_score = []
    for g in range(N_GROUPS):
        a, b, c, d = sel[g * EXPERTS_PER_GROUP:(g + 1) * EXPERTS_PER_GROUP]
        hi1, lo1 = jnp.maximum(a, b), jnp.minimum(a, b)
        hi2, lo2 = jnp.maximum(c, d), jnp.minimum(c, d)
        grp_score.append(jnp.maximum(hi1, hi2) + jnp.maximum(jnp.minimum(hi1, hi2), jnp.maximum(lo1, lo2)))
    best = jnp.zeros_like(grp_score[0], dtype=jnp.int32)
    best_val = grp_score[0]
    for g in range(1, N_GROUPS):
        better = grp_score[g] > best_val
        best = jnp.where(better, g, best)
        best_val = jnp.where(better, grp_score[g], best_val)
    chosen = []
    for e in range(N_EXPERTS):
        g = e // EXPERTS_PER_GROUP
        rank = jnp.zeros_like(best)
        for o in range(g * EXPERTS_PER_GROUP, (g + 1) * EXPERTS_PER_GROUP):
            if o == e:
                continue
            ahead = (sel[o] >= sel[e]) if o < e else (sel[o] > sel[e])
            rank = rank + jnp.where(ahead, 1, 0)
        chosen.append(jnp.where(best == g, rank, 2) < 2)
    taken, gate = [], []
    for j in range(EXPERTS_PER_GROUP):
        t = jnp.zeros_like(best)
        s = jnp.zeros_like(scores[0])
        for g in range(N_GROUPS):
            e = g * EXPERTS_PER_GROUP + j
            t = t + jnp.where(chosen[e], 1, 0)
            s = s + jnp.where(chosen[e], scores[e], 0.0)
        taken.append(t > 0)
        gate.append(s)
    total = gate[0] + gate[1] + gate[2] + gate[3]
    pair = jnp.where(taken[0], jnp.where(taken[1], 0, jnp.where(taken[2], 1, 2)),
                     jnp.where(taken[1], jnp.where(taken[2], 3, 4), 5))
    w_lo = jnp.where(taken[0], gate[0], jnp.where(taken[1], gate[1], gate[2])) / total
    w_hi = jnp.where(taken[3], gate[3], jnp.where(taken[2], gate[2], gate[1])) / total
    return best * PAIRS_PER_GROUP + pair, w_lo, w_hi


def _merge_kernel(x_ref, att_ref, rnn_ref, gr_ref, ga_ref, gb_ref, g1_ref, sh2_ref, sc2_ref, n2_ref,
                  wap_ref, wrp_ref, wo_ref, rw_ref, rb_ref,
                  xo_ref, h2_ref, bucket_ref):
    att = jnp.dot(att_ref[...], wap_ref[...], preferred_element_type=F32)
    gated = (jax.nn.gelu(gr_ref[...].astype(F32)) * rnn_ref[...].astype(F32)).astype(BF16)
    rnn = jnp.dot(gated, wrp_ref[...], preferred_element_type=F32)
    merged = _sigmoid(ga_ref[...].astype(F32)) * att + _sigmoid(gb_ref[...].astype(F32)) * rnn
    x = x_ref[...] + g1_ref[...] * jnp.dot(merged.astype(BF16), wo_ref[...], preferred_element_type=F32)
    xo_ref[...] = x
    y = x * lax.rsqrt(jnp.mean(x * x, axis=-1, keepdims=True) + EPS) * n2_ref[...]
    h2 = y * (1.0 + sc2_ref[...]) + sh2_ref[...]
    h2_ref[:, :D_MODEL] = h2
    h_hi = h2.astype(BF16)
    h_lo = (h2 - h_hi.astype(F32)).astype(BF16)
    rw = rw_ref[...]
    w_hi = rw.astype(BF16)
    w_lo = (rw - w_hi.astype(F32)).astype(BF16)
    logits = _qk(w_hi, h_hi) + (_qk(w_hi, h_lo) + _qk(w_lo, h_hi))
    score = _sigmoid(logits)
    sel = score + rb_ref[...]
    bucket, w_lo, w_hi = _route([score[e:e + 1, :] for e in range(N_EXPERTS)],
                                [sel[e:e + 1, :] for e in range(N_EXPERTS)])
    bucket_ref[...] = bucket
    pad = jnp.zeros((LANES - 2, TOK_TILE), F32)
    h2_ref[:, D_MODEL:] = jnp.concatenate([w_lo, w_hi, pad], axis=0).T


def _merge(x, att, rnn, gr, ga, gb, mods, norm_g, wap, wrp, wo, router_wt, router_b, layer):
    tok = lambda i: (i, 0)
    wide = pl.BlockSpec((TOK_TILE, D_MODEL), tok)
    mod = lambda col: pl.BlockSpec((None, 1, D_MODEL), lambda i: (layer * COND_ROWS + _mod_row(i), 0, col))
    mat = pl.BlockSpec((None, D_MODEL, D_MODEL), lambda i: (layer, 0, 0))
    return pl.pallas_call(
        _merge_kernel,
        grid=(N_TOK_TILES,),
        in_specs=[
            wide, wide, wide, wide, wide, wide,
            mod(2), mod(3), mod(4),
            pl.BlockSpec((None, 1, D_MODEL), lambda i: (layer, 0, 0)),
            mat, mat, mat,
            pl.BlockSpec((N_EXPERTS, D_MODEL), lambda i: (0, 0)),
            pl.BlockSpec((N_EXPERTS, 1), lambda i: (0, 0)),
        ],
        out_specs=[wide, pl.BlockSpec((TOK_TILE, ROW_W), tok),
                   pl.BlockSpec((None, 1, TOK_TILE), lambda i: (i, 0, 0))],
        out_shape=[
            jax.ShapeDtypeStruct((T_ALL, D_MODEL), F32),
            jax.ShapeDtypeStruct((T_ALL, ROW_W), F32),
            jax.ShapeDtypeStruct((N_TOK_TILES, 1, TOK_TILE), jnp.int32),
        ],
        compiler_params=_params("arbitrary"),
        name="merge",
    )(x, att, rnn, gr, ga, gb, mods, mods, mods, norm_g.reshape(DEPTH, 1, D_MODEL),
      wap, wrp, wo, router_wt, router_b.reshape(N_EXPERTS, 1))


def _plan_kernel(b_ref, pos_ref, tiles_ref):
    b = b_ref[...]
    r = lax.broadcasted_iota(jnp.int32, (TOK_TILE, TOK_TILE), 0)
    c = lax.broadcasted_iota(jnp.int32, (TOK_TILE, TOK_TILE), 1)
    before_in_tile = jnp.where(r < c, 1.0, 0.0).astype(BF16)
    br = lax.broadcasted_iota(jnp.int32, (N_TOK_TILES, N_TOK_TILES), 0)
    bc = lax.broadcasted_iota(jnp.int32, (N_TOK_TILES, N_TOK_TILES), 1)
    earlier_tiles = jnp.where(bc < br, 1.0, 0.0).astype(BF16)
    lane = lax.broadcasted_iota(jnp.int32, (1, LANES), 1)
    tile_start = (lane * MOE_TM).astype(F32)
    start = jnp.zeros((1, 1), F32)
    pos = jnp.zeros(b.shape, F32)
    tile_bucket = jnp.zeros((1, LANES), F32)
    for k in range(N_BUCKETS):
        mask = jnp.where(b == k, 1.0, 0.0)
        rank = jnp.dot(mask.astype(BF16), before_in_tile, preferred_element_type=F32)
        per_tile = jnp.sum(mask, axis=1, keepdims=True)
        tile_off = jnp.dot(earlier_tiles, jnp.broadcast_to(per_tile, (N_TOK_TILES, LANES)).astype(BF16),
                           preferred_element_type=F32)[:, :1]
        total = jnp.sum(per_tile, axis=0, keepdims=True)
        padded = jnp.floor((total + (MOE_TM - 1)) * (1.0 / MOE_TM)) * MOE_TM
        pos = pos + mask * (start + tile_off + rank)
        start = start + padded
        tile_bucket = tile_bucket + jnp.where(tile_start >= start, 1.0, 0.0)
    pos_ref[...] = pos.astype(jnp.int32)
    tiles_ref[...] = jnp.where(lane == LANES - 1, start * (1.0 / MOE_TM), tile_bucket).astype(jnp.int32)


def _plan(bucket):
    return pl.pallas_call(
        _plan_kernel,
        out_shape=[
            jax.ShapeDtypeStruct((N_TOK_TILES, TOK_TILE), jnp.int32),
            jax.ShapeDtypeStruct((1, LANES), jnp.int32),
        ],
        compiler_params=pltpu.CompilerParams(vmem_limit_bytes=VMEM_LIMIT),
        name="plan",
    )(bucket.reshape(N_TOK_TILES, TOK_TILE))


def _row_copy(src, dst, sem):
    return pltpu.make_async_copy(src, dst, sem)


def _dispatch_kernel(pos_ref, h_ref, zero_ref, xs_ref, sem):
    del zero_ref

    def issue(r, carry):
        _row_copy(h_ref.at[pl.ds(r, 1), :], xs_ref.at[pl.ds(pos_ref[0, r], 1), :], sem).start()
        return carry

    lax.fori_loop(0, TOK_TILE, issue, 0, unroll=8)

    def wait(r, carry):
        _row_copy(h_ref.at[pl.ds(0, 1), :], xs_ref.at[pl.ds(0, 1), :], sem).wait()
        return carry

    lax.fori_loop(0, TOK_TILE, wait, 0, unroll=8)


def _dispatch(pos, h2, zeros):
    return pl.pallas_call(
        _dispatch_kernel,
        grid=(N_TOK_TILES,),
        in_specs=[
            pl.BlockSpec((None, 1, TOK_TILE), lambda i: (i, 0, 0), memory_space=pltpu.SMEM),
            pl.BlockSpec((TOK_TILE, ROW_W), lambda i: (i, 0)),
            pl.BlockSpec(memory_space=pl.ANY),
        ],
        out_specs=pl.BlockSpec(memory_space=pl.ANY),
        out_shape=jax.ShapeDtypeStruct((MOE_ROWS, ROW_W), F32),
        input_output_aliases={2: 0},
        scratch_shapes=[pltpu.SemaphoreType.DMA(())],
        compiler_params=_params("arbitrary"),
        name="dispatch",
    )(pos.reshape(N_TOK_TILES, 1, TOK_TILE), h2, zeros)


def _tile_bucket(i, tiles_ref):
    return tiles_ref[0, jnp.minimum(i, tiles_ref[0, LANES - 1] - 1)]


def _bucket_expert(k, hi):
    g = k // PAIRS_PER_GROUP
    p = k % PAIRS_PER_GROUP
    lo_idx = jnp.where(p >= 3, 1, 0) + jnp.where(p >= 5, 1, 0)
    hi_idx = jnp.where(p < 3, p + 1, jnp.where(p < 5, p - 1, 3))
    return g * EXPERTS_PER_GROUP + (hi_idx if hi else lo_idx)


def _experts_kernel(tiles_ref, xs_ref, wgu_lo_ref, wgu_hi_ref, wd_lo_ref, wd_hi_ref, y_ref):
    in_use = pl.program_id(0) < tiles_ref[0, LANES - 1]

    @pl.when(jnp.logical_not(in_use))
    def _():
        y_ref[...] = jnp.zeros_like(y_ref)

    @pl.when(in_use)
    def _():
        x = xs_ref[:, :D_MODEL].astype(BF16)

        def branch(wgu_ref, wd_ref, gate):
            gu = jnp.dot(x, wgu_ref[...], preferred_element_type=F32)
            act = jax.nn.silu(gu[:, :D_EXPERT]) * gu[:, D_EXPERT:] * gate
            return jnp.dot(act.astype(BF16), wd_ref[...], preferred_element_type=F32)

        y_ref[...] = (branch(wgu_lo_ref, wd_lo_ref, xs_ref[:, D_MODEL:D_MODEL + 1])
                      + branch(wgu_hi_ref, wd_hi_ref, xs_ref[:, D_MODEL + 1:D_MODEL + 2]))


def _experts(tiles, xs, wgu, wd, layer):
    def weight(shape, hi):
        return pl.BlockSpec((None, None) + shape,
                            lambda i, t: (layer, _bucket_expert(_tile_bucket(i, t), hi), 0, 0))

    return pl.pallas_call(
        _experts_kernel,
        grid_spec=pltpu.PrefetchScalarGridSpec(
            num_scalar_prefetch=1,
            grid=(N_MOE_TILES,),
            in_specs=[
                pl.BlockSpec((MOE_TM, ROW_W), lambda i, t: (i, 0)),
                weight((D_MODEL, 2 * D_EXPERT), False), weight((D_MODEL, 2 * D_EXPERT), True),
                weight((D_EXPERT, D_MODEL), False), weight((D_EXPERT, D_MODEL), True),
            ],
            out_specs=pl.BlockSpec((MOE_TM, D_MODEL), lambda i, t: (i, 0)),
        ),
        out_shape=jax.ShapeDtypeStruct((MOE_ROWS, D_MODEL), F32),
        compiler_params=_params("arbitrary"),
        name="experts",
    )(tiles, xs, wgu, wgu, wd, wd)


def _combine_kernel(pos_ref, x_ref, g2_ref, y_ref, o_ref, rows_ref, sem):
    def issue(r, carry):
        _row_copy(y_ref.at[pl.ds(pos_ref[0, r], 1), :], rows_ref.at[pl.ds(r, 1), :], sem).start()
        return carry

    lax.fori_loop(0, TOK_TILE, issue, 0, unroll=8)

    def wait(r, carry):
        _row_copy(y_ref.at[pl.ds(0, 1), :], rows_ref.at[pl.ds(0, 1), :], sem).wait()
        return carry

    lax.fori_loop(0, TOK_TILE, wait, 0, unroll=8)
    o_ref[...] = x_ref[...] + g2_ref[...] * rows_ref[...]


def _combine(pos, x, mods, y, layer):
    return pl.pallas_call(
        _combine_kernel,
        grid=(N_TOK_TILES,),
        in_specs=[
            pl.BlockSpec((None, 1, TOK_TILE), lambda i: (i, 0, 0), memory_space=pltpu.SMEM),
            pl.BlockSpec((TOK_TILE, D_MODEL), lambda i: (i, 0)),
            pl.BlockSpec((None, 1, D_MODEL), lambda i: (layer * COND_ROWS + _mod_row(i), 0, 5)),
            pl.BlockSpec(memory_space=pl.ANY),
        ],
        out_specs=pl.BlockSpec((TOK_TILE, D_MODEL), lambda i: (i, 0)),
        out_shape=jax.ShapeDtypeStruct((T_ALL, D_MODEL), F32),
        scratch_shapes=[pltpu.VMEM((TOK_TILE, D_MODEL), F32), pltpu.SemaphoreType.DMA(())],
        compiler_params=_params("arbitrary"),
        name="combine",
    )(pos.reshape(N_TOK_TILES, 1, TOK_TILE), x, mods, y)


def _final_norm_kernel(x_ref, g_ref, o_ref):
    x = x_ref[...]
    o_ref[...] = x * lax.rsqrt(jnp.mean(x * x, axis=-1, keepdims=True) + EPS) * g_ref[...]


def _final_norm(x, g, first_tile, n_tiles):
    return pl.pallas_call(
        _final_norm_kernel,
        grid=(n_tiles,),
        in_specs=[
            pl.BlockSpec((TOK_TILE, D_MODEL), lambda i: (first_tile + i, 0)),
            pl.BlockSpec((1, D_MODEL), lambda i: (0, 0)),
        ],
        out_specs=pl.BlockSpec((TOK_TILE, D_MODEL), lambda i: (i, 0)),
        out_shape=jax.ShapeDtypeStruct((n_tiles * TOK_TILE, D_MODEL), F32),
        compiler_params=_params("arbitrary"),
        name="final_norm",
    )(x, g.reshape(1, D_MODEL))


def _rope_tables():
    t = jnp.arange(DEC_SEQ, dtype=jnp.int32)
    row = (t // GRID_W).astype(F32)
    col = (t % GRID_W).astype(F32)
    inv = ROPE_BASE ** (-jnp.arange(ROPE_PAIRS_PER_AXIS, dtype=F32) / ROPE_PAIRS_PER_AXIS)
    ang = jnp.concatenate([row[:, None] * inv, col[:, None] * inv], axis=-1)
    cos, sin = jnp.cos(ang), jnp.sin(ang)
    cos = jnp.concatenate([cos, cos], axis=-1)
    sin = jnp.concatenate([-sin, sin], axis=-1)
    cos = jnp.concatenate([cos, jnp.ones((TOK_TILE, HEAD_DIM), F32)], axis=0)
    sin = jnp.concatenate([sin, jnp.zeros((TOK_TILE, HEAD_DIM), F32)], axis=0)
    return cos, sin


def kernel(x_prompt, x_sample, cache_k, cache_v, state_h, c, c_ctx, norm1_g, norm2_g, w_ada, b_ada, w_in, conv_w, conv_b, lru_wa, lru_ba, lru_wx, lru_bx, lru_lambda, attn_sink, w_attn_proj, w_rnn_proj, w_out, router_w, router_b, w_gate_up, w_down, final_norm_g):
    x = jnp.concatenate([x_sample.reshape(T_LAT, D_MODEL), x_prompt.reshape(T_CTX, D_MODEL)], axis=0)
    cond = jnp.concatenate([c_ctx[None, :], c, jnp.zeros((COND_ROWS - 1 - DEC_BATCH, D_MODEL), F32)], axis=0)
    mods = _ada(cond, w_ada, b_ada).reshape(DEPTH * COND_ROWS, N_ADA, D_MODEL)
    mods = mods.reshape(DEPTH * COND_ROWS, 1, N_ADA * D_MODEL)
    rope_cos, rope_sin = _rope_tables()

    w_in_b = w_in.astype(BF16)
    wap_b = w_attn_proj.astype(BF16)
    wrp_b = w_rnn_proj.astype(BF16)
    wo_b = w_out.astype(BF16)
    wgu_b = w_gate_up.astype(BF16)
    wd_b = w_down.astype(BF16)
    lru_w, lru_b = _lru_block_weights(lru_wa, lru_wx, lru_ba, lru_bx)
    router_wt = router_w.T

    h0_lat = jnp.transpose(state_h.astype(F32), (1, 2, 0, 3)).reshape(DEPTH, 1, 2, DEC_BATCH, D_RNN)
    h0_ctx = jnp.zeros((BATCH // SUBLANES, 2, SUBLANES, D_RNN), F32)
    ck = jnp.transpose(cache_k, (1, 0, 2, 3, 4)).reshape(DEPTH, DEC_BATCH, PAST_LEN, KV_W)
    cv = jnp.transpose(cache_v, (1, 0, 2, 3, 4)).reshape(DEPTH, DEC_BATCH, PAST_LEN, KV_W)

    new_k, new_v, new_h = [], [], []
    for l in range(DEPTH):
        q, k, v, xr, gr, ga, gb = _inproj(x, norm1_g, mods, rope_cos, rope_sin, w_in_b, l)
        att_lat = _lat_attn(q, k, v, ck[l], cv[l], attn_sink[l])
        att_ctx = _ctx_attn(q, k, v, attn_sink[l])
        y_lat, _ = _rnn(xr.reshape(T_ALL // DEC_SEQ, DEC_SEQ, D_RNN), conv_w, conv_b, lru_w, lru_b,
                        lru_lambda, h0_lat[l], l, DEC_SEQ, DEC_BATCH // SUBLANES, 0)
        y_ctx, h_ctx = _rnn(xr.reshape(T_ALL // SEQ, SEQ, D_RNN), conv_w, conv_b, lru_w, lru_b,
                            lru_lambda, h0_ctx, l, SEQ, BATCH // SUBLANES, T_LAT // SEQ // SUBLANES)
        att = jnp.concatenate([att_lat, att_ctx], axis=0)
        rnn = jnp.concatenate([y_lat.reshape(T_LAT, D_RNN), y_ctx.reshape(T_CTX, D_RNN)], axis=0)
        x, h2, bucket = _merge(x, att, rnn, gr, ga, gb, mods, norm2_g, wap_b, wrp_b, wo_b,
                               router_wt, router_b, l)
        pos, tiles = _plan(bucket)
        xs = _dispatch(pos, h2, jnp.zeros((MOE_ROWS, ROW_W), F32))
        x = _combine(pos, x, mods, _experts(tiles, xs, wgu_b, wd_b, l), l)
        new_k.append(k[T_LAT:].reshape(BATCH, SEQ, N_KV_HEADS, HEAD_DIM))
        new_v.append(v[T_LAT:].reshape(BATCH, SEQ, N_KV_HEADS, HEAD_DIM))
        new_h.append(jnp.transpose(h_ctx, (0, 2, 1, 3)).reshape(BATCH, 2, D_RNN))

    y_sample = _final_norm(x, final_norm_g, 0, N_LAT_TILES).reshape(DEC_BATCH, DEC_SEQ, D_MODEL)
    y_prompt = _final_norm(x, final_norm_g, N_LAT_TILES, N_TOK_TILES - N_LAT_TILES).reshape(BATCH, SEQ, D_MODEL)
    return (y_prompt, y_sample, jnp.stack(new_k, axis=1), jnp.stack(new_v, axis=1),
            jnp.stack(new_h, axis=1).astype(x_prompt.dtype))
```

```python
import functools

import jax
import jax.numpy as jnp
from jax import lax
from jax.experimental import pallas as pl
from jax.experimental.pallas import tpu as pltpu

F32 = jnp.float32
BF16 = jnp.bfloat16

D_MODEL = 1024
BATCH = 16
SEQ = 256
DEPTH = 2
DEC_BATCH = 8
DEC_SEQ = 1024
PAST_LEN = 512
GRID_W = 64
N_HEADS = 8
N_KV_HEADS = 2
GQA_GROUP = N_HEADS // N_KV_HEADS
HEAD_DIM = 128
ATTN_W = N_HEADS * HEAD_DIM
KV_W = N_KV_HEADS * HEAD_DIM
WINDOW = 128
Q_BLOCK = 128
ROPE_BASE = 10000.0
ROPE_PAIRS_PER_AXIS = HEAD_DIM // 4
D_RNN = 1024
LRU_BLOCKS = 16
LRU_BW = D_RNN // LRU_BLOCKS
LRU_C = 8.0
CONV_W = 4
CONV_LEFT = 2
N_EXPERTS = 16
N_GROUPS = 4
EXPERTS_PER_GROUP = N_EXPERTS // N_GROUPS
D_EXPERT = 512
N_ADA = 6
EPS = 1e-6
IN_COLS = ATTN_W + 2 * KV_W + 2 * D_RNN + 2 * D_MODEL

T_LAT = DEC_BATCH * DEC_SEQ
T_CTX = BATCH * SEQ
T_ALL = T_LAT + T_CTX

SUBLANES = 8
LANES = 128
VMEM_LIMIT = 56 * 1024 * 1024

TOK_TILE = 256
N_TOK_TILES = T_ALL // TOK_TILE
N_LAT_TILES = T_LAT // TOK_TILE
LAT_TILES_PER_BATCH = DEC_SEQ // TOK_TILE
COND_ROWS = 16
ROPE_ID_BLOCK = DEC_SEQ // TOK_TILE

RNN_CB = 256
RNN_CHUNK = 64
RNN_UNROLL = 8
LOG2_E = 1.4426950408889634
PAIRS_PER_GROUP = EXPERTS_PER_GROUP * (EXPERTS_PER_GROUP - 1) // 2
N_BUCKETS = N_GROUPS * PAIRS_PER_GROUP
ROW_W = D_MODEL + LANES
MOE_TM = 256
N_MOE_TILES = (T_ALL + N_BUCKETS * (MOE_TM - 1) + MOE_TM - 1) // MOE_TM
MOE_ROWS = N_MOE_TILES * MOE_TM


def _params(*sem):
    return pltpu.CompilerParams(dimension_semantics=sem, vmem_limit_bytes=VMEM_LIMIT)


def _sigmoid(x):
    return 0.5 * jnp.tanh(0.5 * x) + 0.5


def _mod_row(i):
    return jnp.where(i < N_LAT_TILES, 1 + i // LAT_TILES_PER_BATCH, 0)


def _stream_specs(width, ctx_first_tile):
    lat = pl.BlockSpec((TOK_TILE, width), lambda i: (jnp.minimum(i, N_LAT_TILES - 1), 0))
    ctx = pl.BlockSpec((TOK_TILE, width), lambda i: (ctx_first_tile + jnp.maximum(i - N_LAT_TILES, 0), 0))
    return [lat, ctx]


def _pick(lat_ref, ctx_ref):
    return lax.cond(pl.program_id(0) < N_LAT_TILES, lambda: lat_ref[...], lambda: ctx_ref[...])


def _ada_kernel(cond_ref, w_ref, b_ref, o_ref):
    s = jax.nn.silu(cond_ref[...]).astype(BF16)
    o_ref[...] = jnp.dot(s, w_ref[...].astype(BF16), preferred_element_type=F32) + b_ref[...]


def _ada(cond, w_ada, b_ada):
    cols = N_ADA * D_MODEL
    tn = 1536
    return pl.pallas_call(
        _ada_kernel,
        grid=(DEPTH, cols // tn),
        in_specs=[
            pl.BlockSpec((COND_ROWS, D_MODEL), lambda l, j: (0, 0)),
            pl.BlockSpec((None, D_MODEL, tn), lambda l, j: (l, 0, j)),
            pl.BlockSpec((None, 1, tn), lambda l, j: (l, 0, j)),
        ],
        out_specs=pl.BlockSpec((None, COND_ROWS, tn), lambda l, j: (l, 0, j)),
        out_shape=jax.ShapeDtypeStruct((DEPTH, COND_ROWS, cols), F32),
        compiler_params=_params("arbitrary", "arbitrary"),
        name="ada",
    )(cond, w_ada, b_ada.reshape(DEPTH, 1, cols))


def _inproj_kernel(xl_ref, xc_ref, g_ref, sh_ref, sc_ref, cos_ref, sin_ref, w_ref,
                   q_ref, k_ref, v_ref, xr_ref, gr_ref, ga_ref, gb_ref):
    x = _pick(xl_ref, xc_ref)
    y = x * lax.rsqrt(jnp.mean(x * x, axis=-1, keepdims=True) + EPS) * g_ref[...]
    h = (y * (1.0 + sc_ref[...]) + sh_ref[...]).astype(BF16)
    cos = cos_ref[...]
    sin = sin_ref[...]

    def proj(lo, width):
        return jnp.dot(h, w_ref[:, lo:lo + width], preferred_element_type=F32)

    def rope(t):
        return t * cos + pltpu.roll(t, HEAD_DIM // 2, 1) * sin

    scale = HEAD_DIM ** -0.5
    for hd in range(N_HEADS):
        q = proj(hd * HEAD_DIM, HEAD_DIM)
        q_ref[:, hd * HEAD_DIM:(hd + 1) * HEAD_DIM] = (rope(q) * scale).astype(BF16)
    for g in range(N_KV_HEADS):
        k = proj(ATTN_W + g * HEAD_DIM, HEAD_DIM)
        k_ref[:, g * HEAD_DIM:(g + 1) * HEAD_DIM] = rope(k)
    v_ref[...] = proj(ATTN_W + KV_W, KV_W)
    base = ATTN_W + 2 * KV_W
    xr_ref[...] = proj(base, D_RNN).astype(BF16)
    gr_ref[...] = proj(base + D_RNN, D_RNN).astype(BF16)
    ga_ref[...] = proj(base + 2 * D_RNN, D_MODEL).astype(BF16)
    gb_ref[...] = proj(base + 2 * D_RNN + D_MODEL, D_MODEL).astype(BF16)


def _inproj(x_lat, x_ctx, ctx_first_tile, norm_g, mods, rope_cos, rope_sin, w_in, layer):
    row = lambda i: layer * COND_ROWS + _mod_row(i)
    tok = lambda i: (i, 0)
    rope_blk = lambda i: (jnp.where(i < N_LAT_TILES, i % LAT_TILES_PER_BATCH, ROPE_ID_BLOCK), 0)
    wide = pl.BlockSpec((TOK_TILE, D_MODEL), tok)
    kv = pl.BlockSpec((TOK_TILE, KV_W), tok)
    return pl.pallas_call(
        _inproj_kernel,
        grid=(N_TOK_TILES,),
        in_specs=_stream_specs(D_MODEL, ctx_first_tile) + [
            pl.BlockSpec((None, 1, D_MODEL), lambda i: (layer, 0, 0)),
            pl.BlockSpec((None, 1, D_MODEL), lambda i: (row(i), 0, 0)),
            pl.BlockSpec((None, 1, D_MODEL), lambda i: (row(i), 0, 1)),
            pl.BlockSpec((TOK_TILE, HEAD_DIM), rope_blk),
            pl.BlockSpec((TOK_TILE, HEAD_DIM), rope_blk),
            pl.BlockSpec((None, D_MODEL, IN_COLS), lambda i: (layer, 0, 0)),
        ],
        out_specs=[wide, kv, kv, wide, wide, wide, wide],
        out_shape=[
            jax.ShapeDtypeStruct((T_ALL, ATTN_W), BF16),
            jax.ShapeDtypeStruct((T_ALL, KV_W), F32),
            jax.ShapeDtypeStruct((T_ALL, KV_W), F32),
            jax.ShapeDtypeStruct((T_ALL, D_RNN), BF16),
            jax.ShapeDtypeStruct((T_ALL, D_RNN), BF16),
            jax.ShapeDtypeStruct((T_ALL, D_MODEL), BF16),
            jax.ShapeDtypeStruct((T_ALL, D_MODEL), BF16),
        ],
        compiler_params=_params("arbitrary"),
        name="inproj",
    )(x_lat, x_ctx, norm_g.reshape(DEPTH, 1, D_MODEL), mods, mods, rope_cos, rope_sin, w_in)


def _stack_heads(q_ref, g):
    lo = g * GQA_GROUP
    return jnp.concatenate(
        [q_ref[:, (lo + r) * HEAD_DIM:(lo + r + 1) * HEAD_DIM] for r in range(GQA_GROUP)], axis=0)


def _sink_column(sink_ref, g, rows):
    return jnp.concatenate(
        [jnp.full((rows, 1), sink_ref[g * GQA_GROUP + r], F32) for r in range(GQA_GROUP)], axis=0)


def _qk(q, k):
    return lax.dot_general(q, k, (((1,), (1,)), ((), ())), preferred_element_type=F32)


def _ctx_attn_kernel(sink_ref, q_ref, k_ref, v_ref, o_ref):
    for g in range(N_KV_HEADS):
        q = _stack_heads(q_ref, g)
        k = k_ref[:, g * HEAD_DIM:(g + 1) * HEAD_DIM].astype(BF16)
        v = v_ref[:, g * HEAD_DIM:(g + 1) * HEAD_DIM].astype(BF16)
        sink = _sink_column(sink_ref, g, SEQ)
        s = _qk(q, k)
        m = jnp.maximum(jnp.max(s, axis=-1, keepdims=True), sink)
        p = jnp.exp(s - m)
        denom = jnp.sum(p, axis=-1, keepdims=True) + jnp.exp(sink - m)
        o = jnp.dot(p.astype(BF16), v, preferred_element_type=F32) / denom
        for r in range(GQA_GROUP):
            hd = g * GQA_GROUP + r
            o_ref[:, hd * HEAD_DIM:(hd + 1) * HEAD_DIM] = o[r * SEQ:(r + 1) * SEQ].astype(BF16)


def _ctx_attn(q, k, v, sink):
    first = T_LAT // SEQ
    blk = lambda b: (first + b, 0)
    return pl.pallas_call(
        _ctx_attn_kernel,
        grid=(BATCH,),
        in_specs=[
            pl.BlockSpec(memory_space=pltpu.SMEM),
            pl.BlockSpec((SEQ, ATTN_W), blk),
            pl.BlockSpec((SEQ, KV_W), blk),
            pl.BlockSpec((SEQ, KV_W), blk),
        ],
        out_specs=pl.BlockSpec((SEQ, ATTN_W), lambda b: (b, 0)),
        out_shape=jax.ShapeDtypeStruct((T_CTX, ATTN_W), BF16),
        compiler_params=_params("arbitrary"),
        name="ctx_attn",
    )(sink, q, k, v)


def _lat_attn_kernel(sink_ref, q_ref, kp_ref, kc_ref, kn_ref, vp_ref, vc_ref, vn_ref,
                     ck_ref, cv_ref, o_ref):
    j = pl.program_id(1)
    rows = GQA_GROUP * Q_BLOCK
    band = Q_BLOCK + 2 * WINDOW
    qpos = j * Q_BLOCK + lax.broadcasted_iota(jnp.int32, (rows, band), 0) % Q_BLOCK
    kpos = j * Q_BLOCK - WINDOW + lax.broadcasted_iota(jnp.int32, (rows, band), 1)
    diff = qpos - kpos
    valid = (kpos >= 0) & (kpos < DEC_SEQ) & (diff <= WINDOW) & (diff >= -WINDOW)
    for g in range(N_KV_HEADS):
        cols = slice(g * HEAD_DIM, (g + 1) * HEAD_DIM)
        q = _stack_heads(q_ref, g)
        k_loc = jnp.concatenate([kp_ref[:, cols], kc_ref[:, cols], kn_ref[:, cols]], axis=0).astype(BF16)
        v_loc = jnp.concatenate([vp_ref[:, cols], vc_ref[:, cols], vn_ref[:, cols]], axis=0).astype(BF16)
        k_ctx = ck_ref[:, cols].astype(BF16)
        v_ctx = cv_ref[:, cols].astype(BF16)
        sink = _sink_column(sink_ref, g, Q_BLOCK)
        s_loc = jnp.where(valid, _qk(q, k_loc), -jnp.inf)
        s_ctx = _qk(q, k_ctx)
        m = jnp.maximum(jnp.maximum(jnp.max(s_loc, axis=-1, keepdims=True),
                                    jnp.max(s_ctx, axis=-1, keepdims=True)), sink)
        p_loc = jnp.exp(s_loc - m)
        p_ctx = jnp.exp(s_ctx - m)
        denom = (jnp.sum(p_loc, axis=-1, keepdims=True) + jnp.sum(p_ctx, axis=-1, keepdims=True)
                 + jnp.exp(sink - m))
        o = (jnp.dot(p_loc.astype(BF16), v_loc, preferred_element_type=F32)
             + jnp.dot(p_ctx.astype(BF16), v_ctx, preferred_element_type=F32)) / denom
        for r in range(GQA_GROUP):
            hd = g * GQA_GROUP + r
            o_ref[:, hd * HEAD_DIM:(hd + 1) * HEAD_DIM] = o[r * Q_BLOCK:(r + 1) * Q_BLOCK].astype(BF16)


def _lat_attn(q, k, v, cache_k, cache_v, sink):
    nb = DEC_SEQ // Q_BLOCK
    cur = lambda b, j: (b * nb + j, 0)
    prev = lambda b, j: (b * nb + jnp.maximum(j - 1, 0), 0)
    nxt = lambda b, j: (b * nb + jnp.minimum(j + 1, nb - 1), 0)
    kvb = lambda im: pl.BlockSpec((Q_BLOCK, KV_W), im)
    cache = pl.BlockSpec((None, PAST_LEN, KV_W), lambda b, j: (b, 0, 0))
    return pl.pallas_call(
        _lat_attn_kernel,
        grid=(DEC_BATCH, nb),
        in_specs=[
            pl.BlockSpec(memory_space=pltpu.SMEM),
            pl.BlockSpec((Q_BLOCK, ATTN_W), cur),
            kvb(prev), kvb(cur), kvb(nxt), kvb(prev), kvb(cur), kvb(nxt),
            cache, cache,
        ],
        out_specs=pl.BlockSpec((Q_BLOCK, ATTN_W), cur),
        out_shape=jax.ShapeDtypeStruct((T_LAT, ATTN_W), BF16),
        compiler_params=_params("arbitrary", "arbitrary"),
        name="lat_attn",
    )(sink, q, k, k, k, v, v, v, cache_k, cache_v)


def _rnn_kernel(seq, xr_ref, cw_ref, cb_ref, w_ref, b_ref, lam_ref, h0_ref,
                y_ref, hl_ref, xt_ref, yt_ref, a_ref, u_ref):
    n_chunks = seq // RNN_CHUNK
    rows = RNN_CHUNK * SUBLANES
    halo = jnp.zeros((CONV_LEFT, SUBLANES, RNN_CB), F32)
    xt_ref[0:CONV_LEFT] = halo
    xt_ref[seq + CONV_LEFT:seq + 2 * CONV_LEFT] = halo

    def load_chunk(c, carry):
        t0 = pl.multiple_of(c * RNN_CHUNK, RNN_CHUNK)
        x = xr_ref[:, pl.ds(t0, RNN_CHUNK), :].astype(F32)
        xt_ref[pl.ds(t0 + CONV_LEFT, RNN_CHUNK)] = jnp.swapaxes(x, 0, 1)
        return carry

    lax.fori_loop(0, n_chunks, load_chunk, 0)

    def conv_chunk(t0):
        acc = cb_ref[...].reshape(1, 1, RNN_CB)
        for tap in range(CONV_W):
            acc = acc + xt_ref[pl.ds(t0 + tap, RNN_CHUNK)] * cw_ref[tap:tap + 1, :].reshape(1, 1, RNN_CB)
        return acc

    for d in range(2):
        decay = (-0.5 * LRU_C * LOG2_E) * jax.nn.softplus(-lam_ref[d:d + 1, :])
        half_bias = b_ref[d:d + 1, :]

        def chunk(ci, h, d=d, decay=decay, half_bias=half_bias):
            c = ci if d == 0 else n_chunks - 1 - ci
            t0 = pl.multiple_of(c * RNN_CHUNK, RNN_CHUNK)
            xc = conv_chunk(t0).reshape(rows, RNN_CB)
            z = jnp.dot(xc.astype(BF16), w_ref[d], preferred_element_type=F32) + half_bias
            a = jnp.exp2(decay * jnp.tanh(z[:, :RNN_CB]) + decay)
            m = 1.0 - a * a
            mult = jnp.where(m == 0.0, 0.0, m * lax.rsqrt(m))
            u = mult * ((jnp.tanh(z[:, RNN_CB:]) + 1.0) * (0.5 * xc))
            a_ref[...] = a.reshape(RNN_CHUNK, SUBLANES, RNN_CB)
            u_ref[...] = u.reshape(RNN_CHUNK, SUBLANES, RNN_CB)

            def steps(gi, h):
                s0 = pl.multiple_of((gi if d == 0 else RNN_CHUNK // RNN_UNROLL - 1 - gi) * RNN_UNROLL, RNN_UNROLL)
                for j in (range(RNN_UNROLL) if d == 0 else reversed(range(RNN_UNROLL))):
                    h = a_ref[s0 + j] * h + u_ref[s0 + j]
                    if d == 0:
                        yt_ref[t0 + s0 + j] = h
                    else:
                        yt_ref[t0 + s0 + j] = yt_ref[t0 + s0 + j] + h
                return h

            return lax.fori_loop(0, RNN_CHUNK // RNN_UNROLL, steps, h)

        hl_ref[d] = lax.fori_loop(0, n_chunks, chunk, h0_ref[d])

    def store_chunk(c, carry):
        t0 = pl.multiple_of(c * RNN_CHUNK, RNN_CHUNK)
        y = jnp.swapaxes(yt_ref[pl.ds(t0, RNN_CHUNK)], 0, 1)
        y_ref[:, pl.ds(t0, RNN_CHUNK), :] = y.astype(BF16)
        return carry

    lax.fori_loop(0, n_chunks, store_chunk, 0)


def _rnn(xr, conv_w, conv_b, w_blk, b_blk, lam, h0, layer, seq, n_groups, first_group):
    n_cb = D_RNN // RNN_CB
    return pl.pallas_call(
        functools.partial(_rnn_kernel, seq),
        grid=(n_groups, n_cb),
        in_specs=[
            pl.BlockSpec((SUBLANES, seq, RNN_CB), lambda g, j: (first_group + g, 0, j)),
            pl.BlockSpec((None, CONV_W, RNN_CB), lambda g, j: (layer, 0, j)),
            pl.BlockSpec((None, 1, RNN_CB), lambda g, j: (layer, 0, j)),
            pl.BlockSpec((None, None, 2, RNN_CB, 2 * RNN_CB), lambda g, j: (layer, j, 0, 0, 0)),
            pl.BlockSpec((None, None, 2, 2 * RNN_CB), lambda g, j: (layer, j, 0, 0)),
            pl.BlockSpec((None, 2, RNN_CB), lambda g, j: (layer, 0, j)),
            pl.BlockSpec((None, 2, SUBLANES, RNN_CB), lambda g, j: (g, 0, 0, j)),
        ],
        out_specs=[
            pl.BlockSpec((SUBLANES, seq, RNN_CB), lambda g, j: (g, 0, j)),
            pl.BlockSpec((None, 2, SUBLANES, RNN_CB), lambda g, j: (g, 0, 0, j)),
        ],
        out_shape=[
            jax.ShapeDtypeStruct((n_groups * SUBLANES, seq, D_RNN), BF16),
            jax.ShapeDtypeStruct((n_groups, 2, SUBLANES, D_RNN), F32),
        ],
        scratch_shapes=[
            pltpu.VMEM((seq + 2 * CONV_LEFT, SUBLANES, RNN_CB), F32),
            pltpu.VMEM((seq, SUBLANES, RNN_CB), F32),
            pltpu.VMEM((RNN_CHUNK, SUBLANES, RNN_CB), F32),
            pltpu.VMEM((RNN_CHUNK, SUBLANES, RNN_CB), F32),
        ],
        compiler_params=_params("arbitrary", "arbitrary"),
        name=f"rnn_s{seq}",
    )(xr, conv_w, conv_b.reshape(DEPTH, 1, D_RNN), w_blk, b_blk, lam, h0)


def _lru_block_weights(lru_wa, lru_wx, lru_ba, lru_bx):
    n_cb = D_RNN // RNN_CB
    per = RNN_CB // LRU_BW

    def dense(w):
        w = w.reshape(DEPTH, 2, n_cb, per, LRU_BW, LRU_BW)
        eye = jnp.eye(per, dtype=w.dtype)
        full = jnp.einsum("ldcpkj,pq->ldcpkqj", w, eye)
        return full.reshape(DEPTH, 2, n_cb, RNN_CB, RNN_CB)

    w = jnp.concatenate([dense(lru_wa), dense(lru_wx)], axis=-1)
    w = jnp.transpose(0.5 * w, (0, 2, 1, 3, 4)).astype(BF16)
    b = jnp.concatenate([lru_ba.reshape(DEPTH, 2, n_cb, RNN_CB),
                         lru_bx.reshape(DEPTH, 2, n_cb, RNN_CB)], axis=-1)
    return w, jnp.transpose(0.5 * b, (0, 2, 1, 3))


def _route(scores, sel):
    grp_score = []
    for g in range(N_GROUPS):
        a, b, c, d = sel[g * EXPERTS_PER_GROUP:(g + 1) * EXPERTS_PER_GROUP]
        hi1, lo1 = jnp.maximum(a, b), jnp.minimum(a, b)
        hi2, lo2 = jnp.maximum(c, d), jnp.minimum(c, d)
        grp_score.append(jnp.maximum(hi1, hi2) + jnp.maximum(jnp.minimum(hi1, hi2), jnp.maximum(lo1, lo2)))
    best = jnp.zeros_like(grp_score[0], dtype=jnp.int32)
    best_val = grp_score[0]
    for g in range(1, N_GROUPS):
        better = grp_score[g] > best_val
        best = jnp.where(better, g, best)
        best_val = jnp.where(better, grp_score[g], best_val)
    chosen = []
    for e in range(N_EXPERTS):
        g = e // EXPERTS_PER_GROUP
        rank = jnp.zeros_like(best)
        for o in range(g * EXPERTS_PER_GROUP, (g + 1) * EXPERTS_PER_GROUP):
            if o == e:
                continue
            ahead = (sel[o] >= sel[e]) if o < e else (sel[o] > sel[e])
            rank = rank + jnp.where(ahead, 1, 0)
        chosen.append(jnp.where(best == g, rank, 2) < 2)
    taken, gate = [], []
    for j in range(EXPERTS_PER_GROUP):
        t = jnp.zeros_like(best)
        s = jnp.zeros_like(scores[0])
        for g in range(N_GROUPS):
            e = g * EXPERTS_PER_GROUP + j
            t = t + jnp.where(chosen[e], 1, 0)
            s = s + jnp.where(chosen[e], scores[e], 0.0)
        taken.append(t > 0)
        gate.append(s)
    total = gate[0] + gate[1] + gate[2] + gate[3]
    pair = jnp.where(taken[0], jnp.where(taken[1], 0, jnp.where(taken[2], 1, 2)),
                     jnp.where(taken[1], jnp.where(taken[2], 3, 4), 5))
    w_lo = jnp.where(taken[0], gate[0], jnp.where(taken[1], gate[1], gate[2])) / total
    w_hi = jnp.where(taken[3], gate[3], jnp.where(taken[2], gate[2], gate[1])) / total
    return best * PAIRS_PER_GROUP + pair, w_lo, w_hi


def _merge_kernel(xl_ref, xc_ref, al_ref, ac_ref, rl_ref, rc_ref, gr_ref, ga_ref, gb_ref,
                  g1_ref, sh2_ref, sc2_ref, n2_ref, wap_ref, wrp_ref, wo_ref, rw_ref, rb_ref,
                  xo_ref, h2_ref, bucket_ref):
    att = jnp.dot(_pick(al_ref, ac_ref), wap_ref[...], preferred_element_type=F32)
    gated = (jax.nn.gelu(gr_ref[...].astype(F32)) * _pick(rl_ref, rc_ref).astype(F32)).astype(BF16)
    rnn = jnp.dot(gated, wrp_ref[...], preferred_element_type=F32)
    merged = _sigmoid(ga_ref[...].astype(F32)) * att + _sigmoid(gb_ref[...].astype(F32)) * rnn
    x = _pick(xl_ref, xc_ref) + g1_ref[...] * jnp.dot(merged.astype(BF16), wo_ref[...],
                                                      preferred_element_type=F32)
    xo_ref[...] = x
    y = x * lax.rsqrt(jnp.mean(x * x, axis=-1, keepdims=True) + EPS) * n2_ref[...]
    h2 = y * (1.0 + sc2_ref[...]) + sh2_ref[...]
    h2_ref[:, :D_MODEL] = h2
    h_hi = h2.astype(BF16)
    h_lo = (h2 - h_hi.astype(F32)).astype(BF16)
    rw = rw_ref[...]
    w_hi = rw.astype(BF16)
    w_lo = (rw - w_hi.astype(F32)).astype(BF16)
    logits = _qk(w_hi, h_hi) + (_qk(w_hi, h_lo) + _qk(w_lo, h_hi))
    score = _sigmoid(logits)
    sel = score + rb_ref[...]
    bucket, w_lo, w_hi = _route([score[e:e + 1, :] for e in range(N_EXPERTS)],
                                [sel[e:e + 1, :] for e in range(N_EXPERTS)])
    bucket_ref[...] = bucket
    pad = jnp.zeros((LANES - 2, TOK_TILE), F32)
    h2_ref[:, D_MODEL:] = jnp.concatenate([w_lo, w_hi, pad], axis=0).T


def _merge(x_lat, x_ctx, ctx_first_tile, att_lat, att_ctx, rnn_lat, rnn_ctx, gr, ga, gb, mods, norm_g,
           wap, wrp, wo, router_wt, router_b, layer):
    tok = lambda i: (i, 0)
    wide = pl.BlockSpec((TOK_TILE, D_MODEL), tok)
    mod = lambda col: pl.BlockSpec((None, 1, D_MODEL), lambda i: (layer * COND_ROWS + _mod_row(i), 0, col))
    mat = pl.BlockSpec((None, D_MODEL, D_MODEL), lambda i: (layer, 0, 0))
    return pl.pallas_call(
        _merge_kernel,
        grid=(N_TOK_TILES,),
        in_specs=_stream_specs(D_MODEL, ctx_first_tile) + _stream_specs(ATTN_W, 0) + _stream_specs(D_RNN, 0) + [
            wide, wide, wide,
            mod(2), mod(3), mod(4),
            pl.BlockSpec((None, 1, D_MODEL), lambda i: (layer, 0, 0)),
            mat, mat, mat,
            pl.BlockSpec((N_EXPERTS, D_MODEL), lambda i: (0, 0)),
            pl.BlockSpec((N_EXPERTS, 1), lambda i: (0, 0)),
        ],
        out_specs=[wide, pl.BlockSpec((TOK_TILE, ROW_W), tok),
                   pl.BlockSpec((None, 1, TOK_TILE), lambda i: (i, 0, 0))],
        out_shape=[
            jax.ShapeDtypeStruct((T_ALL, D_MODEL), F32),
            jax.ShapeDtypeStruct((T_ALL, ROW_W), F32),
            jax.ShapeDtypeStruct((N_TOK_TILES, 1, TOK_TILE), jnp.int32),
        ],
        compiler_params=_params("arbitrary"),
        name="merge",
    )(x_lat, x_ctx, att_lat, att_ctx, rnn_lat, rnn_ctx, gr, ga, gb, mods, mods, mods,
      norm_g.reshape(DEPTH, 1, D_MODEL), wap, wrp, wo, router_wt, router_b.reshape(N_EXPERTS, 1))


def _plan_kernel(b_ref, pos_ref, tiles_ref):
    b = b_ref[...]
    r = lax.broadcasted_iota(jnp.int32, (TOK_TILE, TOK_TILE), 0)
    c = lax.broadcasted_iota(jnp.int32, (TOK_TILE, TOK_TILE), 1)
    before_in_tile = jnp.where(r < c, 1.0, 0.0).astype(BF16)
    br = lax.broadcasted_iota(jnp.int32, (N_TOK_TILES, N_TOK_TILES), 0)
    bc = lax.broadcasted_iota(jnp.int32, (N_TOK_TILES, N_TOK_TILES), 1)
    earlier_tiles = jnp.where(bc < br, 1.0, 0.0).astype(BF16)
    lane = lax.broadcasted_iota(jnp.int32, (1, LANES), 1)
    tile_start = (lane * MOE_TM).astype(F32)
    start = jnp.zeros((1, 1), F32)
    pos = jnp.zeros(b.shape, F32)
    tile_bucket = jnp.zeros((1, LANES), F32)
    for k in range(N_BUCKETS):
        mask = jnp.where(b == k, 1.0, 0.0)
        rank = jnp.dot(mask.astype(BF16), before_in_tile, preferred_element_type=F32)
        per_tile = jnp.sum(mask, axis=1, keepdims=True)
        tile_off = jnp.dot(earlier_tiles, jnp.broadcast_to(per_tile, (N_TOK_TILES, LANES)).astype(BF16),
                           preferred_element_type=F32)[:, :1]
        total = jnp.sum(per_tile, axis=0, keepdims=True)
        padded = jnp.floor((total + (MOE_TM - 1)) * (1.0 / MOE_TM)) * MOE_TM
        pos = pos + mask * (start + tile_off + rank)
        start = start + padded
        tile_bucket = tile_bucket + jnp.where(tile_start >= start, 1.0, 0.0)
    pos_ref[...] = pos.astype(jnp.int32)
    tiles_ref[...] = jnp.where(lane == LANES - 1, start * (1.0 / MOE_TM), tile_bucket).astype(jnp.int32)


def _plan(bucket):
    return pl.pallas_call(
        _plan_kernel,
        out_shape=[
            jax.ShapeDtypeStruct((N_TOK_TILES, TOK_TILE), jnp.int32),
            jax.ShapeDtypeStruct((1, LANES), jnp.int32),
        ],
        compiler_params=pltpu.CompilerParams(vmem_limit_bytes=VMEM_LIMIT),
        name="plan",
    )(bucket.reshape(N_TOK_TILES, TOK_TILE))


def _row_copy(src, dst, sem):
    return pltpu.make_async_copy(src, dst, sem)


def _dispatch_kernel(pos_ref, h_ref, zero_ref, xs_ref, sem):
    del zero_ref

    def issue(pair, carry):
        for lane in range(2):
            r = 2 * pair + lane
            _row_copy(h_ref.at[pl.ds(r, 1), :], xs_ref.at[pl.ds(pos_ref[0, r], 1), :], sem).start(priority=lane)
        return carry

    lax.fori_loop(0, TOK_TILE // 2, issue, 0, unroll=4)

    def wait(r, carry):
        _row_copy(h_ref.at[pl.ds(0, 1), :], xs_ref.at[pl.ds(0, 1), :], sem).wait()
        return carry

    lax.fori_loop(0, TOK_TILE, wait, 0, unroll=8)


def _dispatch(pos, h2, zeros):
    return pl.pallas_call(
        _dispatch_kernel,
        grid=(N_TOK_TILES,),
        in_specs=[
            pl.BlockSpec((None, 1, TOK_TILE), lambda i: (i, 0, 0), memory_space=pltpu.SMEM),
            pl.BlockSpec((TOK_TILE, ROW_W), lambda i: (i, 0)),
            pl.BlockSpec(memory_space=pl.ANY),
        ],
        out_specs=pl.BlockSpec(memory_space=pl.ANY),
        out_shape=jax.ShapeDtypeStruct((MOE_ROWS, ROW_W), F32),
        input_output_aliases={2: 0},
        scratch_shapes=[pltpu.SemaphoreType.DMA(())],
        compiler_params=_params("arbitrary"),
        name="dispatch",
    )(pos.reshape(N_TOK_TILES, 1, TOK_TILE), h2, zeros)


def _tile_bucket(i, tiles_ref):
    return tiles_ref[0, jnp.minimum(i, tiles_ref[0, LANES - 1] - 1)]


def _bucket_expert(k, hi):
    g = k // PAIRS_PER_GROUP
    p = k % PAIRS_PER_GROUP
    lo_idx = jnp.where(p >= 3, 1, 0) + jnp.where(p >= 5, 1, 0)
    hi_idx = jnp.where(p < 3, p + 1, jnp.where(p < 5, p - 1, 3))
    return g * EXPERTS_PER_GROUP + (hi_idx if hi else lo_idx)


def _experts_kernel(tiles_ref, xs_ref, wgu_lo_ref, wgu_hi_ref, wd_lo_ref, wd_hi_ref, y_ref,
                    wgu_lo_b, wgu_hi_b, wd_lo_b, wd_hi_b):
    i = pl.program_id(0)
    in_use = i < tiles_ref[0, LANES - 1]

    @pl.when(jnp.logical_and(in_use, jnp.logical_or(i == 0, tiles_ref[0, i] != tiles_ref[0, jnp.maximum(i - 1, 0)])))
    def _():
        wgu_lo_b[...] = wgu_lo_ref[...].astype(BF16)
        wgu_hi_b[...] = wgu_hi_ref[...].astype(BF16)
        wd_lo_b[...] = wd_lo_ref[...].astype(BF16)
        wd_hi_b[...] = wd_hi_ref[...].astype(BF16)

    @pl.when(jnp.logical_not(in_use))
    def _():
        y_ref[...] = jnp.zeros_like(y_ref)

    @pl.when(in_use)
    def _():
        x = xs_ref[:, :D_MODEL].astype(BF16)

        def branch(wgu_ref, wd_ref, gate):
            gu = jnp.dot(x, wgu_ref[...], preferred_element_type=F32)
            act = jax.nn.silu(gu[:, :D_EXPERT]) * gu[:, D_EXPERT:] * gate
            return jnp.dot(act.astype(BF16), wd_ref[...], preferred_element_type=F32)

        y_ref[...] = (branch(wgu_lo_b, wd_lo_b, xs_ref[:, D_MODEL:D_MODEL + 1])
                      + branch(wgu_hi_b, wd_hi_b, xs_ref[:, D_MODEL + 1:D_MODEL + 2]))


def _experts(tiles, xs, wgu, wd, layer):
    def weight(shape, hi):
        return pl.BlockSpec((None, None) + shape,
                            lambda i, t: (layer, _bucket_expert(_tile_bucket(i, t), hi), 0, 0))

    return pl.pallas_call(
        _experts_kernel,
        grid_spec=pltpu.PrefetchScalarGridSpec(
            num_scalar_prefetch=1,
            grid=(N_MOE_TILES,),
            in_specs=[
                pl.BlockSpec((MOE_TM, ROW_W), lambda i, t: (i, 0)),
                weight((D_MODEL, 2 * D_EXPERT), False), weight((D_MODEL, 2 * D_EXPERT), True),
                weight((D_EXPERT, D_MODEL), False), weight((D_EXPERT, D_MODEL), True),
            ],
            out_specs=pl.BlockSpec((MOE_TM, D_MODEL), lambda i, t: (i, 0)),
            scratch_shapes=[
                pltpu.VMEM((D_MODEL, 2 * D_EXPERT), BF16), pltpu.VMEM((D_MODEL, 2 * D_EXPERT), BF16),
                pltpu.VMEM((D_EXPERT, D_MODEL), BF16), pltpu.VMEM((D_EXPERT, D_MODEL), BF16),
            ],
        ),
        out_shape=jax.ShapeDtypeStruct((MOE_ROWS, D_MODEL), F32),
        compiler_params=_params("arbitrary"),
        name="experts",
    )(tiles, xs, wgu, wgu, wd, wd)


def _combine_kernel(pos_ref, next_pos_ref, x_ref, g2_ref, y_ref, o_ref, rows_ref, sem):
    i = pl.program_id(0)
    slot = i % 2

    def gather(p_ref, into):
        def issue(pair, carry):
            for lane in range(2):
                r = 2 * pair + lane
                _row_copy(y_ref.at[pl.ds(p_ref[0, r], 1), :], rows_ref.at[into, pl.ds(r, 1), :],
                          sem.at[into]).start(priority=lane)
            return carry

        lax.fori_loop(0, TOK_TILE // 2, issue, 0, unroll=4)

    @pl.when(i == 0)
    def _():
        gather(pos_ref, 0)

    @pl.when(i + 1 < N_TOK_TILES)
    def _():
        gather(next_pos_ref, 1 - slot)

    def wait(r, carry):
        _row_copy(y_ref.at[pl.ds(0, 1), :], rows_ref.at[slot, pl.ds(0, 1), :], sem.at[slot]).wait()
        return carry

    lax.fori_loop(0, TOK_TILE, wait, 0, unroll=8)
    o_ref[...] = x_ref[...] + g2_ref[...] * rows_ref[slot]


def _combine(pos, x, mods, y, layer):
    pos = pos.reshape(N_TOK_TILES, 1, TOK_TILE)
    return pl.pallas_call(
        _combine_kernel,
        grid=(N_TOK_TILES,),
        in_specs=[
            pl.BlockSpec((None, 1, TOK_TILE), lambda i: (i, 0, 0), memory_space=pltpu.SMEM),
            pl.BlockSpec((None, 1, TOK_TILE), lambda i: (jnp.minimum(i + 1, N_TOK_TILES - 1), 0, 0),
                         memory_space=pltpu.SMEM),
            pl.BlockSpec((TOK_TILE, D_MODEL), lambda i: (i, 0)),
            pl.BlockSpec((None, 1, D_MODEL), lambda i: (layer * COND_ROWS + _mod_row(i), 0, 5)),
            pl.BlockSpec(memory_space=pl.ANY),
        ],
        out_specs=pl.BlockSpec((TOK_TILE, D_MODEL), lambda i: (i, 0)),
        out_shape=jax.ShapeDtypeStruct((T_ALL, D_MODEL), F32),
        scratch_shapes=[pltpu.VMEM((2, TOK_TILE, D_MODEL), F32), pltpu.SemaphoreType.DMA((2,))],
        compiler_params=_params("arbitrary"),
        name="combine",
    )(pos, pos, x, mods, y)


def _final_norm_kernel(x_ref, g_ref, o_ref):
    x = x_ref[...]
    o_ref[...] = x * lax.rsqrt(jnp.mean(x * x, axis=-1, keepdims=True) + EPS) * g_ref[...]


def _final_norm(x, g, first_tile, n_tiles):
    return pl.pallas_call(
        _final_norm_kernel,
        grid=(n_tiles,),
        in_specs=[
            pl.BlockSpec((TOK_TILE, D_MODEL), lambda i: (first_tile + i, 0)),
            pl.BlockSpec((1, D_MODEL), lambda i: (0, 0)),
        ],
        out_specs=pl.BlockSpec((TOK_TILE, D_MODEL), lambda i: (i, 0)),
        out_shape=jax.ShapeDtypeStruct((n_tiles * TOK_TILE, D_MODEL), F32),
        compiler_params=_params("arbitrary"),
        name="final_norm",
    )(x, g.reshape(1, D_MODEL))


def _rope_tables():
    t = jnp.arange(DEC_SEQ, dtype=jnp.int32)
    row = (t // GRID_W).astype(F32)
    col = (t % GRID_W).astype(F32)
    inv = ROPE_BASE ** (-jnp.arange(ROPE_PAIRS_PER_AXIS, dtype=F32) / ROPE_PAIRS_PER_AXIS)
    ang = jnp.concatenate([row[:, None] * inv, col[:, None] * inv], axis=-1)
    cos, sin = jnp.cos(ang), jnp.sin(ang)
    cos = jnp.concatenate([cos, cos], axis=-1)
    sin = jnp.concatenate([-sin, sin], axis=-1)
    cos = jnp.concatenate([cos, jnp.ones((TOK_TILE, HEAD_DIM), F32)], axis=0)
    sin = jnp.concatenate([sin, jnp.zeros((TOK_TILE, HEAD_DIM), F32)], axis=0)
    return cos, sin


def kernel(x_prompt, x_sample, cache_k, cache_v, state_h, c, c_ctx, norm1_g, norm2_g, w_ada, b_ada, w_in, conv_w, conv_b, lru_wa, lru_ba, lru_wx, lru_bx, lru_lambda, attn_sink, w_attn_proj, w_rnn_proj, w_out, router_w, router_b, w_gate_up, w_down, final_norm_g):
    x_lat, x_ctx, ctx_first_tile = x_sample.reshape(T_LAT, D_MODEL), x_prompt.reshape(T_CTX, D_MODEL), 0
    cond = jnp.concatenate([c_ctx[None, :], c, jnp.zeros((COND_ROWS - 1 - DEC_BATCH, D_MODEL), F32)], axis=0)
    mods = _ada(cond, w_ada, b_ada).reshape(DEPTH * COND_ROWS, N_ADA, D_MODEL)
    mods = mods.reshape(DEPTH * COND_ROWS, 1, N_ADA * D_MODEL)
    rope_cos, rope_sin = _rope_tables()

    w_in_b = w_in.astype(BF16)
    wap_b = w_attn_proj.astype(BF16)
    wrp_b = w_rnn_proj.astype(BF16)
    wo_b = w_out.astype(BF16)
    lru_w, lru_b = _lru_block_weights(lru_wa, lru_wx, lru_ba, lru_bx)
    router_wt = router_w.T

    h0_lat = jnp.transpose(state_h.astype(F32), (1, 2, 0, 3)).reshape(DEPTH, 1, 2, DEC_BATCH, D_RNN)
    h0_ctx = jnp.zeros((BATCH // SUBLANES, 2, SUBLANES, D_RNN), F32)
    ck = jnp.transpose(cache_k, (1, 0, 2, 3, 4)).reshape(DEPTH, DEC_BATCH, PAST_LEN, KV_W)
    cv = jnp.transpose(cache_v, (1, 0, 2, 3, 4)).reshape(DEPTH, DEC_BATCH, PAST_LEN, KV_W)

    new_k, new_v, new_h = [], [], []
    for l in range(DEPTH):
        q, k, v, xr, gr, ga, gb = _inproj(x_lat, x_ctx, ctx_first_tile, norm1_g, mods, rope_cos, rope_sin,
                                          w_in_b, l)
        att_lat = _lat_attn(q, k, v, ck[l], cv[l], attn_sink[l])
        att_ctx = _ctx_attn(q, k, v, attn_sink[l])
        y_lat, _ = _rnn(xr.reshape(T_ALL // DEC_SEQ, DEC_SEQ, D_RNN), conv_w, conv_b, lru_w, lru_b,
                        lru_lambda, h0_lat[l], l, DEC_SEQ, DEC_BATCH // SUBLANES, 0)
        y_ctx, h_ctx = _rnn(xr.reshape(T_ALL // SEQ, SEQ, D_RNN), conv_w, conv_b, lru_w, lru_b,
                            lru_lambda, h0_ctx, l, SEQ, BATCH // SUBLANES, T_LAT // SEQ // SUBLANES)
        x, h2, bucket = _merge(x_lat, x_ctx, ctx_first_tile, att_lat, att_ctx,
                               y_lat.reshape(T_LAT, D_RNN), y_ctx.reshape(T_CTX, D_RNN), gr, ga, gb,
                               mods, norm2_g, wap_b, wrp_b, wo_b, router_wt, router_b, l)
        pos, tiles = _plan(bucket)
        xs = _dispatch(pos, h2, jnp.zeros((MOE_ROWS, ROW_W), F32))
        x = _combine(pos, x, mods, _experts(tiles, xs, w_gate_up, w_down, l), l)
        x_lat, x_ctx, ctx_first_tile = x, x, N_LAT_TILES
        new_k.append(k[T_LAT:].reshape(BATCH, SEQ, N_KV_HEADS, HEAD_DIM))
        new_v.append(v[T_LAT:].reshape(BATCH, SEQ, N_KV_HEADS, HEAD_DIM))
        new_h.append(jnp.transpose(h_ctx, (0, 2, 1, 3)).reshape(BATCH, 2, D_RNN))

    y_sample = _final_norm(x, final_norm_g, 0, N_LAT_TILES).reshape(DEC_BATCH, DEC_SEQ, D_MODEL)
    y_prompt = _final_norm(x, final_norm_g, N_LAT_TILES, N_TOK_TILES - N_LAT_TILES).reshape(BATCH, SEQ, D_MODEL)
    return (y_prompt, y_sample, jnp.stack(new_k, axis=1), jnp.stack(new_v, axis=1),
            jnp.stack(new_h, axis=1).astype(x_prompt.dtype))
```

```python
import functools
from typing import NamedTuple

import jax
import jax.numpy as jnp
from jax import lax
from jax.experimental import pallas as pl
from jax.experimental.pallas import tpu as pltpu

F32 = jnp.float32
BF16 = jnp.bfloat16

D_MODEL = 1024
BATCH = 16
SEQ = 256
DEPTH = 2
DEC_BATCH = 8
DEC_SEQ = 1024
PAST_LEN = 512
GRID_W = 64
N_HEADS = 8
N_KV_HEADS = 2
GQA_GROUP = N_HEADS // N_KV_HEADS
HEAD_DIM = 128
ATTN_W = N_HEADS * HEAD_DIM
KV_W = N_KV_HEADS * HEAD_DIM
WINDOW = 128
Q_BLOCK = 128
ROPE_BASE = 10000.0
ROPE_PAIRS_PER_AXIS = HEAD_DIM // 4
D_RNN = 1024
LRU_BLOCKS = 16
LRU_BW = D_RNN // LRU_BLOCKS
LRU_C = 8.0
CONV_W = 4
CONV_LEFT = 2
N_EXPERTS = 16
N_GROUPS = 4
EXPERTS_PER_GROUP = N_EXPERTS // N_GROUPS
D_EXPERT = 512
N_ADA = 6
EPS = 1e-6
IN_COLS = ATTN_W + 2 * KV_W + 2 * D_RNN + 2 * D_MODEL

T_LAT = DEC_BATCH * DEC_SEQ
T_CTX = BATCH * SEQ
T_ALL = T_LAT + T_CTX

SUBLANES = 8
LANES = 128
VMEM_LIMIT = 56 * 1024 * 1024

TOK_TILE = 256
N_TOK_TILES = T_ALL // TOK_TILE
COND_ROWS = 16


class _Stream(NamedTuple):
    name: str
    batch: int
    seq: int
    first_tile: int
    rotary: bool

    @property
    def rows(self):
        return self.batch * self.seq

    @property
    def n_tiles(self):
        return self.rows // TOK_TILE

    def mod_row(self, i):
        return 1 + i // (self.seq // TOK_TILE) if self.rotary else 0


LATENT = _Stream("lat", DEC_BATCH, DEC_SEQ, 0, True)
CONTEXT = _Stream("ctx", BATCH, SEQ, T_LAT // TOK_TILE, False)

RNN_CB = 256
RNN_CHUNK = 64
RNN_UNROLL = 8
LOG2_E = 1.4426950408889634
PAIRS_PER_GROUP = EXPERTS_PER_GROUP * (EXPERTS_PER_GROUP - 1) // 2
N_BUCKETS = N_GROUPS * PAIRS_PER_GROUP
ROW_W = D_MODEL + LANES
MOE_TM = 256
N_MOE_TILES = (T_ALL + N_BUCKETS * (MOE_TM - 1) + MOE_TM - 1) // MOE_TM
MOE_ROWS = N_MOE_TILES * MOE_TM


def _params(*sem):
    return pltpu.CompilerParams(dimension_semantics=sem, vmem_limit_bytes=VMEM_LIMIT)


def _sigmoid(x):
    return 0.5 * jnp.tanh(0.5 * x) + 0.5


def _ada_kernel(cond_ref, w_ref, b_ref, o_ref):
    s = jax.nn.silu(cond_ref[...]).astype(BF16)
    o_ref[...] = jnp.dot(s, w_ref[...].astype(BF16), preferred_element_type=F32) + b_ref[...]


def _ada(cond, w_ada, b_ada):
    cols = N_ADA * D_MODEL
    tn = 1536
    return pl.pallas_call(
        _ada_kernel,
        grid=(DEPTH, cols // tn),
        in_specs=[
            pl.BlockSpec((COND_ROWS, D_MODEL), lambda l, j: (0, 0)),
            pl.BlockSpec((None, D_MODEL, tn), lambda l, j: (l, 0, j)),
            pl.BlockSpec((None, 1, tn), lambda l, j: (l, 0, j)),
        ],
        out_specs=pl.BlockSpec((None, COND_ROWS, tn), lambda l, j: (l, 0, j)),
        out_shape=jax.ShapeDtypeStruct((DEPTH, COND_ROWS, cols), F32),
        compiler_params=_params("arbitrary", "arbitrary"),
        name="ada",
    )(cond, w_ada, b_ada.reshape(DEPTH, 1, cols))


def _inproj_kernel(rotary, x_ref, g_ref, sh_ref, sc_ref, *refs):
    if rotary:
        cos_ref, sin_ref, w_ref, q_ref, k_ref, v_ref, xr_ref, gr_ref, ga_ref, gb_ref = refs
    else:
        w_ref, q_ref, k_ref, v_ref, xr_ref, gr_ref, ga_ref, gb_ref = refs
    x = x_ref[...]
    y = x * lax.rsqrt(jnp.mean(x * x, axis=-1, keepdims=True) + EPS) * g_ref[...]
    h = (y * (1.0 + sc_ref[...]) + sh_ref[...]).astype(BF16)

    def proj(lo, width):
        return jnp.dot(h, w_ref[:, lo:lo + width], preferred_element_type=F32)

    def rope(t):
        if not rotary:
            return t
        return t * cos_ref[...] + pltpu.roll(t, HEAD_DIM // 2, 1) * sin_ref[...]

    scale = HEAD_DIM ** -0.5
    for hd in range(N_HEADS):
        q = proj(hd * HEAD_DIM, HEAD_DIM)
        q_ref[:, hd * HEAD_DIM:(hd + 1) * HEAD_DIM] = (rope(q) * scale).astype(BF16)
    for g in range(N_KV_HEADS):
        k = proj(ATTN_W + g * HEAD_DIM, HEAD_DIM)
        k_ref[:, g * HEAD_DIM:(g + 1) * HEAD_DIM] = rope(k)
    v_ref[...] = proj(ATTN_W + KV_W, KV_W)
    base = ATTN_W + 2 * KV_W
    xr_ref[...] = proj(base, D_RNN).astype(BF16)
    gr_ref[...] = proj(base + D_RNN, D_RNN).astype(BF16)
    ga_ref[...] = proj(base + 2 * D_RNN, D_MODEL).astype(BF16)
    gb_ref[...] = proj(base + 2 * D_RNN + D_MODEL, D_MODEL).astype(BF16)


def _inproj(stream, x, norm_g, mods, rope_cos, rope_sin, w_in, layer):
    row = lambda i: layer * COND_ROWS + stream.mod_row(i)
    tok = lambda i: (i, 0)
    rope_blk = lambda i: (i % (stream.seq // TOK_TILE), 0)
    wide = pl.BlockSpec((TOK_TILE, D_MODEL), tok)
    kv = pl.BlockSpec((TOK_TILE, KV_W), tok)
    rope_specs = [pl.BlockSpec((TOK_TILE, HEAD_DIM), rope_blk)] * 2 if stream.rotary else []
    rope_args = (rope_cos, rope_sin) if stream.rotary else ()
    return pl.pallas_call(
        functools.partial(_inproj_kernel, stream.rotary),
        grid=(stream.n_tiles,),
        in_specs=[
            wide,
            pl.BlockSpec((None, 1, D_MODEL), lambda i: (layer, 0, 0)),
            pl.BlockSpec((None, 1, D_MODEL), lambda i: (row(i), 0, 0)),
            pl.BlockSpec((None, 1, D_MODEL), lambda i: (row(i), 0, 1)),
        ] + rope_specs + [
            pl.BlockSpec((None, D_MODEL, IN_COLS), lambda i: (layer, 0, 0)),
        ],
        out_specs=[wide, kv, kv, wide, wide, wide, wide],
        out_shape=[
            jax.ShapeDtypeStruct((stream.rows, ATTN_W), BF16),
            jax.ShapeDtypeStruct((stream.rows, KV_W), F32),
            jax.ShapeDtypeStruct((stream.rows, KV_W), F32),
            jax.ShapeDtypeStruct((stream.rows, D_RNN), BF16),
            jax.ShapeDtypeStruct((stream.rows, D_RNN), BF16),
            jax.ShapeDtypeStruct((stream.rows, D_MODEL), BF16),
            jax.ShapeDtypeStruct((stream.rows, D_MODEL), BF16),
        ],
        compiler_params=_params("arbitrary"),
        name=f"inproj_{stream.name}",
    )(x, norm_g.reshape(DEPTH, 1, D_MODEL), mods, mods, *rope_args, w_in)


def _stack_heads(q_ref, g):
    lo = g * GQA_GROUP
    return jnp.concatenate(
        [q_ref[:, (lo + r) * HEAD_DIM:(lo + r + 1) * HEAD_DIM] for r in range(GQA_GROUP)], axis=0)


def _sink_column(sink_ref, g, rows):
    return jnp.concatenate(
        [jnp.full((rows, 1), sink_ref[g * GQA_GROUP + r], F32) for r in range(GQA_GROUP)], axis=0)


def _qk(q, k):
    return lax.dot_general(q, k, (((1,), (1,)), ((), ())), preferred_element_type=F32)


def _ctx_attn_kernel(sink_ref, q_ref, k_ref, v_ref, o_ref):
    for g in range(N_KV_HEADS):
        q = _stack_heads(q_ref, g)
        k = k_ref[:, g * HEAD_DIM:(g + 1) * HEAD_DIM].astype(BF16)
        v = v_ref[:, g * HEAD_DIM:(g + 1) * HEAD_DIM].astype(BF16)
        sink = _sink_column(sink_ref, g, SEQ)
        s = _qk(q, k)
        m = jnp.maximum(jnp.max(s, axis=-1, keepdims=True), sink)
        p = jnp.exp(s - m)
        denom = jnp.sum(p, axis=-1, keepdims=True) + jnp.exp(sink - m)
        o = jnp.dot(p.astype(BF16), v, preferred_element_type=F32) / denom
        for r in range(GQA_GROUP):
            hd = g * GQA_GROUP + r
            o_ref[:, hd * HEAD_DIM:(hd + 1) * HEAD_DIM] = o[r * SEQ:(r + 1) * SEQ].astype(BF16)


def _ctx_attn(q, k, v, sink):
    blk = lambda b: (b, 0)
    return pl.pallas_call(
        _ctx_attn_kernel,
        grid=(BATCH,),
        in_specs=[
            pl.BlockSpec(memory_space=pltpu.SMEM),
            pl.BlockSpec((SEQ, ATTN_W), blk),
            pl.BlockSpec((SEQ, KV_W), blk),
            pl.BlockSpec((SEQ, KV_W), blk),
        ],
        out_specs=pl.BlockSpec((SEQ, ATTN_W), lambda b: (b, 0)),
        out_shape=jax.ShapeDtypeStruct((T_CTX, ATTN_W), BF16),
        compiler_params=_params("arbitrary"),
        name="ctx_attn",
    )(sink, q, k, v)


def _lat_attn_kernel(sink_ref, q_ref, kp_ref, kc_ref, kn_ref, vp_ref, vc_ref, vn_ref,
                     ck_ref, cv_ref, o_ref):
    j = pl.program_id(1)
    rows = GQA_GROUP * Q_BLOCK
    band = Q_BLOCK + 2 * WINDOW
    qpos = j * Q_BLOCK + lax.broadcasted_iota(jnp.int32, (rows, band), 0) % Q_BLOCK
    kpos = j * Q_BLOCK - WINDOW + lax.broadcasted_iota(jnp.int32, (rows, band), 1)
    diff = qpos - kpos
    valid = (kpos >= 0) & (kpos < DEC_SEQ) & (diff <= WINDOW) & (diff >= -WINDOW)
    for g in range(N_KV_HEADS):
        cols = slice(g * HEAD_DIM, (g + 1) * HEAD_DIM)
        q = _stack_heads(q_ref, g)
        k_loc = jnp.concatenate([kp_ref[:, cols], kc_ref[:, cols], kn_ref[:, cols]], axis=0).astype(BF16)
        v_loc = jnp.concatenate([vp_ref[:, cols], vc_ref[:, cols], vn_ref[:, cols]], axis=0).astype(BF16)
        k_ctx = ck_ref[:, cols].astype(BF16)
        v_ctx = cv_ref[:, cols].astype(BF16)
        sink = _sink_column(sink_ref, g, Q_BLOCK)
        s_loc = jnp.where(valid, _qk(q, k_loc), -jnp.inf)
        s_ctx = _qk(q, k_ctx)
        m = jnp.maximum(jnp.maximum(jnp.max(s_loc, axis=-1, keepdims=True),
                                    jnp.max(s_ctx, axis=-1, keepdims=True)), sink)
        p_loc = jnp.exp(s_loc - m)
        p_ctx = jnp.exp(s_ctx - m)
        denom = (jnp.sum(p_loc, axis=-1, keepdims=True) + jnp.sum(p_ctx, axis=-1, keepdims=True)
                 + jnp.exp(sink - m))
        o = (jnp.dot(p_loc.astype(BF16), v_loc, preferred_element_type=F32)
             + jnp.dot(p_ctx.astype(BF16), v_ctx, preferred_element_type=F32)) / denom
        for r in range(GQA_GROUP):
            hd = g * GQA_GROUP + r
            o_ref[:, hd * HEAD_DIM:(hd + 1) * HEAD_DIM] = o[r * Q_BLOCK:(r + 1) * Q_BLOCK].astype(BF16)


def _lat_attn(q, k, v, cache_k, cache_v, sink):
    nb = DEC_SEQ // Q_BLOCK
    cur = lambda b, j: (b * nb + j, 0)
    prev = lambda b, j: (b * nb + jnp.maximum(j - 1, 0), 0)
    nxt = lambda b, j: (b * nb + jnp.minimum(j + 1, nb - 1), 0)
    kvb = lambda im: pl.BlockSpec((Q_BLOCK, KV_W), im)
    cache = pl.BlockSpec((None, PAST_LEN, KV_W), lambda b, j: (b, 0, 0))
    return pl.pallas_call(
        _lat_attn_kernel,
        grid=(DEC_BATCH, nb),
        in_specs=[
            pl.BlockSpec(memory_space=pltpu.SMEM),
            pl.BlockSpec((Q_BLOCK, ATTN_W), cur),
            kvb(prev), kvb(cur), kvb(nxt), kvb(prev), kvb(cur), kvb(nxt),
            cache, cache,
        ],
        out_specs=pl.BlockSpec((Q_BLOCK, ATTN_W), cur),
        out_shape=jax.ShapeDtypeStruct((T_LAT, ATTN_W), BF16),
        compiler_params=_params("arbitrary", "arbitrary"),
        name="lat_attn",
    )(sink, q, k, k, k, v, v, v, cache_k, cache_v)


def _rnn_kernel(seq, xr_ref, cw_ref, cb_ref, w_ref, b_ref, lam_ref, h0_ref,
                y_ref, hl_ref, xt_ref, yt_ref, a_ref, u_ref):
    n_chunks = seq // RNN_CHUNK
    rows = RNN_CHUNK * SUBLANES
    halo = jnp.zeros((CONV_LEFT, SUBLANES, RNN_CB), F32)
    xt_ref[0:CONV_LEFT] = halo
    xt_ref[seq + CONV_LEFT:seq + 2 * CONV_LEFT] = halo

    def load_chunk(c, carry):
        t0 = pl.multiple_of(c * RNN_CHUNK, RNN_CHUNK)
        x = xr_ref[:, pl.ds(t0, RNN_CHUNK), :].astype(F32)
        xt_ref[pl.ds(t0 + CONV_LEFT, RNN_CHUNK)] = jnp.swapaxes(x, 0, 1)
        return carry

    lax.fori_loop(0, n_chunks, load_chunk, 0)

    def conv_chunk(t0):
        acc = cb_ref[...].reshape(1, 1, RNN_CB)
        for tap in range(CONV_W):
            acc = acc + xt_ref[pl.ds(t0 + tap, RNN_CHUNK)] * cw_ref[tap:tap + 1, :].reshape(1, 1, RNN_CB)
        return acc

    for d in range(2):
        decay = (-0.5 * LRU_C * LOG2_E) * jax.nn.softplus(-lam_ref[d:d + 1, :])
        half_bias = b_ref[d:d + 1, :]

        def chunk(ci, h, d=d, decay=decay, half_bias=half_bias):
            c = ci if d == 0 else n_chunks - 1 - ci
            t0 = pl.multiple_of(c * RNN_CHUNK, RNN_CHUNK)
            xc = conv_chunk(t0).reshape(rows, RNN_CB)
            z = jnp.dot(xc.astype(BF16), w_ref[d], preferred_element_type=F32) + half_bias
            a = jnp.exp2(decay * jnp.tanh(z[:, :RNN_CB]) + decay)
            m = 1.0 - a * a
            mult = jnp.where(m == 0.0, 0.0, m * lax.rsqrt(m))
            u = mult * ((jnp.tanh(z[:, RNN_CB:]) + 1.0) * (0.5 * xc))
            a_ref[...] = a.reshape(RNN_CHUNK, SUBLANES, RNN_CB)
            u_ref[...] = u.reshape(RNN_CHUNK, SUBLANES, RNN_CB)

            def steps(gi, h):
                s0 = pl.multiple_of((gi if d == 0 else RNN_CHUNK // RNN_UNROLL - 1 - gi) * RNN_UNROLL, RNN_UNROLL)
                for j in (range(RNN_UNROLL) if d == 0 else reversed(range(RNN_UNROLL))):
                    h = a_ref[s0 + j] * h + u_ref[s0 + j]
                    if d == 0:
                        yt_ref[t0 + s0 + j] = h
                    else:
                        yt_ref[t0 + s0 + j] = yt_ref[t0 + s0 + j] + h
                return h

            return lax.fori_loop(0, RNN_CHUNK // RNN_UNROLL, steps, h)

        hl_ref[d] = lax.fori_loop(0, n_chunks, chunk, h0_ref[d])

    def store_chunk(c, carry):
        t0 = pl.multiple_of(c * RNN_CHUNK, RNN_CHUNK)
        y = jnp.swapaxes(yt_ref[pl.ds(t0, RNN_CHUNK)], 0, 1)
        y_ref[:, pl.ds(t0, RNN_CHUNK), :] = y.astype(BF16)
        return carry

    lax.fori_loop(0, n_chunks, store_chunk, 0)


def _rnn(stream, xr, conv_w, conv_b, w_blk, b_blk, lam, h0, layer):
    n_cb = D_RNN // RNN_CB
    seq = stream.seq
    n_groups = stream.batch // SUBLANES
    return pl.pallas_call(
        functools.partial(_rnn_kernel, seq),
        grid=(n_groups, n_cb),
        in_specs=[
            pl.BlockSpec((SUBLANES, seq, RNN_CB), lambda g, j: (g, 0, j)),
            pl.BlockSpec((None, CONV_W, RNN_CB), lambda g, j: (layer, 0, j)),
            pl.BlockSpec((None, 1, RNN_CB), lambda g, j: (layer, 0, j)),
            pl.BlockSpec((None, None, 2, RNN_CB, 2 * RNN_CB), lambda g, j: (layer, j, 0, 0, 0)),
            pl.BlockSpec((None, None, 2, 2 * RNN_CB), lambda g, j: (layer, j, 0, 0)),
            pl.BlockSpec((None, 2, RNN_CB), lambda g, j: (layer, 0, j)),
            pl.BlockSpec((None, 2, SUBLANES, RNN_CB), lambda g, j: (g, 0, 0, j)),
        ],
        out_specs=[
            pl.BlockSpec((SUBLANES, seq, RNN_CB), lambda g, j: (g, 0, j)),
            pl.BlockSpec((None, 2, SUBLANES, RNN_CB), lambda g, j: (g, 0, 0, j)),
        ],
        out_shape=[
            jax.ShapeDtypeStruct((n_groups * SUBLANES, seq, D_RNN), BF16),
            jax.ShapeDtypeStruct((n_groups, 2, SUBLANES, D_RNN), F32),
        ],
        scratch_shapes=[
            pltpu.VMEM((seq + 2 * CONV_LEFT, SUBLANES, RNN_CB), F32),
            pltpu.VMEM((seq, SUBLANES, RNN_CB), F32),
            pltpu.VMEM((RNN_CHUNK, SUBLANES, RNN_CB), F32),
            pltpu.VMEM((RNN_CHUNK, SUBLANES, RNN_CB), F32),
        ],
        compiler_params=_params("arbitrary", "arbitrary"),
        name=f"rnn_{stream.name}",
    )(xr.reshape(stream.batch, seq, D_RNN), conv_w, conv_b.reshape(DEPTH, 1, D_RNN), w_blk, b_blk, lam, h0)


def _lru_block_weights(lru_wa, lru_wx, lru_ba, lru_bx):
    n_cb = D_RNN // RNN_CB
    per = RNN_CB // LRU_BW

    def dense(w):
        w = w.reshape(DEPTH, 2, n_cb, per, LRU_BW, LRU_BW)
        eye = jnp.eye(per, dtype=w.dtype)
        full = jnp.einsum("ldcpkj,pq->ldcpkqj", w, eye)
        return full.reshape(DEPTH, 2, n_cb, RNN_CB, RNN_CB)

    w = jnp.concatenate([dense(lru_wa), dense(lru_wx)], axis=-1)
    w = jnp.transpose(0.5 * w, (0, 2, 1, 3, 4)).astype(BF16)
    b = jnp.concatenate([lru_ba.reshape(DEPTH, 2, n_cb, RNN_CB),
                         lru_bx.reshape(DEPTH, 2, n_cb, RNN_CB)], axis=-1)
    return w, jnp.transpose(0.5 * b, (0, 2, 1, 3))


def _route(scores, sel):
    grp_score = []
    for g in range(N_GROUPS):
        a, b, c, d = sel[g * EXPERTS_PER_GROUP:(g + 1) * EXPERTS_PER_GROUP]
        hi1, lo1 = jnp.maximum(a, b), jnp.minimum(a, b)
        hi2, lo2 = jnp.maximum(c, d), jnp.minimum(c, d)
        grp_score.append(jnp.maximum(hi1, hi2) + jnp.maximum(jnp.minimum(hi1, hi2), jnp.maximum(lo1, lo2)))
    best = jnp.zeros_like(grp_score[0], dtype=jnp.int32)
    best_val = grp_score[0]
    for g in range(1, N_GROUPS):
        better = grp_score[g] > best_val
        best = jnp.where(better, g, best)
        best_val = jnp.where(better, grp_score[g], best_val)
    chosen = []
    for e in range(N_EXPERTS):
        g = e // EXPERTS_PER_GROUP
        rank = jnp.zeros_like(best)
        for o in range(g * EXPERTS_PER_GROUP, (g + 1) * EXPERTS_PER_GROUP):
            if o == e:
                continue
            ahead = (sel[o] >= sel[e]) if o < e else (sel[o] > sel[e])
            rank = rank + jnp.where(ahead, 1, 0)
        chosen.append(jnp.where(best == g, rank, 2) < 2)
    taken, gate = [], []
    for j in range(EXPERTS_PER_GROUP):
        t = jnp.zeros_like(best)
        s = jnp.zeros_like(scores[0])
        for g in range(N_GROUPS):
            e = g * EXPERTS_PER_GROUP + j
            t = t + jnp.where(chosen[e], 1, 0)
            s = s + jnp.where(chosen[e], scores[e], 0.0)
        taken.append(t > 0)
        gate.append(s)
    total = gate[0] + gate[1] + gate[2] + gate[3]
    pair = jnp.where(taken[0], jnp.where(taken[1], 0, jnp.where(taken[2], 1, 2)),
                     jnp.where(taken[1], jnp.where(taken[2], 3, 4), 5))
    w_lo = jnp.where(taken[0], gate[0], jnp.where(taken[1], gate[1], gate[2])) / total
    w_hi = jnp.where(taken[3], gate[3], jnp.where(taken[2], gate[2], gate[1])) / total
    return best * PAIRS_PER_GROUP + pair, w_lo, w_hi


def _merge_kernel(x_ref, att_ref, rnn_ref, gr_ref, ga_ref, gb_ref,
                  g1_ref, sh2_ref, sc2_ref, n2_ref, wap_ref, wrp_ref, wo_ref, rw_ref, rb_ref,
                  xo_ref, h2_ref, bucket_ref):
    att = jnp.dot(att_ref[...], wap_ref[...], preferred_element_type=F32)
    gated = (jax.nn.gelu(gr_ref[...].astype(F32)) * rnn_ref[...].astype(F32)).astype(BF16)
    rnn = jnp.dot(gated, wrp_ref[...], preferred_element_type=F32)
    merged = _sigmoid(ga_ref[...].astype(F32)) * att + _sigmoid(gb_ref[...].astype(F32)) * rnn
    x = x_ref[...] + g1_ref[...] * jnp.dot(merged.astype(BF16), wo_ref[...], preferred_element_type=F32)
    xo_ref[...] = x
    y = x * lax.rsqrt(jnp.mean(x * x, axis=-1, keepdims=True) + EPS) * n2_ref[...]
    h2 = y * (1.0 + sc2_ref[...]) + sh2_ref[...]
    h2_ref[:, :D_MODEL] = h2
    h_hi = h2.astype(BF16)
    h_lo = (h2 - h_hi.astype(F32)).astype(BF16)
    rw = rw_ref[...]
    w_hi = rw.astype(BF16)
    w_lo = (rw - w_hi.astype(F32)).astype(BF16)
    logits = _qk(w_hi, h_hi) + (_qk(w_hi, h_lo) + _qk(w_lo, h_hi))
    score = _sigmoid(logits)
    sel = score + rb_ref[...]
    bucket, w_lo, w_hi = _route([score[e:e + 1, :] for e in range(N_EXPERTS)],
                                [sel[e:e + 1, :] for e in range(N_EXPERTS)])
    bucket_ref[...] = bucket
    pad = jnp.zeros((LANES - 2, TOK_TILE), F32)
    h2_ref[:, D_MODEL:] = jnp.concatenate([w_lo, w_hi, pad], axis=0).T


def _merge(stream, x, att, rnn, gr, ga, gb, mods, norm_g, wap, wrp, wo, router_wt, router_b, layer):
    tok = lambda i: (i, 0)
    wide = pl.BlockSpec((TOK_TILE, D_MODEL), tok)
    mod = lambda col: pl.BlockSpec((None, 1, D_MODEL),
                                   lambda i: (layer * COND_ROWS + stream.mod_row(i), 0, col))
    mat = pl.BlockSpec((None, D_MODEL, D_MODEL), lambda i: (layer, 0, 0))
    return pl.pallas_call(
        _merge_kernel,
        grid=(stream.n_tiles,),
        in_specs=[
            wide, wide, wide, wide, wide, wide,
            mod(2), mod(3), mod(4),
            pl.BlockSpec((None, 1, D_MODEL), lambda i: (layer, 0, 0)),
            mat, mat, mat,
            pl.BlockSpec((N_EXPERTS, D_MODEL), lambda i: (0, 0)),
            pl.BlockSpec((N_EXPERTS, 1), lambda i: (0, 0)),
        ],
        out_specs=[wide, pl.BlockSpec((TOK_TILE, ROW_W), tok),
                   pl.BlockSpec((None, 1, TOK_TILE), lambda i: (i, 0, 0))],
        out_shape=[
            jax.ShapeDtypeStruct((stream.rows, D_MODEL), F32),
            jax.ShapeDtypeStruct((stream.rows, ROW_W), F32),
            jax.ShapeDtypeStruct((stream.n_tiles, 1, TOK_TILE), jnp.int32),
        ],
        compiler_params=_params("arbitrary"),
        name=f"merge_{stream.name}",
    )(x, att, rnn.reshape(stream.rows, D_RNN), gr, ga, gb, mods, mods, mods,
      norm_g.reshape(DEPTH, 1, D_MODEL), wap, wrp, wo, router_wt, router_b.reshape(N_EXPERTS, 1))


def _plan_kernel(b_ref, pos_ref, tiles_ref):
    b = b_ref[...]
    r = lax.broadcasted_iota(jnp.int32, (TOK_TILE, TOK_TILE), 0)
    c = lax.broadcasted_iota(jnp.int32, (TOK_TILE, TOK_TILE), 1)
    before_in_tile = jnp.where(r < c, 1.0, 0.0).astype(BF16)
    br = lax.broadcasted_iota(jnp.int32, (N_TOK_TILES, N_TOK_TILES), 0)
    bc = lax.broadcasted_iota(jnp.int32, (N_TOK_TILES, N_TOK_TILES), 1)
    earlier_tiles = jnp.where(bc < br, 1.0, 0.0).astype(BF16)
    lane = lax.broadcasted_iota(jnp.int32, (1, LANES), 1)
    tile_start = (lane * MOE_TM).astype(F32)
    start = jnp.zeros((1, 1), F32)
    pos = jnp.zeros(b.shape, F32)
    tile_bucket = jnp.zeros((1, LANES), F32)
    for k in range(N_BUCKETS):
        mask = jnp.where(b == k, 1.0, 0.0)
        rank = jnp.dot(mask.astype(BF16), before_in_tile, preferred_element_type=F32)
        per_tile = jnp.sum(mask, axis=1, keepdims=True)
        tile_off = jnp.dot(earlier_tiles, jnp.broadcast_to(per_tile, (N_TOK_TILES, LANES)).astype(BF16),
                           preferred_element_type=F32)[:, :1]
        total = jnp.sum(per_tile, axis=0, keepdims=True)
        padded = jnp.floor((total + (MOE_TM - 1)) * (1.0 / MOE_TM)) * MOE_TM
        pos = pos + mask * (start + tile_off + rank)
        start = start + padded
        tile_bucket = tile_bucket + jnp.where(tile_start >= start, 1.0, 0.0)
    pos_ref[...] = pos.astype(jnp.int32)
    tiles_ref[...] = jnp.where(lane == LANES - 1, start * (1.0 / MOE_TM), tile_bucket).astype(jnp.int32)


def _plan(bucket_lat, bucket_ctx):
    bucket = jnp.concatenate([bucket_lat, bucket_ctx], axis=0)
    return pl.pallas_call(
        _plan_kernel,
        out_shape=[
            jax.ShapeDtypeStruct((N_TOK_TILES, TOK_TILE), jnp.int32),
            jax.ShapeDtypeStruct((1, LANES), jnp.int32),
        ],
        compiler_params=pltpu.CompilerParams(vmem_limit_bytes=VMEM_LIMIT),
        name="plan",
    )(bucket.reshape(N_TOK_TILES, TOK_TILE))


def _row_copy(src, dst, sem):
    return pltpu.make_async_copy(src, dst, sem)


def _dispatch_kernel(pos_ref, h_ref, xs_in_ref, xs_ref, sem):
    del xs_in_ref

    for r in range(TOK_TILE):
        _row_copy(h_ref.at[pl.ds(r, 1), :], xs_ref.at[pl.ds(pos_ref[0, r], 1), :], sem).start(priority=r % 2)

    def wait(r, carry):
        _row_copy(h_ref.at[pl.ds(0, 1), :], xs_ref.at[pl.ds(0, 1), :], sem).wait()
        return carry

    lax.fori_loop(0, TOK_TILE, wait, 0, unroll=8)


def _dispatch(stream, pos, h2, xs):
    return pl.pallas_call(
        _dispatch_kernel,
        grid=(stream.n_tiles,),
        in_specs=[
            pl.BlockSpec((None, 1, TOK_TILE), lambda i: (stream.first_tile + i, 0, 0), memory_space=pltpu.SMEM),
            pl.BlockSpec((TOK_TILE, ROW_W), lambda i: (i, 0)),
            pl.BlockSpec(memory_space=pl.ANY),
        ],
        out_specs=pl.BlockSpec(memory_space=pl.ANY),
        out_shape=jax.ShapeDtypeStruct((MOE_ROWS, ROW_W), F32),
        input_output_aliases={2: 0},
        scratch_shapes=[pltpu.SemaphoreType.DMA(())],
        compiler_params=_params("arbitrary"),
        name=f"dispatch_{stream.name}",
    )(pos, h2, xs)


def _tile_bucket(i, tiles_ref):
    return tiles_ref[0, jnp.minimum(i, tiles_ref[0, LANES - 1] - 1)]


def _bucket_expert(k, hi):
    g = k // PAIRS_PER_GROUP
    p = k % PAIRS_PER_GROUP
    lo_idx = jnp.where(p >= 3, 1, 0) + jnp.where(p >= 5, 1, 0)
    hi_idx = jnp.where(p < 3, p + 1, jnp.where(p < 5, p - 1, 3))
    return g * EXPERTS_PER_GROUP + (hi_idx if hi else lo_idx)


def _experts_kernel(tiles_ref, xs_ref, wgu_lo_ref, wgu_hi_ref, wd_lo_ref, wd_hi_ref, y_ref,
                    wgu_lo_b, wgu_hi_b, wd_lo_b, wd_hi_b):
    i = pl.program_id(0)
    in_use = i < tiles_ref[0, LANES - 1]

    @pl.when(jnp.logical_and(in_use, jnp.logical_or(i == 0, tiles_ref[0, i] != tiles_ref[0, jnp.maximum(i - 1, 0)])))
    def _():
        wgu_lo_b[...] = wgu_lo_ref[...].astype(BF16)
        wgu_hi_b[...] = wgu_hi_ref[...].astype(BF16)
        wd_lo_b[...] = wd_lo_ref[...].astype(BF16)
        wd_hi_b[...] = wd_hi_ref[...].astype(BF16)

    @pl.when(jnp.logical_not(in_use))
    def _():
        y_ref[...] = jnp.zeros_like(y_ref)

    @pl.when(in_use)
    def _():
        x = xs_ref[:, :D_MODEL].astype(BF16)

        def branch(wgu_ref, wd_ref, gate):
            gu = jnp.dot(x, wgu_ref[...], preferred_element_type=F32)
            act = jax.nn.silu(gu[:, :D_EXPERT]) * gu[:, D_EXPERT:] * gate
            return jnp.dot(act.astype(BF16), wd_ref[...], preferred_element_type=F32)

        y_ref[...] = (branch(wgu_lo_b, wd_lo_b, xs_ref[:, D_MODEL:D_MODEL + 1])
                      + branch(wgu_hi_b, wd_hi_b, xs_ref[:, D_MODEL + 1:D_MODEL + 2]))


def _experts(tiles, xs, wgu, wd, layer):
    def weight(shape, hi):
        return pl.BlockSpec((None, None) + shape,
                            lambda i, t: (layer, _bucket_expert(_tile_bucket(i, t), hi), 0, 0))

    return pl.pallas_call(
        _experts_kernel,
        grid_spec=pltpu.PrefetchScalarGridSpec(
            num_scalar_prefetch=1,
            grid=(N_MOE_TILES,),
            in_specs=[
                pl.BlockSpec((MOE_TM, ROW_W), lambda i, t: (i, 0)),
                weight((D_MODEL, 2 * D_EXPERT), False), weight((D_MODEL, 2 * D_EXPERT), True),
                weight((D_EXPERT, D_MODEL), False), weight((D_EXPERT, D_MODEL), True),
            ],
            out_specs=pl.BlockSpec((MOE_TM, D_MODEL), lambda i, t: (i, 0)),
            scratch_shapes=[
                pltpu.VMEM((D_MODEL, 2 * D_EXPERT), BF16), pltpu.VMEM((D_MODEL, 2 * D_EXPERT), BF16),
                pltpu.VMEM((D_EXPERT, D_MODEL), BF16), pltpu.VMEM((D_EXPERT, D_MODEL), BF16),
            ],
        ),
        out_shape=jax.ShapeDtypeStruct((MOE_ROWS, D_MODEL), F32),
        compiler_params=_params("arbitrary"),
        name="experts",
    )(tiles, xs, wgu, wgu, wd, wd)


def _combine_kernel(n_tiles, pos_ref, next_pos_ref, x_ref, g2_ref, y_ref, o_ref, rows_ref, sem):
    i = pl.program_id(0)
    slot = i % 2

    def gather(p_ref, into):
        for r in range(TOK_TILE):
            _row_copy(y_ref.at[pl.ds(p_ref[0, r], 1), :], rows_ref.at[into, pl.ds(r, 1), :],
                      sem.at[into]).start(priority=r % 2)

    @pl.when(i == 0)
    def _():
        gather(pos_ref, 0)

    @pl.when(i + 1 < n_tiles)
    def _():
        gather(next_pos_ref, 1 - slot)

    def wait(r, carry):
        _row_copy(y_ref.at[pl.ds(0, 1), :], rows_ref.at[slot, pl.ds(0, 1), :], sem.at[slot]).wait()
        return carry

    lax.fori_loop(0, TOK_TILE, wait, 0, unroll=8)
    o_ref[...] = x_ref[...] + g2_ref[...] * rows_ref[slot]


def _combine(stream, pos, x, mods, y, layer):
    first, last = stream.first_tile, stream.first_tile + stream.n_tiles - 1
    return pl.pallas_call(
        functools.partial(_combine_kernel, stream.n_tiles),
        grid=(stream.n_tiles,),
        in_specs=[
            pl.BlockSpec((None, 1, TOK_TILE), lambda i: (first + i, 0, 0), memory_space=pltpu.SMEM),
            pl.BlockSpec((None, 1, TOK_TILE), lambda i: (jnp.minimum(first + i + 1, last), 0, 0),
                         memory_space=pltpu.SMEM),
            pl.BlockSpec((TOK_TILE, D_MODEL), lambda i: (i, 0)),
            pl.BlockSpec((None, 1, D_MODEL), lambda i: (layer * COND_ROWS + stream.mod_row(i), 0, 5)),
            pl.BlockSpec(memory_space=pl.ANY),
        ],
        out_specs=pl.BlockSpec((TOK_TILE, D_MODEL), lambda i: (i, 0)),
        out_shape=jax.ShapeDtypeStruct((stream.rows, D_MODEL), F32),
        scratch_shapes=[pltpu.VMEM((2, TOK_TILE, D_MODEL), F32), pltpu.SemaphoreType.DMA((2,))],
        compiler_params=_params("arbitrary"),
        name=f"combine_{stream.name}",
    )(pos, pos, x, mods, y)


def _final_norm_kernel(x_ref, g_ref, o_ref):
    x = x_ref[...]
    o_ref[...] = x * lax.rsqrt(jnp.mean(x * x, axis=-1, keepdims=True) + EPS) * g_ref[...]


def _final_norm(stream, x, g):
    return pl.pallas_call(
        _final_norm_kernel,
        grid=(stream.n_tiles,),
        in_specs=[
            pl.BlockSpec((TOK_TILE, D_MODEL), lambda i: (i, 0)),
            pl.BlockSpec((1, D_MODEL), lambda i: (0, 0)),
        ],
        out_specs=pl.BlockSpec((TOK_TILE, D_MODEL), lambda i: (i, 0)),
        out_shape=jax.ShapeDtypeStruct((stream.rows, D_MODEL), F32),
        compiler_params=_params("arbitrary"),
        name=f"final_norm_{stream.name}",
    )(x, g.reshape(1, D_MODEL)).reshape(stream.batch, stream.seq, D_MODEL)


def _rope_tables():
    t = jnp.arange(DEC_SEQ, dtype=jnp.int32)
    row = (t // GRID_W).astype(F32)
    col = (t % GRID_W).astype(F32)
    inv = ROPE_BASE ** (-jnp.arange(ROPE_PAIRS_PER_AXIS, dtype=F32) / ROPE_PAIRS_PER_AXIS)
    ang = jnp.concatenate([row[:, None] * inv, col[:, None] * inv], axis=-1)
    cos, sin = jnp.cos(ang), jnp.sin(ang)
    return jnp.concatenate([cos, cos], axis=-1), jnp.concatenate([-sin, sin], axis=-1)


def kernel(x_prompt, x_sample, cache_k, cache_v, state_h, c, c_ctx, norm1_g, norm2_g, w_ada, b_ada, w_in, conv_w, conv_b, lru_wa, lru_ba, lru_wx, lru_bx, lru_lambda, attn_sink, w_attn_proj, w_rnn_proj, w_out, router_w, router_b, w_gate_up, w_down, final_norm_g):
    xs_tok = {LATENT: x_sample.reshape(T_LAT, D_MODEL), CONTEXT: x_prompt.reshape(T_CTX, D_MODEL)}
    cond = jnp.concatenate([c_ctx[None, :], c, jnp.zeros((COND_ROWS - 1 - DEC_BATCH, D_MODEL), F32)], axis=0)
    mods = _ada(cond, w_ada, b_ada).reshape(DEPTH * COND_ROWS, 1, N_ADA * D_MODEL)
    rope_cos, rope_sin = _rope_tables()

    w_in_b = w_in.astype(BF16)
    wap_b = w_attn_proj.astype(BF16)
    wrp_b = w_rnn_proj.astype(BF16)
    wo_b = w_out.astype(BF16)
    lru_w, lru_b = _lru_block_weights(lru_wa, lru_wx, lru_ba, lru_bx)
    router_wt = router_w.T

    h0 = {LATENT: jnp.transpose(state_h.astype(F32), (1, 2, 0, 3)).reshape(DEPTH, 1, 2, DEC_BATCH, D_RNN),
          CONTEXT: jnp.zeros((DEPTH, BATCH // SUBLANES, 2, SUBLANES, D_RNN), F32)}
    ck = jnp.transpose(cache_k, (1, 0, 2, 3, 4)).reshape(DEPTH, DEC_BATCH, PAST_LEN, KV_W)
    cv = jnp.transpose(cache_v, (1, 0, 2, 3, 4)).reshape(DEPTH, DEC_BATCH, PAST_LEN, KV_W)

    new_k, new_v, new_h = [], [], []
    for l in range(DEPTH):
        merged = {}
        for stream in (LATENT, CONTEXT):
            q, k, v, xr, gr, ga, gb = _inproj(stream, xs_tok[stream], norm1_g, mods, rope_cos, rope_sin,
                                              w_in_b, l)
            if stream.rotary:
                att = _lat_attn(q, k, v, ck[l], cv[l], attn_sink[l])
            else:
                att = _ctx_attn(q, k, v, attn_sink[l])
                new_k.append(k.reshape(BATCH, SEQ, N_KV_HEADS, HEAD_DIM))
                new_v.append(v.reshape(BATCH, SEQ, N_KV_HEADS, HEAD_DIM))
            y, h_last = _rnn(stream, xr, conv_w, conv_b, lru_w, lru_b, lru_lambda, h0[stream][l], l)
            if not stream.rotary:
                new_h.append(jnp.transpose(h_last, (0, 2, 1, 3)).reshape(BATCH, 2, D_RNN))
            merged[stream] = _merge(stream, xs_tok[stream], att, y, gr, ga, gb, mods, norm2_g,
                                    wap_b, wrp_b, wo_b, router_wt, router_b, l)
        pos, tiles = _plan(merged[LATENT][2], merged[CONTEXT][2])
        pos = pos.reshape(N_TOK_TILES, 1, TOK_TILE)
        xs = jnp.zeros((MOE_ROWS, ROW_W), F32)
        for stream in (LATENT, CONTEXT):
            xs = _dispatch(stream, pos, merged[stream][1], xs)
        y_sorted = _experts(tiles, xs, w_gate_up, w_down, l)
        for stream in (LATENT, CONTEXT):
            xs_tok[stream] = _combine(stream, pos, merged[stream][0], mods, y_sorted, l)

    return (_final_norm(CONTEXT, xs_tok[CONTEXT], final_norm_g), _final_norm(LATENT, xs_tok[LATENT], final_norm_g),
            jnp.stack(new_k, axis=1), jnp.stack(new_v, axis=1),
            jnp.stack(new_h, axis=1).astype(x_prompt.dtype))
```

```python
import functools
from typing import NamedTuple

import jax
import jax.numpy as jnp
from jax import lax
from jax.experimental import pallas as pl
from jax.experimental.pallas import tpu as pltpu

F32 = jnp.float32
BF16 = jnp.bfloat16

D_MODEL = 1024
BATCH = 16
SEQ = 256
DEPTH = 2
DEC_BATCH = 8
DEC_SEQ = 1024
PAST_LEN = 512
GRID_W = 64
N_HEADS = 8
N_KV_HEADS = 2
GQA_GROUP = N_HEADS // N_KV_HEADS
HEAD_DIM = 128
ATTN_W = N_HEADS * HEAD_DIM
KV_W = N_KV_HEADS * HEAD_DIM
WINDOW = 128
Q_BLOCK = 128
ROPE_BASE = 10000.0
ROPE_PAIRS_PER_AXIS = HEAD_DIM // 4
D_RNN = 1024
LRU_BLOCKS = 16
LRU_BW = D_RNN // LRU_BLOCKS
LRU_C = 8.0
CONV_W = 4
CONV_LEFT = 2
N_EXPERTS = 16
N_GROUPS = 4
EXPERTS_PER_GROUP = N_EXPERTS // N_GROUPS
D_EXPERT = 512
N_ADA = 6
EPS = 1e-6
IN_COLS = ATTN_W + 2 * KV_W + 2 * D_RNN + 2 * D_MODEL

T_LAT = DEC_BATCH * DEC_SEQ
T_CTX = BATCH * SEQ
T_ALL = T_LAT + T_CTX

SUBLANES = 8
LANES = 128
VMEM_LIMIT = 56 * 1024 * 1024

TOK_TILE = 256
N_TOK_TILES = T_ALL // TOK_TILE
COND_ROWS = 16


class _Stream(NamedTuple):
    name: str
    batch: int
    seq: int
    first_tile: int
    rotary: bool

    @property
    def rows(self):
        return self.batch * self.seq

    @property
    def n_tiles(self):
        return self.rows // TOK_TILE

    def mod_row(self, i):
        return 1 + i // (self.seq // TOK_TILE) if self.rotary else 0


LATENT = _Stream("lat", DEC_BATCH, DEC_SEQ, 0, True)
CONTEXT = _Stream("ctx", BATCH, SEQ, T_LAT // TOK_TILE, False)

RNN_CB = 256
RNN_CHUNK = 64
RNN_UNROLL = 8
LOG2_E = 1.4426950408889634
PAIRS_PER_GROUP = EXPERTS_PER_GROUP * (EXPERTS_PER_GROUP - 1) // 2
N_BUCKETS = N_GROUPS * PAIRS_PER_GROUP
ROW_W = D_MODEL + LANES
MOE_TM = 512
N_MOE_TILES = (T_ALL + N_BUCKETS * (MOE_TM - 1) + MOE_TM - 1) // MOE_TM
MOE_ROWS = N_MOE_TILES * MOE_TM


def _params(*sem):
    return pltpu.CompilerParams(dimension_semantics=sem, vmem_limit_bytes=VMEM_LIMIT)


def _sigmoid(x):
    return 0.5 * jnp.tanh(0.5 * x) + 0.5


def _ada_kernel(cond_ref, w_ref, b_ref, o_ref):
    s = jax.nn.silu(cond_ref[...]).astype(BF16)
    o_ref[...] = jnp.dot(s, w_ref[...].astype(BF16), preferred_element_type=F32) + b_ref[...]


def _ada(cond, w_ada, b_ada):
    cols = N_ADA * D_MODEL
    tn = 1536
    return pl.pallas_call(
        _ada_kernel,
        grid=(DEPTH, cols // tn),
        in_specs=[
            pl.BlockSpec((COND_ROWS, D_MODEL), lambda l, j: (0, 0)),
            pl.BlockSpec((None, D_MODEL, tn), lambda l, j: (l, 0, j)),
            pl.BlockSpec((None, 1, tn), lambda l, j: (l, 0, j)),
        ],
        out_specs=pl.BlockSpec((None, COND_ROWS, tn), lambda l, j: (l, 0, j)),
        out_shape=jax.ShapeDtypeStruct((DEPTH, COND_ROWS, cols), F32),
        compiler_params=_params("arbitrary", "arbitrary"),
        name="ada",
    )(cond, w_ada, b_ada.reshape(DEPTH, 1, cols))


def _inproj_kernel(rotary, x_ref, g_ref, sh_ref, sc_ref, *refs):
    if rotary:
        cos_ref, sin_ref, w_ref, q_ref, k_ref, v_ref, xr_ref, gr_ref, ga_ref, gb_ref = refs
    else:
        w_ref, q_ref, k_ref, v_ref, xr_ref, gr_ref, ga_ref, gb_ref = refs
    x = x_ref[...]
    y = x * lax.rsqrt(jnp.mean(x * x, axis=-1, keepdims=True) + EPS) * g_ref[...]
    h = (y * (1.0 + sc_ref[...]) + sh_ref[...]).astype(BF16)

    def proj(lo, width):
        return jnp.dot(h, w_ref[:, lo:lo + width], preferred_element_type=F32)

    def rope(t):
        if not rotary:
            return t
        return t * cos_ref[...] + pltpu.roll(t, HEAD_DIM // 2, 1) * sin_ref[...]

    scale = HEAD_DIM ** -0.5
    for hd in range(N_HEADS):
        q = proj(hd * HEAD_DIM, HEAD_DIM)
        q_ref[:, hd * HEAD_DIM:(hd + 1) * HEAD_DIM] = (rope(q) * scale).astype(BF16)
    for g in range(N_KV_HEADS):
        k = proj(ATTN_W + g * HEAD_DIM, HEAD_DIM)
        k_ref[:, g * HEAD_DIM:(g + 1) * HEAD_DIM] = rope(k)
    v_ref[...] = proj(ATTN_W + KV_W, KV_W)
    base = ATTN_W + 2 * KV_W
    xr_ref[...] = proj(base, D_RNN).astype(BF16)
    gr_ref[...] = proj(base + D_RNN, D_RNN).astype(BF16)
    ga_ref[...] = proj(base + 2 * D_RNN, D_MODEL).astype(BF16)
    gb_ref[...] = proj(base + 2 * D_RNN + D_MODEL, D_MODEL).astype(BF16)


def _inproj(stream, x, norm_g, mods, rope_cos, rope_sin, w_in, layer):
    row = lambda i: layer * COND_ROWS + stream.mod_row(i)
    tok = lambda i: (i, 0)
    rope_blk = lambda i: (i % (stream.seq // TOK_TILE), 0)
    wide = pl.BlockSpec((TOK_TILE, D_MODEL), tok)
    kv = pl.BlockSpec((TOK_TILE, KV_W), tok)
    rope_specs = [pl.BlockSpec((TOK_TILE, HEAD_DIM), rope_blk)] * 2 if stream.rotary else []
    rope_args = (rope_cos, rope_sin) if stream.rotary else ()
    return pl.pallas_call(
        functools.partial(_inproj_kernel, stream.rotary),
        grid=(stream.n_tiles,),
        in_specs=[
            wide,
            pl.BlockSpec((None, 1, D_MODEL), lambda i: (layer, 0, 0)),
            pl.BlockSpec((None, 1, D_MODEL), lambda i: (row(i), 0, 0)),
            pl.BlockSpec((None, 1, D_MODEL), lambda i: (row(i), 0, 1)),
        ] + rope_specs + [
            pl.BlockSpec((None, D_MODEL, IN_COLS), lambda i: (layer, 0, 0)),
        ],
        out_specs=[wide, kv, kv, wide, wide, wide, wide],
        out_shape=[
            jax.ShapeDtypeStruct((stream.rows, ATTN_W), BF16),
            jax.ShapeDtypeStruct((stream.rows, KV_W), F32),
            jax.ShapeDtypeStruct((stream.rows, KV_W), F32),
            jax.ShapeDtypeStruct((stream.rows, D_RNN), BF16),
            jax.ShapeDtypeStruct((stream.rows, D_RNN), BF16),
            jax.ShapeDtypeStruct((stream.rows, D_MODEL), BF16),
            jax.ShapeDtypeStruct((stream.rows, D_MODEL), BF16),
        ],
        compiler_params=_params("arbitrary"),
        name=f"inproj_{stream.name}",
    )(x, norm_g.reshape(DEPTH, 1, D_MODEL), mods, mods, *rope_args, w_in)


def _stack_heads(q_ref, g):
    lo = g * GQA_GROUP
    return jnp.concatenate(
        [q_ref[:, (lo + r) * HEAD_DIM:(lo + r + 1) * HEAD_DIM] for r in range(GQA_GROUP)], axis=0)


def _sink_column(sink_ref, g, rows):
    return jnp.concatenate(
        [jnp.full((rows, 1), sink_ref[g * GQA_GROUP + r], F32) for r in range(GQA_GROUP)], axis=0)


def _qk(q, k):
    return lax.dot_general(q, k, (((1,), (1,)), ((), ())), preferred_element_type=F32)


def _ctx_attn_kernel(sink_ref, q_ref, k_ref, v_ref, o_ref):
    for g in range(N_KV_HEADS):
        q = _stack_heads(q_ref, g)
        k = k_ref[:, g * HEAD_DIM:(g + 1) * HEAD_DIM].astype(BF16)
        v = v_ref[:, g * HEAD_DIM:(g + 1) * HEAD_DIM].astype(BF16)
        sink = _sink_column(sink_ref, g, SEQ)
        s = _qk(q, k)
        m = jnp.maximum(jnp.max(s, axis=-1, keepdims=True), sink)
        p = jnp.exp(s - m)
        denom = jnp.sum(p, axis=-1, keepdims=True) + jnp.exp(sink - m)
        o = jnp.dot(p.astype(BF16), v, preferred_element_type=F32) / denom
        for r in range(GQA_GROUP):
            hd = g * GQA_GROUP + r
            o_ref[:, hd * HEAD_DIM:(hd + 1) * HEAD_DIM] = o[r * SEQ:(r + 1) * SEQ].astype(BF16)


def _ctx_attn(q, k, v, sink):
    blk = lambda b: (b, 0)
    return pl.pallas_call(
        _ctx_attn_kernel,
        grid=(BATCH,),
        in_specs=[
            pl.BlockSpec(memory_space=pltpu.SMEM),
            pl.BlockSpec((SEQ, ATTN_W), blk),
            pl.BlockSpec((SEQ, KV_W), blk),
            pl.BlockSpec((SEQ, KV_W), blk),
        ],
        out_specs=pl.BlockSpec((SEQ, ATTN_W), lambda b: (b, 0)),
        out_shape=jax.ShapeDtypeStruct((T_CTX, ATTN_W), BF16),
        compiler_params=_params("arbitrary"),
        name="ctx_attn",
    )(sink, q, k, v)


def _lat_attn_kernel(sink_ref, q_ref, kp_ref, kc_ref, kn_ref, vp_ref, vc_ref, vn_ref,
                     ck_ref, cv_ref, o_ref):
    j = pl.program_id(1)
    rows = GQA_GROUP * Q_BLOCK
    band = Q_BLOCK + 2 * WINDOW
    qpos = j * Q_BLOCK + lax.broadcasted_iota(jnp.int32, (rows, band), 0) % Q_BLOCK
    kpos = j * Q_BLOCK - WINDOW + lax.broadcasted_iota(jnp.int32, (rows, band), 1)
    diff = qpos - kpos
    valid = (kpos >= 0) & (kpos < DEC_SEQ) & (diff <= WINDOW) & (diff >= -WINDOW)
    for g in range(N_KV_HEADS):
        cols = slice(g * HEAD_DIM, (g + 1) * HEAD_DIM)
        q = _stack_heads(q_ref, g)
        k_loc = jnp.concatenate([kp_ref[:, cols], kc_ref[:, cols], kn_ref[:, cols]], axis=0).astype(BF16)
        v_loc = jnp.concatenate([vp_ref[:, cols], vc_ref[:, cols], vn_ref[:, cols]], axis=0).astype(BF16)
        k_ctx = ck_ref[:, cols].astype(BF16)
        v_ctx = cv_ref[:, cols].astype(BF16)
        sink = _sink_column(sink_ref, g, Q_BLOCK)
        s_loc = jnp.where(valid, _qk(q, k_loc), -jnp.inf)
        s_ctx = _qk(q, k_ctx)
        m = jnp.maximum(jnp.maximum(jnp.max(s_loc, axis=-1, keepdims=True),
                                    jnp.max(s_ctx, axis=-1, keepdims=True)), sink)
        p_loc = jnp.exp(s_loc - m)
        p_ctx = jnp.exp(s_ctx - m)
        denom = (jnp.sum(p_loc, axis=-1, keepdims=True) + jnp.sum(p_ctx, axis=-1, keepdims=True)
                 + jnp.exp(sink - m))
        o = (jnp.dot(p_loc.astype(BF16), v_loc, preferred_element_type=F32)
             + jnp.dot(p_ctx.astype(BF16), v_ctx, preferred_element_type=F32)) / denom
        for r in range(GQA_GROUP):
            hd = g * GQA_GROUP + r
            o_ref[:, hd * HEAD_DIM:(hd + 1) * HEAD_DIM] = o[r * Q_BLOCK:(r + 1) * Q_BLOCK].astype(BF16)


def _lat_attn(q, k, v, cache_k, cache_v, sink):
    nb = DEC_SEQ // Q_BLOCK
    cur = lambda b, j: (b * nb + j, 0)
    prev = lambda b, j: (b * nb + jnp.maximum(j - 1, 0), 0)
    nxt = lambda b, j: (b * nb + jnp.minimum(j + 1, nb - 1), 0)
    kvb = lambda im: pl.BlockSpec((Q_BLOCK, KV_W), im)
    cache = pl.BlockSpec((None, PAST_LEN, KV_W), lambda b, j: (b, 0, 0))
    return pl.pallas_call(
        _lat_attn_kernel,
        grid=(DEC_BATCH, nb),
        in_specs=[
            pl.BlockSpec(memory_space=pltpu.SMEM),
            pl.BlockSpec((Q_BLOCK, ATTN_W), cur),
            kvb(prev), kvb(cur), kvb(nxt), kvb(prev), kvb(cur), kvb(nxt),
            cache, cache,
        ],
        out_specs=pl.BlockSpec((Q_BLOCK, ATTN_W), cur),
        out_shape=jax.ShapeDtypeStruct((T_LAT, ATTN_W), BF16),
        compiler_params=_params("arbitrary", "arbitrary"),
        name="lat_attn",
    )(sink, q, k, k, k, v, v, v, cache_k, cache_v)


def _rnn_kernel(seq, xr_ref, cw_ref, cb_ref, w_ref, b_ref, lam_ref, h0_ref,
                y_ref, hl_ref, xt_ref, yt_ref, a_ref, u_ref):
    n_chunks = seq // RNN_CHUNK
    rows = RNN_CHUNK * SUBLANES
    halo = jnp.zeros((CONV_LEFT, SUBLANES, RNN_CB), F32)
    xt_ref[0:CONV_LEFT] = halo
    xt_ref[seq + CONV_LEFT:seq + 2 * CONV_LEFT] = halo

    def load_chunk(c, carry):
        t0 = pl.multiple_of(c * RNN_CHUNK, RNN_CHUNK)
        x = xr_ref[:, pl.ds(t0, RNN_CHUNK), :].astype(F32)
        xt_ref[pl.ds(t0 + CONV_LEFT, RNN_CHUNK)] = jnp.swapaxes(x, 0, 1)
        return carry

    lax.fori_loop(0, n_chunks, load_chunk, 0)

    def conv_chunk(t0):
        acc = cb_ref[...].reshape(1, 1, RNN_CB)
        for tap in range(CONV_W):
            acc = acc + xt_ref[pl.ds(t0 + tap, RNN_CHUNK)] * cw_ref[tap:tap + 1, :].reshape(1, 1, RNN_CB)
        return acc

    for d in range(2):
        decay = (-0.5 * LRU_C * LOG2_E) * jax.nn.softplus(-lam_ref[d:d + 1, :])
        half_bias = b_ref[d:d + 1, :]

        def chunk(ci, h, d=d, decay=decay, half_bias=half_bias):
            c = ci if d == 0 else n_chunks - 1 - ci
            t0 = pl.multiple_of(c * RNN_CHUNK, RNN_CHUNK)
            xc = conv_chunk(t0).reshape(rows, RNN_CB)
            z = jnp.dot(xc.astype(BF16), w_ref[d], preferred_element_type=F32) + half_bias
            a = jnp.exp2(decay * jnp.tanh(z[:, :RNN_CB]) + decay)
            m = 1.0 - a * a
            mult = jnp.where(m == 0.0, 0.0, m * lax.rsqrt(m))
            u = mult * ((jnp.tanh(z[:, RNN_CB:]) + 1.0) * (0.5 * xc))
            a_ref[...] = a.reshape(RNN_CHUNK, SUBLANES, RNN_CB)
            u_ref[...] = u.reshape(RNN_CHUNK, SUBLANES, RNN_CB)

            def steps(gi, h):
                s0 = pl.multiple_of((gi if d == 0 else RNN_CHUNK // RNN_UNROLL - 1 - gi) * RNN_UNROLL, RNN_UNROLL)
                for j in (range(RNN_UNROLL) if d == 0 else reversed(range(RNN_UNROLL))):
                    h = a_ref[s0 + j] * h + u_ref[s0 + j]
                    if d == 0:
                        yt_ref[t0 + s0 + j] = h
                    else:
                        yt_ref[t0 + s0 + j] = yt_ref[t0 + s0 + j] + h
                return h

            return lax.fori_loop(0, RNN_CHUNK // RNN_UNROLL, steps, h)

        hl_ref[d] = lax.fori_loop(0, n_chunks, chunk, h0_ref[d])

    def store_chunk(c, carry):
        t0 = pl.multiple_of(c * RNN_CHUNK, RNN_CHUNK)
        y = jnp.swapaxes(yt_ref[pl.ds(t0, RNN_CHUNK)], 0, 1)
        y_ref[:, pl.ds(t0, RNN_CHUNK), :] = y.astype(BF16)
        return carry

    lax.fori_loop(0, n_chunks, store_chunk, 0)


def _rnn(stream, xr, conv_w, conv_b, w_blk, b_blk, lam, h0, layer):
    n_cb = D_RNN // RNN_CB
    seq = stream.seq
    n_groups = stream.batch // SUBLANES
    return pl.pallas_call(
        functools.partial(_rnn_kernel, seq),
        grid=(n_groups, n_cb),
        in_specs=[
            pl.BlockSpec((SUBLANES, seq, RNN_CB), lambda g, j: (g, 0, j)),
            pl.BlockSpec((None, CONV_W, RNN_CB), lambda g, j: (layer, 0, j)),
            pl.BlockSpec((None, 1, RNN_CB), lambda g, j: (layer, 0, j)),
            pl.BlockSpec((None, None, 2, RNN_CB, 2 * RNN_CB), lambda g, j: (layer, j, 0, 0, 0)),
            pl.BlockSpec((None, None, 2, 2 * RNN_CB), lambda g, j: (layer, j, 0, 0)),
            pl.BlockSpec((None, 2, RNN_CB), lambda g, j: (layer, 0, j)),
            pl.BlockSpec((None, 2, SUBLANES, RNN_CB), lambda g, j: (g, 0, 0, j)),
        ],
        out_specs=[
            pl.BlockSpec((SUBLANES, seq, RNN_CB), lambda g, j: (g, 0, j)),
            pl.BlockSpec((None, 2, SUBLANES, RNN_CB), lambda g, j: (g, 0, 0, j)),
        ],
        out_shape=[
            jax.ShapeDtypeStruct((n_groups * SUBLANES, seq, D_RNN), BF16),
            jax.ShapeDtypeStruct((n_groups, 2, SUBLANES, D_RNN), F32),
        ],
        scratch_shapes=[
            pltpu.VMEM((seq + 2 * CONV_LEFT, SUBLANES, RNN_CB), F32),
            pltpu.VMEM((seq, SUBLANES, RNN_CB), F32),
            pltpu.VMEM((RNN_CHUNK, SUBLANES, RNN_CB), F32),
            pltpu.VMEM((RNN_CHUNK, SUBLANES, RNN_CB), F32),
        ],
        compiler_params=_params("arbitrary", "arbitrary"),
        name=f"rnn_{stream.name}",
    )(xr.reshape(stream.batch, seq, D_RNN), conv_w, conv_b.reshape(DEPTH, 1, D_RNN), w_blk, b_blk, lam, h0)


def _lru_block_weights(lru_wa, lru_wx, lru_ba, lru_bx):
    n_cb = D_RNN // RNN_CB
    per = RNN_CB // LRU_BW

    def dense(w):
        w = w.reshape(DEPTH, 2, n_cb, per, LRU_BW, LRU_BW)
        eye = jnp.eye(per, dtype=w.dtype)
        full = jnp.einsum("ldcpkj,pq->ldcpkqj", w, eye)
        return full.reshape(DEPTH, 2, n_cb, RNN_CB, RNN_CB)

    w = jnp.concatenate([dense(lru_wa), dense(lru_wx)], axis=-1)
    w = jnp.transpose(0.5 * w, (0, 2, 1, 3, 4)).astype(BF16)
    b = jnp.concatenate([lru_ba.reshape(DEPTH, 2, n_cb, RNN_CB),
                         lru_bx.reshape(DEPTH, 2, n_cb, RNN_CB)], axis=-1)
    return w, jnp.transpose(0.5 * b, (0, 2, 1, 3))


def _route(scores, sel):
    grp_score = []
    for g in range(N_GROUPS):
        a, b, c, d = sel[g * EXPERTS_PER_GROUP:(g + 1) * EXPERTS_PER_GROUP]
        hi1, lo1 = jnp.maximum(a, b), jnp.minimum(a, b)
        hi2, lo2 = jnp.maximum(c, d), jnp.minimum(c, d)
        grp_score.append(jnp.maximum(hi1, hi2) + jnp.maximum(jnp.minimum(hi1, hi2), jnp.maximum(lo1, lo2)))
    best = jnp.zeros_like(grp_score[0], dtype=jnp.int32)
    best_val = grp_score[0]
    for g in range(1, N_GROUPS):
        better = grp_score[g] > best_val
        best = jnp.where(better, g, best)
        best_val = jnp.where(better, grp_score[g], best_val)
    chosen = []
    for e in range(N_EXPERTS):
        g = e // EXPERTS_PER_GROUP
        rank = jnp.zeros_like(best)
        for o in range(g * EXPERTS_PER_GROUP, (g + 1) * EXPERTS_PER_GROUP):
            if o == e:
                continue
            ahead = (sel[o] >= sel[e]) if o < e else (sel[o] > sel[e])
            rank = rank + jnp.where(ahead, 1, 0)
        chosen.append(jnp.where(best == g, rank, 2) < 2)
    taken, gate = [], []
    for j in range(EXPERTS_PER_GROUP):
        t = jnp.zeros_like(best)
        s = jnp.zeros_like(scores[0])
        for g in range(N_GROUPS):
            e = g * EXPERTS_PER_GROUP + j
            t = t + jnp.where(chosen[e], 1, 0)
            s = s + jnp.where(chosen[e], scores[e], 0.0)
        taken.append(t > 0)
        gate.append(s)
    total = gate[0] + gate[1] + gate[2] + gate[3]
    pair = jnp.where(taken[0], jnp.where(taken[1], 0, jnp.where(taken[2], 1, 2)),
                     jnp.where(taken[1], jnp.where(taken[2], 3, 4), 5))
    w_lo = jnp.where(taken[0], gate[0], jnp.where(taken[1], gate[1], gate[2])) / total
    w_hi = jnp.where(taken[3], gate[3], jnp.where(taken[2], gate[2], gate[1])) / total
    return best * PAIRS_PER_GROUP + pair, w_lo, w_hi


def _merge_kernel(x_ref, att_ref, rnn_ref, gr_ref, ga_ref, gb_ref,
                  g1_ref, sh2_ref, sc2_ref, n2_ref, wap_ref, wrp_ref, wo_ref, rw_ref, rb_ref,
                  xo_ref, h2_ref, bucket_ref):
    att = jnp.dot(att_ref[...], wap_ref[...], preferred_element_type=F32)
    gated = (jax.nn.gelu(gr_ref[...].astype(F32)) * rnn_ref[...].astype(F32)).astype(BF16)
    rnn = jnp.dot(gated, wrp_ref[...], preferred_element_type=F32)
    merged = _sigmoid(ga_ref[...].astype(F32)) * att + _sigmoid(gb_ref[...].astype(F32)) * rnn
    x = x_ref[...] + g1_ref[...] * jnp.dot(merged.astype(BF16), wo_ref[...], preferred_element_type=F32)
    xo_ref[...] = x
    y = x * lax.rsqrt(jnp.mean(x * x, axis=-1, keepdims=True) + EPS) * n2_ref[...]
    h2 = y * (1.0 + sc2_ref[...]) + sh2_ref[...]
    h2_ref[:, :D_MODEL] = h2
    h_hi = h2.astype(BF16)
    h_lo = (h2 - h_hi.astype(F32)).astype(BF16)
    rw = rw_ref[...]
    w_hi = rw.astype(BF16)
    w_lo = (rw - w_hi.astype(F32)).astype(BF16)
    logits = _qk(w_hi, h_hi) + (_qk(w_hi, h_lo) + _qk(w_lo, h_hi))
    score = _sigmoid(logits)
    sel = score + rb_ref[...]
    bucket, w_lo, w_hi = _route([score[e:e + 1, :] for e in range(N_EXPERTS)],
                                [sel[e:e + 1, :] for e in range(N_EXPERTS)])
    bucket_ref[...] = bucket
    pad = jnp.zeros((LANES - 2, TOK_TILE), F32)
    h2_ref[:, D_MODEL:] = jnp.concatenate([w_lo, w_hi, pad], axis=0).T


def _merge(stream, x, att, rnn, gr, ga, gb, mods, norm_g, wap, wrp, wo, router_wt, router_b, layer):
    tok = lambda i: (i, 0)
    wide = pl.BlockSpec((TOK_TILE, D_MODEL), tok)
    mod = lambda col: pl.BlockSpec((None, 1, D_MODEL),
                                   lambda i: (layer * COND_ROWS + stream.mod_row(i), 0, col))
    mat = pl.BlockSpec((None, D_MODEL, D_MODEL), lambda i: (layer, 0, 0))
    return pl.pallas_call(
        _merge_kernel,
        grid=(stream.n_tiles,),
        in_specs=[
            wide, wide, wide, wide, wide, wide,
            mod(2), mod(3), mod(4),
            pl.BlockSpec((None, 1, D_MODEL), lambda i: (layer, 0, 0)),
            mat, mat, mat,
            pl.BlockSpec((N_EXPERTS, D_MODEL), lambda i: (0, 0)),
            pl.BlockSpec((N_EXPERTS, 1), lambda i: (0, 0)),
        ],
        out_specs=[wide, pl.BlockSpec((TOK_TILE, ROW_W), tok),
                   pl.BlockSpec((None, 1, TOK_TILE), lambda i: (i, 0, 0))],
        out_shape=[
            jax.ShapeDtypeStruct((stream.rows, D_MODEL), F32),
            jax.ShapeDtypeStruct((stream.rows, ROW_W), F32),
            jax.ShapeDtypeStruct((stream.n_tiles, 1, TOK_TILE), jnp.int32),
        ],
        compiler_params=_params("arbitrary"),
        name=f"merge_{stream.name}",
    )(x, att, rnn.reshape(stream.rows, D_RNN), gr, ga, gb, mods, mods, mods,
      norm_g.reshape(DEPTH, 1, D_MODEL), wap, wrp, wo, router_wt, router_b.reshape(N_EXPERTS, 1))


def _plan_kernel(b_ref, pos_ref, tiles_ref):
    b = b_ref[...]
    r = lax.broadcasted_iota(jnp.int32, (TOK_TILE, TOK_TILE), 0)
    c = lax.broadcasted_iota(jnp.int32, (TOK_TILE, TOK_TILE), 1)
    before_in_tile = jnp.where(r < c, 1.0, 0.0).astype(BF16)
    br = lax.broadcasted_iota(jnp.int32, (N_TOK_TILES, N_TOK_TILES), 0)
    bc = lax.broadcasted_iota(jnp.int32, (N_TOK_TILES, N_TOK_TILES), 1)
    earlier_tiles = jnp.where(bc < br, 1.0, 0.0).astype(BF16)
    lane = lax.broadcasted_iota(jnp.int32, (1, LANES), 1)
    tile_start = (lane * MOE_TM).astype(F32)
    start = jnp.zeros((1, 1), F32)
    pos = jnp.zeros(b.shape, F32)
    tile_bucket = jnp.zeros((1, LANES), F32)
    for k in range(N_BUCKETS):
        mask = jnp.where(b == k, 1.0, 0.0)
        rank = jnp.dot(mask.astype(BF16), before_in_tile, preferred_element_type=F32)
        per_tile = jnp.sum(mask, axis=1, keepdims=True)
        tile_off = jnp.dot(earlier_tiles, jnp.broadcast_to(per_tile, (N_TOK_TILES, LANES)).astype(BF16),
                           preferred_element_type=F32)[:, :1]
        total = jnp.sum(per_tile, axis=0, keepdims=True)
        padded = jnp.floor((total + (MOE_TM - 1)) * (1.0 / MOE_TM)) * MOE_TM
        pos = pos + mask * (start + tile_off + rank)
        start = start + padded
        tile_bucket = tile_bucket + jnp.where(tile_start >= start, 1.0, 0.0)
    pos_ref[...] = pos.astype(jnp.int32)
    tiles_ref[...] = jnp.where(lane == LANES - 1, start * (1.0 / MOE_TM), tile_bucket).astype(jnp.int32)


def _plan(bucket_lat, bucket_ctx):
    bucket = jnp.concatenate([bucket_lat, bucket_ctx], axis=0)
    return pl.pallas_call(
        _plan_kernel,
        out_shape=[
            jax.ShapeDtypeStruct((N_TOK_TILES, TOK_TILE), jnp.int32),
            jax.ShapeDtypeStruct((1, LANES), jnp.int32),
        ],
        compiler_params=pltpu.CompilerParams(vmem_limit_bytes=VMEM_LIMIT),
        name="plan",
    )(bucket.reshape(N_TOK_TILES, TOK_TILE))


def _row_copy(src, dst, sem):
    return pltpu.make_async_copy(src, dst, sem)


def _dispatch_kernel(pos_ref, h_ref, xs_in_ref, xs_ref, sem):
    del xs_in_ref

    for r in range(TOK_TILE):
        _row_copy(h_ref.at[pl.ds(r, 1), :], xs_ref.at[pl.ds(pos_ref[0, r], 1), :], sem).start(priority=r % 2)

    def wait(r, carry):
        _row_copy(h_ref.at[pl.ds(0, 1), :], xs_ref.at[pl.ds(0, 1), :], sem).wait()
        return carry

    lax.fori_loop(0, TOK_TILE, wait, 0, unroll=8)


def _dispatch(stream, pos, h2, xs):
    return pl.pallas_call(
        _dispatch_kernel,
        grid=(stream.n_tiles,),
        in_specs=[
            pl.BlockSpec((None, 1, TOK_TILE), lambda i: (stream.first_tile + i, 0, 0), memory_space=pltpu.SMEM),
            pl.BlockSpec((TOK_TILE, ROW_W), lambda i: (i, 0)),
            pl.BlockSpec(memory_space=pl.ANY),
        ],
        out_specs=pl.BlockSpec(memory_space=pl.ANY),
        out_shape=jax.ShapeDtypeStruct((MOE_ROWS, ROW_W), F32),
        input_output_aliases={2: 0},
        scratch_shapes=[pltpu.SemaphoreType.DMA(())],
        compiler_params=_params("arbitrary"),
        name=f"dispatch_{stream.name}",
    )(pos, h2, xs)


def _tile_bucket(i, tiles_ref):
    return tiles_ref[0, jnp.minimum(i, tiles_ref[0, LANES - 1] - 1)]


def _bucket_expert(k, hi):
    g = k // PAIRS_PER_GROUP
    p = k % PAIRS_PER_GROUP
    lo_idx = jnp.where(p >= 3, 1, 0) + jnp.where(p >= 5, 1, 0)
    hi_idx = jnp.where(p < 3, p + 1, jnp.where(p < 5, p - 1, 3))
    return g * EXPERTS_PER_GROUP + (hi_idx if hi else lo_idx)


def _experts_kernel(tiles_ref, xs_ref, wgu_lo_ref, wgu_hi_ref, wd_lo_ref, wd_hi_ref, y_ref,
                    wgu_lo_b, wgu_hi_b, wd_lo_b, wd_hi_b):
    i = pl.program_id(0)
    in_use = i < tiles_ref[0, LANES - 1]

    @pl.when(jnp.logical_and(in_use, jnp.logical_or(i == 0, tiles_ref[0, i] != tiles_ref[0, jnp.maximum(i - 1, 0)])))
    def _():
        wgu_lo_b[...] = wgu_lo_ref[...].astype(BF16)
        wgu_hi_b[...] = wgu_hi_ref[...].astype(BF16)
        wd_lo_b[...] = wd_lo_ref[...].astype(BF16)
        wd_hi_b[...] = wd_hi_ref[...].astype(BF16)

    @pl.when(jnp.logical_not(in_use))
    def _():
        y_ref[...] = jnp.zeros_like(y_ref)

    @pl.when(in_use)
    def _():
        x = xs_ref[:, :D_MODEL].astype(BF16)

        def branch(wgu_ref, wd_ref, gate):
            gu = jnp.dot(x, wgu_ref[...], preferred_element_type=F32)
            act = jax.nn.silu(gu[:, :D_EXPERT]) * gu[:, D_EXPERT:] * gate
            return jnp.dot(act.astype(BF16), wd_ref[...], preferred_element_type=F32)

        y_ref[...] = (branch(wgu_lo_b, wd_lo_b, xs_ref[:, D_MODEL:D_MODEL + 1])
                      + branch(wgu_hi_b, wd_hi_b, xs_ref[:, D_MODEL + 1:D_MODEL + 2]))


def _experts(tiles, xs, wgu, wd, layer):
    def weight(shape, hi):
        return pl.BlockSpec((None, None) + shape,
                            lambda i, t: (layer, _bucket_expert(_tile_bucket(i, t), hi), 0, 0))

    return pl.pallas_call(
        _experts_kernel,
        grid_spec=pltpu.PrefetchScalarGridSpec(
            num_scalar_prefetch=1,
            grid=(N_MOE_TILES,),
            in_specs=[
                pl.BlockSpec((MOE_TM, ROW_W), lambda i, t: (i, 0)),
                weight((D_MODEL, 2 * D_EXPERT), False), weight((D_MODEL, 2 * D_EXPERT), True),
                weight((D_EXPERT, D_MODEL), False), weight((D_EXPERT, D_MODEL), True),
            ],
            out_specs=pl.BlockSpec((MOE_TM, D_MODEL), lambda i, t: (i, 0)),
            scratch_shapes=[
                pltpu.VMEM((D_MODEL, 2 * D_EXPERT), BF16), pltpu.VMEM((D_MODEL, 2 * D_EXPERT), BF16),
                pltpu.VMEM((D_EXPERT, D_MODEL), BF16), pltpu.VMEM((D_EXPERT, D_MODEL), BF16),
            ],
        ),
        out_shape=jax.ShapeDtypeStruct((MOE_ROWS, D_MODEL), F32),
        compiler_params=_params("arbitrary"),
        name="experts",
    )(tiles, xs, wgu, wgu, wd, wd)


def _combine_kernel(n_tiles, final, pos_ref, next_pos_ref, x_ref, g2_ref, fg_ref, y_ref, o_ref, rows_ref, sem):
    i = pl.program_id(0)

    def gather(p_ref, into):
        for r in range(TOK_TILE):
            _row_copy(y_ref.at[pl.ds(p_ref[0, r], 1), :], rows_ref.at[into, pl.ds(r, 1), :],
                      sem.at[into]).start(priority=r % 2)

    def step(slot):
        if slot == 0:
            @pl.when(i == 0)
            def _():
                gather(pos_ref, 0)

        @pl.when(i + 1 < n_tiles)
        def _():
            gather(next_pos_ref, 1 - slot)

        def wait(r, carry):
            _row_copy(y_ref.at[pl.ds(0, 1), :], rows_ref.at[slot, pl.ds(0, 1), :], sem.at[slot]).wait()
            return carry

        lax.fori_loop(0, TOK_TILE, wait, 0, unroll=8)
        x = x_ref[...] + g2_ref[...] * rows_ref[slot]
        if final:
            x = x * lax.rsqrt(jnp.mean(x * x, axis=-1, keepdims=True) + EPS) * fg_ref[...]
        o_ref[...] = x

    for parity in range(2):
        pl.when(i % 2 == parity)(functools.partial(step, parity))


def _combine(stream, pos, x, mods, y, final_g, layer):
    first, last = stream.first_tile, stream.first_tile + stream.n_tiles - 1
    return pl.pallas_call(
        functools.partial(_combine_kernel, stream.n_tiles, layer == DEPTH - 1),
        grid=(stream.n_tiles,),
        in_specs=[
            pl.BlockSpec((None, 1, TOK_TILE), lambda i: (first + i, 0, 0), memory_space=pltpu.SMEM),
            pl.BlockSpec((None, 1, TOK_TILE), lambda i: (jnp.minimum(first + i + 1, last), 0, 0),
                         memory_space=pltpu.SMEM),
            pl.BlockSpec((TOK_TILE, D_MODEL), lambda i: (i, 0)),
            pl.BlockSpec((None, 1, D_MODEL), lambda i: (layer * COND_ROWS + stream.mod_row(i), 0, 5)),
            pl.BlockSpec((1, D_MODEL), lambda i: (0, 0)),
            pl.BlockSpec(memory_space=pl.ANY),
        ],
        out_specs=pl.BlockSpec((TOK_TILE, D_MODEL), lambda i: (i, 0)),
        out_shape=jax.ShapeDtypeStruct((stream.rows, D_MODEL), F32),
        scratch_shapes=[pltpu.VMEM((2, TOK_TILE, D_MODEL), F32), pltpu.SemaphoreType.DMA((2,))],
        compiler_params=_params("arbitrary"),
        name=f"combine_{stream.name}",
    )(pos, pos, x, mods, final_g.reshape(1, D_MODEL), y)


def _rope_tables():
    t = jnp.arange(DEC_SEQ, dtype=jnp.int32)
    row = (t // GRID_W).astype(F32)
    col = (t % GRID_W).astype(F32)
    inv = ROPE_BASE ** (-jnp.arange(ROPE_PAIRS_PER_AXIS, dtype=F32) / ROPE_PAIRS_PER_AXIS)
    ang = jnp.concatenate([row[:, None] * inv, col[:, None] * inv], axis=-1)
    cos, sin = jnp.cos(ang), jnp.sin(ang)
    return jnp.concatenate([cos, cos], axis=-1), jnp.concatenate([-sin, sin], axis=-1)


def kernel(x_prompt, x_sample, cache_k, cache_v, state_h, c, c_ctx, norm1_g, norm2_g, w_ada, b_ada, w_in, conv_w, conv_b, lru_wa, lru_ba, lru_wx, lru_bx, lru_lambda, attn_sink, w_attn_proj, w_rnn_proj, w_out, router_w, router_b, w_gate_up, w_down, final_norm_g):
    xs_tok = {LATENT: x_sample.reshape(T_LAT, D_MODEL), CONTEXT: x_prompt.reshape(T_CTX, D_MODEL)}
    cond = jnp.concatenate([c_ctx[None, :], c, jnp.zeros((COND_ROWS - 1 - DEC_BATCH, D_MODEL), F32)], axis=0)
    mods = _ada(cond, w_ada, b_ada).reshape(DEPTH * COND_ROWS, 1, N_ADA * D_MODEL)
    rope_cos, rope_sin = _rope_tables()

    w_in_b = w_in.astype(BF16)
    wap_b = w_attn_proj.astype(BF16)
    wrp_b = w_rnn_proj.astype(BF16)
    wo_b = w_out.astype(BF16)
    lru_w, lru_b = _lru_block_weights(lru_wa, lru_wx, lru_ba, lru_bx)
    router_wt = router_w.T

    h0 = {LATENT: jnp.transpose(state_h.astype(F32), (1, 2, 0, 3)).reshape(DEPTH, 1, 2, DEC_BATCH, D_RNN),
          CONTEXT: jnp.zeros((DEPTH, BATCH // SUBLANES, 2, SUBLANES, D_RNN), F32)}
    ck = jnp.transpose(cache_k, (1, 0, 2, 3, 4)).reshape(DEPTH, DEC_BATCH, PAST_LEN, KV_W)
    cv = jnp.transpose(cache_v, (1, 0, 2, 3, 4)).reshape(DEPTH, DEC_BATCH, PAST_LEN, KV_W)

    new_k, new_v, new_h = [], [], []
    for l in range(DEPTH):
        merged = {}
        for stream in (LATENT, CONTEXT):
            q, k, v, xr, gr, ga, gb = _inproj(stream, xs_tok[stream], norm1_g, mods, rope_cos, rope_sin,
                                              w_in_b, l)
            if stream.rotary:
                att = _lat_attn(q, k, v, ck[l], cv[l], attn_sink[l])
            else:
                att = _ctx_attn(q, k, v, attn_sink[l])
                new_k.append(k.reshape(BATCH, SEQ, N_KV_HEADS, HEAD_DIM))
                new_v.append(v.reshape(BATCH, SEQ, N_KV_HEADS, HEAD_DIM))
            y, h_last = _rnn(stream, xr, conv_w, conv_b, lru_w, lru_b, lru_lambda, h0[stream][l], l)
            if not stream.rotary:
                new_h.append(jnp.transpose(h_last, (0, 2, 1, 3)).reshape(BATCH, 2, D_RNN))
            merged[stream] = _merge(stream, xs_tok[stream], att, y, gr, ga, gb, mods, norm2_g,
                                    wap_b, wrp_b, wo_b, router_wt, router_b, l)
        pos, tiles = _plan(merged[LATENT][2], merged[CONTEXT][2])
        pos = pos.reshape(N_TOK_TILES, 1, TOK_TILE)
        xs = jnp.zeros((MOE_ROWS, ROW_W), F32)
        for stream in (LATENT, CONTEXT):
            xs = _dispatch(stream, pos, merged[stream][1], xs)
        y_sorted = _experts(tiles, xs, w_gate_up, w_down, l)
        for stream in (LATENT, CONTEXT):
            xs_tok[stream] = _combine(stream, pos, merged[stream][0], mods, y_sorted, final_norm_g, l)

    return (xs_tok[CONTEXT].reshape(BATCH, SEQ, D_MODEL), xs_tok[LATENT].reshape(DEC_BATCH, DEC_SEQ, D_MODEL),
            jnp.stack(new_k, axis=1), jnp.stack(new_v, axis=1),
            jnp.stack(new_h, axis=1).astype(x_prompt.dtype))
```

```python
import functools
from typing import NamedTuple

import jax
import jax.numpy as jnp
from jax import lax
from jax.experimental import pallas as pl
from jax.experimental.pallas import tpu as pltpu

F32 = jnp.float32
BF16 = jnp.bfloat16

D_MODEL = 1024
BATCH = 16
SEQ = 256
DEPTH = 2
DEC_BATCH = 8
DEC_SEQ = 1024
PAST_LEN = 512
GRID_W = 64
N_HEADS = 8
N_KV_HEADS = 2
GQA_GROUP = N_HEADS // N_KV_HEADS
HEAD_DIM = 128
ATTN_W = N_HEADS * HEAD_DIM
KV_W = N_KV_HEADS * HEAD_DIM
WINDOW = 128
Q_BLOCK = 128
ROPE_BASE = 10000.0
ROPE_PAIRS_PER_AXIS = HEAD_DIM // 4
D_RNN = 1024
LRU_BLOCKS = 16
LRU_BW = D_RNN // LRU_BLOCKS
LRU_C = 8.0
CONV_W = 4
CONV_LEFT = 2
N_EXPERTS = 16
N_GROUPS = 4
EXPERTS_PER_GROUP = N_EXPERTS // N_GROUPS
D_EXPERT = 512
N_ADA = 6
EPS = 1e-6
IN_COLS = ATTN_W + 2 * KV_W + 2 * D_RNN + 2 * D_MODEL

T_LAT = DEC_BATCH * DEC_SEQ
T_CTX = BATCH * SEQ
T_ALL = T_LAT + T_CTX

SUBLANES = 8
LANES = 128
VMEM_LIMIT = 56 * 1024 * 1024

TOK_TILE = 256
MM_TILE = 512
N_TOK_TILES = T_ALL // TOK_TILE
COND_ROWS = 16


class _Stream(NamedTuple):
    name: str
    batch: int
    seq: int
    first_tile: int
    rotary: bool

    @property
    def rows(self):
        return self.batch * self.seq

    @property
    def n_tiles(self):
        return self.rows // TOK_TILE

    def mod_row(self, i, tile):
        return 1 + i // (self.seq // tile) if self.rotary else 0


LATENT = _Stream("lat", DEC_BATCH, DEC_SEQ, 0, True)
CONTEXT = _Stream("ctx", BATCH, SEQ, T_LAT // TOK_TILE, False)

RNN_CB = 256
RNN_CHUNK = 64
RNN_UNROLL = 8
LOG2_E = 1.4426950408889634
PAIRS_PER_GROUP = EXPERTS_PER_GROUP * (EXPERTS_PER_GROUP - 1) // 2
N_BUCKETS = N_GROUPS * PAIRS_PER_GROUP
ROW_W = D_MODEL + LANES
MOE_TM = 256
N_MOE_TILES = (T_ALL + N_BUCKETS * (MOE_TM - 1) + MOE_TM - 1) // MOE_TM
MOE_ROWS = N_MOE_TILES * MOE_TM


def _params(*sem):
    return pltpu.CompilerParams(dimension_semantics=sem, vmem_limit_bytes=VMEM_LIMIT)


def _sigmoid(x):
    return 0.5 * jnp.tanh(0.5 * x) + 0.5


def _ada_kernel(cond_ref, w_ref, b_ref, o_ref):
    s = jax.nn.silu(cond_ref[...]).astype(BF16)
    o_ref[...] = jnp.dot(s, w_ref[...].astype(BF16), preferred_element_type=F32) + b_ref[...]


def _ada(cond, w_ada, b_ada):
    cols = N_ADA * D_MODEL
    tn = 1536
    return pl.pallas_call(
        _ada_kernel,
        grid=(DEPTH, cols // tn),
        in_specs=[
            pl.BlockSpec((COND_ROWS, D_MODEL), lambda l, j: (0, 0)),
            pl.BlockSpec((None, D_MODEL, tn), lambda l, j: (l, 0, j)),
            pl.BlockSpec((None, 1, tn), lambda l, j: (l, 0, j)),
        ],
        out_specs=pl.BlockSpec((None, COND_ROWS, tn), lambda l, j: (l, 0, j)),
        out_shape=jax.ShapeDtypeStruct((DEPTH, COND_ROWS, cols), F32),
        compiler_params=_params("arbitrary", "arbitrary"),
        name="ada",
    )(cond, w_ada, b_ada.reshape(DEPTH, 1, cols))


def _inproj_kernel(rotary, x_ref, g_ref, sh_ref, sc_ref, *refs):
    if rotary:
        cos_ref, sin_ref, w_ref, q_ref, k_ref, v_ref, xr_ref, gr_ref, ga_ref, gb_ref = refs
    else:
        w_ref, q_ref, k_ref, v_ref, xr_ref, gr_ref, ga_ref, gb_ref = refs
    x = x_ref[...]
    y = x * lax.rsqrt(jnp.mean(x * x, axis=-1, keepdims=True) + EPS) * g_ref[...]
    h = (y * (1.0 + sc_ref[...]) + sh_ref[...]).astype(BF16)

    def proj(lo, width):
        return jnp.dot(h, w_ref[:, lo:lo + width], preferred_element_type=F32)

    def rope(t):
        if not rotary:
            return t
        return t * cos_ref[...] + pltpu.roll(t, HEAD_DIM // 2, 1) * sin_ref[...]

    scale = HEAD_DIM ** -0.5
    qk = proj(0, ATTN_W + KV_W)
    for hd in range(N_HEADS):
        q = qk[:, hd * HEAD_DIM:(hd + 1) * HEAD_DIM]
        q_ref[:, hd * HEAD_DIM:(hd + 1) * HEAD_DIM] = (rope(q) * scale).astype(BF16)
    for g in range(N_KV_HEADS):
        k = qk[:, ATTN_W + g * HEAD_DIM:ATTN_W + (g + 1) * HEAD_DIM]
        k_ref[:, g * HEAD_DIM:(g + 1) * HEAD_DIM] = rope(k)
    v_ref[...] = proj(ATTN_W + KV_W, KV_W)
    base = ATTN_W + 2 * KV_W
    xr_ref[...] = proj(base, D_RNN).astype(BF16)
    gr_ref[...] = proj(base + D_RNN, D_RNN).astype(BF16)
    ga_ref[...] = proj(base + 2 * D_RNN, D_MODEL).astype(BF16)
    gb_ref[...] = proj(base + 2 * D_RNN + D_MODEL, D_MODEL).astype(BF16)


def _inproj(stream, x, norm_g, mods, rope_cos, rope_sin, w_in, layer):
    row = lambda i: layer * COND_ROWS + stream.mod_row(i, MM_TILE)
    tok = lambda i: (i, 0)
    rope_blk = lambda i: (i % (stream.seq // MM_TILE), 0)
    wide = pl.BlockSpec((MM_TILE, D_MODEL), tok)
    kv = pl.BlockSpec((MM_TILE, KV_W), tok)
    rope_specs = [pl.BlockSpec((MM_TILE, HEAD_DIM), rope_blk)] * 2 if stream.rotary else []
    rope_args = (rope_cos, rope_sin) if stream.rotary else ()
    return pl.pallas_call(
        functools.partial(_inproj_kernel, stream.rotary),
        grid=(stream.rows // MM_TILE,),
        in_specs=[
            wide,
            pl.BlockSpec((None, 1, D_MODEL), lambda i: (layer, 0, 0)),
            pl.BlockSpec((None, 1, D_MODEL), lambda i: (row(i), 0, 0)),
            pl.BlockSpec((None, 1, D_MODEL), lambda i: (row(i), 0, 1)),
        ] + rope_specs + [
            pl.BlockSpec((None, D_MODEL, IN_COLS), lambda i: (layer, 0, 0)),
        ],
        out_specs=[wide, kv, kv, wide, wide, wide, wide],
        out_shape=[
            jax.ShapeDtypeStruct((stream.rows, ATTN_W), BF16),
            jax.ShapeDtypeStruct((stream.rows, KV_W), F32),
            jax.ShapeDtypeStruct((stream.rows, KV_W), F32),
            jax.ShapeDtypeStruct((stream.rows, D_RNN), BF16),
            jax.ShapeDtypeStruct((stream.rows, D_RNN), BF16),
            jax.ShapeDtypeStruct((stream.rows, D_MODEL), BF16),
            jax.ShapeDtypeStruct((stream.rows, D_MODEL), BF16),
        ],
        compiler_params=_params("arbitrary"),
        name=f"inproj_{stream.name}",
    )(x, norm_g.reshape(DEPTH, 1, D_MODEL), mods, mods, *rope_args, w_in)


def _stack_heads(q_ref, g):
    lo = g * GQA_GROUP
    return jnp.concatenate(
        [q_ref[:, (lo + r) * HEAD_DIM:(lo + r + 1) * HEAD_DIM] for r in range(GQA_GROUP)], axis=0)


def _sink_column(sink_ref, g, rows):
    return jnp.concatenate(
        [jnp.full((rows, 1), sink_ref[g * GQA_GROUP + r], F32) for r in range(GQA_GROUP)], axis=0)


def _qk(q, k):
    return lax.dot_general(q, k, (((1,), (1,)), ((), ())), preferred_element_type=F32)


def _ctx_attn_kernel(sink_ref, q_ref, k_ref, v_ref, o_ref):
    for g in range(N_KV_HEADS):
        q = _stack_heads(q_ref, g)
        k = k_ref[:, g * HEAD_DIM:(g + 1) * HEAD_DIM].astype(BF16)
        v = v_ref[:, g * HEAD_DIM:(g + 1) * HEAD_DIM].astype(BF16)
        sink = _sink_column(sink_ref, g, SEQ)
        s = _qk(q, k)
        m = jnp.maximum(jnp.max(s, axis=-1, keepdims=True), sink)
        p = jnp.exp(s - m)
        denom = jnp.sum(p, axis=-1, keepdims=True) + jnp.exp(sink - m)
        o = jnp.dot(p.astype(BF16), v, preferred_element_type=F32) / denom
        for r in range(GQA_GROUP):
            hd = g * GQA_GROUP + r
            o_ref[:, hd * HEAD_DIM:(hd + 1) * HEAD_DIM] = o[r * SEQ:(r + 1) * SEQ].astype(BF16)


def _ctx_attn(q, k, v, sink):
    blk = lambda b: (b, 0)
    return pl.pallas_call(
        _ctx_attn_kernel,
        grid=(BATCH,),
        in_specs=[
            pl.BlockSpec(memory_space=pltpu.SMEM),
            pl.BlockSpec((SEQ, ATTN_W), blk),
            pl.BlockSpec((SEQ, KV_W), blk),
            pl.BlockSpec((SEQ, KV_W), blk),
        ],
        out_specs=pl.BlockSpec((SEQ, ATTN_W), lambda b: (b, 0)),
        out_shape=jax.ShapeDtypeStruct((T_CTX, ATTN_W), BF16),
        compiler_params=_params("arbitrary"),
        name="ctx_attn",
    )(sink, q, k, v)


def _lat_attn_kernel(sink_ref, q_ref, kp_ref, kc_ref, kn_ref, vp_ref, vc_ref, vn_ref,
                     ck_ref, cv_ref, o_ref):
    j = pl.program_id(1)
    rows = GQA_GROUP * Q_BLOCK
    band = Q_BLOCK + 2 * WINDOW
    qpos = j * Q_BLOCK + lax.broadcasted_iota(jnp.int32, (rows, band), 0) % Q_BLOCK
    kpos = j * Q_BLOCK - WINDOW + lax.broadcasted_iota(jnp.int32, (rows, band), 1)
    diff = qpos - kpos
    valid = (kpos >= 0) & (kpos < DEC_SEQ) & (diff <= WINDOW) & (diff >= -WINDOW)
    for g in range(N_KV_HEADS):
        cols = slice(g * HEAD_DIM, (g + 1) * HEAD_DIM)
        q = _stack_heads(q_ref, g)
        k_loc = jnp.concatenate([kp_ref[:, cols], kc_ref[:, cols], kn_ref[:, cols]], axis=0).astype(BF16)
        v_loc = jnp.concatenate([vp_ref[:, cols], vc_ref[:, cols], vn_ref[:, cols]], axis=0).astype(BF16)
        k_ctx = ck_ref[:, cols].astype(BF16)
        v_ctx = cv_ref[:, cols].astype(BF16)
        sink = _sink_column(sink_ref, g, Q_BLOCK)
        s_loc = jnp.where(valid, _qk(q, k_loc), -jnp.inf)
        s_ctx = _qk(q, k_ctx)
        m = jnp.maximum(jnp.maximum(jnp.max(s_loc, axis=-1, keepdims=True),
                                    jnp.max(s_ctx, axis=-1, keepdims=True)), sink)
        p_loc = jnp.exp(s_loc - m)
        p_ctx = jnp.exp(s_ctx - m)
        denom = (jnp.sum(p_loc, axis=-1, keepdims=True) + jnp.sum(p_ctx, axis=-1, keepdims=True)
                 + jnp.exp(sink - m))
        o = (jnp.dot(p_loc.astype(BF16), v_loc, preferred_element_type=F32)
             + jnp.dot(p_ctx.astype(BF16), v_ctx, preferred_element_type=F32)) / denom
        for r in range(GQA_GROUP):
            hd = g * GQA_GROUP + r
            o_ref[:, hd * HEAD_DIM:(hd + 1) * HEAD_DIM] = o[r * Q_BLOCK:(r + 1) * Q_BLOCK].astype(BF16)


def _lat_attn(q, k, v, cache_k, cache_v, sink):
    nb = DEC_SEQ // Q_BLOCK
    cur = lambda b, j: (b * nb + j, 0)
    prev = lambda b, j: (b * nb + jnp.maximum(j - 1, 0), 0)
    nxt = lambda b, j: (b * nb + jnp.minimum(j + 1, nb - 1), 0)
    kvb = lambda im: pl.BlockSpec((Q_BLOCK, KV_W), im)
    cache = pl.BlockSpec((None, PAST_LEN, KV_W), lambda b, j: (b, 0, 0))
    return pl.pallas_call(
        _lat_attn_kernel,
        grid=(DEC_BATCH, nb),
        in_specs=[
            pl.BlockSpec(memory_space=pltpu.SMEM),
            pl.BlockSpec((Q_BLOCK, ATTN_W), cur),
            kvb(prev), kvb(cur), kvb(nxt), kvb(prev), kvb(cur), kvb(nxt),
            cache, cache,
        ],
        out_specs=pl.BlockSpec((Q_BLOCK, ATTN_W), cur),
        out_shape=jax.ShapeDtypeStruct((T_LAT, ATTN_W), BF16),
        compiler_params=_params("arbitrary", "arbitrary"),
        name="lat_attn",
    )(sink, q, k, k, k, v, v, v, cache_k, cache_v)


def _rnn_kernel(seq, xr_ref, cw_ref, cb_ref, w_ref, b_ref, lam_ref, h0_ref,
                y_ref, hl_ref, xt_ref, yt_ref, a_ref, u_ref):
    n_chunks = seq // RNN_CHUNK
    rows = RNN_CHUNK * SUBLANES
    halo = jnp.zeros((CONV_LEFT, SUBLANES, RNN_CB), F32)
    xt_ref[0:CONV_LEFT] = halo
    xt_ref[seq + CONV_LEFT:seq + 2 * CONV_LEFT] = halo

    def load_chunk(c, carry):
        t0 = pl.multiple_of(c * RNN_CHUNK, RNN_CHUNK)
        x = xr_ref[:, pl.ds(t0, RNN_CHUNK), :].astype(F32)
        xt_ref[pl.ds(t0 + CONV_LEFT, RNN_CHUNK)] = jnp.swapaxes(x, 0, 1)
        return carry

    lax.fori_loop(0, n_chunks, load_chunk, 0)

    def conv_chunk(t0):
        acc = cb_ref[...].reshape(1, 1, RNN_CB)
        for tap in range(CONV_W):
            acc = acc + xt_ref[pl.ds(t0 + tap, RNN_CHUNK)] * cw_ref[tap:tap + 1, :].reshape(1, 1, RNN_CB)
        return acc

    for d in range(2):
        decay = (-0.5 * LRU_C * LOG2_E) * jax.nn.softplus(-lam_ref[d:d + 1, :])
        half_bias = b_ref[d:d + 1, :]

        def chunk(ci, h, d=d, decay=decay, half_bias=half_bias):
            c = ci if d == 0 else n_chunks - 1 - ci
            t0 = pl.multiple_of(c * RNN_CHUNK, RNN_CHUNK)
            xc = conv_chunk(t0).reshape(rows, RNN_CB)
            z = jnp.dot(xc.astype(BF16), w_ref[d], preferred_element_type=F32) + half_bias
            a = jnp.exp2(decay * jnp.tanh(z[:, :RNN_CB]) + decay)
            m = 1.0 - a * a
            mult = jnp.where(m == 0.0, 0.0, m * lax.rsqrt(m))
            u = mult * ((jnp.tanh(z[:, RNN_CB:]) + 1.0) * (0.5 * xc))
            a_ref[...] = a.reshape(RNN_CHUNK, SUBLANES, RNN_CB)
            u_ref[...] = u.reshape(RNN_CHUNK, SUBLANES, RNN_CB)

            def steps(gi, h):
                s0 = pl.multiple_of((gi if d == 0 else RNN_CHUNK // RNN_UNROLL - 1 - gi) * RNN_UNROLL, RNN_UNROLL)
                for j in (range(RNN_UNROLL) if d == 0 else reversed(range(RNN_UNROLL))):
                    h = a_ref[s0 + j] * h + u_ref[s0 + j]
                    if d == 0:
                        yt_ref[t0 + s0 + j] = h
                    else:
                        yt_ref[t0 + s0 + j] = yt_ref[t0 + s0 + j] + h
                return h

            return lax.fori_loop(0, RNN_CHUNK // RNN_UNROLL, steps, h)

        hl_ref[d] = lax.fori_loop(0, n_chunks, chunk, h0_ref[d])

    def store_chunk(c, carry):
        t0 = pl.multiple_of(c * RNN_CHUNK, RNN_CHUNK)
        y = jnp.swapaxes(yt_ref[pl.ds(t0, RNN_CHUNK)], 0, 1)
        y_ref[:, pl.ds(t0, RNN_CHUNK), :] = y.astype(BF16)
        return carry

    lax.fori_loop(0, n_chunks, store_chunk, 0)


def _rnn(stream, xr, conv_w, conv_b, w_blk, b_blk, lam, h0, layer):
    n_cb = D_RNN // RNN_CB
    seq = stream.seq
    n_groups = stream.batch // SUBLANES
    return pl.pallas_call(
        functools.partial(_rnn_kernel, seq),
        grid=(n_groups, n_cb),
        in_specs=[
            pl.BlockSpec((SUBLANES, seq, RNN_CB), lambda g, j: (g, 0, j)),
            pl.BlockSpec((None, CONV_W, RNN_CB), lambda g, j: (layer, 0, j)),
            pl.BlockSpec((None, 1, RNN_CB), lambda g, j: (layer, 0, j)),
            pl.BlockSpec((None, None, 2, RNN_CB, 2 * RNN_CB), lambda g, j: (layer, j, 0, 0, 0)),
            pl.BlockSpec((None, None, 2, 2 * RNN_CB), lambda g, j: (layer, j, 0, 0)),
            pl.BlockSpec((None, 2, RNN_CB), lambda g, j: (layer, 0, j)),
            pl.BlockSpec((None, 2, SUBLANES, RNN_CB), lambda g, j: (g, 0, 0, j)),
        ],
        out_specs=[
            pl.BlockSpec((SUBLANES, seq, RNN_CB), lambda g, j: (g, 0, j)),
            pl.BlockSpec((None, 2, SUBLANES, RNN_CB), lambda g, j: (g, 0, 0, j)),
        ],
        out_shape=[
            jax.ShapeDtypeStruct((n_groups * SUBLANES, seq, D_RNN), BF16),
            jax.ShapeDtypeStruct((n_groups, 2, SUBLANES, D_RNN), F32),
        ],
        scratch_shapes=[
            pltpu.VMEM((seq + 2 * CONV_LEFT, SUBLANES, RNN_CB), F32),
            pltpu.VMEM((seq, SUBLANES, RNN_CB), F32),
            pltpu.VMEM((RNN_CHUNK, SUBLANES, RNN_CB), F32),
            pltpu.VMEM((RNN_CHUNK, SUBLANES, RNN_CB), F32),
        ],
        compiler_params=_params("arbitrary", "arbitrary"),
        name=f"rnn_{stream.name}",
    )(xr.reshape(stream.batch, seq, D_RNN), conv_w, conv_b.reshape(DEPTH, 1, D_RNN), w_blk, b_blk, lam, h0)


def _lru_block_weights(lru_wa, lru_wx, lru_ba, lru_bx):
    n_cb = D_RNN // RNN_CB
    per = RNN_CB // LRU_BW

    def dense(w):
        w = w.reshape(DEPTH, 2, n_cb, per, LRU_BW, LRU_BW)
        eye = jnp.eye(per, dtype=w.dtype)
        full = jnp.einsum("ldcpkj,pq->ldcpkqj", w, eye)
        return full.reshape(DEPTH, 2, n_cb, RNN_CB, RNN_CB)

    w = jnp.concatenate([dense(lru_wa), dense(lru_wx)], axis=-1)
    w = jnp.transpose(0.5 * w, (0, 2, 1, 3, 4)).astype(BF16)
    b = jnp.concatenate([lru_ba.reshape(DEPTH, 2, n_cb, RNN_CB),
                         lru_bx.reshape(DEPTH, 2, n_cb, RNN_CB)], axis=-1)
    return w, jnp.transpose(0.5 * b, (0, 2, 1, 3))


def _route(scores, sel):
    grp_score = []
    for g in range(N_GROUPS):
        a, b, c, d = sel[g * EXPERTS_PER_GROUP:(g + 1) * EXPERTS_PER_GROUP]
        hi1, lo1 = jnp.maximum(a, b), jnp.minimum(a, b)
        hi2, lo2 = jnp.maximum(c, d), jnp.minimum(c, d)
        grp_score.append(jnp.maximum(hi1, hi2) + jnp.maximum(jnp.minimum(hi1, hi2), jnp.maximum(lo1, lo2)))
    best = jnp.zeros_like(grp_score[0], dtype=jnp.int32)
    best_val = grp_score[0]
    for g in range(1, N_GROUPS):
        better = grp_score[g] > best_val
        best = jnp.where(better, g, best)
        best_val = jnp.where(better, grp_score[g], best_val)
    chosen = []
    for e in range(N_EXPERTS):
        g = e // EXPERTS_PER_GROUP
        rank = jnp.zeros_like(best)
        for o in range(g * EXPERTS_PER_GROUP, (g + 1) * EXPERTS_PER_GROUP):
            if o == e:
                continue
            ahead = (sel[o] >= sel[e]) if o < e else (sel[o] > sel[e])
            rank = rank + jnp.where(ahead, 1, 0)
        chosen.append(jnp.where(best == g, rank, 2) < 2)
    taken, gate = [], []
    for j in range(EXPERTS_PER_GROUP):
        t = jnp.zeros_like(best)
        s = jnp.zeros_like(scores[0])
        for g in range(N_GROUPS):
            e = g * EXPERTS_PER_GROUP + j
            t = t + jnp.where(chosen[e], 1, 0)
            s = s + jnp.where(chosen[e], scores[e], 0.0)
        taken.append(t > 0)
        gate.append(s)
    total = gate[0] + gate[1] + gate[2] + gate[3]
    pair = jnp.where(taken[0], jnp.where(taken[1], 0, jnp.where(taken[2], 1, 2)),
                     jnp.where(taken[1], jnp.where(taken[2], 3, 4), 5))
    w_lo = jnp.where(taken[0], gate[0], jnp.where(taken[1], gate[1], gate[2])) / total
    w_hi = jnp.where(taken[3], gate[3], jnp.where(taken[2], gate[2], gate[1])) / total
    return best * PAIRS_PER_GROUP + pair, w_lo, w_hi


def _merge_kernel(x_ref, att_ref, rnn_ref, gr_ref, ga_ref, gb_ref,
                  g1_ref, sh2_ref, sc2_ref, n2_ref, wap_ref, wrp_ref, wo_ref, rw_ref, rb_ref,
                  xo_ref, h2_ref, bucket_ref):
    att = jnp.dot(att_ref[...], wap_ref[...], preferred_element_type=F32)
    gated = (jax.nn.gelu(gr_ref[...].astype(F32)) * rnn_ref[...].astype(F32)).astype(BF16)
    rnn = jnp.dot(gated, wrp_ref[...], preferred_element_type=F32)
    merged = _sigmoid(ga_ref[...].astype(F32)) * att + _sigmoid(gb_ref[...].astype(F32)) * rnn
    x = x_ref[...] + g1_ref[...] * jnp.dot(merged.astype(BF16), wo_ref[...], preferred_element_type=F32)
    xo_ref[...] = x
    y = x * lax.rsqrt(jnp.mean(x * x, axis=-1, keepdims=True) + EPS) * n2_ref[...]
    h2 = y * (1.0 + sc2_ref[...]) + sh2_ref[...]
    h2_ref[:, :D_MODEL] = h2
    h_hi = h2.astype(BF16)
    h_lo = (h2 - h_hi.astype(F32)).astype(BF16)
    rw = rw_ref[...]
    w_hi = rw.astype(BF16)
    w_lo = (rw - w_hi.astype(F32)).astype(BF16)
    logits = _qk(w_hi, h_hi) + (_qk(w_hi, h_lo) + _qk(w_lo, h_hi))
    score = _sigmoid(logits)
    sel = score + rb_ref[...]
    bucket, w_lo, w_hi = _route([score[e:e + 1, :] for e in range(N_EXPERTS)],
                                [sel[e:e + 1, :] for e in range(N_EXPERTS)])
    bucket_ref[...] = bucket
    pad = jnp.zeros((LANES - 2, MM_TILE), F32)
    h2_ref[:, D_MODEL:] = jnp.concatenate([w_lo, w_hi, pad], axis=0).T


def _merge(stream, x, att, rnn, gr, ga, gb, mods, norm_g, wap, wrp, wo, router_wt, router_b, layer):
    tok = lambda i: (i, 0)
    wide = pl.BlockSpec((MM_TILE, D_MODEL), tok)
    mod = lambda col: pl.BlockSpec((None, 1, D_MODEL),
                                   lambda i: (layer * COND_ROWS + stream.mod_row(i, MM_TILE), 0, col))
    mat = pl.BlockSpec((None, D_MODEL, D_MODEL), lambda i: (layer, 0, 0))
    return pl.pallas_call(
        _merge_kernel,
        grid=(stream.rows // MM_TILE,),
        in_specs=[
            wide, wide, wide, wide, wide, wide,
            mod(2), mod(3), mod(4),
            pl.BlockSpec((None, 1, D_MODEL), lambda i: (layer, 0, 0)),
            mat, mat, mat,
            pl.BlockSpec((N_EXPERTS, D_MODEL), lambda i: (0, 0)),
            pl.BlockSpec((N_EXPERTS, 1), lambda i: (0, 0)),
        ],
        out_specs=[wide, pl.BlockSpec((MM_TILE, ROW_W), tok),
                   pl.BlockSpec((None, 1, MM_TILE), lambda i: (i, 0, 0))],
        out_shape=[
            jax.ShapeDtypeStruct((stream.rows, D_MODEL), F32),
            jax.ShapeDtypeStruct((stream.rows, ROW_W), F32),
            jax.ShapeDtypeStruct((stream.rows // MM_TILE, 1, MM_TILE), jnp.int32),
        ],
        compiler_params=_params("arbitrary"),
        name=f"merge_{stream.name}",
    )(x, att, rnn.reshape(stream.rows, D_RNN), gr, ga, gb, mods, mods, mods,
      norm_g.reshape(DEPTH, 1, D_MODEL), wap, wrp, wo, router_wt, router_b.reshape(N_EXPERTS, 1))


def _plan_kernel(b_ref, pos_ref, tiles_ref):
    b = b_ref[...]
    r = lax.broadcasted_iota(jnp.int32, (TOK_TILE, TOK_TILE), 0)
    c = lax.broadcasted_iota(jnp.int32, (TOK_TILE, TOK_TILE), 1)
    before_in_tile = jnp.where(r < c, 1.0, 0.0).astype(BF16)
    br = lax.broadcasted_iota(jnp.int32, (N_TOK_TILES, N_TOK_TILES), 0)
    bc = lax.broadcasted_iota(jnp.int32, (N_TOK_TILES, N_TOK_TILES), 1)
    earlier_tiles = jnp.where(bc < br, 1.0, 0.0).astype(BF16)
    lane = lax.broadcasted_iota(jnp.int32, (1, LANES), 1)
    tile_start = (lane * MOE_TM).astype(F32)
    start = jnp.zeros((1, 1), F32)
    pos = jnp.zeros(b.shape, F32)
    tile_bucket = jnp.zeros((1, LANES), F32)
    for k in range(N_BUCKETS):
        mask = jnp.where(b == k, 1.0, 0.0)
        rank = jnp.dot(mask.astype(BF16), before_in_tile, preferred_element_type=F32)
        per_tile = jnp.sum(mask, axis=1, keepdims=True)
        tile_off = jnp.dot(earlier_tiles, jnp.broadcast_to(per_tile, (N_TOK_TILES, LANES)).astype(BF16),
                           preferred_element_type=F32)[:, :1]
        total = jnp.sum(per_tile, axis=0, keepdims=True)
        padded = jnp.floor((total + (MOE_TM - 1)) * (1.0 / MOE_TM)) * MOE_TM
        pos = pos + mask * (start + tile_off + rank)
        start = start + padded
        tile_bucket = tile_bucket + jnp.where(tile_start >= start, 1.0, 0.0)
    pos_ref[...] = pos.astype(jnp.int32)
    tiles_ref[...] = jnp.where(lane == LANES - 1, start * (1.0 / MOE_TM), tile_bucket).astype(jnp.int32)


def _plan(bucket_lat, bucket_ctx):
    bucket = jnp.concatenate([bucket_lat.reshape(-1, TOK_TILE), bucket_ctx.reshape(-1, TOK_TILE)], axis=0)
    return pl.pallas_call(
        _plan_kernel,
        out_shape=[
            jax.ShapeDtypeStruct((N_TOK_TILES, TOK_TILE), jnp.int32),
            jax.ShapeDtypeStruct((1, LANES), jnp.int32),
        ],
        compiler_params=pltpu.CompilerParams(vmem_limit_bytes=VMEM_LIMIT),
        name="plan",
    )(bucket)


def _row_copy(src, dst, sem):
    return pltpu.make_async_copy(src, dst, sem)


def _dispatch_kernel(pos_ref, h_ref, xs_in_ref, xs_ref, sem):
    del xs_in_ref

    for r in range(TOK_TILE):
        _row_copy(h_ref.at[pl.ds(r, 1), :], xs_ref.at[pl.ds(pos_ref[0, r], 1), :], sem).start(priority=r % 2)

    def wait(r, carry):
        _row_copy(h_ref.at[pl.ds(0, 1), :], xs_ref.at[pl.ds(0, 1), :], sem).wait()
        return carry

    lax.fori_loop(0, TOK_TILE, wait, 0, unroll=8)


def _dispatch(stream, pos, h2, xs):
    return pl.pallas_call(
        _dispatch_kernel,
        grid=(stream.n_tiles,),
        in_specs=[
            pl.BlockSpec((None, 1, TOK_TILE), lambda i: (stream.first_tile + i, 0, 0), memory_space=pltpu.SMEM),
            pl.BlockSpec((TOK_TILE, ROW_W), lambda i: (i, 0)),
            pl.BlockSpec(memory_space=pl.ANY),
        ],
        out_specs=pl.BlockSpec(memory_space=pl.ANY),
        out_shape=jax.ShapeDtypeStruct((MOE_ROWS, ROW_W), F32),
        input_output_aliases={2: 0},
        scratch_shapes=[pltpu.SemaphoreType.DMA(())],
        compiler_params=_params("arbitrary"),
        name=f"dispatch_{stream.name}",
    )(pos, h2, xs)


def _bucket_expert(k, hi):
    g = k // PAIRS_PER_GROUP
    p = k % PAIRS_PER_GROUP
    lo_idx = jnp.where(p >= 3, 1, 0) + jnp.where(p >= 5, 1, 0)
    hi_idx = jnp.where(p < 3, p + 1, jnp.where(p < 5, p - 1, 3))
    return g * EXPERTS_PER_GROUP + (hi_idx if hi else lo_idx)


def _weight_copies(layer, wgu_ref, wd_ref, wgu_f, wd_f, sem, bucket, slot):
    out = []
    for which in range(2):
        e = _bucket_expert(bucket, which == 1)
        out.append(pltpu.make_async_copy(wgu_ref.at[layer, e], wgu_f.at[slot, which], sem.at[slot, 2 * which]))
        out.append(pltpu.make_async_copy(wd_ref.at[layer, e], wd_f.at[slot, which], sem.at[slot, 2 * which + 1]))
    return out


def _experts_kernel(layer, tiles_ref, xs_ref, wgu_ref, wd_ref, y_ref, wgu_f, wd_f, wgu_b, wd_b, sem, count_ref):
    i = pl.program_id(0)
    n_used = tiles_ref[0, LANES - 1]
    in_use = i < n_used
    cur = tiles_ref[0, i]
    prev = tiles_ref[0, jnp.maximum(i - 1, 0)]
    copies = functools.partial(_weight_copies, layer, wgu_ref, wd_ref, wgu_f, wd_f, sem)

    def for_each_copy(bucket, slot, lo_changed, act):
        lo_gu, lo_d, hi_gu, hi_d = copies(bucket, slot)
        act(hi_gu)
        act(hi_d)

        @pl.when(lo_changed)
        def _():
            act(lo_gu)
            act(lo_d)

    @pl.when(jnp.logical_and(in_use, jnp.logical_or(i == 0, cur != prev)))
    def _():
        @pl.when(i == 0)
        def _():
            count_ref[0] = 0
            for_each_copy(cur, 0, True, lambda c: c.start())

        slot = count_ref[0] % 2
        lo_changed = jnp.logical_or(i == 0, _bucket_expert(cur, False) != _bucket_expert(prev, False))
        for_each_copy(cur, slot, lo_changed, lambda c: c.wait())
        nxt_i = lax.while_loop(lambda j: jnp.logical_and(j < n_used, tiles_ref[0, jnp.minimum(j, n_used - 1)] == cur),
                               lambda j: j + 1, i + 1)

        @pl.when(nxt_i < n_used)
        def _():
            nxt = tiles_ref[0, nxt_i]
            for_each_copy(nxt, 1 - slot, _bucket_expert(nxt, False) != _bucket_expert(cur, False),
                          lambda c: c.start())

        wgu_b[1] = wgu_f[slot, 1].astype(BF16)
        wd_b[1] = wd_f[slot, 1].astype(BF16)

        @pl.when(lo_changed)
        def _():
            wgu_b[0] = wgu_f[slot, 0].astype(BF16)
            wd_b[0] = wd_f[slot, 0].astype(BF16)

        count_ref[0] = count_ref[0] + 1

    @pl.when(jnp.logical_not(in_use))
    def _():
        y_ref[...] = jnp.zeros_like(y_ref)

    @pl.when(in_use)
    def _():
        x = xs_ref[:, :D_MODEL].astype(BF16)

        def branch(which, gate):
            gu = jnp.dot(x, wgu_b[which], preferred_element_type=F32)
            act = jax.nn.silu(gu[:, :D_EXPERT]) * gu[:, D_EXPERT:] * gate
            return jnp.dot(act.astype(BF16), wd_b[which], preferred_element_type=F32)

        y_ref[...] = (branch(0, xs_ref[:, D_MODEL:D_MODEL + 1])
                      + branch(1, xs_ref[:, D_MODEL + 1:D_MODEL + 2]))


def _experts(tiles, xs, wgu, wd, layer):
    return pl.pallas_call(
        functools.partial(_experts_kernel, layer),
        grid_spec=pltpu.PrefetchScalarGridSpec(
            num_scalar_prefetch=1,
            grid=(N_MOE_TILES,),
            in_specs=[
                pl.BlockSpec((MOE_TM, ROW_W), lambda i, t: (i, 0)),
                pl.BlockSpec(memory_space=pl.ANY),
                pl.BlockSpec(memory_space=pl.ANY),
            ],
            out_specs=pl.BlockSpec((MOE_TM, D_MODEL), lambda i, t: (i, 0)),
            scratch_shapes=[
                pltpu.VMEM((2, 2, D_MODEL, 2 * D_EXPERT), F32), pltpu.VMEM((2, 2, D_EXPERT, D_MODEL), F32),
                pltpu.VMEM((2, D_MODEL, 2 * D_EXPERT), BF16), pltpu.VMEM((2, D_EXPERT, D_MODEL), BF16),
                pltpu.SemaphoreType.DMA((2, 4)), pltpu.SMEM((1,), jnp.int32),
            ],
        ),
        out_shape=jax.ShapeDtypeStruct((MOE_ROWS, D_MODEL), F32),
        compiler_params=_params("arbitrary"),
        name="experts",
    )(tiles, xs, wgu, wd)


def _combine_kernel(n_tiles, final, pos_ref, next_pos_ref, x_ref, g2_ref, fg_ref, y_ref, o_ref, rows_ref, sem):
    i = pl.program_id(0)

    def gather(p_ref, into):
        for r in range(TOK_TILE):
            _row_copy(y_ref.at[pl.ds(p_ref[0, r], 1), :], rows_ref.at[into, pl.ds(r, 1), :],
                      sem.at[into]).start(priority=r % 2)

    def step(slot):
        if slot == 0:
            @pl.when(i == 0)
            def _():
                gather(pos_ref, 0)

        @pl.when(i + 1 < n_tiles)
        def _():
            gather(next_pos_ref, 1 - slot)

        def wait(r, carry):
            _row_copy(y_ref.at[pl.ds(0, 1), :], rows_ref.at[slot, pl.ds(0, 1), :], sem.at[slot]).wait()
            return carry

        lax.fori_loop(0, TOK_TILE, wait, 0, unroll=8)
        x = x_ref[...] + g2_ref[...] * rows_ref[slot]
        if final:
            x = x * lax.rsqrt(jnp.mean(x * x, axis=-1, keepdims=True) + EPS) * fg_ref[...]
        o_ref[...] = x

    for parity in range(2):
        pl.when(i % 2 == parity)(functools.partial(step, parity))


def _combine(stream, pos, x, mods, y, final_g, layer):
    first, last = stream.first_tile, stream.first_tile + stream.n_tiles - 1
    return pl.pallas_call(
        functools.partial(_combine_kernel, stream.n_tiles, layer == DEPTH - 1),
        grid=(stream.n_tiles,),
        in_specs=[
            pl.BlockSpec((None, 1, TOK_TILE), lambda i: (first + i, 0, 0), memory_space=pltpu.SMEM),
            pl.BlockSpec((None, 1, TOK_TILE), lambda i: (jnp.minimum(first + i + 1, last), 0, 0),
                         memory_space=pltpu.SMEM),
            pl.BlockSpec((TOK_TILE, D_MODEL), lambda i: (i, 0)),
            pl.BlockSpec((None, 1, D_MODEL), lambda i: (layer * COND_ROWS + stream.mod_row(i, TOK_TILE), 0, 5)),
            pl.BlockSpec((1, D_MODEL), lambda i: (0, 0)),
            pl.BlockSpec(memory_space=pl.ANY),
        ],
        out_specs=pl.BlockSpec((TOK_TILE, D_MODEL), lambda i: (i, 0)),
        out_shape=jax.ShapeDtypeStruct((stream.rows, D_MODEL), F32),
        scratch_shapes=[pltpu.VMEM((2, TOK_TILE, D_MODEL), F32), pltpu.SemaphoreType.DMA((2,))],
        compiler_params=_params("arbitrary"),
        name=f"combine_{stream.name}",
    )(pos, pos, x, mods, final_g.reshape(1, D_MODEL), y)


def _rope_tables():
    t = jnp.arange(DEC_SEQ, dtype=jnp.int32)
    row = (t // GRID_W).astype(F32)
    col = (t % GRID_W).astype(F32)
    inv = ROPE_BASE ** (-jnp.arange(ROPE_PAIRS_PER_AXIS, dtype=F32) / ROPE_PAIRS_PER_AXIS)
    ang = jnp.concatenate([row[:, None] * inv, col[:, None] * inv], axis=-1)
    cos, sin = jnp.cos(ang), jnp.sin(ang)
    return jnp.concatenate([cos, cos], axis=-1), jnp.concatenate([-sin, sin], axis=-1)


def kernel(x_prompt, x_sample, cache_k, cache_v, state_h, c, c_ctx, norm1_g, norm2_g, w_ada, b_ada, w_in, conv_w, conv_b, lru_wa, lru_ba, lru_wx, lru_bx, lru_lambda, attn_sink, w_attn_proj, w_rnn_proj, w_out, router_w, router_b, w_gate_up, w_down, final_norm_g):
    xs_tok = {LATENT: x_sample.reshape(T_LAT, D_MODEL), CONTEXT: x_prompt.reshape(T_CTX, D_MODEL)}
    cond = jnp.concatenate([c_ctx[None, :], c, jnp.zeros((COND_ROWS - 1 - DEC_BATCH, D_MODEL), F32)], axis=0)
    mods = _ada(cond, w_ada, b_ada).reshape(DEPTH * COND_ROWS, 1, N_ADA * D_MODEL)
    rope_cos, rope_sin = _rope_tables()

    w_in_b = w_in.astype(BF16)
    wap_b = w_attn_proj.astype(BF16)
    wrp_b = w_rnn_proj.astype(BF16)
    wo_b = w_out.astype(BF16)
    lru_w, lru_b = _lru_block_weights(lru_wa, lru_wx, lru_ba, lru_bx)
    router_wt = router_w.T

    h0 = {LATENT: jnp.transpose(state_h.astype(F32), (1, 2, 0, 3)).reshape(DEPTH, 1, 2, DEC_BATCH, D_RNN),
          CONTEXT: jnp.zeros((DEPTH, BATCH // SUBLANES, 2, SUBLANES, D_RNN), F32)}
    ck = jnp.transpose(cache_k, (1, 0, 2, 3, 4)).reshape(DEPTH, DEC_BATCH, PAST_LEN, KV_W)
    cv = jnp.transpose(cache_v, (1, 0, 2, 3, 4)).reshape(DEPTH, DEC_BATCH, PAST_LEN, KV_W)

    new_k, new_v, new_h = [], [], []
    for l in range(DEPTH):
        merged = {}
        for stream in (LATENT, CONTEXT):
            q, k, v, xr, gr, ga, gb = _inproj(stream, xs_tok[stream], norm1_g, mods, rope_cos, rope_sin,
                                              w_in_b, l)
            if stream.rotary:
                att = _lat_attn(q, k, v, ck[l], cv[l], attn_sink[l])
            else:
                att = _ctx_attn(q, k, v, attn_sink[l])
                new_k.append(k.reshape(BATCH, SEQ, N_KV_HEADS, HEAD_DIM))
                new_v.append(v.reshape(BATCH, SEQ, N_KV_HEADS, HEAD_DIM))
            y, h_last = _rnn(stream, xr, conv_w, conv_b, lru_w, lru_b, lru_lambda, h0[stream][l], l)
            if not stream.rotary:
                new_h.append(jnp.transpose(h_last, (0, 2, 1, 3)).reshape(BATCH, 2, D_RNN))
            merged[stream] = _merge(stream, xs_tok[stream], att, y, gr, ga, gb, mods, norm2_g,
                                    wap_b, wrp_b, wo_b, router_wt, router_b, l)
        pos, tiles = _plan(merged[LATENT][2], merged[CONTEXT][2])
        pos = pos.reshape(N_TOK_TILES, 1, TOK_TILE)
        xs = jnp.zeros((MOE_ROWS, ROW_W), F32)
        for stream in (LATENT, CONTEXT):
            xs = _dispatch(stream, pos, merged[stream][1], xs)
        y_sorted = _experts(tiles, xs, w_gate_up, w_down, l)
        for stream in (LATENT, CONTEXT):
            xs_tok[stream] = _combine(stream, pos, merged[stream][0], mods, y_sorted, final_norm_g, l)

    return (xs_tok[CONTEXT].reshape(BATCH, SEQ, D_MODEL), xs_tok[LATENT].reshape(DEC_BATCH, DEC_SEQ, D_MODEL),
            jnp.stack(new_k, axis=1), jnp.stack(new_v, axis=1),
            jnp.stack(new_h, axis=1).astype(x_prompt.dtype))
```

```python
import functools
from typing import NamedTuple

import jax
import jax.numpy as jnp
from jax import lax
from jax.experimental import pallas as pl
from jax.experimental.pallas import tpu as pltpu

F32 = jnp.float32
BF16 = jnp.bfloat16

D_MODEL = 1024
BATCH = 16
SEQ = 256
DEPTH = 2
DEC_BATCH = 8
DEC_SEQ = 1024
PAST_LEN = 512
GRID_W = 64
N_HEADS = 8
N_KV_HEADS = 2
GQA_GROUP = N_HEADS // N_KV_HEADS
HEAD_DIM = 128
ATTN_W = N_HEADS * HEAD_DIM
KV_W = N_KV_HEADS * HEAD_DIM
WINDOW = 128
Q_BLOCK = 128
ROPE_BASE = 10000.0
ROPE_PAIRS_PER_AXIS = HEAD_DIM // 4
D_RNN = 1024
LRU_BLOCKS = 16
LRU_BW = D_RNN // LRU_BLOCKS
LRU_C = 8.0
CONV_W = 4
CONV_LEFT = 2
N_EXPERTS = 16
N_GROUPS = 4
EXPERTS_PER_GROUP = N_EXPERTS // N_GROUPS
D_EXPERT = 512
N_ADA = 6
EPS = 1e-6
IN_COLS = ATTN_W + 2 * KV_W + 2 * D_RNN + 2 * D_MODEL

T_LAT = DEC_BATCH * DEC_SEQ
T_CTX = BATCH * SEQ
T_ALL = T_LAT + T_CTX

SUBLANES = 8
LANES = 128
VMEM_LIMIT = 56 * 1024 * 1024

TOK_TILE = 256
MM_TILE = 512
N_TOK_TILES = T_ALL // TOK_TILE
COND_ROWS = 16


class _Stream(NamedTuple):
    name: str
    batch: int
    seq: int
    first_tile: int
    rotary: bool

    @property
    def rows(self):
        return self.batch * self.seq

    @property
    def n_tiles(self):
        return self.rows // TOK_TILE

    def mod_row(self, i, tile):
        return 1 + i // (self.seq // tile) if self.rotary else 0


LATENT = _Stream("lat", DEC_BATCH, DEC_SEQ, 0, True)
CONTEXT = _Stream("ctx", BATCH, SEQ, T_LAT // TOK_TILE, False)

RNN_CB = 256
RNN_CHUNK = 64
RNN_UNROLL = 8
LOG2_E = 1.4426950408889634
PAIR_SLOTS = ((0, 1), (0, 2), (0, 3), (1, 3), (1, 2), (3, 2))
PAIRS_PER_GROUP = len(PAIR_SLOTS)
N_BUCKETS = N_GROUPS * PAIRS_PER_GROUP
ROW_W = D_MODEL + LANES
MOE_TM = 256
N_MOE_TILES = (T_ALL + N_BUCKETS * (MOE_TM - 1) + MOE_TM - 1) // MOE_TM
MOE_ROWS = N_MOE_TILES * MOE_TM


def _params(*sem):
    return pltpu.CompilerParams(dimension_semantics=sem, vmem_limit_bytes=VMEM_LIMIT)


def _sigmoid(x):
    return 0.5 * jnp.tanh(0.5 * x) + 0.5


def _ada_kernel(cond_ref, w_ref, b_ref, o_ref):
    s = jax.nn.silu(cond_ref[...]).astype(BF16)
    o_ref[...] = jnp.dot(s, w_ref[...].astype(BF16), preferred_element_type=F32) + b_ref[...]


def _ada(cond, w_ada, b_ada):
    cols = N_ADA * D_MODEL
    tn = 1536
    return pl.pallas_call(
        _ada_kernel,
        grid=(DEPTH, cols // tn),
        in_specs=[
            pl.BlockSpec((COND_ROWS, D_MODEL), lambda l, j: (0, 0)),
            pl.BlockSpec((None, D_MODEL, tn), lambda l, j: (l, 0, j)),
            pl.BlockSpec((None, 1, tn), lambda l, j: (l, 0, j)),
        ],
        out_specs=pl.BlockSpec((None, COND_ROWS, tn), lambda l, j: (l, 0, j)),
        out_shape=jax.ShapeDtypeStruct((DEPTH, COND_ROWS, cols), F32),
        compiler_params=_params("arbitrary", "arbitrary"),
        name="ada",
    )(cond, w_ada, b_ada.reshape(DEPTH, 1, cols))


def _inproj_kernel(rotary, x_ref, g_ref, sh_ref, sc_ref, *refs):
    if rotary:
        cos_ref, sin_ref, w_ref, q_ref, k_ref, v_ref, xr_ref, gr_ref, ga_ref, gb_ref = refs
    else:
        w_ref, q_ref, k_ref, v_ref, xr_ref, gr_ref, ga_ref, gb_ref = refs
    x = x_ref[...]
    y = x * lax.rsqrt(jnp.mean(x * x, axis=-1, keepdims=True) + EPS) * g_ref[...]
    h = (y * (1.0 + sc_ref[...]) + sh_ref[...]).astype(BF16)

    def proj(lo, width):
        return jnp.dot(h, w_ref[:, lo:lo + width], preferred_element_type=F32)

    def rope(t):
        if not rotary:
            return t
        return t * cos_ref[...] + pltpu.roll(t, HEAD_DIM // 2, 1) * sin_ref[...]

    scale = HEAD_DIM ** -0.5
    qk = proj(0, ATTN_W + KV_W)
    for hd in range(N_HEADS):
        q = qk[:, hd * HEAD_DIM:(hd + 1) * HEAD_DIM]
        q_ref[:, hd * HEAD_DIM:(hd + 1) * HEAD_DIM] = (rope(q) * scale).astype(BF16)
    for g in range(N_KV_HEADS):
        k = qk[:, ATTN_W + g * HEAD_DIM:ATTN_W + (g + 1) * HEAD_DIM]
        k_ref[:, g * HEAD_DIM:(g + 1) * HEAD_DIM] = rope(k)
    v_ref[...] = proj(ATTN_W + KV_W, KV_W)
    base = ATTN_W + 2 * KV_W
    xr_ref[...] = proj(base, D_RNN).astype(BF16)
    gr_ref[...] = proj(base + D_RNN, D_RNN).astype(BF16)
    ga_ref[...] = proj(base + 2 * D_RNN, D_MODEL).astype(BF16)
    gb_ref[...] = proj(base + 2 * D_RNN + D_MODEL, D_MODEL).astype(BF16)


def _inproj(stream, x, norm_g, mods, rope_cos, rope_sin, w_in, layer):
    row = lambda i: layer * COND_ROWS + stream.mod_row(i, MM_TILE)
    tok = lambda i: (i, 0)
    rope_blk = lambda i: (i % (stream.seq // MM_TILE), 0)
    wide = pl.BlockSpec((MM_TILE, D_MODEL), tok)
    kv = pl.BlockSpec((MM_TILE, KV_W), tok)
    rope_specs = [pl.BlockSpec((MM_TILE, HEAD_DIM), rope_blk)] * 2 if stream.rotary else []
    rope_args = (rope_cos, rope_sin) if stream.rotary else ()
    return pl.pallas_call(
        functools.partial(_inproj_kernel, stream.rotary),
        grid=(stream.rows // MM_TILE,),
        in_specs=[
            wide,
            pl.BlockSpec((None, 1, D_MODEL), lambda i: (layer, 0, 0)),
            pl.BlockSpec((None, 1, D_MODEL), lambda i: (row(i), 0, 0)),
            pl.BlockSpec((None, 1, D_MODEL), lambda i: (row(i), 0, 1)),
        ] + rope_specs + [
            pl.BlockSpec((None, D_MODEL, IN_COLS), lambda i: (layer, 0, 0)),
        ],
        out_specs=[wide, kv, kv, wide, wide, wide, wide],
        out_shape=[
            jax.ShapeDtypeStruct((stream.rows, ATTN_W), BF16),
            jax.ShapeDtypeStruct((stream.rows, KV_W), F32),
            jax.ShapeDtypeStruct((stream.rows, KV_W), F32),
            jax.ShapeDtypeStruct((stream.rows, D_RNN), BF16),
            jax.ShapeDtypeStruct((stream.rows, D_RNN), BF16),
            jax.ShapeDtypeStruct((stream.rows, D_MODEL), BF16),
            jax.ShapeDtypeStruct((stream.rows, D_MODEL), BF16),
        ],
        compiler_params=_params("arbitrary"),
        name=f"inproj_{stream.name}",
    )(x, norm_g.reshape(DEPTH, 1, D_MODEL), mods, mods, *rope_args, w_in)


def _stack_heads(q_ref, g):
    lo = g * GQA_GROUP
    return jnp.concatenate(
        [q_ref[:, (lo + r) * HEAD_DIM:(lo + r + 1) * HEAD_DIM] for r in range(GQA_GROUP)], axis=0)


def _sink_column(sink_ref, g, rows):
    return jnp.concatenate(
        [jnp.full((rows, 1), sink_ref[g * GQA_GROUP + r], F32) for r in range(GQA_GROUP)], axis=0)


def _qk(q, k):
    return lax.dot_general(q, k, (((1,), (1,)), ((), ())), preferred_element_type=F32)


def _ctx_attn_kernel(sink_ref, q_ref, k_ref, v_ref, o_ref):
    for g in range(N_KV_HEADS):
        q = _stack_heads(q_ref, g)
        k = k_ref[:, g * HEAD_DIM:(g + 1) * HEAD_DIM].astype(BF16)
        v = v_ref[:, g * HEAD_DIM:(g + 1) * HEAD_DIM].astype(BF16)
        sink = _sink_column(sink_ref, g, SEQ)
        s = _qk(q, k)
        m = jnp.maximum(jnp.max(s, axis=-1, keepdims=True), sink)
        p = jnp.exp(s - m)
        denom = jnp.sum(p, axis=-1, keepdims=True) + jnp.exp(sink - m)
        o = jnp.dot(p.astype(BF16), v, preferred_element_type=F32) / denom
        for r in range(GQA_GROUP):
            hd = g * GQA_GROUP + r
            o_ref[:, hd * HEAD_DIM:(hd + 1) * HEAD_DIM] = o[r * SEQ:(r + 1) * SEQ].astype(BF16)


def _ctx_attn(q, k, v, sink):
    blk = lambda b: (b, 0)
    return pl.pallas_call(
        _ctx_attn_kernel,
        grid=(BATCH,),
        in_specs=[
            pl.BlockSpec(memory_space=pltpu.SMEM),
            pl.BlockSpec((SEQ, ATTN_W), blk),
            pl.BlockSpec((SEQ, KV_W), blk),
            pl.BlockSpec((SEQ, KV_W), blk),
        ],
        out_specs=pl.BlockSpec((SEQ, ATTN_W), lambda b: (b, 0)),
        out_shape=jax.ShapeDtypeStruct((T_CTX, ATTN_W), BF16),
        compiler_params=_params("arbitrary"),
        name="ctx_attn",
    )(sink, q, k, v)


def _lat_attn_kernel(sink_ref, q_ref, kp_ref, kc_ref, kn_ref, vp_ref, vc_ref, vn_ref,
                     ck_ref, cv_ref, o_ref):
    j = pl.program_id(1)
    rows = GQA_GROUP * Q_BLOCK
    band = Q_BLOCK + 2 * WINDOW
    qpos = j * Q_BLOCK + lax.broadcasted_iota(jnp.int32, (rows, band), 0) % Q_BLOCK
    kpos = j * Q_BLOCK - WINDOW + lax.broadcasted_iota(jnp.int32, (rows, band), 1)
    diff = qpos - kpos
    valid = (kpos >= 0) & (kpos < DEC_SEQ) & (diff <= WINDOW) & (diff >= -WINDOW)
    for g in range(N_KV_HEADS):
        cols = slice(g * HEAD_DIM, (g + 1) * HEAD_DIM)
        q = _stack_heads(q_ref, g)
        k_loc = jnp.concatenate([kp_ref[:, cols], kc_ref[:, cols], kn_ref[:, cols]], axis=0).astype(BF16)
        v_loc = jnp.concatenate([vp_ref[:, cols], vc_ref[:, cols], vn_ref[:, cols]], axis=0).astype(BF16)
        k_ctx = ck_ref[:, cols].astype(BF16)
        v_ctx = cv_ref[:, cols].astype(BF16)
        sink = _sink_column(sink_ref, g, Q_BLOCK)
        s_loc = jnp.where(valid, _qk(q, k_loc), -jnp.inf)
        s_ctx = _qk(q, k_ctx)
        m = jnp.maximum(jnp.maximum(jnp.max(s_loc, axis=-1, keepdims=True),
                                    jnp.max(s_ctx, axis=-1, keepdims=True)), sink)
        p_loc = jnp.exp(s_loc - m)
        p_ctx = jnp.exp(s_ctx - m)
        denom = (jnp.sum(p_loc, axis=-1, keepdims=True) + jnp.sum(p_ctx, axis=-1, keepdims=True)
                 + jnp.exp(sink - m))
        o = (jnp.dot(p_loc.astype(BF16), v_loc, preferred_element_type=F32)
             + jnp.dot(p_ctx.astype(BF16), v_ctx, preferred_element_type=F32)) / denom
        for r in range(GQA_GROUP):
            hd = g * GQA_GROUP + r
            o_ref[:, hd * HEAD_DIM:(hd + 1) * HEAD_DIM] = o[r * Q_BLOCK:(r + 1) * Q_BLOCK].astype(BF16)


def _lat_attn(q, k, v, cache_k, cache_v, sink):
    nb = DEC_SEQ // Q_BLOCK
    cur = lambda b, j: (b * nb + j, 0)
    prev = lambda b, j: (b * nb + jnp.maximum(j - 1, 0), 0)
    nxt = lambda b, j: (b * nb + jnp.minimum(j + 1, nb - 1), 0)
    kvb = lambda im: pl.BlockSpec((Q_BLOCK, KV_W), im)
    cache = pl.BlockSpec((None, PAST_LEN, KV_W), lambda b, j: (b, 0, 0))
    return pl.pallas_call(
        _lat_attn_kernel,
        grid=(DEC_BATCH, nb),
        in_specs=[
            pl.BlockSpec(memory_space=pltpu.SMEM),
            pl.BlockSpec((Q_BLOCK, ATTN_W), cur),
            kvb(prev), kvb(cur), kvb(nxt), kvb(prev), kvb(cur), kvb(nxt),
            cache, cache,
        ],
        out_specs=pl.BlockSpec((Q_BLOCK, ATTN_W), cur),
        out_shape=jax.ShapeDtypeStruct((T_LAT, ATTN_W), BF16),
        compiler_params=_params("arbitrary", "arbitrary"),
        name="lat_attn",
    )(sink, q, k, k, k, v, v, v, cache_k, cache_v)


def _rnn_kernel(seq, xr_ref, cw_ref, cb_ref, w_ref, b_ref, lam_ref, h0_ref,
                y_ref, hl_ref, xt_ref, yt_ref, a_ref, u_ref):
    n_chunks = seq // RNN_CHUNK
    rows = RNN_CHUNK * SUBLANES
    halo = jnp.zeros((CONV_LEFT, SUBLANES, RNN_CB), F32)
    xt_ref[0:CONV_LEFT] = halo
    xt_ref[seq + CONV_LEFT:seq + 2 * CONV_LEFT] = halo

    def load_chunk(c, carry):
        t0 = pl.multiple_of(c * RNN_CHUNK, RNN_CHUNK)
        x = xr_ref[:, pl.ds(t0, RNN_CHUNK), :].astype(F32)
        xt_ref[pl.ds(t0 + CONV_LEFT, RNN_CHUNK)] = jnp.swapaxes(x, 0, 1)
        return carry

    lax.fori_loop(0, n_chunks, load_chunk, 0)

    half_w = 0.5 * cw_ref[...]
    half_b = 0.5 * cb_ref[...]

    def conv_chunk(c, carry):
        t0 = pl.multiple_of(c * RNN_CHUNK, RNN_CHUNK)
        acc = half_b.reshape(1, 1, RNN_CB)
        for tap in range(CONV_W):
            acc = acc + xt_ref[pl.ds(t0 + tap, RNN_CHUNK)] * half_w[tap:tap + 1, :].reshape(1, 1, RNN_CB)
        xt_ref[pl.ds(t0, RNN_CHUNK)] = acc
        return carry

    lax.fori_loop(0, n_chunks, conv_chunk, 0)

    for d in range(2):
        decay = (-0.5 * LRU_C * LOG2_E) * jax.nn.softplus(-lam_ref[d:d + 1, :])
        half_bias = b_ref[d:d + 1, :]

        def chunk(ci, h, d=d, decay=decay, half_bias=half_bias):
            c = ci if d == 0 else n_chunks - 1 - ci
            t0 = pl.multiple_of(c * RNN_CHUNK, RNN_CHUNK)
            hx = xt_ref[pl.ds(t0, RNN_CHUNK)].reshape(rows, RNN_CB)
            z = jnp.dot(hx.astype(BF16), w_ref[d], preferred_element_type=F32) + half_bias
            a = jnp.exp2(decay * jnp.tanh(z[:, :RNN_CB]) + decay)
            m = 1.0 - a * a
            mult = jnp.where(m == 0.0, 0.0, m * lax.rsqrt(m))
            u = mult * ((jnp.tanh(z[:, RNN_CB:]) + 1.0) * hx)
            a_ref[...] = a.reshape(RNN_CHUNK, SUBLANES, RNN_CB)
            u_ref[...] = u.reshape(RNN_CHUNK, SUBLANES, RNN_CB)

            def steps(gi, h):
                s0 = pl.multiple_of((gi if d == 0 else RNN_CHUNK // RNN_UNROLL - 1 - gi) * RNN_UNROLL, RNN_UNROLL)
                for j in (range(RNN_UNROLL) if d == 0 else reversed(range(RNN_UNROLL))):
                    h = a_ref[s0 + j] * h + u_ref[s0 + j]
                    if d == 0:
                        yt_ref[t0 + s0 + j] = h
                    else:
                        yt_ref[t0 + s0 + j] = yt_ref[t0 + s0 + j] + h
                return h

            return lax.fori_loop(0, RNN_CHUNK // RNN_UNROLL, steps, h)

        hl_ref[d] = lax.fori_loop(0, n_chunks, chunk, h0_ref[d])

    def store_chunk(c, carry):
        t0 = pl.multiple_of(c * RNN_CHUNK, RNN_CHUNK)
        y = jnp.swapaxes(yt_ref[pl.ds(t0, RNN_CHUNK)], 0, 1)
        y_ref[:, pl.ds(t0, RNN_CHUNK), :] = y.astype(BF16)
        return carry

    lax.fori_loop(0, n_chunks, store_chunk, 0)


def _rnn(stream, xr, conv_w, conv_b, w_blk, b_blk, lam, h0, layer):
    n_cb = D_RNN // RNN_CB
    seq = stream.seq
    n_groups = stream.batch // SUBLANES
    return pl.pallas_call(
        functools.partial(_rnn_kernel, seq),
        grid=(n_groups, n_cb),
        in_specs=[
            pl.BlockSpec((SUBLANES, seq, RNN_CB), lambda g, j: (g, 0, j)),
            pl.BlockSpec((None, CONV_W, RNN_CB), lambda g, j: (layer, 0, j)),
            pl.BlockSpec((None, 1, RNN_CB), lambda g, j: (layer, 0, j)),
            pl.BlockSpec((None, None, 2, RNN_CB, 2 * RNN_CB), lambda g, j: (layer, j, 0, 0, 0)),
            pl.BlockSpec((None, None, 2, 2 * RNN_CB), lambda g, j: (layer, j, 0, 0)),
            pl.BlockSpec((None, 2, RNN_CB), lambda g, j: (layer, 0, j)),
            pl.BlockSpec((None, 2, SUBLANES, RNN_CB), lambda g, j: (g, 0, 0, j)),
        ],
        out_specs=[
            pl.BlockSpec((SUBLANES, seq, RNN_CB), lambda g, j: (g, 0, j)),
            pl.BlockSpec((None, 2, SUBLANES, RNN_CB), lambda g, j: (g, 0, 0, j)),
        ],
        out_shape=[
            jax.ShapeDtypeStruct((n_groups * SUBLANES, seq, D_RNN), BF16),
            jax.ShapeDtypeStruct((n_groups, 2, SUBLANES, D_RNN), F32),
        ],
        scratch_shapes=[
            pltpu.VMEM((seq + 2 * CONV_LEFT, SUBLANES, RNN_CB), F32),
            pltpu.VMEM((seq, SUBLANES, RNN_CB), F32),
            pltpu.VMEM((RNN_CHUNK, SUBLANES, RNN_CB), F32),
            pltpu.VMEM((RNN_CHUNK, SUBLANES, RNN_CB), F32),
        ],
        compiler_params=_params("arbitrary", "arbitrary"),
        name=f"rnn_{stream.name}",
    )(xr.reshape(stream.batch, seq, D_RNN), conv_w, conv_b.reshape(DEPTH, 1, D_RNN), w_blk, b_blk, lam, h0)


def _lru_block_weights(lru_wa, lru_wx, lru_ba, lru_bx):
    n_cb = D_RNN // RNN_CB
    per = RNN_CB // LRU_BW

    def dense(w):
        w = w.reshape(DEPTH, 2, n_cb, per, LRU_BW, LRU_BW)
        eye = jnp.eye(per, dtype=w.dtype)
        full = jnp.einsum("ldcpkj,pq->ldcpkqj", w, eye)
        return full.reshape(DEPTH, 2, n_cb, RNN_CB, RNN_CB)

    w = jnp.concatenate([dense(lru_wa), dense(lru_wx)], axis=-1)
    w = jnp.transpose(w, (0, 2, 1, 3, 4)).astype(BF16)
    b = jnp.concatenate([lru_ba.reshape(DEPTH, 2, n_cb, RNN_CB),
                         lru_bx.reshape(DEPTH, 2, n_cb, RNN_CB)], axis=-1)
    return w, jnp.transpose(0.5 * b, (0, 2, 1, 3))


def _route(scores, sel):
    grp_score = []
    for g in range(N_GROUPS):
        a, b, c, d = sel[g * EXPERTS_PER_GROUP:(g + 1) * EXPERTS_PER_GROUP]
        hi1, lo1 = jnp.maximum(a, b), jnp.minimum(a, b)
        hi2, lo2 = jnp.maximum(c, d), jnp.minimum(c, d)
        grp_score.append(jnp.maximum(hi1, hi2) + jnp.maximum(jnp.minimum(hi1, hi2), jnp.maximum(lo1, lo2)))
    best = jnp.zeros_like(grp_score[0], dtype=jnp.int32)
    best_val = grp_score[0]
    for g in range(1, N_GROUPS):
        better = grp_score[g] > best_val
        best = jnp.where(better, g, best)
        best_val = jnp.where(better, grp_score[g], best_val)
    chosen = []
    for e in range(N_EXPERTS):
        g = e // EXPERTS_PER_GROUP
        rank = jnp.zeros_like(best)
        for o in range(g * EXPERTS_PER_GROUP, (g + 1) * EXPERTS_PER_GROUP):
            if o == e:
                continue
            ahead = (sel[o] >= sel[e]) if o < e else (sel[o] > sel[e])
            rank = rank + jnp.where(ahead, 1, 0)
        chosen.append(jnp.where(best == g, rank, 2) < 2)
    taken, gate = [], []
    for j in range(EXPERTS_PER_GROUP):
        t = jnp.zeros_like(best)
        s = jnp.zeros_like(scores[0])
        for g in range(N_GROUPS):
            e = g * EXPERTS_PER_GROUP + j
            t = t + jnp.where(chosen[e], 1, 0)
            s = s + jnp.where(chosen[e], scores[e], 0.0)
        taken.append(t > 0)
        gate.append(s)
    total = gate[0] + gate[1] + gate[2] + gate[3]
    pair = jnp.zeros_like(best)
    w_a = jnp.zeros_like(total)
    w_b = jnp.zeros_like(total)
    for order, (a, b) in enumerate(PAIR_SLOTS):
        both = taken[a] & taken[b]
        pair = jnp.where(both, order, pair)
        w_a = jnp.where(both, gate[a], w_a)
        w_b = jnp.where(both, gate[b], w_b)
    return best * PAIRS_PER_GROUP + pair, w_a / total, w_b / total


def _merge_kernel(x_ref, att_ref, rnn_ref, gr_ref, ga_ref, gb_ref,
                  g1_ref, sh2_ref, sc2_ref, n2_ref, wap_ref, wrp_ref, wo_ref, rw_ref, rb_ref,
                  xo_ref, h2_ref, bucket_ref):
    att = jnp.dot(att_ref[...], wap_ref[...], preferred_element_type=F32)
    gated = (jax.nn.gelu(gr_ref[...].astype(F32)) * rnn_ref[...].astype(F32)).astype(BF16)
    rnn = jnp.dot(gated, wrp_ref[...], preferred_element_type=F32)
    merged = _sigmoid(ga_ref[...].astype(F32)) * att + _sigmoid(gb_ref[...].astype(F32)) * rnn
    x = x_ref[...] + g1_ref[...] * jnp.dot(merged.astype(BF16), wo_ref[...], preferred_element_type=F32)
    xo_ref[...] = x
    y = x * lax.rsqrt(jnp.mean(x * x, axis=-1, keepdims=True) + EPS) * n2_ref[...]
    h2 = y * (1.0 + sc2_ref[...]) + sh2_ref[...]
    h2_ref[:, :D_MODEL] = h2
    h_hi = h2.astype(BF16)
    h_lo = (h2 - h_hi.astype(F32)).astype(BF16)
    rw = rw_ref[...]
    w_hi = rw.astype(BF16)
    w_lo = (rw - w_hi.astype(F32)).astype(BF16)
    logits = _qk(w_hi, h_hi) + (_qk(w_hi, h_lo) + _qk(w_lo, h_hi))
    score = _sigmoid(logits)
    sel = score + rb_ref[...]
    bucket, gate_a, gate_b = _route([score[e:e + 1, :] for e in range(N_EXPERTS)],
                                [sel[e:e + 1, :] for e in range(N_EXPERTS)])
    bucket_ref[...] = bucket
    pad = jnp.zeros((LANES - 2, MM_TILE), F32)
    h2_ref[:, D_MODEL:] = jnp.concatenate([gate_a, gate_b, pad], axis=0).T


def _merge(stream, x, att, rnn, gr, ga, gb, mods, norm_g, wap, wrp, wo, router_wt, router_b, layer):
    tok = lambda i: (i, 0)
    wide = pl.BlockSpec((MM_TILE, D_MODEL), tok)
    mod = lambda col: pl.BlockSpec((None, 1, D_MODEL),
                                   lambda i: (layer * COND_ROWS + stream.mod_row(i, MM_TILE), 0, col))
    mat = pl.BlockSpec((None, D_MODEL, D_MODEL), lambda i: (layer, 0, 0))
    return pl.pallas_call(
        _merge_kernel,
        grid=(stream.rows // MM_TILE,),
        in_specs=[
            wide, wide, wide, wide, wide, wide,
            mod(2), mod(3), mod(4),
            pl.BlockSpec((None, 1, D_MODEL), lambda i: (layer, 0, 0)),
            mat, mat, mat,
            pl.BlockSpec((N_EXPERTS, D_MODEL), lambda i: (0, 0)),
            pl.BlockSpec((N_EXPERTS, 1), lambda i: (0, 0)),
        ],
        out_specs=[wide, pl.BlockSpec((MM_TILE, ROW_W), tok),
                   pl.BlockSpec((None, 1, MM_TILE), lambda i: (i, 0, 0))],
        out_shape=[
            jax.ShapeDtypeStruct((stream.rows, D_MODEL), F32),
            jax.ShapeDtypeStruct((stream.rows, ROW_W), F32),
            jax.ShapeDtypeStruct((stream.rows // MM_TILE, 1, MM_TILE), jnp.int32),
        ],
        compiler_params=_params("arbitrary"),
        name=f"merge_{stream.name}",
    )(x, att, rnn.reshape(stream.rows, D_RNN), gr, ga, gb, mods, mods, mods,
      norm_g.reshape(DEPTH, 1, D_MODEL), wap, wrp, wo, router_wt, router_b.reshape(N_EXPERTS, 1))


def _plan_kernel(b_ref, pos_ref, tiles_ref):
    b = b_ref[...]
    r = lax.broadcasted_iota(jnp.int32, (TOK_TILE, TOK_TILE), 0)
    c = lax.broadcasted_iota(jnp.int32, (TOK_TILE, TOK_TILE), 1)
    before_in_tile = jnp.where(r < c, 1.0, 0.0).astype(BF16)
    br = lax.broadcasted_iota(jnp.int32, (N_TOK_TILES, N_TOK_TILES), 0)
    bc = lax.broadcasted_iota(jnp.int32, (N_TOK_TILES, N_TOK_TILES), 1)
    earlier_tiles = jnp.where(bc < br, 1.0, 0.0).astype(BF16)
    lane = lax.broadcasted_iota(jnp.int32, (1, LANES), 1)
    tile_start = (lane * MOE_TM).astype(F32)
    start = jnp.zeros((1, 1), F32)
    pos = jnp.zeros(b.shape, F32)
    tile_bucket = jnp.zeros((1, LANES), F32)
    for k in range(N_BUCKETS):
        mask = jnp.where(b == k, 1.0, 0.0)
        rank = jnp.dot(mask.astype(BF16), before_in_tile, preferred_element_type=F32)
        per_tile = jnp.sum(mask, axis=1, keepdims=True)
        tile_off = jnp.dot(earlier_tiles, jnp.broadcast_to(per_tile, (N_TOK_TILES, LANES)).astype(BF16),
                           preferred_element_type=F32)[:, :1]
        total = jnp.sum(per_tile, axis=0, keepdims=True)
        padded = jnp.floor((total + (MOE_TM - 1)) * (1.0 / MOE_TM)) * MOE_TM
        pos = pos + mask * (start + tile_off + rank)
        start = start + padded
        tile_bucket = tile_bucket + jnp.where(tile_start >= start, 1.0, 0.0)
    pos_ref[...] = pos.astype(jnp.int32)
    tiles_ref[...] = jnp.where(lane == LANES - 1, start * (1.0 / MOE_TM), tile_bucket).astype(jnp.int32)


def _plan(bucket_lat, bucket_ctx):
    bucket = jnp.concatenate([bucket_lat.reshape(-1, TOK_TILE), bucket_ctx.reshape(-1, TOK_TILE)], axis=0)
    return pl.pallas_call(
        _plan_kernel,
        out_shape=[
            jax.ShapeDtypeStruct((N_TOK_TILES, TOK_TILE), jnp.int32),
            jax.ShapeDtypeStruct((1, LANES), jnp.int32),
        ],
        compiler_params=pltpu.CompilerParams(vmem_limit_bytes=VMEM_LIMIT),
        name="plan",
    )(bucket)


def _row_copy(src, dst, sem):
    return pltpu.make_async_copy(src, dst, sem)


def _dispatch_kernel(pos_ref, h_ref, xs_in_ref, xs_ref, sem):
    del xs_in_ref

    for r in range(TOK_TILE):
        _row_copy(h_ref.at[pl.ds(r, 1), :], xs_ref.at[pl.ds(pos_ref[0, r], 1), :], sem).start(priority=r % 2)

    def wait(r, carry):
        _row_copy(h_ref.at[pl.ds(0, 1), :], xs_ref.at[pl.ds(0, 1), :], sem).wait()
        return carry

    lax.fori_loop(0, TOK_TILE, wait, 0, unroll=8)


def _dispatch(stream, pos, h2, xs):
    return pl.pallas_call(
        _dispatch_kernel,
        grid=(stream.n_tiles,),
        in_specs=[
            pl.BlockSpec((None, 1, TOK_TILE), lambda i: (stream.first_tile + i, 0, 0), memory_space=pltpu.SMEM),
            pl.BlockSpec((TOK_TILE, ROW_W), lambda i: (i, 0)),
            pl.BlockSpec(memory_space=pl.ANY),
        ],
        out_specs=pl.BlockSpec(memory_space=pl.ANY),
        out_shape=jax.ShapeDtypeStruct((MOE_ROWS, ROW_W), F32),
        input_output_aliases={2: 0},
        scratch_shapes=[pltpu.SemaphoreType.DMA(())],
        compiler_params=_params("arbitrary"),
        name=f"dispatch_{stream.name}",
    )(pos, h2, xs)


def _bucket_expert(k, which):
    p = k % PAIRS_PER_GROUP
    in_group = 0
    for order, slots in enumerate(PAIR_SLOTS):
        in_group = jnp.where(p == order, slots[which], in_group)
    return (k // PAIRS_PER_GROUP) * EXPERTS_PER_GROUP + in_group


def _weight_copies(layer, wgu_ref, wd_ref, wgu_f, wd_f, sem, bucket, slot):
    out = []
    for which in range(2):
        e = _bucket_expert(bucket, which)
        out.append(pltpu.make_async_copy(wgu_ref.at[layer, e], wgu_f.at[slot, which], sem.at[slot, 2 * which]))
        out.append(pltpu.make_async_copy(wd_ref.at[layer, e], wd_f.at[slot, which], sem.at[slot, 2 * which + 1]))
    return out


def _experts_kernel(layer, tiles_ref, xs_ref, wgu_ref, wd_ref, y_ref, wgu_f, wd_f, wgu_b, wd_b, sem, count_ref):
    i = pl.program_id(0)
    n_used = tiles_ref[0, LANES - 1]
    in_use = i < n_used
    cur = tiles_ref[0, i]
    prev = tiles_ref[0, jnp.maximum(i - 1, 0)]
    copies = functools.partial(_weight_copies, layer, wgu_ref, wd_ref, wgu_f, wd_f, sem)

    def changed(bucket, before):
        return [_bucket_expert(bucket, which) != _bucket_expert(before, which) for which in range(2)]

    def for_each_copy(bucket, slot, needed, act):
        cps = copies(bucket, slot)
        for which in range(2):
            def both(which=which):
                act(cps[2 * which])
                act(cps[2 * which + 1])

            if needed[which] is True:
                both()
            else:
                pl.when(needed[which])(both)

    @pl.when(jnp.logical_and(in_use, jnp.logical_or(i == 0, cur != prev)))
    def _():
        @pl.when(i == 0)
        def _():
            count_ref[0] = 0
            for_each_copy(cur, 0, [True, True], lambda c: c.start())

        slot = count_ref[0] % 2
        needed = [jnp.logical_or(i == 0, c) for c in changed(cur, prev)]
        for_each_copy(cur, slot, needed, lambda c: c.wait())
        nxt_i = lax.while_loop(lambda j: jnp.logical_and(j < n_used, tiles_ref[0, jnp.minimum(j, n_used - 1)] == cur),
                               lambda j: j + 1, i + 1)

        @pl.when(nxt_i < n_used)
        def _():
            nxt = tiles_ref[0, nxt_i]
            for_each_copy(nxt, 1 - slot, changed(nxt, cur), lambda c: c.start())

        for which in range(2):
            @pl.when(needed[which])
            def _(which=which):
                wgu_b[which] = wgu_f[slot, which].astype(BF16)
                wd_b[which] = wd_f[slot, which].astype(BF16)

        count_ref[0] = count_ref[0] + 1

    @pl.when(jnp.logical_not(in_use))
    def _():
        y_ref[...] = jnp.zeros_like(y_ref)

    @pl.when(in_use)
    def _():
        x = xs_ref[:, :D_MODEL].astype(BF16)

        def branch(which, gate):
            gu = jnp.dot(x, wgu_b[which], preferred_element_type=F32)
            act = jax.nn.silu(gu[:, :D_EXPERT]) * gu[:, D_EXPERT:] * gate
            return jnp.dot(act.astype(BF16), wd_b[which], preferred_element_type=F32)

        y_ref[...] = (branch(0, xs_ref[:, D_MODEL:D_MODEL + 1])
                      + branch(1, xs_ref[:, D_MODEL + 1:D_MODEL + 2]))


def _experts(tiles, xs, wgu, wd, layer):
    return pl.pallas_call(
        functools.partial(_experts_kernel, layer),
        grid_spec=pltpu.PrefetchScalarGridSpec(
            num_scalar_prefetch=1,
            grid=(N_MOE_TILES,),
            in_specs=[
                pl.BlockSpec((MOE_TM, ROW_W), lambda i, t: (i, 0)),
                pl.BlockSpec(memory_space=pl.ANY),
                pl.BlockSpec(memory_space=pl.ANY),
            ],
            out_specs=pl.BlockSpec((MOE_TM, D_MODEL), lambda i, t: (i, 0)),
            scratch_shapes=[
                pltpu.VMEM((2, 2, D_MODEL, 2 * D_EXPERT), F32), pltpu.VMEM((2, 2, D_EXPERT, D_MODEL), F32),
                pltpu.VMEM((2, D_MODEL, 2 * D_EXPERT), BF16), pltpu.VMEM((2, D_EXPERT, D_MODEL), BF16),
                pltpu.SemaphoreType.DMA((2, 4)), pltpu.SMEM((1,), jnp.int32),
            ],
        ),
        out_shape=jax.ShapeDtypeStruct((MOE_ROWS, D_MODEL), F32),
        compiler_params=_params("arbitrary"),
        name="experts",
    )(tiles, xs, wgu, wd)


def _combine_kernel(n_tiles, final, pos_ref, next_pos_ref, x_ref, g2_ref, fg_ref, y_ref, o_ref, rows_ref, sem):
    i = pl.program_id(0)

    def gather(p_ref, into):
        for r in range(TOK_TILE):
            _row_copy(y_ref.at[pl.ds(p_ref[0, r], 1), :], rows_ref.at[into, pl.ds(r, 1), :],
                      sem.at[into]).start(priority=r % 2)

    def step(slot):
        if slot == 0:
            @pl.when(i == 0)
            def _():
                gather(pos_ref, 0)

        @pl.when(i + 1 < n_tiles)
        def _():
            gather(next_pos_ref, 1 - slot)

        def wait(r, carry):
            _row_copy(y_ref.at[pl.ds(0, 1), :], rows_ref.at[slot, pl.ds(0, 1), :], sem.at[slot]).wait()
            return carry

        lax.fori_loop(0, TOK_TILE, wait, 0, unroll=8)
        x = x_ref[...] + g2_ref[...] * rows_ref[slot]
        if final:
            x = x * lax.rsqrt(jnp.mean(x * x, axis=-1, keepdims=True) + EPS) * fg_ref[...]
        o_ref[...] = x

    for parity in range(2):
        pl.when(i % 2 == parity)(functools.partial(step, parity))


def _combine(stream, pos, x, mods, y, final_g, layer):
    first, last = stream.first_tile, stream.first_tile + stream.n_tiles - 1
    return pl.pallas_call(
        functools.partial(_combine_kernel, stream.n_tiles, layer == DEPTH - 1),
        grid=(stream.n_tiles,),
        in_specs=[
            pl.BlockSpec((None, 1, TOK_TILE), lambda i: (first + i, 0, 0), memory_space=pltpu.SMEM),
            pl.BlockSpec((None, 1, TOK_TILE), lambda i: (jnp.minimum(first + i + 1, last), 0, 0),
                         memory_space=pltpu.SMEM),
            pl.BlockSpec((TOK_TILE, D_MODEL), lambda i: (i, 0)),
            pl.BlockSpec((None, 1, D_MODEL), lambda i: (layer * COND_ROWS + stream.mod_row(i, TOK_TILE), 0, 5)),
            pl.BlockSpec((1, D_MODEL), lambda i: (0, 0)),
            pl.BlockSpec(memory_space=pl.ANY),
        ],
        out_specs=pl.BlockSpec((TOK_TILE, D_MODEL), lambda i: (i, 0)),
        out_shape=jax.ShapeDtypeStruct((stream.rows, D_MODEL), F32),
        scratch_shapes=[pltpu.VMEM((2, TOK_TILE, D_MODEL), F32), pltpu.SemaphoreType.DMA((2,))],
        compiler_params=_params("arbitrary"),
        name=f"combine_{stream.name}",
    )(pos, pos, x, mods, final_g.reshape(1, D_MODEL), y)


def _rope_tables():
    t = jnp.arange(DEC_SEQ, dtype=jnp.int32)
    row = (t // GRID_W).astype(F32)
    col = (t % GRID_W).astype(F32)
    inv = ROPE_BASE ** (-jnp.arange(ROPE_PAIRS_PER_AXIS, dtype=F32) / ROPE_PAIRS_PER_AXIS)
    ang = jnp.concatenate([row[:, None] * inv, col[:, None] * inv], axis=-1)
    cos, sin = jnp.cos(ang), jnp.sin(ang)
    return jnp.concatenate([cos, cos], axis=-1), jnp.concatenate([-sin, sin], axis=-1)


def kernel(x_prompt, x_sample, cache_k, cache_v, state_h, c, c_ctx, norm1_g, norm2_g, w_ada, b_ada, w_in, conv_w, conv_b, lru_wa, lru_ba, lru_wx, lru_bx, lru_lambda, attn_sink, w_attn_proj, w_rnn_proj, w_out, router_w, router_b, w_gate_up, w_down, final_norm_g):
    xs_tok = {LATENT: x_sample.reshape(T_LAT, D_MODEL), CONTEXT: x_prompt.reshape(T_CTX, D_MODEL)}
    cond = jnp.concatenate([c_ctx[None, :], c, jnp.zeros((COND_ROWS - 1 - DEC_BATCH, D_MODEL), F32)], axis=0)
    mods = _ada(cond, w_ada, b_ada).reshape(DEPTH * COND_ROWS, 1, N_ADA * D_MODEL)
    rope_cos, rope_sin = _rope_tables()

    w_in_b = w_in.astype(BF16)
    wap_b = w_attn_proj.astype(BF16)
    wrp_b = w_rnn_proj.astype(BF16)
    wo_b = w_out.astype(BF16)
    lru_w, lru_b = _lru_block_weights(lru_wa, lru_wx, lru_ba, lru_bx)
    router_wt = router_w.T

    h0 = {LATENT: jnp.transpose(state_h.astype(F32), (1, 2, 0, 3)).reshape(DEPTH, 1, 2, DEC_BATCH, D_RNN),
          CONTEXT: jnp.zeros((DEPTH, BATCH // SUBLANES, 2, SUBLANES, D_RNN), F32)}
    ck = jnp.transpose(cache_k, (1, 0, 2, 3, 4)).reshape(DEPTH, DEC_BATCH, PAST_LEN, KV_W)
    cv = jnp.transpose(cache_v, (1, 0, 2, 3, 4)).reshape(DEPTH, DEC_BATCH, PAST_LEN, KV_W)

    new_k, new_v, new_h = [], [], []
    for l in range(DEPTH):
        merged = {}
        for stream in (LATENT, CONTEXT):
            q, k, v, xr, gr, ga, gb = _inproj(stream, xs_tok[stream], norm1_g, mods, rope_cos, rope_sin,
                                              w_in_b, l)
            if stream.rotary:
                att = _lat_attn(q, k, v, ck[l], cv[l], attn_sink[l])
            else:
                att = _ctx_attn(q, k, v, attn_sink[l])
                new_k.append(k.reshape(BATCH, SEQ, N_KV_HEADS, HEAD_DIM))
                new_v.append(v.reshape(BATCH, SEQ, N_KV_HEADS, HEAD_DIM))
            y, h_last = _rnn(stream, xr, conv_w, conv_b, lru_w, lru_b, lru_lambda, h0[stream][l], l)
            if not stream.rotary:
                new_h.append(jnp.transpose(h_last, (0, 2, 1, 3)).reshape(BATCH, 2, D_RNN))
            merged[stream] = _merge(stream, xs_tok[stream], att, y, gr, ga, gb, mods, norm2_g,
                                    wap_b, wrp_b, wo_b, router_wt, router_b, l)
        pos, tiles = _plan(merged[LATENT][2], merged[CONTEXT][2])
        pos = pos.reshape(N_TOK_TILES, 1, TOK_TILE)
        xs = jnp.zeros((MOE_ROWS, ROW_W), F32)
        for stream in (LATENT, CONTEXT):
            xs = _dispatch(stream, pos, merged[stream][1], xs)
        y_sorted = _experts(tiles, xs, w_gate_up, w_down, l)
        for stream in (LATENT, CONTEXT):
            xs_tok[stream] = _combine(stream, pos, merged[stream][0], mods, y_sorted, final_norm_g, l)

    return (xs_tok[CONTEXT].reshape(BATCH, SEQ, D_MODEL), xs_tok[LATENT].reshape(DEC_BATCH, DEC_SEQ, D_MODEL),
            jnp.stack(new_k, axis=1), jnp.stack(new_v, axis=1),
            jnp.stack(new_h, axis=1).astype(x_prompt.dtype))
```

```python
import functools
from typing import NamedTuple

import jax
import jax.numpy as jnp
from jax import lax
from jax.experimental import pallas as pl
from jax.experimental.pallas import tpu as pltpu

F32 = jnp.float32
BF16 = jnp.bfloat16

D_MODEL = 1024
BATCH = 16
SEQ = 256
DEPTH = 2
DEC_BATCH = 8
DEC_SEQ = 1024
PAST_LEN = 512
GRID_W = 64
N_HEADS = 8
N_KV_HEADS = 2
GQA_GROUP = N_HEADS // N_KV_HEADS
HEAD_DIM = 128
ATTN_W = N_HEADS * HEAD_DIM
KV_W = N_KV_HEADS * HEAD_DIM
WINDOW = 128
Q_BLOCK = 128
ROPE_BASE = 10000.0
ROPE_PAIRS_PER_AXIS = HEAD_DIM // 4
D_RNN = 1024
LRU_BLOCKS = 16
LRU_BW = D_RNN // LRU_BLOCKS
LRU_C = 8.0
CONV_W = 4
CONV_LEFT = 2
N_EXPERTS = 16
N_GROUPS = 4
EXPERTS_PER_GROUP = N_EXPERTS // N_GROUPS
D_EXPERT = 512
N_ADA = 6
EPS = 1e-6
IN_COLS = ATTN_W + 2 * KV_W + 2 * D_RNN + 2 * D_MODEL

T_LAT = DEC_BATCH * DEC_SEQ
T_CTX = BATCH * SEQ
T_ALL = T_LAT + T_CTX

SUBLANES = 8
LANES = 128
VMEM_LIMIT = 56 * 1024 * 1024

TOK_TILE = 256
MM_TILE = 512
N_TOK_TILES = T_ALL // TOK_TILE
COND_ROWS = 16


class _Stream(NamedTuple):
    name: str
    batch: int
    seq: int
    first_tile: int
    rotary: bool

    @property
    def rows(self):
        return self.batch * self.seq

    @property
    def n_tiles(self):
        return self.rows // TOK_TILE

    def mod_row(self, i, tile):
        return 1 + i // (self.seq // tile) if self.rotary else 0


LATENT = _Stream("lat", DEC_BATCH, DEC_SEQ, 0, True)
CONTEXT = _Stream("ctx", BATCH, SEQ, T_LAT // TOK_TILE, False)

RNN_CB = 256
RNN_CHUNK = 64
RNN_UNROLL = 8
LOG2_E = 1.4426950408889634
PAIR_SLOTS = ((0, 1), (0, 2), (0, 3), (1, 3), (1, 2), (3, 2))
PAIRS_PER_GROUP = len(PAIR_SLOTS)
N_BUCKETS = N_GROUPS * PAIRS_PER_GROUP
ROW_W = D_MODEL + LANES
MOE_TM = 256
N_MOE_TILES = (T_ALL + N_BUCKETS * (MOE_TM - 1) + MOE_TM - 1) // MOE_TM
MOE_ROWS = N_MOE_TILES * MOE_TM


def _params(*sem):
    return pltpu.CompilerParams(dimension_semantics=sem, vmem_limit_bytes=VMEM_LIMIT)


def _sigmoid(x):
    return 0.5 * jnp.tanh(0.5 * x) + 0.5


def _ada_kernel(cond_ref, w_ref, b_ref, o_ref):
    s = jax.nn.silu(cond_ref[...]).astype(BF16)
    o_ref[...] = jnp.dot(s, w_ref[...].astype(BF16), preferred_element_type=F32) + b_ref[...]


def _ada(cond, w_ada, b_ada):
    cols = N_ADA * D_MODEL
    tn = 1536
    return pl.pallas_call(
        _ada_kernel,
        grid=(DEPTH, cols // tn),
        in_specs=[
            pl.BlockSpec((COND_ROWS, D_MODEL), lambda l, j: (0, 0)),
            pl.BlockSpec((None, D_MODEL, tn), lambda l, j: (l, 0, j)),
            pl.BlockSpec((None, 1, tn), lambda l, j: (l, 0, j)),
        ],
        out_specs=pl.BlockSpec((None, COND_ROWS, tn), lambda l, j: (l, 0, j)),
        out_shape=jax.ShapeDtypeStruct((DEPTH, COND_ROWS, cols), F32),
        compiler_params=_params("arbitrary", "arbitrary"),
        name="ada",
    )(cond, w_ada, b_ada.reshape(DEPTH, 1, cols))


def _inproj_kernel(rotary, x_ref, g_ref, sh_ref, sc_ref, *refs):
    if rotary:
        cos_ref, sin_ref, w_ref, q_ref, k_ref, v_ref, xr_ref, gr_ref, ga_ref, gb_ref = refs
    else:
        w_ref, q_ref, k_ref, v_ref, xr_ref, gr_ref, ga_ref, gb_ref = refs
    x = x_ref[...]
    y = x * lax.rsqrt(jnp.mean(x * x, axis=-1, keepdims=True) + EPS) * g_ref[...]
    h = (y * (1.0 + sc_ref[...]) + sh_ref[...]).astype(BF16)

    def proj(lo, width):
        return jnp.dot(h, w_ref[:, lo:lo + width], preferred_element_type=F32)

    def rope(t):
        if not rotary:
            return t
        return t * cos_ref[...] + pltpu.roll(t, HEAD_DIM // 2, 1) * sin_ref[...]

    scale = HEAD_DIM ** -0.5 * LOG2_E
    qk = proj(0, ATTN_W + KV_W)
    for hd in range(N_HEADS):
        q = (rope(qk[:, hd * HEAD_DIM:(hd + 1) * HEAD_DIM]) * scale).astype(BF16)
        for blk in range(MM_TILE // Q_BLOCK):
            q_ref[blk, hd] = q[blk * Q_BLOCK:(blk + 1) * Q_BLOCK]
    for g in range(N_KV_HEADS):
        k = qk[:, ATTN_W + g * HEAD_DIM:ATTN_W + (g + 1) * HEAD_DIM]
        k_ref[:, g * HEAD_DIM:(g + 1) * HEAD_DIM] = rope(k)
    v_ref[...] = proj(ATTN_W + KV_W, KV_W)
    base = ATTN_W + 2 * KV_W
    xr_ref[...] = proj(base, D_RNN).astype(BF16)
    gr_ref[...] = proj(base + D_RNN, D_RNN).astype(BF16)
    ga_ref[...] = proj(base + 2 * D_RNN, D_MODEL).astype(BF16)
    gb_ref[...] = proj(base + 2 * D_RNN + D_MODEL, D_MODEL).astype(BF16)


def _inproj(stream, x, norm_g, mods, rope_cos, rope_sin, w_in, layer):
    row = lambda i: layer * COND_ROWS + stream.mod_row(i, MM_TILE)
    tok = lambda i: (i, 0)
    rope_blk = lambda i: (i % (stream.seq // MM_TILE), 0)
    wide = pl.BlockSpec((MM_TILE, D_MODEL), tok)
    kv = pl.BlockSpec((MM_TILE, KV_W), tok)
    rope_specs = [pl.BlockSpec((MM_TILE, HEAD_DIM), rope_blk)] * 2 if stream.rotary else []
    rope_args = (rope_cos, rope_sin) if stream.rotary else ()
    return pl.pallas_call(
        functools.partial(_inproj_kernel, stream.rotary),
        grid=(stream.rows // MM_TILE,),
        in_specs=[
            wide,
            pl.BlockSpec((None, 1, D_MODEL), lambda i: (layer, 0, 0)),
            pl.BlockSpec((None, 1, D_MODEL), lambda i: (row(i), 0, 0)),
            pl.BlockSpec((None, 1, D_MODEL), lambda i: (row(i), 0, 1)),
        ] + rope_specs + [
            pl.BlockSpec((None, D_MODEL, IN_COLS), lambda i: (layer, 0, 0)),
        ],
        out_specs=[pl.BlockSpec((MM_TILE // Q_BLOCK, N_HEADS, Q_BLOCK, HEAD_DIM), lambda i: (i, 0, 0, 0)),
                   kv, kv, wide, wide, wide, wide],
        out_shape=[
            jax.ShapeDtypeStruct((stream.rows // Q_BLOCK, N_HEADS, Q_BLOCK, HEAD_DIM), BF16),
            jax.ShapeDtypeStruct((stream.rows, KV_W), F32),
            jax.ShapeDtypeStruct((stream.rows, KV_W), F32),
            jax.ShapeDtypeStruct((stream.rows, D_RNN), BF16),
            jax.ShapeDtypeStruct((stream.rows, D_RNN), BF16),
            jax.ShapeDtypeStruct((stream.rows, D_MODEL), BF16),
            jax.ShapeDtypeStruct((stream.rows, D_MODEL), BF16),
        ],
        compiler_params=_params("arbitrary"),
        name=f"inproj_{stream.name}",
    )(x, norm_g.reshape(DEPTH, 1, D_MODEL), mods, mods, *rope_args, w_in)


def _qk(q, k):
    return lax.dot_general(q, k, (((1,), (1,)), ((), ())), preferred_element_type=F32)


def _store_heads(o_ref, g, o, rows):
    for r in range(GQA_GROUP):
        hd = g * GQA_GROUP + r
        o_ref[:, hd * HEAD_DIM:(hd + 1) * HEAD_DIM] = o[r * rows:(r + 1) * rows].astype(BF16)


def _softmax_rows(s_tiles, sink, p_ref, tail_ref, rows):
    top = s_tiles[0][1]
    for _, t in s_tiles[1:]:
        top = jnp.maximum(top, t)
    m = jnp.maximum(jnp.max(top, axis=-1, keepdims=True), sink)
    for c, t in s_tiles:
        p_ref[rows, c:c + LANES] = jnp.exp2(t - m).astype(BF16)
    tail_ref[rows, :] = jnp.broadcast_to(jnp.exp2(sink - m), top.shape)


def _values_and_ones(v):
    ones = jnp.where(lax.broadcasted_iota(jnp.int32, v.shape, 1) == 0, 1.0, 0.0).astype(v.dtype)
    return jnp.concatenate([v, ones], axis=1)


def _attend(p_ref, tail_ref, v):
    acc = jnp.dot(p_ref[...], _values_and_ones(v), preferred_element_type=F32)
    return acc[:, :HEAD_DIM] / (acc[:, HEAD_DIM:HEAD_DIM + 1] + tail_ref[:, :1])


def _ctx_attn_kernel(sink_ref, q_ref, k_ref, v_ref, o_ref, s_ref, p_ref, tail_ref):
    for g in range(N_KV_HEADS):
        heads = q_ref[:, g * GQA_GROUP:(g + 1) * GQA_GROUP]
        q = jnp.swapaxes(heads, 0, 1).reshape(GQA_GROUP * SEQ, HEAD_DIM)
        k = k_ref[:, g * HEAD_DIM:(g + 1) * HEAD_DIM].astype(BF16)
        v = v_ref[:, g * HEAD_DIM:(g + 1) * HEAD_DIM].astype(BF16)
        s_ref[...] = _qk(q, k)
        for r in range(GQA_GROUP):
            rows = slice(r * SEQ, (r + 1) * SEQ)
            tiles = [(c, s_ref[rows, c:c + LANES]) for c in range(0, SEQ, LANES)]
            _softmax_rows(tiles, sink_ref[g * GQA_GROUP + r] * LOG2_E, p_ref, tail_ref, rows)
        _store_heads(o_ref, g, _attend(p_ref, tail_ref, v), SEQ)


def _ctx_attn(q, k, v, sink):
    blk = lambda b: (b, 0)
    rows = GQA_GROUP * SEQ
    return pl.pallas_call(
        _ctx_attn_kernel,
        grid=(BATCH,),
        in_specs=[
            pl.BlockSpec(memory_space=pltpu.SMEM),
            pl.BlockSpec((SEQ // Q_BLOCK, N_HEADS, Q_BLOCK, HEAD_DIM), lambda b: (b, 0, 0, 0)),
            pl.BlockSpec((SEQ, KV_W), blk),
            pl.BlockSpec((SEQ, KV_W), blk),
        ],
        out_specs=pl.BlockSpec((SEQ, ATTN_W), lambda b: (b, 0)),
        out_shape=jax.ShapeDtypeStruct((T_CTX, ATTN_W), BF16),
        scratch_shapes=[pltpu.VMEM((rows, SEQ), F32), pltpu.VMEM((rows, SEQ), BF16),
                        pltpu.VMEM((rows, LANES), F32)],
        compiler_params=_params("arbitrary"),
        name="ctx_attn",
    )(sink, q, k, v)


def _lat_attn_kernel(sink_ref, q_ref, kp_ref, kc_ref, kn_ref, vp_ref, vc_ref, vn_ref,
                     ck_ref, cv_ref, o_ref, band_ref, mask_ref, s_ref, p_ref, tail_ref):
    j = pl.program_id(1)
    rows = GQA_GROUP * Q_BLOCK
    band = Q_BLOCK + 2 * WINDOW

    @pl.when(jnp.logical_and(pl.program_id(0) == 0, j == 0))
    def _():
        ahead = (lax.broadcasted_iota(jnp.int32, (rows, band), 1) - WINDOW
                 - lax.broadcasted_iota(jnp.int32, (rows, band), 0) % Q_BLOCK)
        band_ref[...] = jnp.where(jnp.abs(ahead) <= WINDOW, 0.0, -jnp.inf)

    kpos = j * Q_BLOCK - WINDOW + lax.broadcasted_iota(jnp.int32, (1, band), 1)
    mask_ref[...] = band_ref[...] + jnp.where((kpos >= 0) & (kpos < DEC_SEQ), 0.0, -jnp.inf)
    for g in range(N_KV_HEADS):
        cols = slice(g * HEAD_DIM, (g + 1) * HEAD_DIM)
        q = q_ref[g * GQA_GROUP:(g + 1) * GQA_GROUP].reshape(rows, HEAD_DIM)
        keys = jnp.concatenate([kp_ref[:, cols], kc_ref[:, cols], kn_ref[:, cols], ck_ref[:, cols]],
                               axis=0).astype(BF16)
        vals = jnp.concatenate([vp_ref[:, cols], vc_ref[:, cols], vn_ref[:, cols], cv_ref[:, cols]],
                               axis=0).astype(BF16)
        s_ref[...] = _qk(q, keys)
        for r in range(GQA_GROUP):
            rows_r = slice(r * Q_BLOCK, (r + 1) * Q_BLOCK)
            tiles = [(c, s_ref[rows_r, c:c + LANES] + mask_ref[rows_r, c:c + LANES]) for c in range(0, band, LANES)]
            tiles += [(c, s_ref[rows_r, c:c + LANES]) for c in range(band, band + PAST_LEN, LANES)]
            _softmax_rows(tiles, sink_ref[g * GQA_GROUP + r] * LOG2_E, p_ref, tail_ref, rows_r)
        _store_heads(o_ref, g, _attend(p_ref, tail_ref, vals), Q_BLOCK)


def _lat_attn(q, k, v, cache_k, cache_v, sink):
    nb = DEC_SEQ // Q_BLOCK
    cur = lambda b, j: (b * nb + j, 0)
    prev = lambda b, j: (b * nb + jnp.maximum(j - 1, 0), 0)
    nxt = lambda b, j: (b * nb + jnp.minimum(j + 1, nb - 1), 0)
    kvb = lambda im: pl.BlockSpec((Q_BLOCK, KV_W), im)
    cache = pl.BlockSpec((None, PAST_LEN, KV_W), lambda b, j: (b, 0, 0))
    return pl.pallas_call(
        _lat_attn_kernel,
        grid=(DEC_BATCH, nb),
        in_specs=[
            pl.BlockSpec(memory_space=pltpu.SMEM),
            pl.BlockSpec((None, N_HEADS, Q_BLOCK, HEAD_DIM), lambda b, j: (b * nb + j, 0, 0, 0)),
            kvb(prev), kvb(cur), kvb(nxt), kvb(prev), kvb(cur), kvb(nxt),
            cache, cache,
        ],
        out_specs=pl.BlockSpec((Q_BLOCK, ATTN_W), cur),
        out_shape=jax.ShapeDtypeStruct((T_LAT, ATTN_W), BF16),
        scratch_shapes=[
            pltpu.VMEM((GQA_GROUP * Q_BLOCK, Q_BLOCK + 2 * WINDOW), F32),
            pltpu.VMEM((GQA_GROUP * Q_BLOCK, Q_BLOCK + 2 * WINDOW), F32),
            pltpu.VMEM((GQA_GROUP * Q_BLOCK, Q_BLOCK + 2 * WINDOW + PAST_LEN), F32),
            pltpu.VMEM((GQA_GROUP * Q_BLOCK, Q_BLOCK + 2 * WINDOW + PAST_LEN), BF16),
            pltpu.VMEM((GQA_GROUP * Q_BLOCK, LANES), F32),
        ],
        compiler_params=_params("arbitrary", "arbitrary"),
        name="lat_attn",
    )(sink, q, k, k, k, v, v, v, cache_k, cache_v)


def _rnn_kernel(seq, xr_ref, cw_ref, cb_ref, w_ref, b_ref, lam_ref, h0_ref,
                y_ref, hl_ref, xt_ref, yt_ref, a_ref, u_ref):
    n_chunks = seq // RNN_CHUNK
    rows = RNN_CHUNK * SUBLANES
    halo = jnp.zeros((CONV_LEFT, SUBLANES, RNN_CB), F32)
    xt_ref[0:CONV_LEFT] = halo
    xt_ref[seq + CONV_LEFT:seq + 2 * CONV_LEFT] = halo

    def load_chunk(c, carry):
        t0 = pl.multiple_of(c * RNN_CHUNK, RNN_CHUNK)
        x = xr_ref[:, pl.ds(t0, RNN_CHUNK), :].astype(F32)
        xt_ref[pl.ds(t0 + CONV_LEFT, RNN_CHUNK)] = jnp.swapaxes(x, 0, 1)
        return carry

    lax.fori_loop(0, n_chunks, load_chunk, 0)

    half_w = 0.5 * cw_ref[...]
    half_b = 0.5 * cb_ref[...]

    def half_conv(t0):
        acc = half_b.reshape(1, 1, RNN_CB)
        for tap in range(CONV_W):
            acc = acc + xt_ref[pl.ds(t0 + tap, RNN_CHUNK)] * half_w[tap:tap + 1, :].reshape(1, 1, RNN_CB)
        return acc

    for d in range(2):
        decay = (-0.5 * LRU_C * LOG2_E) * jax.nn.softplus(-lam_ref[d:d + 1, :])
        half_bias = b_ref[d:d + 1, :]

        def chunk(ci, h, d=d, decay=decay, half_bias=half_bias):
            c = ci if d == 0 else n_chunks - 1 - ci
            t0 = pl.multiple_of(c * RNN_CHUNK, RNN_CHUNK)
            hx = half_conv(t0).reshape(rows, RNN_CB)
            z = jnp.dot(hx.astype(BF16), w_ref[d], preferred_element_type=F32) + half_bias
            a = jnp.exp2(decay * jnp.tanh(z[:, :RNN_CB]) + decay)
            m = 1.0 - a * a
            mult = jnp.where(m == 0.0, 0.0, m * lax.rsqrt(m))
            u = mult * ((jnp.tanh(z[:, RNN_CB:]) + 1.0) * hx)
            a_ref[...] = a.reshape(RNN_CHUNK, SUBLANES, RNN_CB)
            u_ref[...] = u.reshape(RNN_CHUNK, SUBLANES, RNN_CB)

            def steps(gi, h):
                s0 = pl.multiple_of((gi if d == 0 else RNN_CHUNK // RNN_UNROLL - 1 - gi) * RNN_UNROLL, RNN_UNROLL)
                for j in (range(RNN_UNROLL) if d == 0 else reversed(range(RNN_UNROLL))):
                    h = a_ref[s0 + j] * h + u_ref[s0 + j]
                    if d == 0:
                        yt_ref[t0 + s0 + j] = h
                    else:
                        yt_ref[t0 + s0 + j] = yt_ref[t0 + s0 + j] + h
                return h

            return lax.fori_loop(0, RNN_CHUNK // RNN_UNROLL, steps, h)

        hl_ref[d] = lax.fori_loop(0, n_chunks, chunk, h0_ref[d])

    def store_chunk(c, carry):
        t0 = pl.multiple_of(c * RNN_CHUNK, RNN_CHUNK)
        y = jnp.swapaxes(yt_ref[pl.ds(t0, RNN_CHUNK)], 0, 1)
        y_ref[:, pl.ds(t0, RNN_CHUNK), :] = y.astype(BF16)
        return carry

    lax.fori_loop(0, n_chunks, store_chunk, 0)


def _rnn(stream, xr, conv_w, conv_b, w_blk, b_blk, lam, h0, layer):
    n_cb = D_RNN // RNN_CB
    seq = stream.seq
    n_groups = stream.batch // SUBLANES
    return pl.pallas_call(
        functools.partial(_rnn_kernel, seq),
        grid=(n_groups, n_cb),
        in_specs=[
            pl.BlockSpec((SUBLANES, seq, RNN_CB), lambda g, j: (g, 0, j)),
            pl.BlockSpec((None, CONV_W, RNN_CB), lambda g, j: (layer, 0, j)),
            pl.BlockSpec((None, 1, RNN_CB), lambda g, j: (layer, 0, j)),
            pl.BlockSpec((None, None, 2, RNN_CB, 2 * RNN_CB), lambda g, j: (layer, j, 0, 0, 0)),
            pl.BlockSpec((None, None, 2, 2 * RNN_CB), lambda g, j: (layer, j, 0, 0)),
            pl.BlockSpec((None, 2, RNN_CB), lambda g, j: (layer, 0, j)),
            pl.BlockSpec((None, 2, SUBLANES, RNN_CB), lambda g, j: (g, 0, 0, j)),
        ],
        out_specs=[
            pl.BlockSpec((SUBLANES, seq, RNN_CB), lambda g, j: (g, 0, j)),
            pl.BlockSpec((None, 2, SUBLANES, RNN_CB), lambda g, j: (g, 0, 0, j)),
        ],
        out_shape=[
            jax.ShapeDtypeStruct((n_groups * SUBLANES, seq, D_RNN), BF16),
            jax.ShapeDtypeStruct((n_groups, 2, SUBLANES, D_RNN), F32),
        ],
        scratch_shapes=[
            pltpu.VMEM((seq + 2 * CONV_LEFT, SUBLANES, RNN_CB), F32),
            pltpu.VMEM((seq, SUBLANES, RNN_CB), F32),
            pltpu.VMEM((RNN_CHUNK, SUBLANES, RNN_CB), F32),
            pltpu.VMEM((RNN_CHUNK, SUBLANES, RNN_CB), F32),
        ],
        compiler_params=_params("arbitrary", "arbitrary"),
        name=f"rnn_{stream.name}",
    )(xr.reshape(stream.batch, seq, D_RNN), conv_w, conv_b.reshape(DEPTH, 1, D_RNN), w_blk, b_blk, lam, h0)


def _lru_block_weights(lru_wa, lru_wx, lru_ba, lru_bx):
    n_cb = D_RNN // RNN_CB
    per = RNN_CB // LRU_BW

    def dense(w):
        w = w.reshape(DEPTH, 2, n_cb, per, LRU_BW, LRU_BW)
        eye = jnp.eye(per, dtype=w.dtype)
        full = jnp.einsum("ldcpkj,pq->ldcpkqj", w, eye)
        return full.reshape(DEPTH, 2, n_cb, RNN_CB, RNN_CB)

    w = jnp.concatenate([dense(lru_wa), dense(lru_wx)], axis=-1)
    w = jnp.transpose(w, (0, 2, 1, 3, 4)).astype(BF16)
    b = jnp.concatenate([lru_ba.reshape(DEPTH, 2, n_cb, RNN_CB),
                         lru_bx.reshape(DEPTH, 2, n_cb, RNN_CB)], axis=-1)
    return w, jnp.transpose(0.5 * b, (0, 2, 1, 3))


def _route(scores, sel):
    grp_score = []
    for g in range(N_GROUPS):
        a, b, c, d = sel[g * EXPERTS_PER_GROUP:(g + 1) * EXPERTS_PER_GROUP]
        hi1, lo1 = jnp.maximum(a, b), jnp.minimum(a, b)
        hi2, lo2 = jnp.maximum(c, d), jnp.minimum(c, d)
        grp_score.append(jnp.maximum(hi1, hi2) + jnp.maximum(jnp.minimum(hi1, hi2), jnp.maximum(lo1, lo2)))
    best = jnp.zeros_like(grp_score[0], dtype=jnp.int32)
    best_val = grp_score[0]
    for g in range(1, N_GROUPS):
        better = grp_score[g] > best_val
        best = jnp.where(better, g, best)
        best_val = jnp.where(better, grp_score[g], best_val)
    chosen = []
    for e in range(N_EXPERTS):
        g = e // EXPERTS_PER_GROUP
        rank = jnp.zeros_like(best)
        for o in range(g * EXPERTS_PER_GROUP, (g + 1) * EXPERTS_PER_GROUP):
            if o == e:
                continue
            ahead = (sel[o] >= sel[e]) if o < e else (sel[o] > sel[e])
            rank = rank + jnp.where(ahead, 1, 0)
        chosen.append(jnp.where(best == g, rank, 2) < 2)
    taken, gate = [], []
    for j in range(EXPERTS_PER_GROUP):
        t = jnp.zeros_like(best)
        s = jnp.zeros_like(scores[0])
        for g in range(N_GROUPS):
            e = g * EXPERTS_PER_GROUP + j
            t = t + jnp.where(chosen[e], 1, 0)
            s = s + jnp.where(chosen[e], scores[e], 0.0)
        taken.append(t > 0)
        gate.append(s)
    total = gate[0] + gate[1] + gate[2] + gate[3]
    pair = jnp.zeros_like(best)
    w_a = jnp.zeros_like(total)
    w_b = jnp.zeros_like(total)
    for order, (a, b) in enumerate(PAIR_SLOTS):
        both = taken[a] & taken[b]
        pair = jnp.where(both, order, pair)
        w_a = jnp.where(both, gate[a], w_a)
        w_b = jnp.where(both, gate[b], w_b)
    return best * PAIRS_PER_GROUP + pair, w_a / total, w_b / total


def _merge_kernel(x_ref, att_ref, rnn_ref, gr_ref, ga_ref, gb_ref,
                  g1_ref, sh2_ref, sc2_ref, n2_ref, wap_ref, wrp_ref, wo_ref, rw_ref, rb_ref,
                  xo_ref, h2_ref, bucket_ref):
    att = jnp.dot(att_ref[...], wap_ref[...], preferred_element_type=F32)
    gated = (jax.nn.gelu(gr_ref[...].astype(F32)) * rnn_ref[...].astype(F32)).astype(BF16)
    rnn = jnp.dot(gated, wrp_ref[...], preferred_element_type=F32)
    merged = _sigmoid(ga_ref[...].astype(F32)) * att + _sigmoid(gb_ref[...].astype(F32)) * rnn
    x = x_ref[...] + g1_ref[...] * jnp.dot(merged.astype(BF16), wo_ref[...], preferred_element_type=F32)
    xo_ref[...] = x
    y = x * lax.rsqrt(jnp.mean(x * x, axis=-1, keepdims=True) + EPS) * n2_ref[...]
    h2 = y * (1.0 + sc2_ref[...]) + sh2_ref[...]
    h2_ref[:, :D_MODEL] = h2
    h_hi = h2.astype(BF16)
    h_lo = (h2 - h_hi.astype(F32)).astype(BF16)
    rw = rw_ref[...]
    w_hi = rw.astype(BF16)
    w_lo = (rw - w_hi.astype(F32)).astype(BF16)
    logits = _qk(w_hi, h_hi) + (_qk(w_hi, h_lo) + _qk(w_lo, h_hi))
    score = _sigmoid(logits)
    sel = score + rb_ref[...]
    bucket, gate_a, gate_b = _route([score[e:e + 1, :] for e in range(N_EXPERTS)],
                                [sel[e:e + 1, :] for e in range(N_EXPERTS)])
    bucket_ref[...] = bucket
    pad = jnp.zeros((LANES - 2, MM_TILE), F32)
    h2_ref[:, D_MODEL:] = jnp.concatenate([gate_a, gate_b, pad], axis=0).T


def _merge(stream, x, att, rnn, gr, ga, gb, mods, norm_g, wap, wrp, wo, router_wt, router_b, layer):
    tok = lambda i: (i, 0)
    wide = pl.BlockSpec((MM_TILE, D_MODEL), tok)
    mod = lambda col: pl.BlockSpec((None, 1, D_MODEL),
                                   lambda i: (layer * COND_ROWS + stream.mod_row(i, MM_TILE), 0, col))
    mat = pl.BlockSpec((None, D_MODEL, D_MODEL), lambda i: (layer, 0, 0))
    return pl.pallas_call(
        _merge_kernel,
        grid=(stream.rows // MM_TILE,),
        in_specs=[
            wide, wide, wide, wide, wide, wide,
            mod(2), mod(3), mod(4),
            pl.BlockSpec((None, 1, D_MODEL), lambda i: (layer, 0, 0)),
            mat, mat, mat,
            pl.BlockSpec((N_EXPERTS, D_MODEL), lambda i: (0, 0)),
            pl.BlockSpec((N_EXPERTS, 1), lambda i: (0, 0)),
        ],
        out_specs=[wide, pl.BlockSpec((MM_TILE, ROW_W), tok),
                   pl.BlockSpec((None, 1, MM_TILE), lambda i: (i, 0, 0))],
        out_shape=[
            jax.ShapeDtypeStruct((stream.rows, D_MODEL), F32),
            jax.ShapeDtypeStruct((stream.rows, ROW_W), F32),
            jax.ShapeDtypeStruct((stream.rows // MM_TILE, 1, MM_TILE), jnp.int32),
        ],
        compiler_params=_params("arbitrary"),
        name=f"merge_{stream.name}",
    )(x, att, rnn.reshape(stream.rows, D_RNN), gr, ga, gb, mods, mods, mods,
      norm_g.reshape(DEPTH, 1, D_MODEL), wap, wrp, wo, router_wt, router_b.reshape(N_EXPERTS, 1))


def _plan_kernel(b_ref, pos_ref, tiles_ref):
    b = b_ref[...]
    r = lax.broadcasted_iota(jnp.int32, (TOK_TILE, TOK_TILE), 0)
    c = lax.broadcasted_iota(jnp.int32, (TOK_TILE, TOK_TILE), 1)
    before_in_tile = jnp.where(r < c, 1.0, 0.0).astype(BF16)
    br = lax.broadcasted_iota(jnp.int32, (N_TOK_TILES, N_TOK_TILES), 0)
    bc = lax.broadcasted_iota(jnp.int32, (N_TOK_TILES, N_TOK_TILES), 1)
    earlier_tiles = jnp.where(bc < br, 1.0, 0.0).astype(BF16)
    lane = lax.broadcasted_iota(jnp.int32, (1, LANES), 1)
    tile_start = (lane * MOE_TM).astype(F32)
    start = jnp.zeros((1, 1), F32)
    pos = jnp.zeros(b.shape, F32)
    tile_bucket = jnp.zeros((1, LANES), F32)
    for k in range(N_BUCKETS):
        mask = jnp.where(b == k, 1.0, 0.0)
        rank = jnp.dot(mask.astype(BF16), before_in_tile, preferred_element_type=F32)
        per_tile = jnp.sum(mask, axis=1, keepdims=True)
        tile_off = jnp.dot(earlier_tiles, jnp.broadcast_to(per_tile, (N_TOK_TILES, LANES)).astype(BF16),
                           preferred_element_type=F32)[:, :1]
        total = jnp.sum(per_tile, axis=0, keepdims=True)
        padded = jnp.floor((total + (MOE_TM - 1)) * (1.0 / MOE_TM)) * MOE_TM
        pos = pos + mask * (start + tile_off + rank)
        start = start + padded
        tile_bucket = tile_bucket + jnp.where(tile_start >= start, 1.0, 0.0)
    pos_ref[...] = pos.astype(jnp.int32)
    tiles_ref[...] = jnp.where(lane == LANES - 1, start * (1.0 / MOE_TM), tile_bucket).astype(jnp.int32)


def _plan(bucket_lat, bucket_ctx):
    bucket = jnp.concatenate([bucket_lat.reshape(-1, TOK_TILE), bucket_ctx.reshape(-1, TOK_TILE)], axis=0)
    return pl.pallas_call(
        _plan_kernel,
        out_shape=[
            jax.ShapeDtypeStruct((N_TOK_TILES, TOK_TILE), jnp.int32),
            jax.ShapeDtypeStruct((1, LANES), jnp.int32),
        ],
        compiler_params=pltpu.CompilerParams(vmem_limit_bytes=VMEM_LIMIT),
        name="plan",
    )(bucket)


def _row_copy(src, dst, sem):
    return pltpu.make_async_copy(src, dst, sem)


def _dispatch_kernel(pos_ref, h_ref, xs_in_ref, xs_ref, sem):
    del xs_in_ref

    for r in range(TOK_TILE):
        _row_copy(h_ref.at[pl.ds(r, 1), :], xs_ref.at[pl.ds(pos_ref[0, r], 1), :], sem).start(priority=r % 2)

    def wait(r, carry):
        _row_copy(h_ref.at[pl.ds(0, 1), :], xs_ref.at[pl.ds(0, 1), :], sem).wait()
        return carry

    lax.fori_loop(0, TOK_TILE, wait, 0, unroll=8)


def _dispatch(stream, pos, h2, xs):
    return pl.pallas_call(
        _dispatch_kernel,
        grid=(stream.n_tiles,),
        in_specs=[
            pl.BlockSpec((None, 1, TOK_TILE), lambda i: (stream.first_tile + i, 0, 0), memory_space=pltpu.SMEM),
            pl.BlockSpec((TOK_TILE, ROW_W), lambda i: (i, 0)),
            pl.BlockSpec(memory_space=pl.ANY),
        ],
        out_specs=pl.BlockSpec(memory_space=pl.ANY),
        out_shape=jax.ShapeDtypeStruct((MOE_ROWS, ROW_W), F32),
        input_output_aliases={2: 0},
        scratch_shapes=[pltpu.SemaphoreType.DMA(())],
        compiler_params=_params("arbitrary"),
        name=f"dispatch_{stream.name}",
    )(pos, h2, xs)


def _bucket_expert(k, which):
    p = k % PAIRS_PER_GROUP
    in_group = 0
    for order, slots in enumerate(PAIR_SLOTS):
        in_group = jnp.where(p == order, slots[which], in_group)
    return (k // PAIRS_PER_GROUP) * EXPERTS_PER_GROUP + in_group


def _weight_copies(layer, wgu_ref, wd_ref, wgu_f, wd_f, sem, bucket, slot):
    out = []
    for which in range(2):
        e = _bucket_expert(bucket, which)
        out.append(pltpu.make_async_copy(wgu_ref.at[layer, e], wgu_f.at[slot, which], sem.at[slot, 2 * which]))
        out.append(pltpu.make_async_copy(wd_ref.at[layer, e], wd_f.at[slot, which], sem.at[slot, 2 * which + 1]))
    return out


def _experts_kernel(layer, tiles_ref, xs_ref, wgu_ref, wd_ref, y_ref, wgu_f, wd_f, wgu_b, wd_b, sem, count_ref):
    i = pl.program_id(0)
    n_used = tiles_ref[0, LANES - 1]
    in_use = i < n_used
    cur = tiles_ref[0, i]
    prev = tiles_ref[0, jnp.maximum(i - 1, 0)]
    copies = functools.partial(_weight_copies, layer, wgu_ref, wd_ref, wgu_f, wd_f, sem)

    def changed(bucket, before):
        return [_bucket_expert(bucket, which) != _bucket_expert(before, which) for which in range(2)]

    def for_each_copy(bucket, slot, needed, act):
        cps = copies(bucket, slot)
        for which in range(2):
            def both(which=which):
                act(cps[2 * which])
                act(cps[2 * which + 1])

            if needed[which] is True:
                both()
            else:
                pl.when(needed[which])(both)

    @pl.when(jnp.logical_and(in_use, jnp.logical_or(i == 0, cur != prev)))
    def _():
        @pl.when(i == 0)
        def _():
            count_ref[0] = 0
            for_each_copy(cur, 0, [True, True], lambda c: c.start())

        slot = count_ref[0] % 2
        needed = [jnp.logical_or(i == 0, c) for c in changed(cur, prev)]
        for_each_copy(cur, slot, needed, lambda c: c.wait())
        nxt_i = lax.while_loop(lambda j: jnp.logical_and(j < n_used, tiles_ref[0, jnp.minimum(j, n_used - 1)] == cur),
                               lambda j: j + 1, i + 1)

        @pl.when(nxt_i < n_used)
        def _():
            nxt = tiles_ref[0, nxt_i]
            for_each_copy(nxt, 1 - slot, changed(nxt, cur), lambda c: c.start())

        for which in range(2):
            @pl.when(needed[which])
            def _(which=which):
                wgu_b[which] = wgu_f[slot, which].astype(BF16)
                wd_b[which] = wd_f[slot, which].astype(BF16)

        count_ref[0] = count_ref[0] + 1

    @pl.when(jnp.logical_not(in_use))
    def _():
        y_ref[...] = jnp.zeros_like(y_ref)

    @pl.when(in_use)
    def _():
        x = xs_ref[:, :D_MODEL].astype(BF16)

        def branch(which, gate):
            gu = jnp.dot(x, wgu_b[which], preferred_element_type=F32)
            act = jax.nn.silu(gu[:, :D_EXPERT]) * gu[:, D_EXPERT:] * gate
            return jnp.dot(act.astype(BF16), wd_b[which], preferred_element_type=F32)

        y_ref[...] = (branch(0, xs_ref[:, D_MODEL:D_MODEL + 1])
                      + branch(1, xs_ref[:, D_MODEL + 1:D_MODEL + 2]))


def _experts(tiles, xs, wgu, wd, layer):
    return pl.pallas_call(
        functools.partial(_experts_kernel, layer),
        grid_spec=pltpu.PrefetchScalarGridSpec(
            num_scalar_prefetch=1,
            grid=(N_MOE_TILES,),
            in_specs=[
                pl.BlockSpec((MOE_TM, ROW_W), lambda i, t: (i, 0)),
                pl.BlockSpec(memory_space=pl.ANY),
                pl.BlockSpec(memory_space=pl.ANY),
            ],
            out_specs=pl.BlockSpec((MOE_TM, D_MODEL), lambda i, t: (i, 0)),
            scratch_shapes=[
                pltpu.VMEM((2, 2, D_MODEL, 2 * D_EXPERT), F32), pltpu.VMEM((2, 2, D_EXPERT, D_MODEL), F32),
                pltpu.VMEM((2, D_MODEL, 2 * D_EXPERT), BF16), pltpu.VMEM((2, D_EXPERT, D_MODEL), BF16),
                pltpu.SemaphoreType.DMA((2, 4)), pltpu.SMEM((1,), jnp.int32),
            ],
        ),
        out_shape=jax.ShapeDtypeStruct((MOE_ROWS, D_MODEL), F32),
        compiler_params=_params("arbitrary"),
        name="experts",
    )(tiles, xs, wgu, wd)


def _combine_kernel(n_tiles, final, pos_ref, next_pos_ref, x_ref, g2_ref, fg_ref, y_ref, o_ref, rows_ref, sem):
    i = pl.program_id(0)

    def gather(p_ref, into):
        for r in range(TOK_TILE):
            _row_copy(y_ref.at[pl.ds(p_ref[0, r], 1), :], rows_ref.at[into, pl.ds(r, 1), :],
                      sem.at[into]).start(priority=r % 2)

    def step(slot):
        if slot == 0:
            @pl.when(i == 0)
            def _():
                gather(pos_ref, 0)

        @pl.when(i + 1 < n_tiles)
        def _():
            gather(next_pos_ref, 1 - slot)

        def wait(r, carry):
            _row_copy(y_ref.at[pl.ds(0, 1), :], rows_ref.at[slot, pl.ds(0, 1), :], sem.at[slot]).wait()
            return carry

        lax.fori_loop(0, TOK_TILE, wait, 0, unroll=8)
        x = x_ref[...] + g2_ref[...] * rows_ref[slot]
        if final:
            x = x * lax.rsqrt(jnp.mean(x * x, axis=-1, keepdims=True) + EPS) * fg_ref[...]
        o_ref[...] = x

    for parity in range(2):
        pl.when(i % 2 == parity)(functools.partial(step, parity))


def _combine(stream, pos, x, mods, y, final_g, layer):
    first, last = stream.first_tile, stream.first_tile + stream.n_tiles - 1
    return pl.pallas_call(
        functools.partial(_combine_kernel, stream.n_tiles, layer == DEPTH - 1),
        grid=(stream.n_tiles,),
        in_specs=[
            pl.BlockSpec((None, 1, TOK_TILE), lambda i: (first + i, 0, 0), memory_space=pltpu.SMEM),
            pl.BlockSpec((None, 1, TOK_TILE), lambda i: (jnp.minimum(first + i + 1, last), 0, 0),
                         memory_space=pltpu.SMEM),
            pl.BlockSpec((TOK_TILE, D_MODEL), lambda i: (i, 0)),
            pl.BlockSpec((None, 1, D_MODEL), lambda i: (layer * COND_ROWS + stream.mod_row(i, TOK_TILE), 0, 5)),
            pl.BlockSpec((1, D_MODEL), lambda i: (0, 0)),
            pl.BlockSpec(memory_space=pl.ANY),
        ],
        out_specs=pl.BlockSpec((TOK_TILE, D_MODEL), lambda i: (i, 0)),
        out_shape=jax.ShapeDtypeStruct((stream.rows, D_MODEL), F32),
        scratch_shapes=[pltpu.VMEM((2, TOK_TILE, D_MODEL), F32), pltpu.SemaphoreType.DMA((2,))],
        compiler_params=_params("arbitrary"),
        name=f"combine_{stream.name}",
    )(pos, pos, x, mods, final_g.reshape(1, D_MODEL), y)


def _rope_tables():
    t = jnp.arange(DEC_SEQ, dtype=jnp.int32)
    row = (t // GRID_W).astype(F32)
    col = (t % GRID_W).astype(F32)
    inv = ROPE_BASE ** (-jnp.arange(ROPE_PAIRS_PER_AXIS, dtype=F32) / ROPE_PAIRS_PER_AXIS)
    ang = jnp.concatenate([row[:, None] * inv, col[:, None] * inv], axis=-1)
    cos, sin = jnp.cos(ang), jnp.sin(ang)
    return jnp.concatenate([cos, cos], axis=-1), jnp.concatenate([-sin, sin], axis=-1)


def kernel(x_prompt, x_sample, cache_k, cache_v, state_h, c, c_ctx, norm1_g, norm2_g, w_ada, b_ada, w_in, conv_w, conv_b, lru_wa, lru_ba, lru_wx, lru_bx, lru_lambda, attn_sink, w_attn_proj, w_rnn_proj, w_out, router_w, router_b, w_gate_up, w_down, final_norm_g):
    xs_tok = {LATENT: x_sample.reshape(T_LAT, D_MODEL), CONTEXT: x_prompt.reshape(T_CTX, D_MODEL)}
    cond = jnp.concatenate([c_ctx[None, :], c, jnp.zeros((COND_ROWS - 1 - DEC_BATCH, D_MODEL), F32)], axis=0)
    mods = _ada(cond, w_ada, b_ada).reshape(DEPTH * COND_ROWS, 1, N_ADA * D_MODEL)
    rope_cos, rope_sin = _rope_tables()

    w_in_b = w_in.astype(BF16)
    wap_b = w_attn_proj.astype(BF16)
    wrp_b = w_rnn_proj.astype(BF16)
    wo_b = w_out.astype(BF16)
    lru_w, lru_b = _lru_block_weights(lru_wa, lru_wx, lru_ba, lru_bx)
    router_wt = router_w.T

    h0 = {LATENT: jnp.transpose(state_h.astype(F32), (1, 2, 0, 3)).reshape(DEPTH, 1, 2, DEC_BATCH, D_RNN),
          CONTEXT: jnp.zeros((DEPTH, BATCH // SUBLANES, 2, SUBLANES, D_RNN), F32)}
    ck = jnp.transpose(cache_k, (1, 0, 2, 3, 4)).reshape(DEPTH, DEC_BATCH, PAST_LEN, KV_W)
    cv = jnp.transpose(cache_v, (1, 0, 2, 3, 4)).reshape(DEPTH, DEC_BATCH, PAST_LEN, KV_W)

    new_k, new_v, new_h = [], [], []
    for l in range(DEPTH):
        merged = {}
        for stream in (LATENT, CONTEXT):
            q, k, v, xr, gr, ga, gb = _inproj(stream, xs_tok[stream], norm1_g, mods, rope_cos, rope_sin,
                                              w_in_b, l)
            if stream.rotary:
                att = _lat_attn(q, k, v, ck[l], cv[l], attn_sink[l])
            else:
                att = _ctx_attn(q, k, v, attn_sink[l])
                new_k.append(k.reshape(BATCH, SEQ, N_KV_HEADS, HEAD_DIM))
                new_v.append(v.reshape(BATCH, SEQ, N_KV_HEADS, HEAD_DIM))
            y, h_last = _rnn(stream, xr, conv_w, conv_b, lru_w, lru_b, lru_lambda, h0[stream][l], l)
            if not stream.rotary:
                new_h.append(jnp.transpose(h_last, (0, 2, 1, 3)).reshape(BATCH, 2, D_RNN))
            merged[stream] = _merge(stream, xs_tok[stream], att, y, gr, ga, gb, mods, norm2_g,
                                    wap_b, wrp_b, wo_b, router_wt, router_b, l)
        pos, tiles = _plan(merged[LATENT][2], merged[CONTEXT][2])
        pos = pos.reshape(N_TOK_TILES, 1, TOK_TILE)
        xs = jnp.zeros((MOE_ROWS, ROW_W), F32)
        for stream in (LATENT, CONTEXT):
            xs = _dispatch(stream, pos, merged[stream][1], xs)
        y_sorted = _experts(tiles, xs, w_gate_up, w_down, l)
        for stream in (LATENT, CONTEXT):
            xs_tok[stream] = _combine(stream, pos, merged[stream][0], mods, y_sorted, final_norm_g, l)

    return (xs_tok[CONTEXT].reshape(BATCH, SEQ, D_MODEL), xs_tok[LATENT].reshape(DEC_BATCH, DEC_SEQ, D_MODEL),
            jnp.stack(new_k, axis=1), jnp.stack(new_v, axis=1),
            jnp.stack(new_h, axis=1).astype(x_prompt.dtype))
```

```python
import functools
from typing import NamedTuple

import jax
import jax.numpy as jnp
from jax import lax
from jax.experimental import pallas as pl
from jax.experimental.pallas import tpu as pltpu

F32 = jnp.float32
BF16 = jnp.bfloat16

D_MODEL = 1024
BATCH = 16
SEQ = 256
DEPTH = 2
DEC_BATCH = 8
DEC_SEQ = 1024
PAST_LEN = 512
GRID_W = 64
N_HEADS = 8
N_KV_HEADS = 2
GQA_GROUP = N_HEADS // N_KV_HEADS
HEAD_DIM = 128
ATTN_W = N_HEADS * HEAD_DIM
KV_W = N_KV_HEADS * HEAD_DIM
WINDOW = 128
Q_BLOCK = 128
ROPE_BASE = 10000.0
ROPE_PAIRS_PER_AXIS = HEAD_DIM // 4
D_RNN = 1024
LRU_BLOCKS = 16
LRU_BW = D_RNN // LRU_BLOCKS
LRU_C = 8.0
CONV_W = 4
CONV_LEFT = 2
N_EXPERTS = 16
N_GROUPS = 4
EXPERTS_PER_GROUP = N_EXPERTS // N_GROUPS
D_EXPERT = 512
N_ADA = 6
EPS = 1e-6
IN_COLS = ATTN_W + 2 * KV_W + 2 * D_RNN + 2 * D_MODEL

T_LAT = DEC_BATCH * DEC_SEQ
T_CTX = BATCH * SEQ
T_ALL = T_LAT + T_CTX

SUBLANES = 8
LANES = 128
VMEM_LIMIT = 56 * 1024 * 1024

TOK_TILE = 256
MM_TILE = 512
N_TOK_TILES = T_ALL // TOK_TILE
COND_ROWS = 16


class _Stream(NamedTuple):
    name: str
    batch: int
    seq: int
    first_tile: int
    rotary: bool

    @property
    def rows(self):
        return self.batch * self.seq

    @property
    def n_tiles(self):
        return self.rows // TOK_TILE

    def mod_row(self, i, tile):
        return 1 + i // (self.seq // tile) if self.rotary else 0


LATENT = _Stream("lat", DEC_BATCH, DEC_SEQ, 0, True)
CONTEXT = _Stream("ctx", BATCH, SEQ, T_LAT // TOK_TILE, False)

RNN_CB = 256
RNN_CHUNK = 64
RNN_UNROLL = 8
LOG2_E = 1.4426950408889634
PAIR_SLOTS = ((0, 1), (0, 2), (0, 3), (1, 3), (1, 2), (3, 2))
PAIRS_PER_GROUP = len(PAIR_SLOTS)
N_BUCKETS = N_GROUPS * PAIRS_PER_GROUP
ROW_W = D_MODEL + LANES
MOE_TM = 256
N_MOE_TILES = (T_ALL + N_BUCKETS * (MOE_TM - 1) + MOE_TM - 1) // MOE_TM
MOE_ROWS = N_MOE_TILES * MOE_TM


def _params(*sem):
    return pltpu.CompilerParams(dimension_semantics=sem, vmem_limit_bytes=VMEM_LIMIT)


def _sigmoid(x):
    return 0.5 * jnp.tanh(0.5 * x) + 0.5


def _ada_kernel(cond_ref, w_ref, b_ref, o_ref):
    s = jax.nn.silu(cond_ref[...]).astype(BF16)
    o_ref[...] = jnp.dot(s, w_ref[...].astype(BF16), preferred_element_type=F32) + b_ref[...]


def _ada(cond, w_ada, b_ada):
    cols = N_ADA * D_MODEL
    tn = 1536
    return pl.pallas_call(
        _ada_kernel,
        grid=(DEPTH, cols // tn),
        in_specs=[
            pl.BlockSpec((COND_ROWS, D_MODEL), lambda l, j: (0, 0)),
            pl.BlockSpec((None, D_MODEL, tn), lambda l, j: (l, 0, j)),
            pl.BlockSpec((None, 1, tn), lambda l, j: (l, 0, j)),
        ],
        out_specs=pl.BlockSpec((None, COND_ROWS, tn), lambda l, j: (l, 0, j)),
        out_shape=jax.ShapeDtypeStruct((DEPTH, COND_ROWS, cols), F32),
        compiler_params=_params("arbitrary", "arbitrary"),
        name="ada",
    )(cond, w_ada, b_ada.reshape(DEPTH, 1, cols))


def _inproj_kernel(rotary, x_ref, g_ref, sh_ref, sc_ref, *refs):
    if rotary:
        cos_ref, sin_ref, w_ref, q_ref, k_ref, v_ref, xr_ref, gr_ref, ga_ref, gb_ref = refs
    else:
        w_ref, q_ref, k_ref, v_ref, xr_ref, gr_ref, ga_ref, gb_ref = refs
    x = x_ref[...]
    y = x * lax.rsqrt(jnp.mean(x * x, axis=-1, keepdims=True) + EPS) * g_ref[...]
    h = (y * (1.0 + sc_ref[...]) + sh_ref[...]).astype(BF16)

    def proj(lo, width):
        return jnp.dot(h, w_ref[:, lo:lo + width], preferred_element_type=F32)

    def rope(t):
        if not rotary:
            return t
        return t * cos_ref[...] + pltpu.roll(t, HEAD_DIM // 2, 1) * sin_ref[...]

    scale = HEAD_DIM ** -0.5 * LOG2_E
    qk = proj(0, ATTN_W + KV_W)
    for hd in range(N_HEADS):
        q = (rope(qk[:, hd * HEAD_DIM:(hd + 1) * HEAD_DIM]) * scale).astype(BF16)
        for blk in range(MM_TILE // Q_BLOCK):
            q_ref[blk, hd] = q[blk * Q_BLOCK:(blk + 1) * Q_BLOCK]
    for g in range(N_KV_HEADS):
        k = qk[:, ATTN_W + g * HEAD_DIM:ATTN_W + (g + 1) * HEAD_DIM]
        k_ref[:, g * HEAD_DIM:(g + 1) * HEAD_DIM] = rope(k)
    v_ref[...] = proj(ATTN_W + KV_W, KV_W)
    base = ATTN_W + 2 * KV_W
    xr_ref[...] = proj(base, D_RNN).astype(BF16)
    gr_ref[...] = proj(base + D_RNN, D_RNN).astype(BF16)
    ga_ref[...] = proj(base + 2 * D_RNN, D_MODEL).astype(BF16)
    gb_ref[...] = proj(base + 2 * D_RNN + D_MODEL, D_MODEL).astype(BF16)


def _inproj(stream, x, norm_g, mods, rope_cos, rope_sin, w_in, layer):
    row = lambda i: layer * COND_ROWS + stream.mod_row(i, MM_TILE)
    tok = lambda i: (i, 0)
    rope_blk = lambda i: (i % (stream.seq // MM_TILE), 0)
    wide = pl.BlockSpec((MM_TILE, D_MODEL), tok)
    kv = pl.BlockSpec((MM_TILE, KV_W), tok)
    rope_specs = [pl.BlockSpec((MM_TILE, HEAD_DIM), rope_blk)] * 2 if stream.rotary else []
    rope_args = (rope_cos, rope_sin) if stream.rotary else ()
    return pl.pallas_call(
        functools.partial(_inproj_kernel, stream.rotary),
        grid=(stream.rows // MM_TILE,),
        in_specs=[
            wide,
            pl.BlockSpec((None, 1, D_MODEL), lambda i: (layer, 0, 0)),
            pl.BlockSpec((None, 1, D_MODEL), lambda i: (row(i), 0, 0)),
            pl.BlockSpec((None, 1, D_MODEL), lambda i: (row(i), 0, 1)),
        ] + rope_specs + [
            pl.BlockSpec((None, D_MODEL, IN_COLS), lambda i: (layer, 0, 0)),
        ],
        out_specs=[pl.BlockSpec((MM_TILE // Q_BLOCK, N_HEADS, Q_BLOCK, HEAD_DIM), lambda i: (i, 0, 0, 0)),
                   kv, kv, wide, wide, wide, wide],
        out_shape=[
            jax.ShapeDtypeStruct((stream.rows // Q_BLOCK, N_HEADS, Q_BLOCK, HEAD_DIM), BF16),
            jax.ShapeDtypeStruct((stream.rows, KV_W), F32),
            jax.ShapeDtypeStruct((stream.rows, KV_W), F32),
            jax.ShapeDtypeStruct((stream.rows, D_RNN), BF16),
            jax.ShapeDtypeStruct((stream.rows, D_RNN), BF16),
            jax.ShapeDtypeStruct((stream.rows, D_MODEL), BF16),
            jax.ShapeDtypeStruct((stream.rows, D_MODEL), BF16),
        ],
        compiler_params=_params("arbitrary"),
        name=f"inproj_{stream.name}",
    )(x, norm_g.reshape(DEPTH, 1, D_MODEL), mods, mods, *rope_args, w_in)


def _qk(q, k):
    return lax.dot_general(q, k, (((1,), (1,)), ((), ())), preferred_element_type=F32)


def _store_heads(o_ref, g, o, rows):
    for r in range(GQA_GROUP):
        hd = g * GQA_GROUP + r
        o_ref[:, hd * HEAD_DIM:(hd + 1) * HEAD_DIM] = o[r * rows:(r + 1) * rows].astype(BF16)


def _softmax_rows(s_tiles, sink, p_ref, tail_ref, rows):
    top = s_tiles[0][1]
    for _, t in s_tiles[1:]:
        top = jnp.maximum(top, t)
    m = jnp.maximum(jnp.max(top, axis=-1, keepdims=True), sink)
    for c, t in s_tiles:
        p_ref[rows, c:c + LANES] = jnp.exp2(t - m).astype(BF16)
    tail_ref[rows, :] = jnp.broadcast_to(jnp.exp2(sink - m), top.shape)


def _values_and_ones(v):
    ones = jnp.where(lax.broadcasted_iota(jnp.int32, v.shape, 1) == 0, 1.0, 0.0).astype(v.dtype)
    return jnp.concatenate([v, ones], axis=1)


def _attend(p_ref, tail_ref, v):
    acc = jnp.dot(p_ref[...], _values_and_ones(v), preferred_element_type=F32)
    return acc[:, :HEAD_DIM] / (acc[:, HEAD_DIM:HEAD_DIM + 1] + tail_ref[:, :1])


def _ctx_attn_kernel(sink_ref, q_ref, k_ref, v_ref, o_ref, s_ref, p_ref, tail_ref):
    for g in range(N_KV_HEADS):
        heads = q_ref[:, g * GQA_GROUP:(g + 1) * GQA_GROUP]
        q = jnp.swapaxes(heads, 0, 1).reshape(GQA_GROUP * SEQ, HEAD_DIM)
        k = k_ref[:, g * HEAD_DIM:(g + 1) * HEAD_DIM].astype(BF16)
        v = v_ref[:, g * HEAD_DIM:(g + 1) * HEAD_DIM].astype(BF16)
        s_ref[...] = _qk(q, k)
        for r in range(GQA_GROUP):
            rows = slice(r * SEQ, (r + 1) * SEQ)
            tiles = [(c, s_ref[rows, c:c + LANES]) for c in range(0, SEQ, LANES)]
            _softmax_rows(tiles, sink_ref[g * GQA_GROUP + r] * LOG2_E, p_ref, tail_ref, rows)
        _store_heads(o_ref, g, _attend(p_ref, tail_ref, v), SEQ)


def _ctx_attn(q, k, v, sink):
    blk = lambda b: (b, 0)
    rows = GQA_GROUP * SEQ
    return pl.pallas_call(
        _ctx_attn_kernel,
        grid=(BATCH,),
        in_specs=[
            pl.BlockSpec(memory_space=pltpu.SMEM),
            pl.BlockSpec((SEQ // Q_BLOCK, N_HEADS, Q_BLOCK, HEAD_DIM), lambda b: (b, 0, 0, 0)),
            pl.BlockSpec((SEQ, KV_W), blk),
            pl.BlockSpec((SEQ, KV_W), blk),
        ],
        out_specs=pl.BlockSpec((SEQ, ATTN_W), lambda b: (b, 0)),
        out_shape=jax.ShapeDtypeStruct((T_CTX, ATTN_W), BF16),
        scratch_shapes=[pltpu.VMEM((rows, SEQ), F32), pltpu.VMEM((rows, SEQ), BF16),
                        pltpu.VMEM((rows, LANES), F32)],
        compiler_params=_params("arbitrary"),
        name="ctx_attn",
    )(sink, q, k, v)


def _lat_attn_kernel(sink_ref, q_ref, kp_ref, kc_ref, kn_ref, vp_ref, vc_ref, vn_ref,
                     ck_ref, cv_ref, o_ref, band_ref, mask_ref, s_ref, p_ref, tail_ref):
    j = pl.program_id(1)
    rows = GQA_GROUP * Q_BLOCK
    band = Q_BLOCK + 2 * WINDOW

    @pl.when(jnp.logical_and(pl.program_id(0) == 0, j == 0))
    def _():
        ahead = (lax.broadcasted_iota(jnp.int32, (rows, band), 1) - WINDOW
                 - lax.broadcasted_iota(jnp.int32, (rows, band), 0) % Q_BLOCK)
        band_ref[...] = jnp.where(jnp.abs(ahead) <= WINDOW, 0.0, -jnp.inf)

    kpos = j * Q_BLOCK - WINDOW + lax.broadcasted_iota(jnp.int32, (1, band), 1)
    mask_ref[...] = band_ref[...] + jnp.where((kpos >= 0) & (kpos < DEC_SEQ), 0.0, -jnp.inf)
    for g in range(N_KV_HEADS):
        cols = slice(g * HEAD_DIM, (g + 1) * HEAD_DIM)
        q = q_ref[g * GQA_GROUP:(g + 1) * GQA_GROUP].reshape(rows, HEAD_DIM)
        keys = jnp.concatenate([kp_ref[:, cols], kc_ref[:, cols], kn_ref[:, cols], ck_ref[:, cols]],
                               axis=0).astype(BF16)
        vals = jnp.concatenate([vp_ref[:, cols], vc_ref[:, cols], vn_ref[:, cols], cv_ref[:, cols]],
                               axis=0).astype(BF16)
        s_ref[...] = _qk(q, keys)
        for r in range(GQA_GROUP):
            rows_r = slice(r * Q_BLOCK, (r + 1) * Q_BLOCK)
            tiles = [(c, s_ref[rows_r, c:c + LANES] + mask_ref[rows_r, c:c + LANES]) for c in range(0, band, LANES)]
            tiles += [(c, s_ref[rows_r, c:c + LANES]) for c in range(band, band + PAST_LEN, LANES)]
            _softmax_rows(tiles, sink_ref[g * GQA_GROUP + r] * LOG2_E, p_ref, tail_ref, rows_r)
        _store_heads(o_ref, g, _attend(p_ref, tail_ref, vals), Q_BLOCK)


def _lat_attn(q, k, v, cache_k, cache_v, sink):
    nb = DEC_SEQ // Q_BLOCK
    cur = lambda b, j: (b * nb + j, 0)
    prev = lambda b, j: (b * nb + jnp.maximum(j - 1, 0), 0)
    nxt = lambda b, j: (b * nb + jnp.minimum(j + 1, nb - 1), 0)
    kvb = lambda im: pl.BlockSpec((Q_BLOCK, KV_W), im)
    cache = pl.BlockSpec((None, PAST_LEN, KV_W), lambda b, j: (b, 0, 0))
    return pl.pallas_call(
        _lat_attn_kernel,
        grid=(DEC_BATCH, nb),
        in_specs=[
            pl.BlockSpec(memory_space=pltpu.SMEM),
            pl.BlockSpec((None, N_HEADS, Q_BLOCK, HEAD_DIM), lambda b, j: (b * nb + j, 0, 0, 0)),
            kvb(prev), kvb(cur), kvb(nxt), kvb(prev), kvb(cur), kvb(nxt),
            cache, cache,
        ],
        out_specs=pl.BlockSpec((Q_BLOCK, ATTN_W), cur),
        out_shape=jax.ShapeDtypeStruct((T_LAT, ATTN_W), BF16),
        scratch_shapes=[
            pltpu.VMEM((GQA_GROUP * Q_BLOCK, Q_BLOCK + 2 * WINDOW), F32),
            pltpu.VMEM((GQA_GROUP * Q_BLOCK, Q_BLOCK + 2 * WINDOW), F32),
            pltpu.VMEM((GQA_GROUP * Q_BLOCK, Q_BLOCK + 2 * WINDOW + PAST_LEN), F32),
            pltpu.VMEM((GQA_GROUP * Q_BLOCK, Q_BLOCK + 2 * WINDOW + PAST_LEN), BF16),
            pltpu.VMEM((GQA_GROUP * Q_BLOCK, LANES), F32),
        ],
        compiler_params=_params("arbitrary", "arbitrary"),
        name="lat_attn",
    )(sink, q, k, k, k, v, v, v, cache_k, cache_v)


def _rnn_kernel(seq, xr_ref, cw_ref, cb_ref, w_ref, b_ref, lam_ref, h0_ref,
                y_ref, hl_ref, xt_ref, yt_ref, a_ref, u_ref):
    n_chunks = seq // RNN_CHUNK
    rows = RNN_CHUNK * SUBLANES
    halo = jnp.zeros((CONV_LEFT, SUBLANES, RNN_CB), F32)
    xt_ref[0:CONV_LEFT] = halo
    xt_ref[seq + CONV_LEFT:seq + 2 * CONV_LEFT] = halo

    def load_chunk(c, carry):
        t0 = pl.multiple_of(c * RNN_CHUNK, RNN_CHUNK)
        x = xr_ref[:, pl.ds(t0, RNN_CHUNK), :].astype(F32)
        xt_ref[pl.ds(t0 + CONV_LEFT, RNN_CHUNK)] = jnp.swapaxes(x, 0, 1)
        return carry

    lax.fori_loop(0, n_chunks, load_chunk, 0)

    half_w = 0.5 * cw_ref[...]
    half_b = 0.5 * cb_ref[...]

    def half_conv(t0):
        acc = half_b.reshape(1, 1, RNN_CB)
        for tap in range(CONV_W):
            acc = acc + xt_ref[pl.ds(t0 + tap, RNN_CHUNK)] * half_w[tap:tap + 1, :].reshape(1, 1, RNN_CB)
        return acc

    for d in range(2):
        decay = (-0.5 * LRU_C * LOG2_E) * jax.nn.softplus(-lam_ref[d:d + 1, :])
        half_bias = b_ref[d:d + 1, :]

        def chunk(ci, h, d=d, decay=decay, half_bias=half_bias):
            c = ci if d == 0 else n_chunks - 1 - ci
            t0 = pl.multiple_of(c * RNN_CHUNK, RNN_CHUNK)
            hx = half_conv(t0).reshape(rows, RNN_CB)
            z = jnp.dot(hx.astype(BF16), w_ref[d], preferred_element_type=F32) + half_bias
            a = jnp.exp2(decay * jnp.tanh(z[:, :RNN_CB]) + decay)
            m = 1.0 - a * a
            mult = jnp.where(m == 0.0, 0.0, m * lax.rsqrt(m))
            u = mult * ((jnp.tanh(z[:, RNN_CB:]) + 1.0) * hx)
            a_ref[...] = a.reshape(RNN_CHUNK, SUBLANES, RNN_CB)
            u_ref[...] = u.reshape(RNN_CHUNK, SUBLANES, RNN_CB)

            def steps(gi, h):
                s0 = pl.multiple_of((gi if d == 0 else RNN_CHUNK // RNN_UNROLL - 1 - gi) * RNN_UNROLL, RNN_UNROLL)
                for j in (range(RNN_UNROLL) if d == 0 else reversed(range(RNN_UNROLL))):
                    h = a_ref[s0 + j] * h + u_ref[s0 + j]
                    if d == 0:
                        yt_ref[t0 + s0 + j] = h
                    else:
                        yt_ref[t0 + s0 + j] = yt_ref[t0 + s0 + j] + h
                return h

            return lax.fori_loop(0, RNN_CHUNK // RNN_UNROLL, steps, h)

        hl_ref[d] = lax.fori_loop(0, n_chunks, chunk, h0_ref[d])

    def store_chunk(c, carry):
        t0 = pl.multiple_of(c * RNN_CHUNK, RNN_CHUNK)
        y = jnp.swapaxes(yt_ref[pl.ds(t0, RNN_CHUNK)], 0, 1)
        y_ref[:, pl.ds(t0, RNN_CHUNK), :] = y.astype(BF16)
        return carry

    lax.fori_loop(0, n_chunks, store_chunk, 0)


def _rnn(stream, xr, conv_w, conv_b, w_blk, b_blk, lam, h0, layer):
    n_cb = D_RNN // RNN_CB
    seq = stream.seq
    n_groups = stream.batch // SUBLANES
    return pl.pallas_call(
        functools.partial(_rnn_kernel, seq),
        grid=(n_groups, n_cb),
        in_specs=[
            pl.BlockSpec((SUBLANES, seq, RNN_CB), lambda g, j: (g, 0, j)),
            pl.BlockSpec((None, CONV_W, RNN_CB), lambda g, j: (layer, 0, j)),
            pl.BlockSpec((None, 1, RNN_CB), lambda g, j: (layer, 0, j)),
            pl.BlockSpec((None, None, 2, RNN_CB, 2 * RNN_CB), lambda g, j: (layer, j, 0, 0, 0)),
            pl.BlockSpec((None, None, 2, 2 * RNN_CB), lambda g, j: (layer, j, 0, 0)),
            pl.BlockSpec((None, 2, RNN_CB), lambda g, j: (layer, 0, j)),
            pl.BlockSpec((None, 2, SUBLANES, RNN_CB), lambda g, j: (g, 0, 0, j)),
        ],
        out_specs=[
            pl.BlockSpec((SUBLANES, seq, RNN_CB), lambda g, j: (g, 0, j)),
            pl.BlockSpec((None, 2, SUBLANES, RNN_CB), lambda g, j: (g, 0, 0, j)),
        ],
        out_shape=[
            jax.ShapeDtypeStruct((n_groups * SUBLANES, seq, D_RNN), BF16),
            jax.ShapeDtypeStruct((n_groups, 2, SUBLANES, D_RNN), F32),
        ],
        scratch_shapes=[
            pltpu.VMEM((seq + 2 * CONV_LEFT, SUBLANES, RNN_CB), F32),
            pltpu.VMEM((seq, SUBLANES, RNN_CB), F32),
            pltpu.VMEM((RNN_CHUNK, SUBLANES, RNN_CB), F32),
            pltpu.VMEM((RNN_CHUNK, SUBLANES, RNN_CB), F32),
        ],
        compiler_params=_params("arbitrary", "arbitrary"),
        name=f"rnn_{stream.name}",
    )(xr.reshape(stream.batch, seq, D_RNN), conv_w, conv_b.reshape(DEPTH, 1, D_RNN), w_blk, b_blk, lam, h0)


def _lru_block_weights(lru_wa, lru_wx, lru_ba, lru_bx):
    n_cb = D_RNN // RNN_CB
    per = RNN_CB // LRU_BW

    def dense(w):
        w = w.reshape(DEPTH, 2, n_cb, per, LRU_BW, LRU_BW)
        eye = jnp.eye(per, dtype=w.dtype)
        full = jnp.einsum("ldcpkj,pq->ldcpkqj", w, eye)
        return full.reshape(DEPTH, 2, n_cb, RNN_CB, RNN_CB)

    w = jnp.concatenate([dense(lru_wa), dense(lru_wx)], axis=-1)
    w = jnp.transpose(w, (0, 2, 1, 3, 4)).astype(BF16)
    b = jnp.concatenate([lru_ba.reshape(DEPTH, 2, n_cb, RNN_CB),
                         lru_bx.reshape(DEPTH, 2, n_cb, RNN_CB)], axis=-1)
    return w, jnp.transpose(0.5 * b, (0, 2, 1, 3))


def _route(scores, sel):
    grp_score = []
    for g in range(N_GROUPS):
        a, b, c, d = sel[g * EXPERTS_PER_GROUP:(g + 1) * EXPERTS_PER_GROUP]
        hi1, lo1 = jnp.maximum(a, b), jnp.minimum(a, b)
        hi2, lo2 = jnp.maximum(c, d), jnp.minimum(c, d)
        grp_score.append(jnp.maximum(hi1, hi2) + jnp.maximum(jnp.minimum(hi1, hi2), jnp.maximum(lo1, lo2)))
    best = jnp.zeros_like(grp_score[0], dtype=jnp.int32)
    best_val = grp_score[0]
    for g in range(1, N_GROUPS):
        better = grp_score[g] > best_val
        best = jnp.where(better, g, best)
        best_val = jnp.where(better, grp_score[g], best_val)
    chosen = []
    for e in range(N_EXPERTS):
        g = e // EXPERTS_PER_GROUP
        rank = jnp.zeros_like(best)
        for o in range(g * EXPERTS_PER_GROUP, (g + 1) * EXPERTS_PER_GROUP):
            if o == e:
                continue
            ahead = (sel[o] >= sel[e]) if o < e else (sel[o] > sel[e])
            rank = rank + jnp.where(ahead, 1, 0)
        chosen.append(jnp.where(best == g, rank, 2) < 2)
    taken, gate = [], []
    for j in range(EXPERTS_PER_GROUP):
        t = jnp.zeros_like(best)
        s = jnp.zeros_like(scores[0])
        for g in range(N_GROUPS):
            e = g * EXPERTS_PER_GROUP + j
            t = t + jnp.where(chosen[e], 1, 0)
            s = s + jnp.where(chosen[e], scores[e], 0.0)
        taken.append(t > 0)
        gate.append(s)
    total = gate[0] + gate[1] + gate[2] + gate[3]
    pair = jnp.zeros_like(best)
    w_a = jnp.zeros_like(total)
    w_b = jnp.zeros_like(total)
    for order, (a, b) in enumerate(PAIR_SLOTS):
        both = taken[a] & taken[b]
        pair = jnp.where(both, order, pair)
        w_a = jnp.where(both, gate[a], w_a)
        w_b = jnp.where(both, gate[b], w_b)
    return best * PAIRS_PER_GROUP + pair, w_a / total, w_b / total


def _merge_kernel(x_ref, att_ref, rnn_ref, gr_ref, ga_ref, gb_ref,
                  g1_ref, sh2_ref, sc2_ref, n2_ref, wap_ref, wrp_ref, wo_ref, rw_ref, rb_ref,
                  xo_ref, h2_ref, bucket_ref):
    att = jnp.dot(att_ref[...], wap_ref[...], preferred_element_type=F32)
    gated = (jax.nn.gelu(gr_ref[...].astype(F32)) * rnn_ref[...].astype(F32)).astype(BF16)
    rnn = jnp.dot(gated, wrp_ref[...], preferred_element_type=F32)
    merged = _sigmoid(ga_ref[...].astype(F32)) * att + _sigmoid(gb_ref[...].astype(F32)) * rnn
    x = x_ref[...] + g1_ref[...] * jnp.dot(merged.astype(BF16), wo_ref[...], preferred_element_type=F32)
    xo_ref[...] = x
    y = x * lax.rsqrt(jnp.mean(x * x, axis=-1, keepdims=True) + EPS) * n2_ref[...]
    h2 = y * (1.0 + sc2_ref[...]) + sh2_ref[...]
    h2_ref[:, :D_MODEL] = h2
    h_hi = h2.astype(BF16)
    h_lo = (h2 - h_hi.astype(F32)).astype(BF16)
    rw = rw_ref[...]
    w_hi = rw.astype(BF16)
    w_lo = (rw - w_hi.astype(F32)).astype(BF16)
    logits = _qk(w_hi, h_hi) + (_qk(w_hi, h_lo) + _qk(w_lo, h_hi))
    score = _sigmoid(logits)
    sel = score + rb_ref[...]
    bucket, gate_a, gate_b = _route([score[e:e + 1, :] for e in range(N_EXPERTS)],
                                [sel[e:e + 1, :] for e in range(N_EXPERTS)])
    bucket_ref[...] = bucket
    pad = jnp.zeros((LANES - 2, MM_TILE), F32)
    h2_ref[:, D_MODEL:] = jnp.concatenate([gate_a, gate_b, pad], axis=0).T


def _merge(stream, x, att, rnn, gr, ga, gb, mods, norm_g, wap, wrp, wo, router_wt, router_b, layer):
    tok = lambda i: (i, 0)
    wide = pl.BlockSpec((MM_TILE, D_MODEL), tok)
    mod = lambda col: pl.BlockSpec((None, 1, D_MODEL),
                                   lambda i: (layer * COND_ROWS + stream.mod_row(i, MM_TILE), 0, col))
    mat = pl.BlockSpec((None, D_MODEL, D_MODEL), lambda i: (layer, 0, 0))
    return pl.pallas_call(
        _merge_kernel,
        grid=(stream.rows // MM_TILE,),
        in_specs=[
            wide, wide, wide, wide, wide, wide,
            mod(2), mod(3), mod(4),
            pl.BlockSpec((None, 1, D_MODEL), lambda i: (layer, 0, 0)),
            mat, mat, mat,
            pl.BlockSpec((N_EXPERTS, D_MODEL), lambda i: (0, 0)),
            pl.BlockSpec((N_EXPERTS, 1), lambda i: (0, 0)),
        ],
        out_specs=[wide, pl.BlockSpec((MM_TILE, ROW_W), tok),
                   pl.BlockSpec((None, 1, MM_TILE), lambda i: (i, 0, 0))],
        out_shape=[
            jax.ShapeDtypeStruct((stream.rows, D_MODEL), F32),
            jax.ShapeDtypeStruct((stream.rows, ROW_W), F32),
            jax.ShapeDtypeStruct((stream.rows // MM_TILE, 1, MM_TILE), jnp.int32),
        ],
        compiler_params=_params("arbitrary"),
        name=f"merge_{stream.name}",
    )(x, att, rnn.reshape(stream.rows, D_RNN), gr, ga, gb, mods, mods, mods,
      norm_g.reshape(DEPTH, 1, D_MODEL), wap, wrp, wo, router_wt, router_b.reshape(N_EXPERTS, 1))


def _plan_kernel(b_ref, pos_ref, tiles_ref):
    b = b_ref[...]
    r = lax.broadcasted_iota(jnp.int32, (TOK_TILE, TOK_TILE), 0)
    c = lax.broadcasted_iota(jnp.int32, (TOK_TILE, TOK_TILE), 1)
    before_in_tile = jnp.where(r < c, 1.0, 0.0).astype(BF16)
    br = lax.broadcasted_iota(jnp.int32, (N_TOK_TILES, N_TOK_TILES), 0)
    bc = lax.broadcasted_iota(jnp.int32, (N_TOK_TILES, N_TOK_TILES), 1)
    earlier_tiles = jnp.where(bc < br, 1.0, 0.0).astype(BF16)
    lane = lax.broadcasted_iota(jnp.int32, (1, LANES), 1)
    tile_start = (lane * MOE_TM).astype(F32)
    start = jnp.zeros((1, 1), F32)
    pos = jnp.zeros(b.shape, F32)
    tile_bucket = jnp.zeros((1, LANES), F32)
    for k in range(N_BUCKETS):
        mask = jnp.where(b == k, 1.0, 0.0)
        rank = jnp.dot(mask.astype(BF16), before_in_tile, preferred_element_type=F32)
        per_tile = jnp.sum(mask, axis=1, keepdims=True)
        tile_off = jnp.dot(earlier_tiles, jnp.broadcast_to(per_tile, (N_TOK_TILES, LANES)).astype(BF16),
                           preferred_element_type=F32)[:, :1]
        total = jnp.sum(per_tile, axis=0, keepdims=True)
        padded = jnp.floor((total + (MOE_TM - 1)) * (1.0 / MOE_TM)) * MOE_TM
        pos = pos + mask * (start + tile_off + rank)
        start = start + padded
        tile_bucket = tile_bucket + jnp.where(tile_start >= start, 1.0, 0.0)
    pos_ref[...] = pos.astype(jnp.int32)
    tiles_ref[...] = jnp.where(lane == LANES - 1, start * (1.0 / MOE_TM), tile_bucket).astype(jnp.int32)


def _plan(bucket_lat, bucket_ctx):
    bucket = jnp.concatenate([bucket_lat.reshape(-1, TOK_TILE), bucket_ctx.reshape(-1, TOK_TILE)], axis=0)
    return pl.pallas_call(
        _plan_kernel,
        out_shape=[
            jax.ShapeDtypeStruct((N_TOK_TILES, TOK_TILE), jnp.int32),
            jax.ShapeDtypeStruct((1, LANES), jnp.int32),
        ],
        compiler_params=pltpu.CompilerParams(vmem_limit_bytes=VMEM_LIMIT),
        name="plan",
    )(bucket)


def _row_copy(src, dst, sem):
    return pltpu.make_async_copy(src, dst, sem)


def _dispatch_kernel(pos_ref, h_ref, xs_in_ref, xs_ref, sem):
    del xs_in_ref

    for r in range(TOK_TILE):
        _row_copy(h_ref.at[pl.ds(r, 1), :], xs_ref.at[pl.ds(pos_ref[0, r], 1), :], sem).start(priority=r % 2)

    def wait(r, carry):
        _row_copy(h_ref.at[pl.ds(0, 1), :], xs_ref.at[pl.ds(0, 1), :], sem).wait()
        return carry

    lax.fori_loop(0, TOK_TILE, wait, 0, unroll=8)


def _dispatch(stream, pos, h2, xs):
    return pl.pallas_call(
        _dispatch_kernel,
        grid=(stream.n_tiles,),
        in_specs=[
            pl.BlockSpec((None, 1, TOK_TILE), lambda i: (stream.first_tile + i, 0, 0), memory_space=pltpu.SMEM),
            pl.BlockSpec((TOK_TILE, ROW_W), lambda i: (i, 0)),
            pl.BlockSpec(memory_space=pl.ANY),
        ],
        out_specs=pl.BlockSpec(memory_space=pl.ANY),
        out_shape=jax.ShapeDtypeStruct((MOE_ROWS, ROW_W), F32),
        input_output_aliases={2: 0},
        scratch_shapes=[pltpu.SemaphoreType.DMA(())],
        compiler_params=_params("arbitrary"),
        name=f"dispatch_{stream.name}",
    )(pos, h2, xs)


def _bucket_expert(k, which):
    p = k % PAIRS_PER_GROUP
    in_group = 0
    for order, slots in enumerate(PAIR_SLOTS):
        in_group = jnp.where(p == order, slots[which], in_group)
    return (k // PAIRS_PER_GROUP) * EXPERTS_PER_GROUP + in_group


def _weight_copies(layer, wgu_ref, wd_ref, wgu_f, wd_f, sem, bucket, which, landing):
    e = _bucket_expert(bucket, which)
    return [pltpu.make_async_copy(wgu_ref.at[layer, e], wgu_f.at[landing, which], sem.at[landing, 2 * which]),
            pltpu.make_async_copy(wd_ref.at[layer, e], wd_f.at[landing, which], sem.at[landing, 2 * which + 1])]


def _experts_kernel(layer, tiles_ref, xs_ref, wgu_ref, wd_ref, y_ref, wgu_f, wd_f, sem, home_ref):
    i = pl.program_id(0)
    n_used = tiles_ref[0, LANES - 1]
    in_use = i < n_used
    cur = tiles_ref[0, i]
    prev = tiles_ref[0, jnp.maximum(i - 1, 0)]

    def changed(bucket, before):
        return [_bucket_expert(bucket, which) != _bucket_expert(before, which) for which in range(2)]

    def copies(bucket, which, landing):
        return _weight_copies(layer, wgu_ref, wd_ref, wgu_f, wd_f, sem, bucket, which, landing)

    @pl.when(jnp.logical_and(in_use, jnp.logical_or(i == 0, cur != prev)))
    def _():
        @pl.when(i == 0)
        def _():
            for which in range(2):
                home_ref[which] = 1
                for c in copies(cur, which, 0):
                    c.start()

        needed = [jnp.logical_or(i == 0, c) for c in changed(cur, prev)]
        for which in range(2):
            @pl.when(needed[which])
            def _(which=which):
                landing = 1 - home_ref[which]
                for c in copies(cur, which, landing):
                    c.wait()
                home_ref[which] = landing

        nxt_i = lax.while_loop(lambda j: jnp.logical_and(j < n_used, tiles_ref[0, jnp.minimum(j, n_used - 1)] == cur),
                               lambda j: j + 1, i + 1)

        @pl.when(nxt_i < n_used)
        def _():
            nxt = tiles_ref[0, nxt_i]
            for which, differs in enumerate(changed(nxt, cur)):
                @pl.when(differs)
                def _(which=which):
                    for c in copies(nxt, which, 1 - home_ref[which]):
                        c.start()

    @pl.when(jnp.logical_not(in_use))
    def _():
        y_ref[...] = jnp.zeros_like(y_ref)

    @pl.when(in_use)
    def _():
        x = xs_ref[:, :D_MODEL].astype(BF16)

        def branch(which, gate):
            home = home_ref[which]
            gu = jnp.dot(x, wgu_f[home, which].astype(BF16), preferred_element_type=F32)
            act = jax.nn.silu(gu[:, :D_EXPERT]) * gu[:, D_EXPERT:] * gate
            return jnp.dot(act.astype(BF16), wd_f[home, which].astype(BF16), preferred_element_type=F32)

        y_ref[...] = (branch(0, xs_ref[:, D_MODEL:D_MODEL + 1])
                      + branch(1, xs_ref[:, D_MODEL + 1:D_MODEL + 2]))


def _experts(tiles, xs, wgu, wd, layer):
    return pl.pallas_call(
        functools.partial(_experts_kernel, layer),
        grid_spec=pltpu.PrefetchScalarGridSpec(
            num_scalar_prefetch=1,
            grid=(N_MOE_TILES,),
            in_specs=[
                pl.BlockSpec((MOE_TM, ROW_W), lambda i, t: (i, 0)),
                pl.BlockSpec(memory_space=pl.ANY),
                pl.BlockSpec(memory_space=pl.ANY),
            ],
            out_specs=pl.BlockSpec((MOE_TM, D_MODEL), lambda i, t: (i, 0)),
            scratch_shapes=[
                pltpu.VMEM((2, 2, D_MODEL, 2 * D_EXPERT), F32), pltpu.VMEM((2, 2, D_EXPERT, D_MODEL), F32),
                pltpu.SemaphoreType.DMA((2, 4)), pltpu.SMEM((2,), jnp.int32),
            ],
        ),
        out_shape=jax.ShapeDtypeStruct((MOE_ROWS, D_MODEL), F32),
        compiler_params=_params("arbitrary"),
        name="experts",
    )(tiles, xs, wgu, wd)


def _combine_kernel(n_tiles, final, pos_ref, next_pos_ref, x_ref, g2_ref, fg_ref, y_ref, o_ref, rows_ref, sem):
    i = pl.program_id(0)

    def gather(p_ref, into):
        for r in range(TOK_TILE):
            _row_copy(y_ref.at[pl.ds(p_ref[0, r], 1), :], rows_ref.at[into, pl.ds(r, 1), :],
                      sem.at[into]).start(priority=r % 2)

    def step(slot):
        if slot == 0:
            @pl.when(i == 0)
            def _():
                gather(pos_ref, 0)

        @pl.when(i + 1 < n_tiles)
        def _():
            gather(next_pos_ref, 1 - slot)

        def wait(r, carry):
            _row_copy(y_ref.at[pl.ds(0, 1), :], rows_ref.at[slot, pl.ds(0, 1), :], sem.at[slot]).wait()
            return carry

        lax.fori_loop(0, TOK_TILE, wait, 0, unroll=8)
        x = x_ref[...] + g2_ref[...] * rows_ref[slot]
        if final:
            x = x * lax.rsqrt(jnp.mean(x * x, axis=-1, keepdims=True) + EPS) * fg_ref[...]
        o_ref[...] = x

    for parity in range(2):
        pl.when(i % 2 == parity)(functools.partial(step, parity))


def _combine(stream, pos, x, mods, y, final_g, layer):
    first, last = stream.first_tile, stream.first_tile + stream.n_tiles - 1
    return pl.pallas_call(
        functools.partial(_combine_kernel, stream.n_tiles, layer == DEPTH - 1),
        grid=(stream.n_tiles,),
        in_specs=[
            pl.BlockSpec((None, 1, TOK_TILE), lambda i: (first + i, 0, 0), memory_space=pltpu.SMEM),
            pl.BlockSpec((None, 1, TOK_TILE), lambda i: (jnp.minimum(first + i + 1, last), 0, 0),
                         memory_space=pltpu.SMEM),
            pl.BlockSpec((TOK_TILE, D_MODEL), lambda i: (i, 0)),
            pl.BlockSpec((None, 1, D_MODEL), lambda i: (layer * COND_ROWS + stream.mod_row(i, TOK_TILE), 0, 5)),
            pl.BlockSpec((1, D_MODEL), lambda i: (0, 0)),
            pl.BlockSpec(memory_space=pl.ANY),
        ],
        out_specs=pl.BlockSpec((TOK_TILE, D_MODEL), lambda i: (i, 0)),
        out_shape=jax.ShapeDtypeStruct((stream.rows, D_MODEL), F32),
        scratch_shapes=[pltpu.VMEM((2, TOK_TILE, D_MODEL), F32), pltpu.SemaphoreType.DMA((2,))],
        compiler_params=_params("arbitrary"),
        name=f"combine_{stream.name}",
    )(pos, pos, x, mods, final_g.reshape(1, D_MODEL), y)


def _rope_tables():
    t = jnp.arange(DEC_SEQ, dtype=jnp.int32)
    row = (t // GRID_W).astype(F32)
    col = (t % GRID_W).astype(F32)
    inv = ROPE_BASE ** (-jnp.arange(ROPE_PAIRS_PER_AXIS, dtype=F32) / ROPE_PAIRS_PER_AXIS)
    ang = jnp.concatenate([row[:, None] * inv, col[:, None] * inv], axis=-1)
    cos, sin = jnp.cos(ang), jnp.sin(ang)
    return jnp.concatenate([cos, cos], axis=-1), jnp.concatenate([-sin, sin], axis=-1)


def kernel(x_prompt, x_sample, cache_k, cache_v, state_h, c, c_ctx, norm1_g, norm2_g, w_ada, b_ada, w_in, conv_w, conv_b, lru_wa, lru_ba, lru_wx, lru_bx, lru_lambda, attn_sink, w_attn_proj, w_rnn_proj, w_out, router_w, router_b, w_gate_up, w_down, final_norm_g):
    xs_tok = {LATENT: x_sample.reshape(T_LAT, D_MODEL), CONTEXT: x_prompt.reshape(T_CTX, D_MODEL)}
    cond = jnp.concatenate([c_ctx[None, :], c, jnp.zeros((COND_ROWS - 1 - DEC_BATCH, D_MODEL), F32)], axis=0)
    mods = _ada(cond, w_ada, b_ada).reshape(DEPTH * COND_ROWS, 1, N_ADA * D_MODEL)
    rope_cos, rope_sin = _rope_tables()

    w_in_b = w_in.astype(BF16)
    wap_b = w_attn_proj.astype(BF16)
    wrp_b = w_rnn_proj.astype(BF16)
    wo_b = w_out.astype(BF16)
    lru_w, lru_b = _lru_block_weights(lru_wa, lru_wx, lru_ba, lru_bx)
    router_wt = router_w.T

    h0 = {LATENT: jnp.transpose(state_h.astype(F32), (1, 2, 0, 3)).reshape(DEPTH, 1, 2, DEC_BATCH, D_RNN),
          CONTEXT: jnp.zeros((DEPTH, BATCH // SUBLANES, 2, SUBLANES, D_RNN), F32)}
    ck = jnp.transpose(cache_k, (1, 0, 2, 3, 4)).reshape(DEPTH, DEC_BATCH, PAST_LEN, KV_W)
    cv = jnp.transpose(cache_v, (1, 0, 2, 3, 4)).reshape(DEPTH, DEC_BATCH, PAST_LEN, KV_W)

    xs = jnp.zeros((MOE_ROWS, ROW_W), F32)
    new_k, new_v, new_h = [], [], []
    for l in range(DEPTH):
        merged = {}
        for stream in (LATENT, CONTEXT):
            q, k, v, xr, gr, ga, gb = _inproj(stream, xs_tok[stream], norm1_g, mods, rope_cos, rope_sin,
                                              w_in_b, l)
            if stream.rotary:
                att = _lat_attn(q, k, v, ck[l], cv[l], attn_sink[l])
            else:
                att = _ctx_attn(q, k, v, attn_sink[l])
                new_k.append(k.reshape(BATCH, SEQ, N_KV_HEADS, HEAD_DIM))
                new_v.append(v.reshape(BATCH, SEQ, N_KV_HEADS, HEAD_DIM))
            y, h_last = _rnn(stream, xr, conv_w, conv_b, lru_w, lru_b, lru_lambda, h0[stream][l], l)
            if not stream.rotary:
                new_h.append(jnp.transpose(h_last, (0, 2, 1, 3)).reshape(BATCH, 2, D_RNN))
            merged[stream] = _merge(stream, xs_tok[stream], att, y, gr, ga, gb, mods, norm2_g,
                                    wap_b, wrp_b, wo_b, router_wt, router_b, l)
        pos, tiles = _plan(merged[LATENT][2], merged[CONTEXT][2])
        pos = pos.reshape(N_TOK_TILES, 1, TOK_TILE)
        for stream in (LATENT, CONTEXT):
            xs = _dispatch(stream, pos, merged[stream][1], xs)
        y_sorted = _experts(tiles, xs, w_gate_up, w_down, l)
        for stream in (LATENT, CONTEXT):
            xs_tok[stream] = _combine(stream, pos, merged[stream][0], mods, y_sorted, final_norm_g, l)

    return (xs_tok[CONTEXT].reshape(BATCH, SEQ, D_MODEL), xs_tok[LATENT].reshape(DEC_BATCH, DEC_SEQ, D_MODEL),
            jnp.stack(new_k, axis=1), jnp.stack(new_v, axis=1),
            jnp.stack(new_h, axis=1).astype(x_prompt.dtype))
```

```python
import functools
from typing import NamedTuple

import jax
import jax.numpy as jnp
from jax import lax
from jax.experimental import pallas as pl
from jax.experimental.pallas import tpu as pltpu

F32 = jnp.float32
BF16 = jnp.bfloat16

D_MODEL = 1024
BATCH = 16
SEQ = 256
DEPTH = 2
DEC_BATCH = 8
DEC_SEQ = 1024
PAST_LEN = 512
GRID_W = 64
N_HEADS = 8
N_KV_HEADS = 2
GQA_GROUP = N_HEADS // N_KV_HEADS
HEAD_DIM = 128
ATTN_W = N_HEADS * HEAD_DIM
KV_W = N_KV_HEADS * HEAD_DIM
WINDOW = 128
Q_BLOCK = 128
ROPE_BASE = 10000.0
ROPE_PAIRS_PER_AXIS = HEAD_DIM // 4
D_RNN = 1024
LRU_BLOCKS = 16
LRU_BW = D_RNN // LRU_BLOCKS
LRU_C = 8.0
CONV_W = 4
CONV_LEFT = 2
N_EXPERTS = 16
N_GROUPS = 4
EXPERTS_PER_GROUP = N_EXPERTS // N_GROUPS
D_EXPERT = 512
N_ADA = 6
EPS = 1e-6
IN_COLS = ATTN_W + 2 * KV_W + 2 * D_RNN + 2 * D_MODEL

T_LAT = DEC_BATCH * DEC_SEQ
T_CTX = BATCH * SEQ
T_ALL = T_LAT + T_CTX

SUBLANES = 8
LANES = 128
VMEM_LIMIT = 56 * 1024 * 1024

TOK_TILE = 256
MM_TILE = 512
N_TOK_TILES = T_ALL // TOK_TILE
COND_ROWS = 16


class _Stream(NamedTuple):
    name: str
    batch: int
    seq: int
    first_tile: int
    rotary: bool

    @property
    def rows(self):
        return self.batch * self.seq

    @property
    def n_tiles(self):
        return self.rows // TOK_TILE

    def mod_row(self, i, tile):
        return 1 + i // (self.seq // tile) if self.rotary else 0


LATENT = _Stream("lat", DEC_BATCH, DEC_SEQ, 0, True)
CONTEXT = _Stream("ctx", BATCH, SEQ, T_LAT // TOK_TILE, False)

RNN_CB = 256
RNN_CHUNK = 128
RNN_UNROLL = 8
LOG2_E = 1.4426950408889634
PAIR_SLOTS = ((0, 1), (0, 2), (0, 3), (1, 3), (1, 2), (3, 2))
PAIRS_PER_GROUP = len(PAIR_SLOTS)
N_BUCKETS = N_GROUPS * PAIRS_PER_GROUP
ROW_W = D_MODEL + LANES
MOE_TM = 384
N_MOE_TILES = (T_ALL + N_BUCKETS * (MOE_TM - 1) + MOE_TM - 1) // MOE_TM
MOE_ROWS = N_MOE_TILES * MOE_TM


def _params(*sem):
    return pltpu.CompilerParams(dimension_semantics=sem, vmem_limit_bytes=VMEM_LIMIT)


def _sigmoid(x):
    return 0.5 * jnp.tanh(0.5 * x) + 0.5


def _ada_kernel(cond_ref, w_ref, b_ref, o_ref):
    s = jax.nn.silu(cond_ref[...]).astype(BF16)
    o_ref[...] = jnp.dot(s, w_ref[...].astype(BF16), preferred_element_type=F32) + b_ref[...]


def _ada(cond, w_ada, b_ada):
    cols = N_ADA * D_MODEL
    tn = 1536
    return pl.pallas_call(
        _ada_kernel,
        grid=(DEPTH, cols // tn),
        in_specs=[
            pl.BlockSpec((COND_ROWS, D_MODEL), lambda l, j: (0, 0)),
            pl.BlockSpec((None, D_MODEL, tn), lambda l, j: (l, 0, j)),
            pl.BlockSpec((None, 1, tn), lambda l, j: (l, 0, j)),
        ],
        out_specs=pl.BlockSpec((None, COND_ROWS, tn), lambda l, j: (l, 0, j)),
        out_shape=jax.ShapeDtypeStruct((DEPTH, COND_ROWS, cols), F32),
        compiler_params=_params("arbitrary", "arbitrary"),
        name="ada",
    )(cond, w_ada, b_ada.reshape(DEPTH, 1, cols))


def _inproj_kernel(rotary, x_ref, g_ref, sh_ref, sc_ref, *refs):
    if rotary:
        cos_ref, sin_ref, w_ref, q_ref, k_ref, v_ref, xr_ref, gr_ref, ga_ref, gb_ref = refs
    else:
        w_ref, q_ref, k_ref, v_ref, xr_ref, gr_ref, ga_ref, gb_ref = refs
    x = x_ref[...]
    y = x * lax.rsqrt(jnp.mean(x * x, axis=-1, keepdims=True) + EPS) * g_ref[...]
    h = (y * (1.0 + sc_ref[...]) + sh_ref[...]).astype(BF16)

    def proj(lo, width):
        return jnp.dot(h, w_ref[:, lo:lo + width], preferred_element_type=F32)

    def rope(t):
        if not rotary:
            return t
        return t * cos_ref[...] + pltpu.roll(t, HEAD_DIM // 2, 1) * sin_ref[...]

    scale = HEAD_DIM ** -0.5 * LOG2_E
    qk = proj(0, ATTN_W + KV_W)
    for hd in range(N_HEADS):
        q = (rope(qk[:, hd * HEAD_DIM:(hd + 1) * HEAD_DIM]) * scale).astype(BF16)
        for blk in range(MM_TILE // Q_BLOCK):
            q_ref[blk, hd] = q[blk * Q_BLOCK:(blk + 1) * Q_BLOCK]
    for g in range(N_KV_HEADS):
        k = qk[:, ATTN_W + g * HEAD_DIM:ATTN_W + (g + 1) * HEAD_DIM]
        k_ref[:, g * HEAD_DIM:(g + 1) * HEAD_DIM] = rope(k)
    v_ref[...] = proj(ATTN_W + KV_W, KV_W)
    base = ATTN_W + 2 * KV_W
    xr_ref[...] = proj(base, D_RNN).astype(BF16)
    gr_ref[...] = proj(base + D_RNN, D_RNN).astype(BF16)
    ga_ref[...] = proj(base + 2 * D_RNN, D_MODEL).astype(BF16)
    gb_ref[...] = proj(base + 2 * D_RNN + D_MODEL, D_MODEL).astype(BF16)


def _inproj(stream, x, norm_g, mods, rope_cos, rope_sin, w_in, layer):
    row = lambda i: layer * COND_ROWS + stream.mod_row(i, MM_TILE)
    tok = lambda i: (i, 0)
    rope_blk = lambda i: (i % (stream.seq // MM_TILE), 0)
    wide = pl.BlockSpec((MM_TILE, D_MODEL), tok)
    kv = pl.BlockSpec((MM_TILE, KV_W), tok)
    rope_specs = [pl.BlockSpec((MM_TILE, HEAD_DIM), rope_blk)] * 2 if stream.rotary else []
    rope_args = (rope_cos, rope_sin) if stream.rotary else ()
    return pl.pallas_call(
        functools.partial(_inproj_kernel, stream.rotary),
        grid=(stream.rows // MM_TILE,),
        in_specs=[
            wide,
            pl.BlockSpec((None, 1, D_MODEL), lambda i: (layer, 0, 0)),
            pl.BlockSpec((None, 1, D_MODEL), lambda i: (row(i), 0, 0)),
            pl.BlockSpec((None, 1, D_MODEL), lambda i: (row(i), 0, 1)),
        ] + rope_specs + [
            pl.BlockSpec((None, D_MODEL, IN_COLS), lambda i: (layer, 0, 0)),
        ],
        out_specs=[pl.BlockSpec((MM_TILE // Q_BLOCK, N_HEADS, Q_BLOCK, HEAD_DIM), lambda i: (i, 0, 0, 0)),
                   kv, kv, wide, wide, wide, wide],
        out_shape=[
            jax.ShapeDtypeStruct((stream.rows // Q_BLOCK, N_HEADS, Q_BLOCK, HEAD_DIM), BF16),
            jax.ShapeDtypeStruct((stream.rows, KV_W), F32),
            jax.ShapeDtypeStruct((stream.rows, KV_W), F32),
            jax.ShapeDtypeStruct((stream.rows, D_RNN), BF16),
            jax.ShapeDtypeStruct((stream.rows, D_RNN), BF16),
            jax.ShapeDtypeStruct((stream.rows, D_MODEL), BF16),
            jax.ShapeDtypeStruct((stream.rows, D_MODEL), BF16),
        ],
        compiler_params=_params("arbitrary"),
        name=f"inproj_{stream.name}",
    )(x, norm_g.reshape(DEPTH, 1, D_MODEL), mods, mods, *rope_args, w_in)


def _qk(q, k):
    return lax.dot_general(q, k, (((1,), (1,)), ((), ())), preferred_element_type=F32)


def _store_heads(o_ref, g, o, rows):
    for r in range(GQA_GROUP):
        hd = g * GQA_GROUP + r
        o_ref[:, hd * HEAD_DIM:(hd + 1) * HEAD_DIM] = o[r * rows:(r + 1) * rows].astype(BF16)


def _softmax_rows(s_tiles, sink, p_ref, tail_ref, rows):
    top = s_tiles[0][1]
    for _, t in s_tiles[1:]:
        top = jnp.maximum(top, t)
    m = jnp.maximum(jnp.max(top, axis=-1, keepdims=True), sink)
    for c, t in s_tiles:
        p_ref[rows, c:c + LANES] = jnp.exp2(t - m).astype(BF16)
    tail_ref[rows, :] = jnp.broadcast_to(jnp.exp2(sink - m), top.shape)


def _values_and_ones(v):
    ones = jnp.where(lax.broadcasted_iota(jnp.int32, v.shape, 1) == 0, 1.0, 0.0).astype(v.dtype)
    return jnp.concatenate([v, ones], axis=1)


def _attend(p_ref, tail_ref, v):
    acc = jnp.dot(p_ref[...], _values_and_ones(v), preferred_element_type=F32)
    return acc[:, :HEAD_DIM] / (acc[:, HEAD_DIM:HEAD_DIM + 1] + tail_ref[:, :1])


def _ctx_attn_kernel(sink_ref, q_ref, k_ref, v_ref, o_ref, s_ref, p_ref, tail_ref):
    for g in range(N_KV_HEADS):
        heads = q_ref[:, g * GQA_GROUP:(g + 1) * GQA_GROUP]
        q = jnp.swapaxes(heads, 0, 1).reshape(GQA_GROUP * SEQ, HEAD_DIM)
        k = k_ref[:, g * HEAD_DIM:(g + 1) * HEAD_DIM].astype(BF16)
        v = v_ref[:, g * HEAD_DIM:(g + 1) * HEAD_DIM].astype(BF16)
        s_ref[...] = _qk(q, k)
        for r in range(GQA_GROUP):
            rows = slice(r * SEQ, (r + 1) * SEQ)
            tiles = [(c, s_ref[rows, c:c + LANES]) for c in range(0, SEQ, LANES)]
            _softmax_rows(tiles, sink_ref[g * GQA_GROUP + r] * LOG2_E, p_ref, tail_ref, rows)
        _store_heads(o_ref, g, _attend(p_ref, tail_ref, v), SEQ)


def _ctx_attn(q, k, v, sink):
    blk = lambda b: (b, 0)
    rows = GQA_GROUP * SEQ
    return pl.pallas_call(
        _ctx_attn_kernel,
        grid=(BATCH,),
        in_specs=[
            pl.BlockSpec(memory_space=pltpu.SMEM),
            pl.BlockSpec((SEQ // Q_BLOCK, N_HEADS, Q_BLOCK, HEAD_DIM), lambda b: (b, 0, 0, 0)),
            pl.BlockSpec((SEQ, KV_W), blk),
            pl.BlockSpec((SEQ, KV_W), blk),
        ],
        out_specs=pl.BlockSpec((SEQ, ATTN_W), lambda b: (b, 0)),
        out_shape=jax.ShapeDtypeStruct((T_CTX, ATTN_W), BF16),
        scratch_shapes=[pltpu.VMEM((rows, SEQ), F32), pltpu.VMEM((rows, SEQ), BF16),
                        pltpu.VMEM((rows, LANES), F32)],
        compiler_params=_params("arbitrary"),
        name="ctx_attn",
    )(sink, q, k, v)


def _lat_attn_kernel(sink_ref, q_ref, kp_ref, kc_ref, kn_ref, vp_ref, vc_ref, vn_ref,
                     ck_ref, cv_ref, o_ref, band_ref, mask_ref, s_ref, p_ref, tail_ref):
    j = pl.program_id(1)
    rows = GQA_GROUP * Q_BLOCK
    band = Q_BLOCK + 2 * WINDOW

    @pl.when(jnp.logical_and(pl.program_id(0) == 0, j == 0))
    def _():
        ahead = (lax.broadcasted_iota(jnp.int32, (rows, band), 1) - WINDOW
                 - lax.broadcasted_iota(jnp.int32, (rows, band), 0) % Q_BLOCK)
        band_ref[...] = jnp.where(jnp.abs(ahead) <= WINDOW, 0.0, -jnp.inf)

    kpos = j * Q_BLOCK - WINDOW + lax.broadcasted_iota(jnp.int32, (1, band), 1)
    mask_ref[...] = band_ref[...] + jnp.where((kpos >= 0) & (kpos < DEC_SEQ), 0.0, -jnp.inf)
    for g in range(N_KV_HEADS):
        cols = slice(g * HEAD_DIM, (g + 1) * HEAD_DIM)
        q = q_ref[g * GQA_GROUP:(g + 1) * GQA_GROUP].reshape(rows, HEAD_DIM)
        keys = jnp.concatenate([kp_ref[:, cols], kc_ref[:, cols], kn_ref[:, cols], ck_ref[:, cols]],
                               axis=0).astype(BF16)
        vals = jnp.concatenate([vp_ref[:, cols], vc_ref[:, cols], vn_ref[:, cols], cv_ref[:, cols]],
                               axis=0).astype(BF16)
        s_ref[...] = _qk(q, keys)
        for r in range(GQA_GROUP):
            rows_r = slice(r * Q_BLOCK, (r + 1) * Q_BLOCK)
            tiles = [(c, s_ref[rows_r, c:c + LANES] + mask_ref[rows_r, c:c + LANES]) for c in range(0, band, LANES)]
            tiles += [(c, s_ref[rows_r, c:c + LANES]) for c in range(band, band + PAST_LEN, LANES)]
            _softmax_rows(tiles, sink_ref[g * GQA_GROUP + r] * LOG2_E, p_ref, tail_ref, rows_r)
        _store_heads(o_ref, g, _attend(p_ref, tail_ref, vals), Q_BLOCK)


def _lat_attn(q, k, v, cache_k, cache_v, sink):
    nb = DEC_SEQ // Q_BLOCK
    cur = lambda b, j: (b * nb + j, 0)
    prev = lambda b, j: (b * nb + jnp.maximum(j - 1, 0), 0)
    nxt = lambda b, j: (b * nb + jnp.minimum(j + 1, nb - 1), 0)
    kvb = lambda im: pl.BlockSpec((Q_BLOCK, KV_W), im)
    cache = pl.BlockSpec((None, PAST_LEN, KV_W), lambda b, j: (b, 0, 0))
    return pl.pallas_call(
        _lat_attn_kernel,
        grid=(DEC_BATCH, nb),
        in_specs=[
            pl.BlockSpec(memory_space=pltpu.SMEM),
            pl.BlockSpec((None, N_HEADS, Q_BLOCK, HEAD_DIM), lambda b, j: (b * nb + j, 0, 0, 0)),
            kvb(prev), kvb(cur), kvb(nxt), kvb(prev), kvb(cur), kvb(nxt),
            cache, cache,
        ],
        out_specs=pl.BlockSpec((Q_BLOCK, ATTN_W), cur),
        out_shape=jax.ShapeDtypeStruct((T_LAT, ATTN_W), BF16),
        scratch_shapes=[
            pltpu.VMEM((GQA_GROUP * Q_BLOCK, Q_BLOCK + 2 * WINDOW), F32),
            pltpu.VMEM((GQA_GROUP * Q_BLOCK, Q_BLOCK + 2 * WINDOW), F32),
            pltpu.VMEM((GQA_GROUP * Q_BLOCK, Q_BLOCK + 2 * WINDOW + PAST_LEN), F32),
            pltpu.VMEM((GQA_GROUP * Q_BLOCK, Q_BLOCK + 2 * WINDOW + PAST_LEN), BF16),
            pltpu.VMEM((GQA_GROUP * Q_BLOCK, LANES), F32),
        ],
        compiler_params=_params("arbitrary", "arbitrary"),
        name="lat_attn",
    )(sink, q, k, k, k, v, v, v, cache_k, cache_v)


def _rnn_kernel(seq, xr_ref, cw_ref, cb_ref, w_ref, b_ref, lam_ref, h0_ref,
                y_ref, hl_ref, xt_ref, yt_ref, a_ref, u_ref):
    n_chunks = seq // RNN_CHUNK
    rows = RNN_CHUNK * SUBLANES
    halo = jnp.zeros((CONV_LEFT, SUBLANES, RNN_CB), F32)
    xt_ref[0:CONV_LEFT] = halo
    xt_ref[seq + CONV_LEFT:seq + 2 * CONV_LEFT] = halo

    def load_chunk(c, carry):
        t0 = pl.multiple_of(c * RNN_CHUNK, RNN_CHUNK)
        x = xr_ref[:, pl.ds(t0, RNN_CHUNK), :].astype(F32)
        xt_ref[pl.ds(t0 + CONV_LEFT, RNN_CHUNK)] = jnp.swapaxes(x, 0, 1)
        return carry

    lax.fori_loop(0, n_chunks, load_chunk, 0)

    half_w = 0.5 * cw_ref[...]
    half_b = 0.5 * cb_ref[...]

    def half_conv(t0):
        acc = half_b.reshape(1, 1, RNN_CB)
        for tap in range(CONV_W):
            acc = acc + xt_ref[pl.ds(t0 + tap, RNN_CHUNK)] * half_w[tap:tap + 1, :].reshape(1, 1, RNN_CB)
        return acc

    for d in range(2):
        decay = (-0.5 * LRU_C * LOG2_E) * jax.nn.softplus(-lam_ref[d:d + 1, :])
        half_bias = b_ref[d:d + 1, :]

        def chunk(ci, h, d=d, decay=decay, half_bias=half_bias):
            c = ci if d == 0 else n_chunks - 1 - ci
            t0 = pl.multiple_of(c * RNN_CHUNK, RNN_CHUNK)
            hx = half_conv(t0).reshape(rows, RNN_CB)
            z = jnp.dot(hx.astype(BF16), w_ref[d], preferred_element_type=F32) + half_bias
            a = jnp.exp2(decay * jnp.tanh(z[:, :RNN_CB]) + decay)
            m = 1.0 - a * a
            mult = jnp.where(m == 0.0, 0.0, m * lax.rsqrt(m))
            u = mult * ((jnp.tanh(z[:, RNN_CB:]) + 1.0) * hx)
            a_ref[...] = a.reshape(RNN_CHUNK, SUBLANES, RNN_CB)
            u_ref[...] = u.reshape(RNN_CHUNK, SUBLANES, RNN_CB)

            def steps(gi, h):
                s0 = pl.multiple_of((gi if d == 0 else RNN_CHUNK // RNN_UNROLL - 1 - gi) * RNN_UNROLL, RNN_UNROLL)
                for j in (range(RNN_UNROLL) if d == 0 else reversed(range(RNN_UNROLL))):
                    h = a_ref[s0 + j] * h + u_ref[s0 + j]
                    if d == 0:
                        yt_ref[t0 + s0 + j] = h
                    else:
                        yt_ref[t0 + s0 + j] = yt_ref[t0 + s0 + j] + h
                return h

            return lax.fori_loop(0, RNN_CHUNK // RNN_UNROLL, steps, h)

        hl_ref[d] = lax.fori_loop(0, n_chunks, chunk, h0_ref[d])

    def store_chunk(c, carry):
        t0 = pl.multiple_of(c * RNN_CHUNK, RNN_CHUNK)
        y = jnp.swapaxes(yt_ref[pl.ds(t0, RNN_CHUNK)], 0, 1)
        y_ref[:, pl.ds(t0, RNN_CHUNK), :] = y.astype(BF16)
        return carry

    lax.fori_loop(0, n_chunks, store_chunk, 0)


def _rnn(stream, xr, conv_w, conv_b, w_blk, b_blk, lam, h0, layer):
    n_cb = D_RNN // RNN_CB
    seq = stream.seq
    n_groups = stream.batch // SUBLANES
    return pl.pallas_call(
        functools.partial(_rnn_kernel, seq),
        grid=(n_groups, n_cb),
        in_specs=[
            pl.BlockSpec((SUBLANES, seq, RNN_CB), lambda g, j: (g, 0, j)),
            pl.BlockSpec((None, CONV_W, RNN_CB), lambda g, j: (layer, 0, j)),
            pl.BlockSpec((None, 1, RNN_CB), lambda g, j: (layer, 0, j)),
            pl.BlockSpec((None, None, 2, RNN_CB, 2 * RNN_CB), lambda g, j: (layer, j, 0, 0, 0)),
            pl.BlockSpec((None, None, 2, 2 * RNN_CB), lambda g, j: (layer, j, 0, 0)),
            pl.BlockSpec((None, 2, RNN_CB), lambda g, j: (layer, 0, j)),
            pl.BlockSpec((None, 2, SUBLANES, RNN_CB), lambda g, j: (g, 0, 0, j)),
        ],
        out_specs=[
            pl.BlockSpec((SUBLANES, seq, RNN_CB), lambda g, j: (g, 0, j)),
            pl.BlockSpec((None, 2, SUBLANES, RNN_CB), lambda g, j: (g, 0, 0, j)),
        ],
        out_shape=[
            jax.ShapeDtypeStruct((n_groups * SUBLANES, seq, D_RNN), BF16),
            jax.ShapeDtypeStruct((n_groups, 2, SUBLANES, D_RNN), F32),
        ],
        scratch_shapes=[
            pltpu.VMEM((seq + 2 * CONV_LEFT, SUBLANES, RNN_CB), F32),
            pltpu.VMEM((seq, SUBLANES, RNN_CB), F32),
            pltpu.VMEM((RNN_CHUNK, SUBLANES, RNN_CB), F32),
            pltpu.VMEM((RNN_CHUNK, SUBLANES, RNN_CB), F32),
        ],
        compiler_params=_params("arbitrary", "arbitrary"),
        name=f"rnn_{stream.name}",
    )(xr.reshape(stream.batch, seq, D_RNN), conv_w, conv_b.reshape(DEPTH, 1, D_RNN), w_blk, b_blk, lam, h0)


def _lru_block_weights(lru_wa, lru_wx, lru_ba, lru_bx):
    n_cb = D_RNN // RNN_CB
    per = RNN_CB // LRU_BW

    def dense(w):
        w = w.reshape(DEPTH, 2, n_cb, per, LRU_BW, LRU_BW)
        eye = jnp.eye(per, dtype=w.dtype)
        full = jnp.einsum("ldcpkj,pq->ldcpkqj", w, eye)
        return full.reshape(DEPTH, 2, n_cb, RNN_CB, RNN_CB)

    w = jnp.concatenate([dense(lru_wa), dense(lru_wx)], axis=-1)
    w = jnp.transpose(w, (0, 2, 1, 3, 4)).astype(BF16)
    b = jnp.concatenate([lru_ba.reshape(DEPTH, 2, n_cb, RNN_CB),
                         lru_bx.reshape(DEPTH, 2, n_cb, RNN_CB)], axis=-1)
    return w, jnp.transpose(0.5 * b, (0, 2, 1, 3))


def _route(scores, sel):
    grp_score = []
    for g in range(N_GROUPS):
        a, b, c, d = sel[g * EXPERTS_PER_GROUP:(g + 1) * EXPERTS_PER_GROUP]
        hi1, lo1 = jnp.maximum(a, b), jnp.minimum(a, b)
        hi2, lo2 = jnp.maximum(c, d), jnp.minimum(c, d)
        grp_score.append(jnp.maximum(hi1, hi2) + jnp.maximum(jnp.minimum(hi1, hi2), jnp.maximum(lo1, lo2)))
    best = jnp.zeros_like(grp_score[0], dtype=jnp.int32)
    best_val = grp_score[0]
    for g in range(1, N_GROUPS):
        better = grp_score[g] > best_val
        best = jnp.where(better, g, best)
        best_val = jnp.where(better, grp_score[g], best_val)
    chosen = []
    for e in range(N_EXPERTS):
        g = e // EXPERTS_PER_GROUP
        rank = jnp.zeros_like(best)
        for o in range(g * EXPERTS_PER_GROUP, (g + 1) * EXPERTS_PER_GROUP):
            if o == e:
                continue
            ahead = (sel[o] >= sel[e]) if o < e else (sel[o] > sel[e])
            rank = rank + jnp.where(ahead, 1, 0)
        chosen.append(jnp.where(best == g, rank, 2) < 2)
    taken, gate = [], []
    for j in range(EXPERTS_PER_GROUP):
        t = jnp.zeros_like(best)
        s = jnp.zeros_like(scores[0])
        for g in range(N_GROUPS):
            e = g * EXPERTS_PER_GROUP + j
            t = t + jnp.where(chosen[e], 1, 0)
            s = s + jnp.where(chosen[e], scores[e], 0.0)
        taken.append(t > 0)
        gate.append(s)
    total = gate[0] + gate[1] + gate[2] + gate[3]
    pair = jnp.zeros_like(best)
    w_a = jnp.zeros_like(total)
    w_b = jnp.zeros_like(total)
    for order, (a, b) in enumerate(PAIR_SLOTS):
        both = taken[a] & taken[b]
        pair = jnp.where(both, order, pair)
        w_a = jnp.where(both, gate[a], w_a)
        w_b = jnp.where(both, gate[b], w_b)
    return best * PAIRS_PER_GROUP + pair, w_a / total, w_b / total


def _merge_kernel(x_ref, att_ref, rnn_ref, gr_ref, ga_ref, gb_ref,
                  g1_ref, sh2_ref, sc2_ref, n2_ref, wap_ref, wrp_ref, wo_ref, rw_ref, rb_ref,
                  xo_ref, h2_ref, bucket_ref):
    att = jnp.dot(att_ref[...], wap_ref[...], preferred_element_type=F32)
    gated = (jax.nn.gelu(gr_ref[...].astype(F32)) * rnn_ref[...].astype(F32)).astype(BF16)
    rnn = jnp.dot(gated, wrp_ref[...], preferred_element_type=F32)
    merged = _sigmoid(ga_ref[...].astype(F32)) * att + _sigmoid(gb_ref[...].astype(F32)) * rnn
    x = x_ref[...] + g1_ref[...] * jnp.dot(merged.astype(BF16), wo_ref[...], preferred_element_type=F32)
    xo_ref[...] = x
    y = x * lax.rsqrt(jnp.mean(x * x, axis=-1, keepdims=True) + EPS) * n2_ref[...]
    h2 = y * (1.0 + sc2_ref[...]) + sh2_ref[...]
    h2_ref[:, :D_MODEL] = h2
    h_hi = h2.astype(BF16)
    h_lo = (h2 - h_hi.astype(F32)).astype(BF16)
    rw = rw_ref[...]
    w_hi = rw.astype(BF16)
    w_lo = (rw - w_hi.astype(F32)).astype(BF16)
    logits = _qk(w_hi, h_hi) + (_qk(w_hi, h_lo) + _qk(w_lo, h_hi))
    score = _sigmoid(logits)
    sel = score + rb_ref[...]
    bucket, gate_a, gate_b = _route([score[e:e + 1, :] for e in range(N_EXPERTS)],
                                [sel[e:e + 1, :] for e in range(N_EXPERTS)])
    bucket_ref[...] = bucket
    pad = jnp.zeros((LANES - 2, MM_TILE), F32)
    h2_ref[:, D_MODEL:] = jnp.concatenate([gate_a, gate_b, pad], axis=0).T


def _merge(stream, x, att, rnn, gr, ga, gb, mods, norm_g, wap, wrp, wo, router_wt, router_b, layer):
    tok = lambda i: (i, 0)
    wide = pl.BlockSpec((MM_TILE, D_MODEL), tok)
    mod = lambda col: pl.BlockSpec((None, 1, D_MODEL),
                                   lambda i: (layer * COND_ROWS + stream.mod_row(i, MM_TILE), 0, col))
    mat = pl.BlockSpec((None, D_MODEL, D_MODEL), lambda i: (layer, 0, 0))
    return pl.pallas_call(
        _merge_kernel,
        grid=(stream.rows // MM_TILE,),
        in_specs=[
            wide, wide, wide, wide, wide, wide,
            mod(2), mod(3), mod(4),
            pl.BlockSpec((None, 1, D_MODEL), lambda i: (layer, 0, 0)),
            mat, mat, mat,
            pl.BlockSpec((N_EXPERTS, D_MODEL), lambda i: (0, 0)),
            pl.BlockSpec((N_EXPERTS, 1), lambda i: (0, 0)),
        ],
        out_specs=[wide, pl.BlockSpec((MM_TILE, ROW_W), tok),
                   pl.BlockSpec((None, 1, MM_TILE), lambda i: (i, 0, 0))],
        out_shape=[
            jax.ShapeDtypeStruct((stream.rows, D_MODEL), F32),
            jax.ShapeDtypeStruct((stream.rows, ROW_W), F32),
            jax.ShapeDtypeStruct((stream.rows // MM_TILE, 1, MM_TILE), jnp.int32),
        ],
        compiler_params=_params("arbitrary"),
        name=f"merge_{stream.name}",
    )(x, att, rnn.reshape(stream.rows, D_RNN), gr, ga, gb, mods, mods, mods,
      norm_g.reshape(DEPTH, 1, D_MODEL), wap, wrp, wo, router_wt, router_b.reshape(N_EXPERTS, 1))


def _plan_kernel(b_ref, pos_ref, tiles_ref):
    b = b_ref[...]
    r = lax.broadcasted_iota(jnp.int32, (TOK_TILE, TOK_TILE), 0)
    c = lax.broadcasted_iota(jnp.int32, (TOK_TILE, TOK_TILE), 1)
    before_in_tile = jnp.where(r < c, 1.0, 0.0).astype(BF16)
    br = lax.broadcasted_iota(jnp.int32, (N_TOK_TILES, N_TOK_TILES), 0)
    bc = lax.broadcasted_iota(jnp.int32, (N_TOK_TILES, N_TOK_TILES), 1)
    earlier_tiles = jnp.where(bc < br, 1.0, 0.0).astype(BF16)
    lane = lax.broadcasted_iota(jnp.int32, (1, LANES), 1)
    tile_start = (lane * MOE_TM).astype(F32)
    start = jnp.zeros((1, 1), F32)
    pos = jnp.zeros(b.shape, F32)
    tile_bucket = jnp.zeros((1, LANES), F32)
    for k in range(N_BUCKETS):
        mask = jnp.where(b == k, 1.0, 0.0)
        rank = jnp.dot(mask.astype(BF16), before_in_tile, preferred_element_type=F32)
        per_tile = jnp.sum(mask, axis=1, keepdims=True)
        tile_off = jnp.dot(earlier_tiles, jnp.broadcast_to(per_tile, (N_TOK_TILES, LANES)).astype(BF16),
                           preferred_element_type=F32)[:, :1]
        total = jnp.sum(per_tile, axis=0, keepdims=True)
        padded = jnp.floor((total + (MOE_TM - 0.5)) * (1.0 / MOE_TM)) * MOE_TM
        pos = pos + mask * (start + tile_off + rank)
        start = start + padded
        tile_bucket = tile_bucket + jnp.where(tile_start >= start, 1.0, 0.0)
    pos_ref[...] = pos.astype(jnp.int32)
    n_used = jnp.floor((start + 0.5) * (1.0 / MOE_TM))
    tiles_ref[...] = jnp.where(lane == LANES - 1, n_used, tile_bucket).astype(jnp.int32)


def _plan(bucket_lat, bucket_ctx):
    bucket = jnp.concatenate([bucket_lat.reshape(-1, TOK_TILE), bucket_ctx.reshape(-1, TOK_TILE)], axis=0)
    return pl.pallas_call(
        _plan_kernel,
        out_shape=[
            jax.ShapeDtypeStruct((N_TOK_TILES, TOK_TILE), jnp.int32),
            jax.ShapeDtypeStruct((1, LANES), jnp.int32),
        ],
        compiler_params=pltpu.CompilerParams(vmem_limit_bytes=VMEM_LIMIT),
        name="plan",
    )(bucket)


def _row_copy(src, dst, sem):
    return pltpu.make_async_copy(src, dst, sem)


def _dispatch_kernel(pos_ref, h_ref, xs_in_ref, xs_ref, sem):
    del xs_in_ref

    for r in range(TOK_TILE):
        _row_copy(h_ref.at[pl.ds(r, 1), :], xs_ref.at[pl.ds(pos_ref[0, r], 1), :], sem).start(priority=r % 2)

    def wait(r, carry):
        _row_copy(h_ref.at[pl.ds(0, 1), :], xs_ref.at[pl.ds(0, 1), :], sem).wait()
        return carry

    lax.fori_loop(0, TOK_TILE, wait, 0, unroll=8)


def _dispatch(stream, pos, h2, xs):
    return pl.pallas_call(
        _dispatch_kernel,
        grid=(stream.n_tiles,),
        in_specs=[
            pl.BlockSpec((None, 1, TOK_TILE), lambda i: (stream.first_tile + i, 0, 0), memory_space=pltpu.SMEM),
            pl.BlockSpec((TOK_TILE, ROW_W), lambda i: (i, 0)),
            pl.BlockSpec(memory_space=pl.ANY),
        ],
        out_specs=pl.BlockSpec(memory_space=pl.ANY),
        out_shape=jax.ShapeDtypeStruct((MOE_ROWS, ROW_W), F32),
        input_output_aliases={2: 0},
        scratch_shapes=[pltpu.SemaphoreType.DMA(())],
        compiler_params=_params("arbitrary"),
        name=f"dispatch_{stream.name}",
    )(pos, h2, xs)


def _bucket_expert(k, which):
    p = k % PAIRS_PER_GROUP
    in_group = 0
    for order, slots in enumerate(PAIR_SLOTS):
        in_group = jnp.where(p == order, slots[which], in_group)
    return (k // PAIRS_PER_GROUP) * EXPERTS_PER_GROUP + in_group


def _weight_copies(layer, wgu_ref, wd_ref, wgu_f, wd_f, sem, bucket, which, landing):
    e = _bucket_expert(bucket, which)
    return [pltpu.make_async_copy(wgu_ref.at[layer, e], wgu_f.at[landing, which], sem.at[landing, 2 * which]),
            pltpu.make_async_copy(wd_ref.at[layer, e], wd_f.at[landing, which], sem.at[landing, 2 * which + 1])]


def _experts_kernel(layer, tiles_ref, xs_ref, wgu_ref, wd_ref, y_ref, wgu_f, wd_f, sem, home_ref):
    i = pl.program_id(0)
    n_used = tiles_ref[0, LANES - 1]
    in_use = i < n_used
    cur = tiles_ref[0, i]
    prev = tiles_ref[0, jnp.maximum(i - 1, 0)]

    def changed(bucket, before):
        return [_bucket_expert(bucket, which) != _bucket_expert(before, which) for which in range(2)]

    def copies(bucket, which, landing):
        return _weight_copies(layer, wgu_ref, wd_ref, wgu_f, wd_f, sem, bucket, which, landing)

    @pl.when(jnp.logical_and(in_use, jnp.logical_or(i == 0, cur != prev)))
    def _():
        @pl.when(i == 0)
        def _():
            for which in range(2):
                home_ref[which] = 1
                for c in copies(cur, which, 0):
                    c.start()

        needed = [jnp.logical_or(i == 0, c) for c in changed(cur, prev)]
        for which in range(2):
            @pl.when(needed[which])
            def _(which=which):
                landing = 1 - home_ref[which]
                for c in copies(cur, which, landing):
                    c.wait()
                home_ref[which] = landing

        nxt_i = lax.while_loop(lambda j: jnp.logical_and(j < n_used, tiles_ref[0, jnp.minimum(j, n_used - 1)] == cur),
                               lambda j: j + 1, i + 1)

        @pl.when(nxt_i < n_used)
        def _():
            nxt = tiles_ref[0, nxt_i]
            for which, differs in enumerate(changed(nxt, cur)):
                @pl.when(differs)
                def _(which=which):
                    for c in copies(nxt, which, 1 - home_ref[which]):
                        c.start()

    @pl.when(jnp.logical_not(in_use))
    def _():
        y_ref[...] = jnp.zeros_like(y_ref)

    @pl.when(in_use)
    def _():
        x = xs_ref[:, :D_MODEL].astype(BF16)

        def branch(which, gate):
            home = home_ref[which]
            gu = jnp.dot(x, wgu_f[home, which].astype(BF16), preferred_element_type=F32)
            act = jax.nn.silu(gu[:, :D_EXPERT]) * gu[:, D_EXPERT:] * gate
            return jnp.dot(act.astype(BF16), wd_f[home, which].astype(BF16), preferred_element_type=F32)

        y_ref[...] = (branch(0, xs_ref[:, D_MODEL:D_MODEL + 1])
                      + branch(1, xs_ref[:, D_MODEL + 1:D_MODEL + 2]))


def _experts(tiles, xs, wgu, wd, layer):
    return pl.pallas_call(
        functools.partial(_experts_kernel, layer),
        grid_spec=pltpu.PrefetchScalarGridSpec(
            num_scalar_prefetch=1,
            grid=(N_MOE_TILES,),
            in_specs=[
                pl.BlockSpec((MOE_TM, ROW_W), lambda i, t: (i, 0)),
                pl.BlockSpec(memory_space=pl.ANY),
                pl.BlockSpec(memory_space=pl.ANY),
            ],
            out_specs=pl.BlockSpec((MOE_TM, D_MODEL), lambda i, t: (i, 0)),
            scratch_shapes=[
                pltpu.VMEM((2, 2, D_MODEL, 2 * D_EXPERT), F32), pltpu.VMEM((2, 2, D_EXPERT, D_MODEL), F32),
                pltpu.SemaphoreType.DMA((2, 4)), pltpu.SMEM((2,), jnp.int32),
            ],
        ),
        out_shape=jax.ShapeDtypeStruct((MOE_ROWS, D_MODEL), F32),
        compiler_params=_params("arbitrary"),
        name="experts",
    )(tiles, xs, wgu, wd)


def _combine_kernel(n_tiles, final, pos_ref, next_pos_ref, x_ref, g2_ref, fg_ref, y_ref, o_ref, rows_ref, sem):
    i = pl.program_id(0)

    def gather(p_ref, into):
        for r in range(TOK_TILE):
            _row_copy(y_ref.at[pl.ds(p_ref[0, r], 1), :], rows_ref.at[into, pl.ds(r, 1), :],
                      sem.at[into]).start(priority=r % 2)

    def step(slot):
        if slot == 0:
            @pl.when(i == 0)
            def _():
                gather(pos_ref, 0)

        @pl.when(i + 1 < n_tiles)
        def _():
            gather(next_pos_ref, 1 - slot)

        def wait(r, carry):
            _row_copy(y_ref.at[pl.ds(0, 1), :], rows_ref.at[slot, pl.ds(0, 1), :], sem.at[slot]).wait()
            return carry

        lax.fori_loop(0, TOK_TILE, wait, 0, unroll=8)
        x = x_ref[...] + g2_ref[...] * rows_ref[slot]
        if final:
            x = x * lax.rsqrt(jnp.mean(x * x, axis=-1, keepdims=True) + EPS) * fg_ref[...]
        o_ref[...] = x

    for parity in range(2):
        pl.when(i % 2 == parity)(functools.partial(step, parity))


def _combine(stream, pos, x, mods, y, final_g, layer):
    first, last = stream.first_tile, stream.first_tile + stream.n_tiles - 1
    return pl.pallas_call(
        functools.partial(_combine_kernel, stream.n_tiles, layer == DEPTH - 1),
        grid=(stream.n_tiles,),
        in_specs=[
            pl.BlockSpec((None, 1, TOK_TILE), lambda i: (first + i, 0, 0), memory_space=pltpu.SMEM),
            pl.BlockSpec((None, 1, TOK_TILE), lambda i: (jnp.minimum(first + i + 1, last), 0, 0),
                         memory_space=pltpu.SMEM),
            pl.BlockSpec((TOK_TILE, D_MODEL), lambda i: (i, 0)),
            pl.BlockSpec((None, 1, D_MODEL), lambda i: (layer * COND_ROWS + stream.mod_row(i, TOK_TILE), 0, 5)),
            pl.BlockSpec((1, D_MODEL), lambda i: (0, 0)),
            pl.BlockSpec(memory_space=pl.ANY),
        ],
        out_specs=pl.BlockSpec((TOK_TILE, D_MODEL), lambda i: (i, 0)),
        out_shape=jax.ShapeDtypeStruct((stream.rows, D_MODEL), F32),
        scratch_shapes=[pltpu.VMEM((2, TOK_TILE, D_MODEL), F32), pltpu.SemaphoreType.DMA((2,))],
        compiler_params=_params("arbitrary"),
        name=f"combine_{stream.name}",
    )(pos, pos, x, mods, final_g.reshape(1, D_MODEL), y)


def _rope_tables():
    t = jnp.arange(DEC_SEQ, dtype=jnp.int32)
    row = (t // GRID_W).astype(F32)
    col = (t % GRID_W).astype(F32)
    inv = ROPE_BASE ** (-jnp.arange(ROPE_PAIRS_PER_AXIS, dtype=F32) / ROPE_PAIRS_PER_AXIS)
    ang = jnp.concatenate([row[:, None] * inv, col[:, None] * inv], axis=-1)
    cos, sin = jnp.cos(ang), jnp.sin(ang)
    return jnp.concatenate([cos, cos], axis=-1), jnp.concatenate([-sin, sin], axis=-1)


def kernel(x_prompt, x_sample, cache_k, cache_v, state_h, c, c_ctx, norm1_g, norm2_g, w_ada, b_ada, w_in, conv_w, conv_b, lru_wa, lru_ba, lru_wx, lru_bx, lru_lambda, attn_sink, w_attn_proj, w_rnn_proj, w_out, router_w, router_b, w_gate_up, w_down, final_norm_g):
    xs_tok = {LATENT: x_sample.reshape(T_LAT, D_MODEL), CONTEXT: x_prompt.reshape(T_CTX, D_MODEL)}
    cond = jnp.concatenate([c_ctx[None, :], c, jnp.zeros((COND_ROWS - 1 - DEC_BATCH, D_MODEL), F32)], axis=0)
    mods = _ada(cond, w_ada, b_ada).reshape(DEPTH * COND_ROWS, 1, N_ADA * D_MODEL)
    rope_cos, rope_sin = _rope_tables()

    w_in_b = w_in.astype(BF16)
    wap_b = w_attn_proj.astype(BF16)
    wrp_b = w_rnn_proj.astype(BF16)
    wo_b = w_out.astype(BF16)
    lru_w, lru_b = _lru_block_weights(lru_wa, lru_wx, lru_ba, lru_bx)
    router_wt = router_w.T

    h0 = {LATENT: jnp.transpose(state_h.astype(F32), (1, 2, 0, 3)).reshape(DEPTH, 1, 2, DEC_BATCH, D_RNN),
          CONTEXT: jnp.zeros((DEPTH, BATCH // SUBLANES, 2, SUBLANES, D_RNN), F32)}
    ck = jnp.transpose(cache_k, (1, 0, 2, 3, 4)).reshape(DEPTH, DEC_BATCH, PAST_LEN, KV_W)
    cv = jnp.transpose(cache_v, (1, 0, 2, 3, 4)).reshape(DEPTH, DEC_BATCH, PAST_LEN, KV_W)

    xs = jnp.zeros((MOE_ROWS, ROW_W), F32)
    new_k, new_v, new_h = [], [], []
    for l in range(DEPTH):
        merged = {}
        for stream in (LATENT, CONTEXT):
            q, k, v, xr, gr, ga, gb = _inproj(stream, xs_tok[stream], norm1_g, mods, rope_cos, rope_sin,
                                              w_in_b, l)
            if stream.rotary:
                att = _lat_attn(q, k, v, ck[l], cv[l], attn_sink[l])
            else:
                att = _ctx_attn(q, k, v, attn_sink[l])
                new_k.append(k.reshape(BATCH, SEQ, N_KV_HEADS, HEAD_DIM))
                new_v.append(v.reshape(BATCH, SEQ, N_KV_HEADS, HEAD_DIM))
            y, h_last = _rnn(stream, xr, conv_w, conv_b, lru_w, lru_b, lru_lambda, h0[stream][l], l)
            if not stream.rotary:
                new_h.append(jnp.transpose(h_last, (0, 2, 1, 3)).reshape(BATCH, 2, D_RNN))
            merged[stream] = _merge(stream, xs_tok[stream], att, y, gr, ga, gb, mods, norm2_g,
                                    wap_b, wrp_b, wo_b, router_wt, router_b, l)
        pos, tiles = _plan(merged[LATENT][2], merged[CONTEXT][2])
        pos = pos.reshape(N_TOK_TILES, 1, TOK_TILE)
        for stream in (LATENT, CONTEXT):
            xs = _dispatch(stream, pos, merged[stream][1], xs)
        y_sorted = _experts(tiles, xs, w_gate_up, w_down, l)
        for stream in (LATENT, CONTEXT):
            xs_tok[stream] = _combine(stream, pos, merged[stream][0], mods, y_sorted, final_norm_g, l)

    return (xs_tok[CONTEXT].reshape(BATCH, SEQ, D_MODEL), xs_tok[LATENT].reshape(DEC_BATCH, DEC_SEQ, D_MODEL),
            jnp.stack(new_k, axis=1), jnp.stack(new_v, axis=1),
            jnp.stack(new_h, axis=1).astype(x_prompt.dtype))
```

```python
import functools
from typing import NamedTuple

import jax
import jax.numpy as jnp
from jax import lax
from jax.experimental import pallas as pl
from jax.experimental.pallas import tpu as pltpu

F32 = jnp.float32
BF16 = jnp.bfloat16

D_MODEL = 1024
BATCH = 16
SEQ = 256
DEPTH = 2
DEC_BATCH = 8
DEC_SEQ = 1024
PAST_LEN = 512
GRID_W = 64
N_HEADS = 8
N_KV_HEADS = 2
GQA_GROUP = N_HEADS // N_KV_HEADS
HEAD_DIM = 128
ATTN_W = N_HEADS * HEAD_DIM
KV_W = N_KV_HEADS * HEAD_DIM
WINDOW = 128
Q_BLOCK = 128
ROPE_BASE = 10000.0
ROPE_PAIRS_PER_AXIS = HEAD_DIM // 4
D_RNN = 1024
LRU_BLOCKS = 16
LRU_BW = D_RNN // LRU_BLOCKS
LRU_C = 8.0
CONV_W = 4
CONV_LEFT = 2
N_EXPERTS = 16
N_GROUPS = 4
EXPERTS_PER_GROUP = N_EXPERTS // N_GROUPS
D_EXPERT = 512
N_ADA = 6
EPS = 1e-6
IN_COLS = ATTN_W + 2 * KV_W + 2 * D_RNN + 2 * D_MODEL

T_LAT = DEC_BATCH * DEC_SEQ
T_CTX = BATCH * SEQ
T_ALL = T_LAT + T_CTX

SUBLANES = 8
LANES = 128
VMEM_LIMIT = 56 * 1024 * 1024

TOK_TILE = 512
MM_TILE = 512
N_TOK_TILES = T_ALL // TOK_TILE
COND_ROWS = 16


class _Stream(NamedTuple):
    name: str
    batch: int
    seq: int
    first_tile: int
    rotary: bool

    @property
    def rows(self):
        return self.batch * self.seq

    @property
    def n_tiles(self):
        return self.rows // TOK_TILE

    def mod_row(self, i, tile):
        return 1 + i // (self.seq // tile) if self.rotary else 0


LATENT = _Stream("lat", DEC_BATCH, DEC_SEQ, 0, True)
CONTEXT = _Stream("ctx", BATCH, SEQ, T_LAT // TOK_TILE, False)

RNN_CB = 256
RNN_CHUNK = 256
RNN_UNROLL = 8
LOG2_E = 1.4426950408889634
PAIR_SLOTS = ((0, 1), (0, 2), (0, 3), (1, 3), (1, 2), (3, 2))
PAIRS_PER_GROUP = len(PAIR_SLOTS)
N_BUCKETS = N_GROUPS * PAIRS_PER_GROUP
ROW_W = D_MODEL + LANES
MOE_TM = 384
N_MOE_TILES = (T_ALL + N_BUCKETS * (MOE_TM - 1) + MOE_TM - 1) // MOE_TM
MOE_ROWS = N_MOE_TILES * MOE_TM


def _params(*sem):
    return pltpu.CompilerParams(dimension_semantics=sem, vmem_limit_bytes=VMEM_LIMIT)


def _sigmoid(x):
    return 0.5 * jnp.tanh(0.5 * x) + 0.5


def _ada_kernel(cond_ref, w_ref, b_ref, o_ref):
    s = jax.nn.silu(cond_ref[...]).astype(BF16)
    o_ref[...] = jnp.dot(s, w_ref[...].astype(BF16), preferred_element_type=F32) + b_ref[...]


def _ada(cond, w_ada, b_ada):
    cols = N_ADA * D_MODEL
    tn = 1536
    return pl.pallas_call(
        _ada_kernel,
        grid=(DEPTH, cols // tn),
        in_specs=[
            pl.BlockSpec((COND_ROWS, D_MODEL), lambda l, j: (0, 0)),
            pl.BlockSpec((None, D_MODEL, tn), lambda l, j: (l, 0, j)),
            pl.BlockSpec((None, 1, tn), lambda l, j: (l, 0, j)),
        ],
        out_specs=pl.BlockSpec((None, COND_ROWS, tn), lambda l, j: (l, 0, j)),
        out_shape=jax.ShapeDtypeStruct((DEPTH, COND_ROWS, cols), F32),
        compiler_params=_params("arbitrary", "arbitrary"),
        name="ada",
    )(cond, w_ada, b_ada.reshape(DEPTH, 1, cols))


def _inproj_kernel(rotary, x_ref, g_ref, sh_ref, sc_ref, *refs):
    if rotary:
        cos_ref, sin_ref, w_ref, q_ref, k_ref, v_ref, xr_ref, gr_ref, ga_ref, gb_ref = refs
    else:
        w_ref, q_ref, k_ref, v_ref, xr_ref, gr_ref, ga_ref, gb_ref = refs
    x = x_ref[...]
    y = x * lax.rsqrt(jnp.mean(x * x, axis=-1, keepdims=True) + EPS) * g_ref[...]
    h = (y * (1.0 + sc_ref[...]) + sh_ref[...]).astype(BF16)

    def proj(lo, width):
        return jnp.dot(h, w_ref[:, lo:lo + width], preferred_element_type=F32)

    def rope(t):
        if not rotary:
            return t
        return t * cos_ref[...] + pltpu.roll(t, HEAD_DIM // 2, 1) * sin_ref[...]

    scale = HEAD_DIM ** -0.5 * LOG2_E
    qk = proj(0, ATTN_W + KV_W)
    for hd in range(N_HEADS):
        q = (rope(qk[:, hd * HEAD_DIM:(hd + 1) * HEAD_DIM]) * scale).astype(BF16)
        for blk in range(MM_TILE // Q_BLOCK):
            q_ref[blk, hd] = q[blk * Q_BLOCK:(blk + 1) * Q_BLOCK]
    for g in range(N_KV_HEADS):
        k = qk[:, ATTN_W + g * HEAD_DIM:ATTN_W + (g + 1) * HEAD_DIM]
        k_ref[:, g * HEAD_DIM:(g + 1) * HEAD_DIM] = rope(k)
    v_ref[...] = proj(ATTN_W + KV_W, KV_W)
    base = ATTN_W + 2 * KV_W
    xr_ref[...] = proj(base, D_RNN).astype(BF16)
    gr_ref[...] = proj(base + D_RNN, D_RNN).astype(BF16)
    ga_ref[...] = proj(base + 2 * D_RNN, D_MODEL).astype(BF16)
    gb_ref[...] = proj(base + 2 * D_RNN + D_MODEL, D_MODEL).astype(BF16)


def _inproj(stream, x, norm_g, mods, rope_cos, rope_sin, w_in, layer):
    row = lambda i: layer * COND_ROWS + stream.mod_row(i, MM_TILE)
    tok = lambda i: (i, 0)
    rope_blk = lambda i: (i % (stream.seq // MM_TILE), 0)
    wide = pl.BlockSpec((MM_TILE, D_MODEL), tok)
    kv = pl.BlockSpec((MM_TILE, KV_W), tok)
    rope_specs = [pl.BlockSpec((MM_TILE, HEAD_DIM), rope_blk)] * 2 if stream.rotary else []
    rope_args = (rope_cos, rope_sin) if stream.rotary else ()
    return pl.pallas_call(
        functools.partial(_inproj_kernel, stream.rotary),
        grid=(stream.rows // MM_TILE,),
        in_specs=[
            wide,
            pl.BlockSpec((None, 1, D_MODEL), lambda i: (layer, 0, 0)),
            pl.BlockSpec((None, 1, D_MODEL), lambda i: (row(i), 0, 0)),
            pl.BlockSpec((None, 1, D_MODEL), lambda i: (row(i), 0, 1)),
        ] + rope_specs + [
            pl.BlockSpec((None, D_MODEL, IN_COLS), lambda i: (layer, 0, 0)),
        ],
        out_specs=[pl.BlockSpec((MM_TILE // Q_BLOCK, N_HEADS, Q_BLOCK, HEAD_DIM), lambda i: (i, 0, 0, 0)),
                   kv, kv, wide, wide, wide, wide],
        out_shape=[
            jax.ShapeDtypeStruct((stream.rows // Q_BLOCK, N_HEADS, Q_BLOCK, HEAD_DIM), BF16),
            jax.ShapeDtypeStruct((stream.rows, KV_W), F32),
            jax.ShapeDtypeStruct((stream.rows, KV_W), F32),
            jax.ShapeDtypeStruct((stream.rows, D_RNN), BF16),
            jax.ShapeDtypeStruct((stream.rows, D_RNN), BF16),
            jax.ShapeDtypeStruct((stream.rows, D_MODEL), BF16),
            jax.ShapeDtypeStruct((stream.rows, D_MODEL), BF16),
        ],
        compiler_params=_params("arbitrary"),
        name=f"inproj_{stream.name}",
    )(x, norm_g.reshape(DEPTH, 1, D_MODEL), mods, mods, *rope_args, w_in)


def _qk(q, k):
    return lax.dot_general(q, k, (((1,), (1,)), ((), ())), preferred_element_type=F32)


def _store_heads(o_ref, g, o, rows):
    for r in range(GQA_GROUP):
        hd = g * GQA_GROUP + r
        o_ref[:, hd * HEAD_DIM:(hd + 1) * HEAD_DIM] = o[r * rows:(r + 1) * rows].astype(BF16)


def _softmax_rows(s_tiles, sink, p_ref, tail_ref, rows):
    top = s_tiles[0][1]
    for _, t in s_tiles[1:]:
        top = jnp.maximum(top, t)
    m = jnp.maximum(jnp.max(top, axis=-1, keepdims=True), sink)
    for c, t in s_tiles:
        p_ref[rows, c:c + LANES] = jnp.exp2(t - m).astype(BF16)
    tail_ref[rows, :] = jnp.broadcast_to(jnp.exp2(sink - m), top.shape)


def _values_and_ones(v):
    ones = jnp.where(lax.broadcasted_iota(jnp.int32, v.shape, 1) == 0, 1.0, 0.0).astype(v.dtype)
    return jnp.concatenate([v, ones], axis=1)


def _attend(p_ref, tail_ref, v):
    acc = jnp.dot(p_ref[...], _values_and_ones(v), preferred_element_type=F32)
    return acc[:, :HEAD_DIM] / (acc[:, HEAD_DIM:HEAD_DIM + 1] + tail_ref[:, :1])


def _ctx_attn_kernel(sink_ref, q_ref, k_ref, v_ref, o_ref, s_ref, p_ref, tail_ref):
    for g in range(N_KV_HEADS):
        heads = q_ref[:, g * GQA_GROUP:(g + 1) * GQA_GROUP]
        q = jnp.swapaxes(heads, 0, 1).reshape(GQA_GROUP * SEQ, HEAD_DIM)
        k = k_ref[:, g * HEAD_DIM:(g + 1) * HEAD_DIM].astype(BF16)
        v = v_ref[:, g * HEAD_DIM:(g + 1) * HEAD_DIM].astype(BF16)
        s_ref[...] = _qk(q, k)
        for r in range(GQA_GROUP):
            rows = slice(r * SEQ, (r + 1) * SEQ)
            tiles = [(c, s_ref[rows, c:c + LANES]) for c in range(0, SEQ, LANES)]
            _softmax_rows(tiles, sink_ref[g * GQA_GROUP + r] * LOG2_E, p_ref, tail_ref, rows)
        _store_heads(o_ref, g, _attend(p_ref, tail_ref, v), SEQ)


def _ctx_attn(q, k, v, sink):
    blk = lambda b: (b, 0)
    rows = GQA_GROUP * SEQ
    return pl.pallas_call(
        _ctx_attn_kernel,
        grid=(BATCH,),
        in_specs=[
            pl.BlockSpec(memory_space=pltpu.SMEM),
            pl.BlockSpec((SEQ // Q_BLOCK, N_HEADS, Q_BLOCK, HEAD_DIM), lambda b: (b, 0, 0, 0)),
            pl.BlockSpec((SEQ, KV_W), blk),
            pl.BlockSpec((SEQ, KV_W), blk),
        ],
        out_specs=pl.BlockSpec((SEQ, ATTN_W), lambda b: (b, 0)),
        out_shape=jax.ShapeDtypeStruct((T_CTX, ATTN_W), BF16),
        scratch_shapes=[pltpu.VMEM((rows, SEQ), F32), pltpu.VMEM((rows, SEQ), BF16),
                        pltpu.VMEM((rows, LANES), F32)],
        compiler_params=_params("arbitrary"),
        name="ctx_attn",
    )(sink, q, k, v)


def _lat_attn_kernel(sink_ref, q_ref, kp_ref, kc_ref, kn_ref, vp_ref, vc_ref, vn_ref,
                     ck_ref, cv_ref, o_ref, band_ref, mask_ref, s_ref, p_ref, tail_ref):
    j = pl.program_id(1)
    rows = GQA_GROUP * Q_BLOCK
    band = Q_BLOCK + 2 * WINDOW

    @pl.when(jnp.logical_and(pl.program_id(0) == 0, j == 0))
    def _():
        ahead = (lax.broadcasted_iota(jnp.int32, (rows, band), 1) - WINDOW
                 - lax.broadcasted_iota(jnp.int32, (rows, band), 0) % Q_BLOCK)
        band_ref[...] = jnp.where(jnp.abs(ahead) <= WINDOW, 0.0, -jnp.inf)

    kpos = j * Q_BLOCK - WINDOW + lax.broadcasted_iota(jnp.int32, (1, band), 1)
    mask_ref[...] = band_ref[...] + jnp.where((kpos >= 0) & (kpos < DEC_SEQ), 0.0, -jnp.inf)
    for g in range(N_KV_HEADS):
        cols = slice(g * HEAD_DIM, (g + 1) * HEAD_DIM)
        q = q_ref[g * GQA_GROUP:(g + 1) * GQA_GROUP].reshape(rows, HEAD_DIM)
        keys = jnp.concatenate([kp_ref[:, cols], kc_ref[:, cols], kn_ref[:, cols], ck_ref[:, cols]],
                               axis=0).astype(BF16)
        vals = jnp.concatenate([vp_ref[:, cols], vc_ref[:, cols], vn_ref[:, cols], cv_ref[:, cols]],
                               axis=0).astype(BF16)
        s_ref[...] = _qk(q, keys)
        for r in range(GQA_GROUP):
            rows_r = slice(r * Q_BLOCK, (r + 1) * Q_BLOCK)
            tiles = [(c, s_ref[rows_r, c:c + LANES] + mask_ref[rows_r, c:c + LANES]) for c in range(0, band, LANES)]
            tiles += [(c, s_ref[rows_r, c:c + LANES]) for c in range(band, band + PAST_LEN, LANES)]
            _softmax_rows(tiles, sink_ref[g * GQA_GROUP + r] * LOG2_E, p_ref, tail_ref, rows_r)
        _store_heads(o_ref, g, _attend(p_ref, tail_ref, vals), Q_BLOCK)


def _lat_attn(q, k, v, cache_k, cache_v, sink):
    nb = DEC_SEQ // Q_BLOCK
    cur = lambda b, j: (b * nb + j, 0)
    prev = lambda b, j: (b * nb + jnp.maximum(j - 1, 0), 0)
    nxt = lambda b, j: (b * nb + jnp.minimum(j + 1, nb - 1), 0)
    kvb = lambda im: pl.BlockSpec((Q_BLOCK, KV_W), im)
    cache = pl.BlockSpec((None, PAST_LEN, KV_W), lambda b, j: (b, 0, 0))
    return pl.pallas_call(
        _lat_attn_kernel,
        grid=(DEC_BATCH, nb),
        in_specs=[
            pl.BlockSpec(memory_space=pltpu.SMEM),
            pl.BlockSpec((None, N_HEADS, Q_BLOCK, HEAD_DIM), lambda b, j: (b * nb + j, 0, 0, 0)),
            kvb(prev), kvb(cur), kvb(nxt), kvb(prev), kvb(cur), kvb(nxt),
            cache, cache,
        ],
        out_specs=pl.BlockSpec((Q_BLOCK, ATTN_W), cur),
        out_shape=jax.ShapeDtypeStruct((T_LAT, ATTN_W), BF16),
        scratch_shapes=[
            pltpu.VMEM((GQA_GROUP * Q_BLOCK, Q_BLOCK + 2 * WINDOW), F32),
            pltpu.VMEM((GQA_GROUP * Q_BLOCK, Q_BLOCK + 2 * WINDOW), F32),
            pltpu.VMEM((GQA_GROUP * Q_BLOCK, Q_BLOCK + 2 * WINDOW + PAST_LEN), F32),
            pltpu.VMEM((GQA_GROUP * Q_BLOCK, Q_BLOCK + 2 * WINDOW + PAST_LEN), BF16),
            pltpu.VMEM((GQA_GROUP * Q_BLOCK, LANES), F32),
        ],
        compiler_params=_params("arbitrary", "arbitrary"),
        name="lat_attn",
    )(sink, q, k, k, k, v, v, v, cache_k, cache_v)


def _rnn_kernel(seq, xr_ref, cw_ref, cb_ref, w_ref, b_ref, lam_ref, h0_ref,
                y_ref, hl_ref, xt_ref, yt_ref, a_ref, u_ref):
    n_chunks = seq // RNN_CHUNK
    rows = RNN_CHUNK * SUBLANES
    halo = jnp.zeros((CONV_LEFT, SUBLANES, RNN_CB), F32)
    xt_ref[0:CONV_LEFT] = halo
    xt_ref[seq + CONV_LEFT:seq + 2 * CONV_LEFT] = halo

    def load_chunk(c, carry):
        t0 = pl.multiple_of(c * RNN_CHUNK, RNN_CHUNK)
        x = xr_ref[:, pl.ds(t0, RNN_CHUNK), :].astype(F32)
        xt_ref[pl.ds(t0 + CONV_LEFT, RNN_CHUNK)] = jnp.swapaxes(x, 0, 1)
        return carry

    lax.fori_loop(0, n_chunks, load_chunk, 0)

    half_w = 0.5 * cw_ref[...]
    half_b = 0.5 * cb_ref[...]

    def half_conv(t0):
        acc = half_b.reshape(1, 1, RNN_CB)
        for tap in range(CONV_W):
            acc = acc + xt_ref[pl.ds(t0 + tap, RNN_CHUNK)] * half_w[tap:tap + 1, :].reshape(1, 1, RNN_CB)
        return acc

    for d in range(2):
        decay = (-0.5 * LRU_C * LOG2_E) * jax.nn.softplus(-lam_ref[d:d + 1, :])
        half_bias = b_ref[d:d + 1, :]

        def chunk(ci, h, d=d, decay=decay, half_bias=half_bias):
            c = ci if d == 0 else n_chunks - 1 - ci
            t0 = pl.multiple_of(c * RNN_CHUNK, RNN_CHUNK)
            hx = half_conv(t0).reshape(rows, RNN_CB)
            z = jnp.dot(hx.astype(BF16), w_ref[d], preferred_element_type=F32) + half_bias
            a = jnp.exp2(decay * jnp.tanh(z[:, :RNN_CB]) + decay)
            m = 1.0 - a * a
            mult = jnp.where(m == 0.0, 0.0, m * lax.rsqrt(m))
            u = mult * ((jnp.tanh(z[:, RNN_CB:]) + 1.0) * hx)
            a_ref[...] = a.reshape(RNN_CHUNK, SUBLANES, RNN_CB)
            u_ref[...] = u.reshape(RNN_CHUNK, SUBLANES, RNN_CB)

            def steps(gi, h):
                s0 = pl.multiple_of((gi if d == 0 else RNN_CHUNK // RNN_UNROLL - 1 - gi) * RNN_UNROLL, RNN_UNROLL)
                for j in (range(RNN_UNROLL) if d == 0 else reversed(range(RNN_UNROLL))):
                    h = a_ref[s0 + j] * h + u_ref[s0 + j]
                    if d == 0:
                        yt_ref[t0 + s0 + j] = h
                    else:
                        yt_ref[t0 + s0 + j] = yt_ref[t0 + s0 + j] + h
                return h

            return lax.fori_loop(0, RNN_CHUNK // RNN_UNROLL, steps, h)

        hl_ref[d] = lax.fori_loop(0, n_chunks, chunk, h0_ref[d])

    def store_chunk(c, carry):
        t0 = pl.multiple_of(c * RNN_CHUNK, RNN_CHUNK)
        y = jnp.swapaxes(yt_ref[pl.ds(t0, RNN_CHUNK)], 0, 1)
        y_ref[:, pl.ds(t0, RNN_CHUNK), :] = y.astype(BF16)
        return carry

    lax.fori_loop(0, n_chunks, store_chunk, 0)


def _rnn(stream, xr, conv_w, conv_b, w_blk, b_blk, lam, h0, layer):
    n_cb = D_RNN // RNN_CB
    seq = stream.seq
    n_groups = stream.batch // SUBLANES
    return pl.pallas_call(
        functools.partial(_rnn_kernel, seq),
        grid=(n_groups, n_cb),
        in_specs=[
            pl.BlockSpec((SUBLANES, seq, RNN_CB), lambda g, j: (g, 0, j)),
            pl.BlockSpec((None, CONV_W, RNN_CB), lambda g, j: (layer, 0, j)),
            pl.BlockSpec((None, 1, RNN_CB), lambda g, j: (layer, 0, j)),
            pl.BlockSpec((None, None, 2, RNN_CB, 2 * RNN_CB), lambda g, j: (layer, j, 0, 0, 0)),
            pl.BlockSpec((None, None, 2, 2 * RNN_CB), lambda g, j: (layer, j, 0, 0)),
            pl.BlockSpec((None, 2, RNN_CB), lambda g, j: (layer, 0, j)),
            pl.BlockSpec((None, 2, SUBLANES, RNN_CB), lambda g, j: (g, 0, 0, j)),
        ],
        out_specs=[
            pl.BlockSpec((SUBLANES, seq, RNN_CB), lambda g, j: (g, 0, j)),
            pl.BlockSpec((None, 2, SUBLANES, RNN_CB), lambda g, j: (g, 0, 0, j)),
        ],
        out_shape=[
            jax.ShapeDtypeStruct((n_groups * SUBLANES, seq, D_RNN), BF16),
            jax.ShapeDtypeStruct((n_groups, 2, SUBLANES, D_RNN), F32),
        ],
        scratch_shapes=[
            pltpu.VMEM((seq + 2 * CONV_LEFT, SUBLANES, RNN_CB), F32),
            pltpu.VMEM((seq, SUBLANES, RNN_CB), F32),
            pltpu.VMEM((RNN_CHUNK, SUBLANES, RNN_CB), F32),
            pltpu.VMEM((RNN_CHUNK, SUBLANES, RNN_CB), F32),
        ],
        compiler_params=_params("arbitrary", "arbitrary"),
        name=f"rnn_{stream.name}",
    )(xr.reshape(stream.batch, seq, D_RNN), conv_w, conv_b.reshape(DEPTH, 1, D_RNN), w_blk, b_blk, lam, h0)


def _lru_block_weights(lru_wa, lru_wx, lru_ba, lru_bx):
    n_cb = D_RNN // RNN_CB
    per = RNN_CB // LRU_BW

    def dense(w):
        w = w.reshape(DEPTH, 2, n_cb, per, LRU_BW, LRU_BW)
        eye = jnp.eye(per, dtype=w.dtype)
        full = jnp.einsum("ldcpkj,pq->ldcpkqj", w, eye)
        return full.reshape(DEPTH, 2, n_cb, RNN_CB, RNN_CB)

    w = jnp.concatenate([dense(lru_wa), dense(lru_wx)], axis=-1)
    w = jnp.transpose(w, (0, 2, 1, 3, 4)).astype(BF16)
    b = jnp.concatenate([lru_ba.reshape(DEPTH, 2, n_cb, RNN_CB),
                         lru_bx.reshape(DEPTH, 2, n_cb, RNN_CB)], axis=-1)
    return w, jnp.transpose(0.5 * b, (0, 2, 1, 3))


def _route(scores, sel):
    grp_score = []
    for g in range(N_GROUPS):
        a, b, c, d = sel[g * EXPERTS_PER_GROUP:(g + 1) * EXPERTS_PER_GROUP]
        hi1, lo1 = jnp.maximum(a, b), jnp.minimum(a, b)
        hi2, lo2 = jnp.maximum(c, d), jnp.minimum(c, d)
        grp_score.append(jnp.maximum(hi1, hi2) + jnp.maximum(jnp.minimum(hi1, hi2), jnp.maximum(lo1, lo2)))
    best = jnp.zeros_like(grp_score[0], dtype=jnp.int32)
    best_val = grp_score[0]
    for g in range(1, N_GROUPS):
        better = grp_score[g] > best_val
        best = jnp.where(better, g, best)
        best_val = jnp.where(better, grp_score[g], best_val)
    chosen = []
    for e in range(N_EXPERTS):
        g = e // EXPERTS_PER_GROUP
        rank = jnp.zeros_like(best)
        for o in range(g * EXPERTS_PER_GROUP, (g + 1) * EXPERTS_PER_GROUP):
            if o == e:
                continue
            ahead = (sel[o] >= sel[e]) if o < e else (sel[o] > sel[e])
            rank = rank + jnp.where(ahead, 1, 0)
        chosen.append(jnp.where(best == g, rank, 2) < 2)
    taken, gate = [], []
    for j in range(EXPERTS_PER_GROUP):
        t = jnp.zeros_like(best)
        s = jnp.zeros_like(scores[0])
        for g in range(N_GROUPS):
            e = g * EXPERTS_PER_GROUP + j
            t = t + jnp.where(chosen[e], 1, 0)
            s = s + jnp.where(chosen[e], scores[e], 0.0)
        taken.append(t > 0)
        gate.append(s)
    total = gate[0] + gate[1] + gate[2] + gate[3]
    pair = jnp.zeros_like(best)
    w_a = jnp.zeros_like(total)
    w_b = jnp.zeros_like(total)
    for order, (a, b) in enumerate(PAIR_SLOTS):
        both = taken[a] & taken[b]
        pair = jnp.where(both, order, pair)
        w_a = jnp.where(both, gate[a], w_a)
        w_b = jnp.where(both, gate[b], w_b)
    return best * PAIRS_PER_GROUP + pair, w_a / total, w_b / total


def _merge_kernel(x_ref, att_ref, rnn_ref, gr_ref, ga_ref, gb_ref,
                  g1_ref, sh2_ref, sc2_ref, n2_ref, wap_ref, wrp_ref, wo_ref, rw_ref, rb_ref,
                  xo_ref, h2_ref, bucket_ref):
    att = jnp.dot(att_ref[...], wap_ref[...], preferred_element_type=F32)
    gated = (jax.nn.gelu(gr_ref[...].astype(F32)) * rnn_ref[...].astype(F32)).astype(BF16)
    rnn = jnp.dot(gated, wrp_ref[...], preferred_element_type=F32)
    merged = _sigmoid(ga_ref[...].astype(F32)) * att + _sigmoid(gb_ref[...].astype(F32)) * rnn
    x = x_ref[...] + g1_ref[...] * jnp.dot(merged.astype(BF16), wo_ref[...], preferred_element_type=F32)
    xo_ref[...] = x
    y = x * lax.rsqrt(jnp.mean(x * x, axis=-1, keepdims=True) + EPS) * n2_ref[...]
    h2 = y * (1.0 + sc2_ref[...]) + sh2_ref[...]
    h2_ref[:, :D_MODEL] = h2
    h_hi = h2.astype(BF16)
    h_lo = (h2 - h_hi.astype(F32)).astype(BF16)
    rw = rw_ref[...]
    w_hi = rw.astype(BF16)
    w_lo = (rw - w_hi.astype(F32)).astype(BF16)
    logits = _qk(w_hi, h_hi) + (_qk(w_hi, h_lo) + _qk(w_lo, h_hi))
    score = _sigmoid(logits)
    sel = score + rb_ref[...]
    bucket, gate_a, gate_b = _route([score[e:e + 1, :] for e in range(N_EXPERTS)],
                                [sel[e:e + 1, :] for e in range(N_EXPERTS)])
    bucket_ref[...] = bucket
    pad = jnp.zeros((LANES - 2, MM_TILE), F32)
    h2_ref[:, D_MODEL:] = jnp.concatenate([gate_a, gate_b, pad], axis=0).T


def _merge(stream, x, att, rnn, gr, ga, gb, mods, norm_g, wap, wrp, wo, router_wt, router_b, layer):
    tok = lambda i: (i, 0)
    wide = pl.BlockSpec((MM_TILE, D_MODEL), tok)
    mod = lambda col: pl.BlockSpec((None, 1, D_MODEL),
                                   lambda i: (layer * COND_ROWS + stream.mod_row(i, MM_TILE), 0, col))
    mat = pl.BlockSpec((None, D_MODEL, D_MODEL), lambda i: (layer, 0, 0))
    return pl.pallas_call(
        _merge_kernel,
        grid=(stream.rows // MM_TILE,),
        in_specs=[
            wide, wide, wide, wide, wide, wide,
            mod(2), mod(3), mod(4),
            pl.BlockSpec((None, 1, D_MODEL), lambda i: (layer, 0, 0)),
            mat, mat, mat,
            pl.BlockSpec((N_EXPERTS, D_MODEL), lambda i: (0, 0)),
            pl.BlockSpec((N_EXPERTS, 1), lambda i: (0, 0)),
        ],
        out_specs=[wide, pl.BlockSpec((MM_TILE, ROW_W), tok),
                   pl.BlockSpec((None, 1, MM_TILE), lambda i: (i, 0, 0))],
        out_shape=[
            jax.ShapeDtypeStruct((stream.rows, D_MODEL), F32),
            jax.ShapeDtypeStruct((stream.rows, ROW_W), F32),
            jax.ShapeDtypeStruct((stream.rows // MM_TILE, 1, MM_TILE), jnp.int32),
        ],
        compiler_params=_params("arbitrary"),
        name=f"merge_{stream.name}",
    )(x, att, rnn.reshape(stream.rows, D_RNN), gr, ga, gb, mods, mods, mods,
      norm_g.reshape(DEPTH, 1, D_MODEL), wap, wrp, wo, router_wt, router_b.reshape(N_EXPERTS, 1))


def _plan_kernel(b_ref, pos_ref, tiles_ref):
    b = b_ref[...]
    r = lax.broadcasted_iota(jnp.int32, (TOK_TILE, TOK_TILE), 0)
    c = lax.broadcasted_iota(jnp.int32, (TOK_TILE, TOK_TILE), 1)
    before_in_tile = jnp.where(r < c, 1.0, 0.0).astype(BF16)
    br = lax.broadcasted_iota(jnp.int32, (N_TOK_TILES, N_TOK_TILES), 0)
    bc = lax.broadcasted_iota(jnp.int32, (N_TOK_TILES, N_TOK_TILES), 1)
    earlier_tiles = jnp.where(bc < br, 1.0, 0.0).astype(BF16)
    lane = lax.broadcasted_iota(jnp.int32, (1, LANES), 1)
    tile_start = (lane * MOE_TM).astype(F32)
    start = jnp.zeros((1, 1), F32)
    pos = jnp.zeros(b.shape, F32)
    tile_bucket = jnp.zeros((1, LANES), F32)
    for k in range(N_BUCKETS):
        mask = jnp.where(b == k, 1.0, 0.0)
        rank = jnp.dot(mask.astype(BF16), before_in_tile, preferred_element_type=F32)
        per_tile = jnp.sum(mask, axis=1, keepdims=True)
        tile_off = jnp.dot(earlier_tiles, jnp.broadcast_to(per_tile, (N_TOK_TILES, LANES)).astype(BF16),
                           preferred_element_type=F32)[:, :1]
        total = jnp.sum(per_tile, axis=0, keepdims=True)
        padded = jnp.floor((total + (MOE_TM - 0.5)) * (1.0 / MOE_TM)) * MOE_TM
        pos = pos + mask * (start + tile_off + rank)
        start = start + padded
        tile_bucket = tile_bucket + jnp.where(tile_start >= start, 1.0, 0.0)
    pos_ref[...] = pos.astype(jnp.int32)
    n_used = jnp.floor((start + 0.5) * (1.0 / MOE_TM))
    tiles_ref[...] = jnp.where(lane == LANES - 1, n_used, tile_bucket).astype(jnp.int32)


def _plan(bucket_lat, bucket_ctx):
    bucket = jnp.concatenate([bucket_lat.reshape(-1, TOK_TILE), bucket_ctx.reshape(-1, TOK_TILE)], axis=0)
    return pl.pallas_call(
        _plan_kernel,
        out_shape=[
            jax.ShapeDtypeStruct((N_TOK_TILES, TOK_TILE), jnp.int32),
            jax.ShapeDtypeStruct((1, LANES), jnp.int32),
        ],
        compiler_params=pltpu.CompilerParams(vmem_limit_bytes=VMEM_LIMIT),
        name="plan",
    )(bucket)


def _row_copy(src, dst, sem):
    return pltpu.make_async_copy(src, dst, sem)


def _dispatch_kernel(pos_ref, h_ref, xs_in_ref, xs_ref, sem):
    del xs_in_ref

    for r in range(TOK_TILE):
        _row_copy(h_ref.at[pl.ds(r, 1), :], xs_ref.at[pl.ds(pos_ref[0, r], 1), :], sem).start(priority=r % 2)

    def wait(r, carry):
        _row_copy(h_ref.at[pl.ds(0, 1), :], xs_ref.at[pl.ds(0, 1), :], sem).wait()
        return carry

    lax.fori_loop(0, TOK_TILE, wait, 0, unroll=8)


def _dispatch(stream, pos, h2, xs):
    return pl.pallas_call(
        _dispatch_kernel,
        grid=(stream.n_tiles,),
        in_specs=[
            pl.BlockSpec((None, 1, TOK_TILE), lambda i: (stream.first_tile + i, 0, 0), memory_space=pltpu.SMEM),
            pl.BlockSpec((TOK_TILE, ROW_W), lambda i: (i, 0)),
            pl.BlockSpec(memory_space=pl.ANY),
        ],
        out_specs=pl.BlockSpec(memory_space=pl.ANY),
        out_shape=jax.ShapeDtypeStruct((MOE_ROWS, ROW_W), F32),
        input_output_aliases={2: 0},
        scratch_shapes=[pltpu.SemaphoreType.DMA(())],
        compiler_params=_params("arbitrary"),
        name=f"dispatch_{stream.name}",
    )(pos, h2, xs)


def _bucket_expert(k, which):
    p = k % PAIRS_PER_GROUP
    in_group = 0
    for order, slots in enumerate(PAIR_SLOTS):
        in_group = jnp.where(p == order, slots[which], in_group)
    return (k // PAIRS_PER_GROUP) * EXPERTS_PER_GROUP + in_group


def _weight_copies(layer, wgu_ref, wd_ref, wgu_f, wd_f, sem, bucket, which, landing):
    e = _bucket_expert(bucket, which)
    return [pltpu.make_async_copy(wgu_ref.at[layer, e], wgu_f.at[landing, which], sem.at[landing, 2 * which]),
            pltpu.make_async_copy(wd_ref.at[layer, e], wd_f.at[landing, which], sem.at[landing, 2 * which + 1])]


def _experts_kernel(layer, tiles_ref, xs_ref, wgu_ref, wd_ref, y_ref, wgu_f, wd_f, sem, home_ref):
    i = pl.program_id(0)
    n_used = tiles_ref[0, LANES - 1]
    in_use = i < n_used
    cur = tiles_ref[0, i]
    prev = tiles_ref[0, jnp.maximum(i - 1, 0)]

    def changed(bucket, before):
        return [_bucket_expert(bucket, which) != _bucket_expert(before, which) for which in range(2)]

    def copies(bucket, which, landing):
        return _weight_copies(layer, wgu_ref, wd_ref, wgu_f, wd_f, sem, bucket, which, landing)

    @pl.when(jnp.logical_and(in_use, jnp.logical_or(i == 0, cur != prev)))
    def _():
        @pl.when(i == 0)
        def _():
            for which in range(2):
                home_ref[which] = 1
                for c in copies(cur, which, 0):
                    c.start()

        needed = [jnp.logical_or(i == 0, c) for c in changed(cur, prev)]
        for which in range(2):
            @pl.when(needed[which])
            def _(which=which):
                landing = 1 - home_ref[which]
                for c in copies(cur, which, landing):
                    c.wait()
                home_ref[which] = landing

        nxt_i = lax.while_loop(lambda j: jnp.logical_and(j < n_used, tiles_ref[0, jnp.minimum(j, n_used - 1)] == cur),
                               lambda j: j + 1, i + 1)

        @pl.when(nxt_i < n_used)
        def _():
            nxt = tiles_ref[0, nxt_i]
            for which, differs in enumerate(changed(nxt, cur)):
                @pl.when(differs)
                def _(which=which):
                    for c in copies(nxt, which, 1 - home_ref[which]):
                        c.start()

    @pl.when(jnp.logical_not(in_use))
    def _():
        y_ref[...] = jnp.zeros_like(y_ref)

    @pl.when(in_use)
    def _():
        x = xs_ref[:, :D_MODEL].astype(BF16)

        def branch(which, gate):
            home = home_ref[which]
            gu = jnp.dot(x, wgu_f[home, which].astype(BF16), preferred_element_type=F32)
            act = jax.nn.silu(gu[:, :D_EXPERT]) * gu[:, D_EXPERT:] * gate
            return jnp.dot(act.astype(BF16), wd_f[home, which].astype(BF16), preferred_element_type=F32)

        y_ref[...] = (branch(0, xs_ref[:, D_MODEL:D_MODEL + 1])
                      + branch(1, xs_ref[:, D_MODEL + 1:D_MODEL + 2]))


def _experts(tiles, xs, wgu, wd, layer):
    return pl.pallas_call(
        functools.partial(_experts_kernel, layer),
        grid_spec=pltpu.PrefetchScalarGridSpec(
            num_scalar_prefetch=1,
            grid=(N_MOE_TILES,),
            in_specs=[
                pl.BlockSpec((MOE_TM, ROW_W), lambda i, t: (i, 0)),
                pl.BlockSpec(memory_space=pl.ANY),
                pl.BlockSpec(memory_space=pl.ANY),
            ],
            out_specs=pl.BlockSpec((MOE_TM, D_MODEL), lambda i, t: (i, 0)),
            scratch_shapes=[
                pltpu.VMEM((2, 2, D_MODEL, 2 * D_EXPERT), F32), pltpu.VMEM((2, 2, D_EXPERT, D_MODEL), F32),
                pltpu.SemaphoreType.DMA((2, 4)), pltpu.SMEM((2,), jnp.int32),
            ],
        ),
        out_shape=jax.ShapeDtypeStruct((MOE_ROWS, D_MODEL), F32),
        compiler_params=_params("arbitrary"),
        name="experts",
    )(tiles, xs, wgu, wd)


def _combine_kernel(n_tiles, final, pos_ref, next_pos_ref, x_ref, g2_ref, fg_ref, y_ref, o_ref, rows_ref, sem):
    i = pl.program_id(0)

    def gather(p_ref, into):
        for r in range(TOK_TILE):
            _row_copy(y_ref.at[pl.ds(p_ref[0, r], 1), :], rows_ref.at[into, pl.ds(r, 1), :],
                      sem.at[into]).start(priority=r % 2)

    def step(slot):
        if slot == 0:
            @pl.when(i == 0)
            def _():
                gather(pos_ref, 0)

        @pl.when(i + 1 < n_tiles)
        def _():
            gather(next_pos_ref, 1 - slot)

        def wait(r, carry):
            _row_copy(y_ref.at[pl.ds(0, 1), :], rows_ref.at[slot, pl.ds(0, 1), :], sem.at[slot]).wait()
            return carry

        lax.fori_loop(0, TOK_TILE, wait, 0, unroll=8)
        x = x_ref[...] + g2_ref[...] * rows_ref[slot]
        if final:
            x = x * lax.rsqrt(jnp.mean(x * x, axis=-1, keepdims=True) + EPS) * fg_ref[...]
        o_ref[...] = x

    for parity in range(2):
        pl.when(i % 2 == parity)(functools.partial(step, parity))


def _combine(stream, pos, x, mods, y, final_g, layer):
    first, last = stream.first_tile, stream.first_tile + stream.n_tiles - 1
    return pl.pallas_call(
        functools.partial(_combine_kernel, stream.n_tiles, layer == DEPTH - 1),
        grid=(stream.n_tiles,),
        in_specs=[
            pl.BlockSpec((None, 1, TOK_TILE), lambda i: (first + i, 0, 0), memory_space=pltpu.SMEM),
            pl.BlockSpec((None, 1, TOK_TILE), lambda i: (jnp.minimum(first + i + 1, last), 0, 0),
                         memory_space=pltpu.SMEM),
            pl.BlockSpec((TOK_TILE, D_MODEL), lambda i: (i, 0)),
            pl.BlockSpec((None, 1, D_MODEL), lambda i: (layer * COND_ROWS + stream.mod_row(i, TOK_TILE), 0, 5)),
            pl.BlockSpec((1, D_MODEL), lambda i: (0, 0)),
            pl.BlockSpec(memory_space=pl.ANY),
        ],
        out_specs=pl.BlockSpec((TOK_TILE, D_MODEL), lambda i: (i, 0)),
        out_shape=jax.ShapeDtypeStruct((stream.rows, D_MODEL), F32),
        scratch_shapes=[pltpu.VMEM((2, TOK_TILE, D_MODEL), F32), pltpu.SemaphoreType.DMA((2,))],
        compiler_params=_params("arbitrary"),
        name=f"combine_{stream.name}",
    )(pos, pos, x, mods, final_g.reshape(1, D_MODEL), y)


def _rope_tables():
    t = jnp.arange(DEC_SEQ, dtype=jnp.int32)
    row = (t // GRID_W).astype(F32)
    col = (t % GRID_W).astype(F32)
    inv = ROPE_BASE ** (-jnp.arange(ROPE_PAIRS_PER_AXIS, dtype=F32) / ROPE_PAIRS_PER_AXIS)
    ang = jnp.concatenate([row[:, None] * inv, col[:, None] * inv], axis=-1)
    cos, sin = jnp.cos(ang), jnp.sin(ang)
    return jnp.concatenate([cos, cos], axis=-1), jnp.concatenate([-sin, sin], axis=-1)


def kernel(x_prompt, x_sample, cache_k, cache_v, state_h, c, c_ctx, norm1_g, norm2_g, w_ada, b_ada, w_in, conv_w, conv_b, lru_wa, lru_ba, lru_wx, lru_bx, lru_lambda, attn_sink, w_attn_proj, w_rnn_proj, w_out, router_w, router_b, w_gate_up, w_down, final_norm_g):
    xs_tok = {LATENT: x_sample.reshape(T_LAT, D_MODEL), CONTEXT: x_prompt.reshape(T_CTX, D_MODEL)}
    cond = jnp.concatenate([c_ctx[None, :], c, jnp.zeros((COND_ROWS - 1 - DEC_BATCH, D_MODEL), F32)], axis=0)
    mods = _ada(cond, w_ada, b_ada).reshape(DEPTH * COND_ROWS, 1, N_ADA * D_MODEL)
    rope_cos, rope_sin = _rope_tables()

    w_in_b = w_in.astype(BF16)
    wap_b = w_attn_proj.astype(BF16)
    wrp_b = w_rnn_proj.astype(BF16)
    wo_b = w_out.astype(BF16)
    lru_w, lru_b = _lru_block_weights(lru_wa, lru_wx, lru_ba, lru_bx)
    router_wt = router_w.T

    h0 = {LATENT: jnp.transpose(state_h.astype(F32), (1, 2, 0, 3)).reshape(DEPTH, 1, 2, DEC_BATCH, D_RNN),
          CONTEXT: jnp.zeros((DEPTH, BATCH // SUBLANES, 2, SUBLANES, D_RNN), F32)}
    ck = jnp.transpose(cache_k, (1, 0, 2, 3, 4)).reshape(DEPTH, DEC_BATCH, PAST_LEN, KV_W)
    cv = jnp.transpose(cache_v, (1, 0, 2, 3, 4)).reshape(DEPTH, DEC_BATCH, PAST_LEN, KV_W)

    xs = jnp.zeros((MOE_ROWS, ROW_W), F32)
    new_k, new_v, new_h = [], [], []
    for l in range(DEPTH):
        merged = {}
        for stream in (LATENT, CONTEXT):
            q, k, v, xr, gr, ga, gb = _inproj(stream, xs_tok[stream], norm1_g, mods, rope_cos, rope_sin,
                                              w_in_b, l)
            if stream.rotary:
                att = _lat_attn(q, k, v, ck[l], cv[l], attn_sink[l])
            else:
                att = _ctx_attn(q, k, v, attn_sink[l])
                new_k.append(k.reshape(BATCH, SEQ, N_KV_HEADS, HEAD_DIM))
                new_v.append(v.reshape(BATCH, SEQ, N_KV_HEADS, HEAD_DIM))
            y, h_last = _rnn(stream, xr, conv_w, conv_b, lru_w, lru_b, lru_lambda, h0[stream][l], l)
            if not stream.rotary:
                new_h.append(jnp.transpose(h_last, (0, 2, 1, 3)).reshape(BATCH, 2, D_RNN))
            merged[stream] = _merge(stream, xs_tok[stream], att, y, gr, ga, gb, mods, norm2_g,
                                    wap_b, wrp_b, wo_b, router_wt, router_b, l)
        pos, tiles = _plan(merged[LATENT][2], merged[CONTEXT][2])
        pos = pos.reshape(N_TOK_TILES, 1, TOK_TILE)
        for stream in (LATENT, CONTEXT):
            xs = _dispatch(stream, pos, merged[stream][1], xs)
        y_sorted = _experts(tiles, xs, w_gate_up, w_down, l)
        for stream in (LATENT, CONTEXT):
            xs_tok[stream] = _combine(stream, pos, merged[stream][0], mods, y_sorted, final_norm_g, l)

    return (xs_tok[CONTEXT].reshape(BATCH, SEQ, D_MODEL), xs_tok[LATENT].reshape(DEC_BATCH, DEC_SEQ, D_MODEL),
            jnp.stack(new_k, axis=1), jnp.stack(new_v, axis=1),
            jnp.stack(new_h, axis=1).astype(x_prompt.dtype))
```

```python
import functools
from typing import NamedTuple

import jax
import jax.numpy as jnp
from jax import lax
from jax.experimental import pallas as pl
from jax.experimental.pallas import tpu as pltpu

F32 = jnp.float32
BF16 = jnp.bfloat16

D_MODEL = 1024
BATCH = 16
SEQ = 256
DEPTH = 2
DEC_BATCH = 8
DEC_SEQ = 1024
PAST_LEN = 512
GRID_W = 64
N_HEADS = 8
N_KV_HEADS = 2
GQA_GROUP = N_HEADS // N_KV_HEADS
HEAD_DIM = 128
ATTN_W = N_HEADS * HEAD_DIM
KV_W = N_KV_HEADS * HEAD_DIM
WINDOW = 128
Q_BLOCK = 128
ROPE_BASE = 10000.0
ROPE_PAIRS_PER_AXIS = HEAD_DIM // 4
D_RNN = 1024
LRU_BLOCKS = 16
LRU_BW = D_RNN // LRU_BLOCKS
LRU_C = 8.0
CONV_W = 4
CONV_LEFT = 2
N_EXPERTS = 16
N_GROUPS = 4
EXPERTS_PER_GROUP = N_EXPERTS // N_GROUPS
D_EXPERT = 512
N_ADA = 6
EPS = 1e-6
IN_COLS = ATTN_W + 2 * KV_W + 2 * D_RNN + 2 * D_MODEL

T_LAT = DEC_BATCH * DEC_SEQ
T_CTX = BATCH * SEQ
T_ALL = T_LAT + T_CTX

SUBLANES = 8
LANES = 128
VMEM_LIMIT = 56 * 1024 * 1024

TOK_TILE = 1024
MM_TILE = 512
N_TOK_TILES = T_ALL // TOK_TILE
COND_ROWS = 16


class _Stream(NamedTuple):
    name: str
    batch: int
    seq: int
    first_tile: int
    rotary: bool

    @property
    def rows(self):
        return self.batch * self.seq

    @property
    def n_tiles(self):
        return self.rows // TOK_TILE

    def mod_row(self, i, tile):
        return 1 + i // (self.seq // tile) if self.rotary else 0


LATENT = _Stream("lat", DEC_BATCH, DEC_SEQ, 0, True)
CONTEXT = _Stream("ctx", BATCH, SEQ, T_LAT // TOK_TILE, False)

RNN_CB = 256
RNN_CHUNK = 256
RNN_UNROLL = 8
LOG2_E = 1.4426950408889634
PAIR_SLOTS = ((0, 1), (0, 2), (0, 3), (1, 3), (1, 2), (3, 2))
PAIRS_PER_GROUP = len(PAIR_SLOTS)
N_BUCKETS = N_GROUPS * PAIRS_PER_GROUP
ROW_W = D_MODEL + LANES
MOE_TM = 384
N_MOE_TILES = (T_ALL + N_BUCKETS * (MOE_TM - 1) + MOE_TM - 1) // MOE_TM
MOE_ROWS = N_MOE_TILES * MOE_TM


def _params(*sem):
    return pltpu.CompilerParams(dimension_semantics=sem, vmem_limit_bytes=VMEM_LIMIT)


def _sigmoid(x):
    return 0.5 * jnp.tanh(0.5 * x) + 0.5


def _ada_kernel(cond_ref, w_ref, b_ref, o_ref):
    s = jax.nn.silu(cond_ref[...]).astype(BF16)
    o_ref[...] = jnp.dot(s, w_ref[...].astype(BF16), preferred_element_type=F32) + b_ref[...]


def _ada(cond, w_ada, b_ada):
    cols = N_ADA * D_MODEL
    tn = 1536
    return pl.pallas_call(
        _ada_kernel,
        grid=(DEPTH, cols // tn),
        in_specs=[
            pl.BlockSpec((COND_ROWS, D_MODEL), lambda l, j: (0, 0)),
            pl.BlockSpec((None, D_MODEL, tn), lambda l, j: (l, 0, j)),
            pl.BlockSpec((None, 1, tn), lambda l, j: (l, 0, j)),
        ],
        out_specs=pl.BlockSpec((None, COND_ROWS, tn), lambda l, j: (l, 0, j)),
        out_shape=jax.ShapeDtypeStruct((DEPTH, COND_ROWS, cols), F32),
        compiler_params=_params("arbitrary", "arbitrary"),
        name="ada",
    )(cond, w_ada, b_ada.reshape(DEPTH, 1, cols))


def _inproj_kernel(rotary, x_ref, g_ref, sh_ref, sc_ref, *refs):
    if rotary:
        cos_ref, sin_ref, w_ref, q_ref, k_ref, v_ref, xr_ref, gr_ref, ga_ref, gb_ref = refs
    else:
        w_ref, q_ref, k_ref, v_ref, xr_ref, gr_ref, ga_ref, gb_ref = refs
    x = x_ref[...]
    y = x * lax.rsqrt(jnp.mean(x * x, axis=-1, keepdims=True) + EPS) * g_ref[...]
    h = (y * (1.0 + sc_ref[...]) + sh_ref[...]).astype(BF16)

    def proj(lo, width):
        return jnp.dot(h, w_ref[:, lo:lo + width], preferred_element_type=F32)

    def rope(t):
        if not rotary:
            return t
        return t * cos_ref[...] + pltpu.roll(t, HEAD_DIM // 2, 1) * sin_ref[...]

    scale = HEAD_DIM ** -0.5 * LOG2_E
    qk = proj(0, ATTN_W + KV_W)
    for hd in range(N_HEADS):
        q = (rope(qk[:, hd * HEAD_DIM:(hd + 1) * HEAD_DIM]) * scale).astype(BF16)
        for blk in range(MM_TILE // Q_BLOCK):
            q_ref[blk, hd] = q[blk * Q_BLOCK:(blk + 1) * Q_BLOCK]
    for g in range(N_KV_HEADS):
        k = qk[:, ATTN_W + g * HEAD_DIM:ATTN_W + (g + 1) * HEAD_DIM]
        k_ref[:, g * HEAD_DIM:(g + 1) * HEAD_DIM] = rope(k)
    v_ref[...] = proj(ATTN_W + KV_W, KV_W)
    base = ATTN_W + 2 * KV_W
    xr_ref[...] = proj(base, D_RNN).astype(BF16)
    gr_ref[...] = proj(base + D_RNN, D_RNN).astype(BF16)
    ga_ref[...] = proj(base + 2 * D_RNN, D_MODEL).astype(BF16)
    gb_ref[...] = proj(base + 2 * D_RNN + D_MODEL, D_MODEL).astype(BF16)


def _inproj(stream, x, norm_g, mods, rope_cos, rope_sin, w_in, layer):
    row = lambda i: layer * COND_ROWS + stream.mod_row(i, MM_TILE)
    tok = lambda i: (i, 0)
    rope_blk = lambda i: (i % (stream.seq // MM_TILE), 0)
    wide = pl.BlockSpec((MM_TILE, D_MODEL), tok)
    kv = pl.BlockSpec((MM_TILE, KV_W), tok)
    rope_specs = [pl.BlockSpec((MM_TILE, HEAD_DIM), rope_blk)] * 2 if stream.rotary else []
    rope_args = (rope_cos, rope_sin) if stream.rotary else ()
    return pl.pallas_call(
        functools.partial(_inproj_kernel, stream.rotary),
        grid=(stream.rows // MM_TILE,),
        in_specs=[
            wide,
            pl.BlockSpec((None, 1, D_MODEL), lambda i: (layer, 0, 0)),
            pl.BlockSpec((None, 1, D_MODEL), lambda i: (row(i), 0, 0)),
            pl.BlockSpec((None, 1, D_MODEL), lambda i: (row(i), 0, 1)),
        ] + rope_specs + [
            pl.BlockSpec((None, D_MODEL, IN_COLS), lambda i: (layer, 0, 0)),
        ],
        out_specs=[pl.BlockSpec((MM_TILE // Q_BLOCK, N_HEADS, Q_BLOCK, HEAD_DIM), lambda i: (i, 0, 0, 0)),
                   kv, kv, wide, wide, wide, wide],
        out_shape=[
            jax.ShapeDtypeStruct((stream.rows // Q_BLOCK, N_HEADS, Q_BLOCK, HEAD_DIM), BF16),
            jax.ShapeDtypeStruct((stream.rows, KV_W), F32),
            jax.ShapeDtypeStruct((stream.rows, KV_W), F32),
            jax.ShapeDtypeStruct((stream.rows, D_RNN), BF16),
            jax.ShapeDtypeStruct((stream.rows, D_RNN), BF16),
            jax.ShapeDtypeStruct((stream.rows, D_MODEL), BF16),
            jax.ShapeDtypeStruct((stream.rows, D_MODEL), BF16),
        ],
        compiler_params=_params("arbitrary"),
        name=f"inproj_{stream.name}",
    )(x, norm_g.reshape(DEPTH, 1, D_MODEL), mods, mods, *rope_args, w_in)


def _qk(q, k):
    return lax.dot_general(q, k, (((1,), (1,)), ((), ())), preferred_element_type=F32)


def _store_heads(o_ref, g, o, rows):
    for r in range(GQA_GROUP):
        hd = g * GQA_GROUP + r
        o_ref[:, hd * HEAD_DIM:(hd + 1) * HEAD_DIM] = o[r * rows:(r + 1) * rows].astype(BF16)


def _softmax_rows(s_tiles, sink, p_ref, tail_ref, rows):
    top = s_tiles[0][1]
    for _, t in s_tiles[1:]:
        top = jnp.maximum(top, t)
    m = jnp.maximum(jnp.max(top, axis=-1, keepdims=True), sink)
    for c, t in s_tiles:
        p_ref[rows, c:c + LANES] = jnp.exp2(t - m).astype(BF16)
    tail_ref[rows, :] = jnp.broadcast_to(jnp.exp2(sink - m), top.shape)


def _values_and_ones(v):
    ones = jnp.where(lax.broadcasted_iota(jnp.int32, v.shape, 1) == 0, 1.0, 0.0).astype(v.dtype)
    return jnp.concatenate([v, ones], axis=1)


def _attend(p_ref, tail_ref, v):
    acc = jnp.dot(p_ref[...], _values_and_ones(v), preferred_element_type=F32)
    return acc[:, :HEAD_DIM] / (acc[:, HEAD_DIM:HEAD_DIM + 1] + tail_ref[:, :1])


def _ctx_attn_kernel(sink_ref, q_ref, k_ref, v_ref, o_ref, s_ref, p_ref, tail_ref):
    for g in range(N_KV_HEADS):
        heads = q_ref[:, g * GQA_GROUP:(g + 1) * GQA_GROUP]
        q = jnp.swapaxes(heads, 0, 1).reshape(GQA_GROUP * SEQ, HEAD_DIM)
        k = k_ref[:, g * HEAD_DIM:(g + 1) * HEAD_DIM].astype(BF16)
        v = v_ref[:, g * HEAD_DIM:(g + 1) * HEAD_DIM].astype(BF16)
        s_ref[...] = _qk(q, k)
        for r in range(GQA_GROUP):
            rows = slice(r * SEQ, (r + 1) * SEQ)
            tiles = [(c, s_ref[rows, c:c + LANES]) for c in range(0, SEQ, LANES)]
            _softmax_rows(tiles, sink_ref[g * GQA_GROUP + r] * LOG2_E, p_ref, tail_ref, rows)
        _store_heads(o_ref, g, _attend(p_ref, tail_ref, v), SEQ)


def _ctx_attn(q, k, v, sink):
    blk = lambda b: (b, 0)
    rows = GQA_GROUP * SEQ
    return pl.pallas_call(
        _ctx_attn_kernel,
        grid=(BATCH,),
        in_specs=[
            pl.BlockSpec(memory_space=pltpu.SMEM),
            pl.BlockSpec((SEQ // Q_BLOCK, N_HEADS, Q_BLOCK, HEAD_DIM), lambda b: (b, 0, 0, 0)),
            pl.BlockSpec((SEQ, KV_W), blk),
            pl.BlockSpec((SEQ, KV_W), blk),
        ],
        out_specs=pl.BlockSpec((SEQ, ATTN_W), lambda b: (b, 0)),
        out_shape=jax.ShapeDtypeStruct((T_CTX, ATTN_W), BF16),
        scratch_shapes=[pltpu.VMEM((rows, SEQ), F32), pltpu.VMEM((rows, SEQ), BF16),
                        pltpu.VMEM((rows, LANES), F32)],
        compiler_params=_params("arbitrary"),
        name="ctx_attn",
    )(sink, q, k, v)


def _lat_attn_kernel(sink_ref, q_ref, kp_ref, kc_ref, kn_ref, vp_ref, vc_ref, vn_ref,
                     ck_ref, cv_ref, o_ref, band_ref, mask_ref, s_ref, p_ref, tail_ref):
    j = pl.program_id(1)
    rows = GQA_GROUP * Q_BLOCK
    band = Q_BLOCK + 2 * WINDOW

    @pl.when(jnp.logical_and(pl.program_id(0) == 0, j == 0))
    def _():
        ahead = (lax.broadcasted_iota(jnp.int32, (rows, band), 1) - WINDOW
                 - lax.broadcasted_iota(jnp.int32, (rows, band), 0) % Q_BLOCK)
        band_ref[...] = jnp.where(jnp.abs(ahead) <= WINDOW, 0.0, -jnp.inf)

    kpos = j * Q_BLOCK - WINDOW + lax.broadcasted_iota(jnp.int32, (1, band), 1)
    mask_ref[...] = band_ref[...] + jnp.where((kpos >= 0) & (kpos < DEC_SEQ), 0.0, -jnp.inf)
    for g in range(N_KV_HEADS):
        cols = slice(g * HEAD_DIM, (g + 1) * HEAD_DIM)
        q = q_ref[g * GQA_GROUP:(g + 1) * GQA_GROUP].reshape(rows, HEAD_DIM)
        keys = jnp.concatenate([kp_ref[:, cols], kc_ref[:, cols], kn_ref[:, cols], ck_ref[:, cols]],
                               axis=0).astype(BF16)
        vals = jnp.concatenate([vp_ref[:, cols], vc_ref[:, cols], vn_ref[:, cols], cv_ref[:, cols]],
                               axis=0).astype(BF16)
        s_ref[...] = _qk(q, keys)
        for r in range(GQA_GROUP):
            rows_r = slice(r * Q_BLOCK, (r + 1) * Q_BLOCK)
            tiles = [(c, s_ref[rows_r, c:c + LANES] + mask_ref[rows_r, c:c + LANES]) for c in range(0, band, LANES)]
            tiles += [(c, s_ref[rows_r, c:c + LANES]) for c in range(band, band + PAST_LEN, LANES)]
            _softmax_rows(tiles, sink_ref[g * GQA_GROUP + r] * LOG2_E, p_ref, tail_ref, rows_r)
        _store_heads(o_ref, g, _attend(p_ref, tail_ref, vals), Q_BLOCK)


def _lat_attn(q, k, v, cache_k, cache_v, sink):
    nb = DEC_SEQ // Q_BLOCK
    cur = lambda b, j: (b * nb + j, 0)
    prev = lambda b, j: (b * nb + jnp.maximum(j - 1, 0), 0)
    nxt = lambda b, j: (b * nb + jnp.minimum(j + 1, nb - 1), 0)
    kvb = lambda im: pl.BlockSpec((Q_BLOCK, KV_W), im)
    cache = pl.BlockSpec((None, PAST_LEN, KV_W), lambda b, j: (b, 0, 0))
    return pl.pallas_call(
        _lat_attn_kernel,
        grid=(DEC_BATCH, nb),
        in_specs=[
            pl.BlockSpec(memory_space=pltpu.SMEM),
            pl.BlockSpec((None, N_HEADS, Q_BLOCK, HEAD_DIM), lambda b, j: (b * nb + j, 0, 0, 0)),
            kvb(prev), kvb(cur), kvb(nxt), kvb(prev), kvb(cur), kvb(nxt),
            cache, cache,
        ],
        out_specs=pl.BlockSpec((Q_BLOCK, ATTN_W), cur),
        out_shape=jax.ShapeDtypeStruct((T_LAT, ATTN_W), BF16),
        scratch_shapes=[
            pltpu.VMEM((GQA_GROUP * Q_BLOCK, Q_BLOCK + 2 * WINDOW), F32),
            pltpu.VMEM((GQA_GROUP * Q_BLOCK, Q_BLOCK + 2 * WINDOW), F32),
            pltpu.VMEM((GQA_GROUP * Q_BLOCK, Q_BLOCK + 2 * WINDOW + PAST_LEN), F32),
            pltpu.VMEM((GQA_GROUP * Q_BLOCK, Q_BLOCK + 2 * WINDOW + PAST_LEN), BF16),
            pltpu.VMEM((GQA_GROUP * Q_BLOCK, LANES), F32),
        ],
        compiler_params=_params("arbitrary", "arbitrary"),
        name="lat_attn",
    )(sink, q, k, k, k, v, v, v, cache_k, cache_v)


def _rnn_kernel(seq, xr_ref, cw_ref, cb_ref, w_ref, b_ref, lam_ref, h0_ref,
                y_ref, hl_ref, xt_ref, yt_ref, a_ref, u_ref):
    n_chunks = seq // RNN_CHUNK
    rows = RNN_CHUNK * SUBLANES
    halo = jnp.zeros((CONV_LEFT, SUBLANES, RNN_CB), F32)
    xt_ref[0:CONV_LEFT] = halo
    xt_ref[seq + CONV_LEFT:seq + 2 * CONV_LEFT] = halo

    def load_chunk(c, carry):
        t0 = pl.multiple_of(c * RNN_CHUNK, RNN_CHUNK)
        x = xr_ref[:, pl.ds(t0, RNN_CHUNK), :].astype(F32)
        xt_ref[pl.ds(t0 + CONV_LEFT, RNN_CHUNK)] = jnp.swapaxes(x, 0, 1)
        return carry

    lax.fori_loop(0, n_chunks, load_chunk, 0)

    half_w = 0.5 * cw_ref[...]
    half_b = 0.5 * cb_ref[...]

    def half_conv(t0):
        acc = half_b.reshape(1, 1, RNN_CB)
        for tap in range(CONV_W):
            acc = acc + xt_ref[pl.ds(t0 + tap, RNN_CHUNK)] * half_w[tap:tap + 1, :].reshape(1, 1, RNN_CB)
        return acc

    for d in range(2):
        decay = (-0.5 * LRU_C * LOG2_E) * jax.nn.softplus(-lam_ref[d:d + 1, :])
        half_bias = b_ref[d:d + 1, :]

        def chunk(ci, h, d=d, decay=decay, half_bias=half_bias):
            c = ci if d == 0 else n_chunks - 1 - ci
            t0 = pl.multiple_of(c * RNN_CHUNK, RNN_CHUNK)
            hx = half_conv(t0).reshape(rows, RNN_CB)
            z = jnp.dot(hx.astype(BF16), w_ref[d], preferred_element_type=F32) + half_bias
            a = jnp.exp2(decay * jnp.tanh(z[:, :RNN_CB]) + decay)
            m = 1.0 - a * a
            mult = jnp.where(m == 0.0, 0.0, m * lax.rsqrt(m))
            u = mult * ((jnp.tanh(z[:, RNN_CB:]) + 1.0) * hx)
            a_ref[...] = a.reshape(RNN_CHUNK, SUBLANES, RNN_CB)
            u_ref[...] = u.reshape(RNN_CHUNK, SUBLANES, RNN_CB)

            def steps(gi, h):
                s0 = pl.multiple_of((gi if d == 0 else RNN_CHUNK // RNN_UNROLL - 1 - gi) * RNN_UNROLL, RNN_UNROLL)
                for j in (range(RNN_UNROLL) if d == 0 else reversed(range(RNN_UNROLL))):
                    h = a_ref[s0 + j] * h + u_ref[s0 + j]
                    if d == 0:
                        yt_ref[t0 + s0 + j] = h
                    else:
                        yt_ref[t0 + s0 + j] = yt_ref[t0 + s0 + j] + h
                return h

            return lax.fori_loop(0, RNN_CHUNK // RNN_UNROLL, steps, h)

        hl_ref[d] = lax.fori_loop(0, n_chunks, chunk, h0_ref[d])

    def store_chunk(c, carry):
        t0 = pl.multiple_of(c * RNN_CHUNK, RNN_CHUNK)
        y = jnp.swapaxes(yt_ref[pl.ds(t0, RNN_CHUNK)], 0, 1)
        y_ref[:, pl.ds(t0, RNN_CHUNK), :] = y.astype(BF16)
        return carry

    lax.fori_loop(0, n_chunks, store_chunk, 0)


def _rnn(stream, xr, conv_w, conv_b, w_blk, b_blk, lam, h0, layer):
    n_cb = D_RNN // RNN_CB
    seq = stream.seq
    n_groups = stream.batch // SUBLANES
    return pl.pallas_call(
        functools.partial(_rnn_kernel, seq),
        grid=(n_groups, n_cb),
        in_specs=[
            pl.BlockSpec((SUBLANES, seq, RNN_CB), lambda g, j: (g, 0, j)),
            pl.BlockSpec((None, CONV_W, RNN_CB), lambda g, j: (layer, 0, j)),
            pl.BlockSpec((None, 1, RNN_CB), lambda g, j: (layer, 0, j)),
            pl.BlockSpec((None, None, 2, RNN_CB, 2 * RNN_CB), lambda g, j: (layer, j, 0, 0, 0)),
            pl.BlockSpec((None, None, 2, 2 * RNN_CB), lambda g, j: (layer, j, 0, 0)),
            pl.BlockSpec((None, 2, RNN_CB), lambda g, j: (layer, 0, j)),
            pl.BlockSpec((None, 2, SUBLANES, RNN_CB), lambda g, j: (g, 0, 0, j)),
        ],
        out_specs=[
            pl.BlockSpec((SUBLANES, seq, RNN_CB), lambda g, j: (g, 0, j)),
            pl.BlockSpec((None, 2, SUBLANES, RNN_CB), lambda g, j: (g, 0, 0, j)),
        ],
        out_shape=[
            jax.ShapeDtypeStruct((n_groups * SUBLANES, seq, D_RNN), BF16),
            jax.ShapeDtypeStruct((n_groups, 2, SUBLANES, D_RNN), F32),
        ],
        scratch_shapes=[
            pltpu.VMEM((seq + 2 * CONV_LEFT, SUBLANES, RNN_CB), F32),
            pltpu.VMEM((seq, SUBLANES, RNN_CB), F32),
            pltpu.VMEM((RNN_CHUNK, SUBLANES, RNN_CB), F32),
            pltpu.VMEM((RNN_CHUNK, SUBLANES, RNN_CB), F32),
        ],
        compiler_params=_params("arbitrary", "arbitrary"),
        name=f"rnn_{stream.name}",
    )(xr.reshape(stream.batch, seq, D_RNN), conv_w, conv_b.reshape(DEPTH, 1, D_RNN), w_blk, b_blk, lam, h0)


def _lru_block_weights(lru_wa, lru_wx, lru_ba, lru_bx):
    n_cb = D_RNN // RNN_CB
    per = RNN_CB // LRU_BW

    def dense(w):
        w = w.reshape(DEPTH, 2, n_cb, per, LRU_BW, LRU_BW)
        eye = jnp.eye(per, dtype=w.dtype)
        full = jnp.einsum("ldcpkj,pq->ldcpkqj", w, eye)
        return full.reshape(DEPTH, 2, n_cb, RNN_CB, RNN_CB)

    w = jnp.concatenate([dense(lru_wa), dense(lru_wx)], axis=-1)
    w = jnp.transpose(w, (0, 2, 1, 3, 4)).astype(BF16)
    b = jnp.concatenate([lru_ba.reshape(DEPTH, 2, n_cb, RNN_CB),
                         lru_bx.reshape(DEPTH, 2, n_cb, RNN_CB)], axis=-1)
    return w, jnp.transpose(0.5 * b, (0, 2, 1, 3))


def _route(scores, sel):
    grp_score = []
    for g in range(N_GROUPS):
        a, b, c, d = sel[g * EXPERTS_PER_GROUP:(g + 1) * EXPERTS_PER_GROUP]
        hi1, lo1 = jnp.maximum(a, b), jnp.minimum(a, b)
        hi2, lo2 = jnp.maximum(c, d), jnp.minimum(c, d)
        grp_score.append(jnp.maximum(hi1, hi2) + jnp.maximum(jnp.minimum(hi1, hi2), jnp.maximum(lo1, lo2)))
    best = jnp.zeros_like(grp_score[0], dtype=jnp.int32)
    best_val = grp_score[0]
    for g in range(1, N_GROUPS):
        better = grp_score[g] > best_val
        best = jnp.where(better, g, best)
        best_val = jnp.where(better, grp_score[g], best_val)
    chosen = []
    for e in range(N_EXPERTS):
        g = e // EXPERTS_PER_GROUP
        rank = jnp.zeros_like(best)
        for o in range(g * EXPERTS_PER_GROUP, (g + 1) * EXPERTS_PER_GROUP):
            if o == e:
                continue
            ahead = (sel[o] >= sel[e]) if o < e else (sel[o] > sel[e])
            rank = rank + jnp.where(ahead, 1, 0)
        chosen.append(jnp.where(best == g, rank, 2) < 2)
    taken, gate = [], []
    for j in range(EXPERTS_PER_GROUP):
        t = jnp.zeros_like(best)
        s = jnp.zeros_like(scores[0])
        for g in range(N_GROUPS):
            e = g * EXPERTS_PER_GROUP + j
            t = t + jnp.where(chosen[e], 1, 0)
            s = s + jnp.where(chosen[e], scores[e], 0.0)
        taken.append(t > 0)
        gate.append(s)
    total = gate[0] + gate[1] + gate[2] + gate[3]
    pair = jnp.zeros_like(best)
    w_a = jnp.zeros_like(total)
    w_b = jnp.zeros_like(total)
    for order, (a, b) in enumerate(PAIR_SLOTS):
        both = taken[a] & taken[b]
        pair = jnp.where(both, order, pair)
        w_a = jnp.where(both, gate[a], w_a)
        w_b = jnp.where(both, gate[b], w_b)
    return best * PAIRS_PER_GROUP + pair, w_a / total, w_b / total


def _merge_kernel(x_ref, att_ref, rnn_ref, gr_ref, ga_ref, gb_ref,
                  g1_ref, sh2_ref, sc2_ref, n2_ref, wap_ref, wrp_ref, wo_ref, rw_ref, rb_ref,
                  xo_ref, h2_ref, bucket_ref):
    att = jnp.dot(att_ref[...], wap_ref[...], preferred_element_type=F32)
    gated = (jax.nn.gelu(gr_ref[...].astype(F32)) * rnn_ref[...].astype(F32)).astype(BF16)
    rnn = jnp.dot(gated, wrp_ref[...], preferred_element_type=F32)
    merged = _sigmoid(ga_ref[...].astype(F32)) * att + _sigmoid(gb_ref[...].astype(F32)) * rnn
    x = x_ref[...] + g1_ref[...] * jnp.dot(merged.astype(BF16), wo_ref[...], preferred_element_type=F32)
    xo_ref[...] = x
    y = x * lax.rsqrt(jnp.mean(x * x, axis=-1, keepdims=True) + EPS) * n2_ref[...]
    h2 = y * (1.0 + sc2_ref[...]) + sh2_ref[...]
    h2_ref[:, :D_MODEL] = h2
    h_hi = h2.astype(BF16)
    h_lo = (h2 - h_hi.astype(F32)).astype(BF16)
    rw = rw_ref[...]
    w_hi = rw.astype(BF16)
    w_lo = (rw - w_hi.astype(F32)).astype(BF16)
    logits = _qk(w_hi, h_hi) + (_qk(w_hi, h_lo) + _qk(w_lo, h_hi))
    score = _sigmoid(logits)
    sel = score + rb_ref[...]
    bucket, gate_a, gate_b = _route([score[e:e + 1, :] for e in range(N_EXPERTS)],
                                [sel[e:e + 1, :] for e in range(N_EXPERTS)])
    bucket_ref[...] = bucket
    pad = jnp.zeros((LANES - 2, MM_TILE), F32)
    h2_ref[:, D_MODEL:] = jnp.concatenate([gate_a, gate_b, pad], axis=0).T


def _merge(stream, x, att, rnn, gr, ga, gb, mods, norm_g, wap, wrp, wo, router_wt, router_b, layer):
    tok = lambda i: (i, 0)
    wide = pl.BlockSpec((MM_TILE, D_MODEL), tok)
    mod = lambda col: pl.BlockSpec((None, 1, D_MODEL),
                                   lambda i: (layer * COND_ROWS + stream.mod_row(i, MM_TILE), 0, col))
    mat = pl.BlockSpec((None, D_MODEL, D_MODEL), lambda i: (layer, 0, 0))
    return pl.pallas_call(
        _merge_kernel,
        grid=(stream.rows // MM_TILE,),
        in_specs=[
            wide, wide, wide, wide, wide, wide,
            mod(2), mod(3), mod(4),
            pl.BlockSpec((None, 1, D_MODEL), lambda i: (layer, 0, 0)),
            mat, mat, mat,
            pl.BlockSpec((N_EXPERTS, D_MODEL), lambda i: (0, 0)),
            pl.BlockSpec((N_EXPERTS, 1), lambda i: (0, 0)),
        ],
        out_specs=[wide, pl.BlockSpec((MM_TILE, ROW_W), tok),
                   pl.BlockSpec((None, 1, MM_TILE), lambda i: (i, 0, 0))],
        out_shape=[
            jax.ShapeDtypeStruct((stream.rows, D_MODEL), F32),
            jax.ShapeDtypeStruct((stream.rows, ROW_W), F32),
            jax.ShapeDtypeStruct((stream.rows // MM_TILE, 1, MM_TILE), jnp.int32),
        ],
        compiler_params=_params("arbitrary"),
        name=f"merge_{stream.name}",
    )(x, att, rnn.reshape(stream.rows, D_RNN), gr, ga, gb, mods, mods, mods,
      norm_g.reshape(DEPTH, 1, D_MODEL), wap, wrp, wo, router_wt, router_b.reshape(N_EXPERTS, 1))


def _plan_kernel(b_ref, pos_ref, tiles_ref):
    b = b_ref[...]
    r = lax.broadcasted_iota(jnp.int32, (TOK_TILE, TOK_TILE), 0)
    c = lax.broadcasted_iota(jnp.int32, (TOK_TILE, TOK_TILE), 1)
    before_in_tile = jnp.where(r < c, 1.0, 0.0).astype(BF16)
    br = lax.broadcasted_iota(jnp.int32, (N_TOK_TILES, N_TOK_TILES), 0)
    bc = lax.broadcasted_iota(jnp.int32, (N_TOK_TILES, N_TOK_TILES), 1)
    earlier_tiles = jnp.where(bc < br, 1.0, 0.0).astype(BF16)
    lane = lax.broadcasted_iota(jnp.int32, (1, LANES), 1)
    tile_start = (lane * MOE_TM).astype(F32)
    start = jnp.zeros((1, 1), F32)
    pos = jnp.zeros(b.shape, F32)
    tile_bucket = jnp.zeros((1, LANES), F32)
    for k in range(N_BUCKETS):
        mask = jnp.where(b == k, 1.0, 0.0)
        rank = jnp.dot(mask.astype(BF16), before_in_tile, preferred_element_type=F32)
        per_tile = jnp.sum(mask, axis=1, keepdims=True)
        tile_off = jnp.dot(earlier_tiles, jnp.broadcast_to(per_tile, (N_TOK_TILES, LANES)).astype(BF16),
                           preferred_element_type=F32)[:, :1]
        total = jnp.sum(per_tile, axis=0, keepdims=True)
        padded = jnp.floor((total + (MOE_TM - 0.5)) * (1.0 / MOE_TM)) * MOE_TM
        pos = pos + mask * (start + tile_off + rank)
        start = start + padded
        tile_bucket = tile_bucket + jnp.where(tile_start >= start, 1.0, 0.0)
    pos_ref[...] = pos.astype(jnp.int32)
    n_used = jnp.floor((start + 0.5) * (1.0 / MOE_TM))
    tiles_ref[...] = jnp.where(lane == LANES - 1, n_used, tile_bucket).astype(jnp.int32)


def _plan(bucket_lat, bucket_ctx):
    bucket = jnp.concatenate([bucket_lat.reshape(-1, TOK_TILE), bucket_ctx.reshape(-1, TOK_TILE)], axis=0)
    return pl.pallas_call(
        _plan_kernel,
        out_shape=[
            jax.ShapeDtypeStruct((N_TOK_TILES, TOK_TILE), jnp.int32),
            jax.ShapeDtypeStruct((1, LANES), jnp.int32),
        ],
        compiler_params=pltpu.CompilerParams(vmem_limit_bytes=VMEM_LIMIT),
        name="plan",
    )(bucket)


def _row_copy(src, dst, sem):
    return pltpu.make_async_copy(src, dst, sem)


def _dispatch_kernel(pos_ref, h_ref, xs_in_ref, xs_ref, sem):
    del xs_in_ref

    for r in range(TOK_TILE):
        _row_copy(h_ref.at[pl.ds(r, 1), :], xs_ref.at[pl.ds(pos_ref[0, r], 1), :], sem).start(priority=r % 2)

    def wait(r, carry):
        _row_copy(h_ref.at[pl.ds(0, 1), :], xs_ref.at[pl.ds(0, 1), :], sem).wait()
        return carry

    lax.fori_loop(0, TOK_TILE, wait, 0, unroll=8)


def _dispatch(stream, pos, h2, xs):
    return pl.pallas_call(
        _dispatch_kernel,
        grid=(stream.n_tiles,),
        in_specs=[
            pl.BlockSpec((None, 1, TOK_TILE), lambda i: (stream.first_tile + i, 0, 0), memory_space=pltpu.SMEM),
            pl.BlockSpec((TOK_TILE, ROW_W), lambda i: (i, 0)),
            pl.BlockSpec(memory_space=pl.ANY),
        ],
        out_specs=pl.BlockSpec(memory_space=pl.ANY),
        out_shape=jax.ShapeDtypeStruct((MOE_ROWS, ROW_W), F32),
        input_output_aliases={2: 0},
        scratch_shapes=[pltpu.SemaphoreType.DMA(())],
        compiler_params=_params("arbitrary"),
        name=f"dispatch_{stream.name}",
    )(pos, h2, xs)


def _bucket_expert(k, which):
    p = k % PAIRS_PER_GROUP
    in_group = 0
    for order, slots in enumerate(PAIR_SLOTS):
        in_group = jnp.where(p == order, slots[which], in_group)
    return (k // PAIRS_PER_GROUP) * EXPERTS_PER_GROUP + in_group


def _weight_copies(layer, wgu_ref, wd_ref, wgu_f, wd_f, sem, bucket, which, landing):
    e = _bucket_expert(bucket, which)
    return [pltpu.make_async_copy(wgu_ref.at[layer, e], wgu_f.at[landing, which], sem.at[landing, 2 * which]),
            pltpu.make_async_copy(wd_ref.at[layer, e], wd_f.at[landing, which], sem.at[landing, 2 * which + 1])]


def _experts_kernel(layer, tiles_ref, xs_ref, wgu_ref, wd_ref, y_ref, wgu_f, wd_f, sem, home_ref):
    i = pl.program_id(0)
    n_used = tiles_ref[0, LANES - 1]
    in_use = i < n_used
    cur = tiles_ref[0, i]
    prev = tiles_ref[0, jnp.maximum(i - 1, 0)]

    def changed(bucket, before):
        return [_bucket_expert(bucket, which) != _bucket_expert(before, which) for which in range(2)]

    def copies(bucket, which, landing):
        return _weight_copies(layer, wgu_ref, wd_ref, wgu_f, wd_f, sem, bucket, which, landing)

    @pl.when(jnp.logical_and(in_use, jnp.logical_or(i == 0, cur != prev)))
    def _():
        @pl.when(i == 0)
        def _():
            for which in range(2):
                home_ref[which] = 1
                for c in copies(cur, which, 0):
                    c.start()

        needed = [jnp.logical_or(i == 0, c) for c in changed(cur, prev)]
        for which in range(2):
            @pl.when(needed[which])
            def _(which=which):
                landing = 1 - home_ref[which]
                for c in copies(cur, which, landing):
                    c.wait()
                home_ref[which] = landing

        nxt_i = lax.while_loop(lambda j: jnp.logical_and(j < n_used, tiles_ref[0, jnp.minimum(j, n_used - 1)] == cur),
                               lambda j: j + 1, i + 1)

        @pl.when(nxt_i < n_used)
        def _():
            nxt = tiles_ref[0, nxt_i]
            for which, differs in enumerate(changed(nxt, cur)):
                @pl.when(differs)
                def _(which=which):
                    for c in copies(nxt, which, 1 - home_ref[which]):
                        c.start()

    @pl.when(jnp.logical_not(in_use))
    def _():
        y_ref[...] = jnp.zeros_like(y_ref)

    @pl.when(in_use)
    def _():
        x = xs_ref[:, :D_MODEL].astype(BF16)

        def branch(which, gate):
            home = home_ref[which]
            gu = jnp.dot(x, wgu_f[home, which].astype(BF16), preferred_element_type=F32)
            act = jax.nn.silu(gu[:, :D_EXPERT]) * gu[:, D_EXPERT:] * gate
            return jnp.dot(act.astype(BF16), wd_f[home, which].astype(BF16), preferred_element_type=F32)

        y_ref[...] = (branch(0, xs_ref[:, D_MODEL:D_MODEL + 1])
                      + branch(1, xs_ref[:, D_MODEL + 1:D_MODEL + 2]))


def _experts(tiles, xs, wgu, wd, layer):
    return pl.pallas_call(
        functools.partial(_experts_kernel, layer),
        grid_spec=pltpu.PrefetchScalarGridSpec(
            num_scalar_prefetch=1,
            grid=(N_MOE_TILES,),
            in_specs=[
                pl.BlockSpec((MOE_TM, ROW_W), lambda i, t: (i, 0)),
                pl.BlockSpec(memory_space=pl.ANY),
                pl.BlockSpec(memory_space=pl.ANY),
            ],
            out_specs=pl.BlockSpec((MOE_TM, D_MODEL), lambda i, t: (i, 0)),
            scratch_shapes=[
                pltpu.VMEM((2, 2, D_MODEL, 2 * D_EXPERT), F32), pltpu.VMEM((2, 2, D_EXPERT, D_MODEL), F32),
                pltpu.SemaphoreType.DMA((2, 4)), pltpu.SMEM((2,), jnp.int32),
            ],
        ),
        out_shape=jax.ShapeDtypeStruct((MOE_ROWS, D_MODEL), F32),
        compiler_params=_params("arbitrary"),
        name="experts",
    )(tiles, xs, wgu, wd)


def _combine_kernel(n_tiles, final, pos_ref, next_pos_ref, x_ref, g2_ref, fg_ref, y_ref, o_ref, rows_ref, sem):
    i = pl.program_id(0)

    def gather(p_ref, into):
        for r in range(TOK_TILE):
            _row_copy(y_ref.at[pl.ds(p_ref[0, r], 1), :], rows_ref.at[into, pl.ds(r, 1), :],
                      sem.at[into]).start(priority=r % 2)

    def step(slot):
        if slot == 0:
            @pl.when(i == 0)
            def _():
                gather(pos_ref, 0)

        @pl.when(i + 1 < n_tiles)
        def _():
            gather(next_pos_ref, 1 - slot)

        def wait(r, carry):
            _row_copy(y_ref.at[pl.ds(0, 1), :], rows_ref.at[slot, pl.ds(0, 1), :], sem.at[slot]).wait()
            return carry

        lax.fori_loop(0, TOK_TILE, wait, 0, unroll=8)
        x = x_ref[...] + g2_ref[...] * rows_ref[slot]
        if final:
            x = x * lax.rsqrt(jnp.mean(x * x, axis=-1, keepdims=True) + EPS) * fg_ref[...]
        o_ref[...] = x

    for parity in range(2):
        pl.when(i % 2 == parity)(functools.partial(step, parity))


def _combine(stream, pos, x, mods, y, final_g, layer):
    first, last = stream.first_tile, stream.first_tile + stream.n_tiles - 1
    return pl.pallas_call(
        functools.partial(_combine_kernel, stream.n_tiles, layer == DEPTH - 1),
        grid=(stream.n_tiles,),
        in_specs=[
            pl.BlockSpec((None, 1, TOK_TILE), lambda i: (first + i, 0, 0), memory_space=pltpu.SMEM),
            pl.BlockSpec((None, 1, TOK_TILE), lambda i: (jnp.minimum(first + i + 1, last), 0, 0),
                         memory_space=pltpu.SMEM),
            pl.BlockSpec((TOK_TILE, D_MODEL), lambda i: (i, 0)),
            pl.BlockSpec((None, 1, D_MODEL), lambda i: (layer * COND_ROWS + stream.mod_row(i, TOK_TILE), 0, 5)),
            pl.BlockSpec((1, D_MODEL), lambda i: (0, 0)),
            pl.BlockSpec(memory_space=pl.ANY),
        ],
        out_specs=pl.BlockSpec((TOK_TILE, D_MODEL), lambda i: (i, 0)),
        out_shape=jax.ShapeDtypeStruct((stream.rows, D_MODEL), F32),
        scratch_shapes=[pltpu.VMEM((2, TOK_TILE, D_MODEL), F32), pltpu.SemaphoreType.DMA((2,))],
        compiler_params=_params("arbitrary"),
        name=f"combine_{stream.name}",
    )(pos, pos, x, mods, final_g.reshape(1, D_MODEL), y)


def _rope_tables():
    t = jnp.arange(DEC_SEQ, dtype=jnp.int32)
    row = (t // GRID_W).astype(F32)
    col = (t % GRID_W).astype(F32)
    inv = ROPE_BASE ** (-jnp.arange(ROPE_PAIRS_PER_AXIS, dtype=F32) / ROPE_PAIRS_PER_AXIS)
    ang = jnp.concatenate([row[:, None] * inv, col[:, None] * inv], axis=-1)
    cos, sin = jnp.cos(ang), jnp.sin(ang)
    return jnp.concatenate([cos, cos], axis=-1), jnp.concatenate([-sin, sin], axis=-1)


def kernel(x_prompt, x_sample, cache_k, cache_v, state_h, c, c_ctx, norm1_g, norm2_g, w_ada, b_ada, w_in, conv_w, conv_b, lru_wa, lru_ba, lru_wx, lru_bx, lru_lambda, attn_sink, w_attn_proj, w_rnn_proj, w_out, router_w, router_b, w_gate_up, w_down, final_norm_g):
    xs_tok = {LATENT: x_sample.reshape(T_LAT, D_MODEL), CONTEXT: x_prompt.reshape(T_CTX, D_MODEL)}
    cond = jnp.concatenate([c_ctx[None, :], c, jnp.zeros((COND_ROWS - 1 - DEC_BATCH, D_MODEL), F32)], axis=0)
    mods = _ada(cond, w_ada, b_ada).reshape(DEPTH * COND_ROWS, 1, N_ADA * D_MODEL)
    rope_cos, rope_sin = _rope_tables()

    w_in_b = w_in.astype(BF16)
    wap_b = w_attn_proj.astype(BF16)
    wrp_b = w_rnn_proj.astype(BF16)
    wo_b = w_out.astype(BF16)
    lru_w, lru_b = _lru_block_weights(lru_wa, lru_wx, lru_ba, lru_bx)
    router_wt = router_w.T

    h0 = {LATENT: jnp.transpose(state_h.astype(F32), (1, 2, 0, 3)).reshape(DEPTH, 1, 2, DEC_BATCH, D_RNN),
          CONTEXT: jnp.zeros((DEPTH, BATCH // SUBLANES, 2, SUBLANES, D_RNN), F32)}
    ck = jnp.transpose(cache_k, (1, 0, 2, 3, 4)).reshape(DEPTH, DEC_BATCH, PAST_LEN, KV_W)
    cv = jnp.transpose(cache_v, (1, 0, 2, 3, 4)).reshape(DEPTH, DEC_BATCH, PAST_LEN, KV_W)

    xs = jnp.zeros((MOE_ROWS, ROW_W), F32)
    new_k, new_v, new_h = [], [], []
    for l in range(DEPTH):
        merged = {}
        for stream in (LATENT, CONTEXT):
            q, k, v, xr, gr, ga, gb = _inproj(stream, xs_tok[stream], norm1_g, mods, rope_cos, rope_sin,
                                              w_in_b, l)
            if stream.rotary:
                att = _lat_attn(q, k, v, ck[l], cv[l], attn_sink[l])
            else:
                att = _ctx_attn(q, k, v, attn_sink[l])
                new_k.append(k.reshape(BATCH, SEQ, N_KV_HEADS, HEAD_DIM))
                new_v.append(v.reshape(BATCH, SEQ, N_KV_HEADS, HEAD_DIM))
            y, h_last = _rnn(stream, xr, conv_w, conv_b, lru_w, lru_b, lru_lambda, h0[stream][l], l)
            if not stream.rotary:
                new_h.append(jnp.transpose(h_last, (0, 2, 1, 3)).reshape(BATCH, 2, D_RNN))
            merged[stream] = _merge(stream, xs_tok[stream], att, y, gr, ga, gb, mods, norm2_g,
                                    wap_b, wrp_b, wo_b, router_wt, router_b, l)
        pos, tiles = _plan(merged[LATENT][2], merged[CONTEXT][2])
        pos = pos.reshape(N_TOK_TILES, 1, TOK_TILE)
        for stream in (LATENT, CONTEXT):
            xs = _dispatch(stream, pos, merged[stream][1], xs)
        y_sorted = _experts(tiles, xs, w_gate_up, w_down, l)
        for stream in (LATENT, CONTEXT):
            xs_tok[stream] = _combine(stream, pos, merged[stream][0], mods, y_sorted, final_norm_g, l)

    return (xs_tok[CONTEXT].reshape(BATCH, SEQ, D_MODEL), xs_tok[LATENT].reshape(DEC_BATCH, DEC_SEQ, D_MODEL),
            jnp.stack(new_k, axis=1), jnp.stack(new_v, axis=1),
            jnp.stack(new_h, axis=1).astype(x_prompt.dtype))
```

```python
import functools
from typing import NamedTuple

import jax
import jax.numpy as jnp
from jax import lax
from jax.experimental import pallas as pl
from jax.experimental.pallas import tpu as pltpu

F32 = jnp.float32
BF16 = jnp.bfloat16

D_MODEL = 1024
BATCH = 16
SEQ = 256
DEPTH = 2
DEC_BATCH = 8
DEC_SEQ = 1024
PAST_LEN = 512
GRID_W = 64
N_HEADS = 8
N_KV_HEADS = 2
GQA_GROUP = N_HEADS // N_KV_HEADS
HEAD_DIM = 128
ATTN_W = N_HEADS * HEAD_DIM
KV_W = N_KV_HEADS * HEAD_DIM
WINDOW = 128
Q_BLOCK = 128
ROPE_BASE = 10000.0
ROPE_PAIRS_PER_AXIS = HEAD_DIM // 4
D_RNN = 1024
LRU_BLOCKS = 16
LRU_BW = D_RNN // LRU_BLOCKS
LRU_C = 8.0
CONV_W = 4
CONV_LEFT = 2
N_EXPERTS = 16
N_GROUPS = 4
EXPERTS_PER_GROUP = N_EXPERTS // N_GROUPS
D_EXPERT = 512
N_ADA = 6
EPS = 1e-6
IN_COLS = ATTN_W + 2 * KV_W + 2 * D_RNN + 2 * D_MODEL

T_LAT = DEC_BATCH * DEC_SEQ
T_CTX = BATCH * SEQ
T_ALL = T_LAT + T_CTX

SUBLANES = 8
LANES = 128
VMEM_LIMIT = 56 * 1024 * 1024

TOK_TILE = 512
DISPATCH_TILE = 1024
MM_TILE = 512
N_TOK_TILES = T_ALL // TOK_TILE
COND_ROWS = 16


class _Stream(NamedTuple):
    name: str
    batch: int
    seq: int
    first_tile: int
    rotary: bool

    @property
    def rows(self):
        return self.batch * self.seq

    @property
    def n_tiles(self):
        return self.rows // TOK_TILE

    def mod_row(self, i, tile):
        return 1 + i // (self.seq // tile) if self.rotary else 0


LATENT = _Stream("lat", DEC_BATCH, DEC_SEQ, 0, True)
CONTEXT = _Stream("ctx", BATCH, SEQ, T_LAT // TOK_TILE, False)

RNN_CB = 256
RNN_CHUNK = 256
RNN_UNROLL = 8
LOG2_E = 1.4426950408889634
PAIR_SLOTS = ((0, 1), (0, 2), (0, 3), (1, 3), (1, 2), (3, 2))
PAIRS_PER_GROUP = len(PAIR_SLOTS)
N_BUCKETS = N_GROUPS * PAIRS_PER_GROUP
ROW_W = D_MODEL + LANES
MOE_TM = 384
N_MOE_TILES = (T_ALL + N_BUCKETS * (MOE_TM - 1) + MOE_TM - 1) // MOE_TM
MOE_ROWS = N_MOE_TILES * MOE_TM


def _params(*sem):
    return pltpu.CompilerParams(dimension_semantics=sem, vmem_limit_bytes=VMEM_LIMIT)


def _sigmoid(x):
    return 0.5 * jnp.tanh(0.5 * x) + 0.5


def _ada_kernel(cond_ref, w_ref, b_ref, o_ref):
    s = jax.nn.silu(cond_ref[...]).astype(BF16)
    o_ref[...] = jnp.dot(s, w_ref[...].astype(BF16), preferred_element_type=F32) + b_ref[...]


def _ada(cond, w_ada, b_ada):
    cols = N_ADA * D_MODEL
    tn = 1536
    return pl.pallas_call(
        _ada_kernel,
        grid=(DEPTH, cols // tn),
        in_specs=[
            pl.BlockSpec((COND_ROWS, D_MODEL), lambda l, j: (0, 0)),
            pl.BlockSpec((None, D_MODEL, tn), lambda l, j: (l, 0, j)),
            pl.BlockSpec((None, 1, tn), lambda l, j: (l, 0, j)),
        ],
        out_specs=pl.BlockSpec((None, COND_ROWS, tn), lambda l, j: (l, 0, j)),
        out_shape=jax.ShapeDtypeStruct((DEPTH, COND_ROWS, cols), F32),
        compiler_params=_params("arbitrary", "arbitrary"),
        name="ada",
    )(cond, w_ada, b_ada.reshape(DEPTH, 1, cols))


def _inproj_kernel(rotary, x_ref, g_ref, sh_ref, sc_ref, *refs):
    if rotary:
        cos_ref, sin_ref, w_ref, q_ref, k_ref, v_ref, xr_ref, gr_ref, ga_ref, gb_ref = refs
    else:
        w_ref, q_ref, k_ref, v_ref, xr_ref, gr_ref, ga_ref, gb_ref = refs
    x = x_ref[...]
    y = x * lax.rsqrt(jnp.mean(x * x, axis=-1, keepdims=True) + EPS) * g_ref[...]
    h = (y * (1.0 + sc_ref[...]) + sh_ref[...]).astype(BF16)

    def proj(lo, width):
        return jnp.dot(h, w_ref[:, lo:lo + width], preferred_element_type=F32)

    def rope(t):
        if not rotary:
            return t
        return t * cos_ref[...] + pltpu.roll(t, HEAD_DIM // 2, 1) * sin_ref[...]

    scale = HEAD_DIM ** -0.5 * LOG2_E
    qk = proj(0, ATTN_W + KV_W)
    for hd in range(N_HEADS):
        q = (rope(qk[:, hd * HEAD_DIM:(hd + 1) * HEAD_DIM]) * scale).astype(BF16)
        for blk in range(MM_TILE // Q_BLOCK):
            q_ref[blk, hd] = q[blk * Q_BLOCK:(blk + 1) * Q_BLOCK]
    for g in range(N_KV_HEADS):
        k = qk[:, ATTN_W + g * HEAD_DIM:ATTN_W + (g + 1) * HEAD_DIM]
        k_ref[:, g * HEAD_DIM:(g + 1) * HEAD_DIM] = rope(k)
    v_ref[...] = proj(ATTN_W + KV_W, KV_W)
    base = ATTN_W + 2 * KV_W
    xr_ref[...] = proj(base, D_RNN).astype(BF16)
    gr_ref[...] = proj(base + D_RNN, D_RNN).astype(BF16)
    ga_ref[...] = proj(base + 2 * D_RNN, D_MODEL).astype(BF16)
    gb_ref[...] = proj(base + 2 * D_RNN + D_MODEL, D_MODEL).astype(BF16)


def _inproj(stream, x, norm_g, mods, rope_cos, rope_sin, w_in, layer):
    row = lambda i: layer * COND_ROWS + stream.mod_row(i, MM_TILE)
    tok = lambda i: (i, 0)
    rope_blk = lambda i: (i % (stream.seq // MM_TILE), 0)
    wide = pl.BlockSpec((MM_TILE, D_MODEL), tok)
    kv = pl.BlockSpec((MM_TILE, KV_W), tok)
    rope_specs = [pl.BlockSpec((MM_TILE, HEAD_DIM), rope_blk)] * 2 if stream.rotary else []
    rope_args = (rope_cos, rope_sin) if stream.rotary else ()
    return pl.pallas_call(
        functools.partial(_inproj_kernel, stream.rotary),
        grid=(stream.rows // MM_TILE,),
        in_specs=[
            wide,
            pl.BlockSpec((None, 1, D_MODEL), lambda i: (layer, 0, 0)),
            pl.BlockSpec((None, 1, D_MODEL), lambda i: (row(i), 0, 0)),
            pl.BlockSpec((None, 1, D_MODEL), lambda i: (row(i), 0, 1)),
        ] + rope_specs + [
            pl.BlockSpec((None, D_MODEL, IN_COLS), lambda i: (layer, 0, 0)),
        ],
        out_specs=[pl.BlockSpec((MM_TILE // Q_BLOCK, N_HEADS, Q_BLOCK, HEAD_DIM), lambda i: (i, 0, 0, 0)),
                   kv, kv, wide, wide, wide, wide],
        out_shape=[
            jax.ShapeDtypeStruct((stream.rows // Q_BLOCK, N_HEADS, Q_BLOCK, HEAD_DIM), BF16),
            jax.ShapeDtypeStruct((stream.rows, KV_W), F32),
            jax.ShapeDtypeStruct((stream.rows, KV_W), F32),
            jax.ShapeDtypeStruct((stream.rows, D_RNN), BF16),
            jax.ShapeDtypeStruct((stream.rows, D_RNN), BF16),
            jax.ShapeDtypeStruct((stream.rows, D_MODEL), BF16),
            jax.ShapeDtypeStruct((stream.rows, D_MODEL), BF16),
        ],
        compiler_params=_params("arbitrary"),
        name=f"inproj_{stream.name}",
    )(x, norm_g.reshape(DEPTH, 1, D_MODEL), mods, mods, *rope_args, w_in)


def _qk(q, k):
    return lax.dot_general(q, k, (((1,), (1,)), ((), ())), preferred_element_type=F32)


def _store_heads(o_ref, g, o, rows):
    for r in range(GQA_GROUP):
        hd = g * GQA_GROUP + r
        o_ref[:, hd * HEAD_DIM:(hd + 1) * HEAD_DIM] = o[r * rows:(r + 1) * rows].astype(BF16)


def _softmax_rows(s_tiles, sink, p_ref, tail_ref, rows):
    top = s_tiles[0][1]
    for _, t in s_tiles[1:]:
        top = jnp.maximum(top, t)
    m = jnp.maximum(jnp.max(top, axis=-1, keepdims=True), sink)
    for c, t in s_tiles:
        p_ref[rows, c:c + LANES] = jnp.exp2(t - m).astype(BF16)
    tail_ref[rows, :] = jnp.broadcast_to(jnp.exp2(sink - m), top.shape)


def _values_and_ones(v):
    ones = jnp.where(lax.broadcasted_iota(jnp.int32, v.shape, 1) == 0, 1.0, 0.0).astype(v.dtype)
    return jnp.concatenate([v, ones], axis=1)


def _attend(p_ref, tail_ref, v):
    acc = jnp.dot(p_ref[...], _values_and_ones(v), preferred_element_type=F32)
    return acc[:, :HEAD_DIM] / (acc[:, HEAD_DIM:HEAD_DIM + 1] + tail_ref[:, :1])


def _ctx_attn_kernel(sink_ref, q_ref, k_ref, v_ref, o_ref, s_ref, p_ref, tail_ref):
    for g in range(N_KV_HEADS):
        heads = q_ref[:, g * GQA_GROUP:(g + 1) * GQA_GROUP]
        q = jnp.swapaxes(heads, 0, 1).reshape(GQA_GROUP * SEQ, HEAD_DIM)
        k = k_ref[:, g * HEAD_DIM:(g + 1) * HEAD_DIM].astype(BF16)
        v = v_ref[:, g * HEAD_DIM:(g + 1) * HEAD_DIM].astype(BF16)
        s_ref[...] = _qk(q, k)
        for r in range(GQA_GROUP):
            rows = slice(r * SEQ, (r + 1) * SEQ)
            tiles = [(c, s_ref[rows, c:c + LANES]) for c in range(0, SEQ, LANES)]
            _softmax_rows(tiles, sink_ref[g * GQA_GROUP + r] * LOG2_E, p_ref, tail_ref, rows)
        _store_heads(o_ref, g, _attend(p_ref, tail_ref, v), SEQ)


def _ctx_attn(q, k, v, sink):
    blk = lambda b: (b, 0)
    rows = GQA_GROUP * SEQ
    return pl.pallas_call(
        _ctx_attn_kernel,
        grid=(BATCH,),
        in_specs=[
            pl.BlockSpec(memory_space=pltpu.SMEM),
            pl.BlockSpec((SEQ // Q_BLOCK, N_HEADS, Q_BLOCK, HEAD_DIM), lambda b: (b, 0, 0, 0)),
            pl.BlockSpec((SEQ, KV_W), blk),
            pl.BlockSpec((SEQ, KV_W), blk),
        ],
        out_specs=pl.BlockSpec((SEQ, ATTN_W), lambda b: (b, 0)),
        out_shape=jax.ShapeDtypeStruct((T_CTX, ATTN_W), BF16),
        scratch_shapes=[pltpu.VMEM((rows, SEQ), F32), pltpu.VMEM((rows, SEQ), BF16),
                        pltpu.VMEM((rows, LANES), F32)],
        compiler_params=_params("arbitrary"),
        name="ctx_attn",
    )(sink, q, k, v)


def _lat_attn_kernel(sink_ref, q_ref, kp_ref, kc_ref, kn_ref, vp_ref, vc_ref, vn_ref,
                     ck_ref, cv_ref, o_ref, band_ref, mask_ref, s_ref, p_ref, tail_ref):
    j = pl.program_id(1)
    rows = GQA_GROUP * Q_BLOCK
    band = Q_BLOCK + 2 * WINDOW

    @pl.when(jnp.logical_and(pl.program_id(0) == 0, j == 0))
    def _():
        ahead = (lax.broadcasted_iota(jnp.int32, (rows, band), 1) - WINDOW
                 - lax.broadcasted_iota(jnp.int32, (rows, band), 0) % Q_BLOCK)
        band_ref[...] = jnp.where(jnp.abs(ahead) <= WINDOW, 0.0, -jnp.inf)

    kpos = j * Q_BLOCK - WINDOW + lax.broadcasted_iota(jnp.int32, (1, band), 1)
    mask_ref[...] = band_ref[...] + jnp.where((kpos >= 0) & (kpos < DEC_SEQ), 0.0, -jnp.inf)
    for g in range(N_KV_HEADS):
        cols = slice(g * HEAD_DIM, (g + 1) * HEAD_DIM)
        q = q_ref[g * GQA_GROUP:(g + 1) * GQA_GROUP].reshape(rows, HEAD_DIM)
        keys = jnp.concatenate([kp_ref[:, cols], kc_ref[:, cols], kn_ref[:, cols], ck_ref[:, cols]],
                               axis=0).astype(BF16)
        vals = jnp.concatenate([vp_ref[:, cols], vc_ref[:, cols], vn_ref[:, cols], cv_ref[:, cols]],
                               axis=0).astype(BF16)
        s_ref[...] = _qk(q, keys)
        for r in range(GQA_GROUP):
            rows_r = slice(r * Q_BLOCK, (r + 1) * Q_BLOCK)
            tiles = [(c, s_ref[rows_r, c:c + LANES] + mask_ref[rows_r, c:c + LANES]) for c in range(0, band, LANES)]
            tiles += [(c, s_ref[rows_r, c:c + LANES]) for c in range(band, band + PAST_LEN, LANES)]
            _softmax_rows(tiles, sink_ref[g * GQA_GROUP + r] * LOG2_E, p_ref, tail_ref, rows_r)
        _store_heads(o_ref, g, _attend(p_ref, tail_ref, vals), Q_BLOCK)


def _lat_attn(q, k, v, cache_k, cache_v, sink):
    nb = DEC_SEQ // Q_BLOCK
    cur = lambda b, j: (b * nb + j, 0)
    prev = lambda b, j: (b * nb + jnp.maximum(j - 1, 0), 0)
    nxt = lambda b, j: (b * nb + jnp.minimum(j + 1, nb - 1), 0)
    kvb = lambda im: pl.BlockSpec((Q_BLOCK, KV_W), im)
    cache = pl.BlockSpec((None, PAST_LEN, KV_W), lambda b, j: (b, 0, 0))
    return pl.pallas_call(
        _lat_attn_kernel,
        grid=(DEC_BATCH, nb),
        in_specs=[
            pl.BlockSpec(memory_space=pltpu.SMEM),
            pl.BlockSpec((None, N_HEADS, Q_BLOCK, HEAD_DIM), lambda b, j: (b * nb + j, 0, 0, 0)),
            kvb(prev), kvb(cur), kvb(nxt), kvb(prev), kvb(cur), kvb(nxt),
            cache, cache,
        ],
        out_specs=pl.BlockSpec((Q_BLOCK, ATTN_W), cur),
        out_shape=jax.ShapeDtypeStruct((T_LAT, ATTN_W), BF16),
        scratch_shapes=[
            pltpu.VMEM((GQA_GROUP * Q_BLOCK, Q_BLOCK + 2 * WINDOW), F32),
            pltpu.VMEM((GQA_GROUP * Q_BLOCK, Q_BLOCK + 2 * WINDOW), F32),
            pltpu.VMEM((GQA_GROUP * Q_BLOCK, Q_BLOCK + 2 * WINDOW + PAST_LEN), F32),
            pltpu.VMEM((GQA_GROUP * Q_BLOCK, Q_BLOCK + 2 * WINDOW + PAST_LEN), BF16),
            pltpu.VMEM((GQA_GROUP * Q_BLOCK, LANES), F32),
        ],
        compiler_params=_params("arbitrary", "arbitrary"),
        name="lat_attn",
    )(sink, q, k, k, k, v, v, v, cache_k, cache_v)


def _rnn_kernel(seq, xr_ref, cw_ref, cb_ref, w_ref, b_ref, lam_ref, h0_ref,
                y_ref, hl_ref, xt_ref, yt_ref, a_ref, u_ref):
    n_chunks = seq // RNN_CHUNK
    rows = RNN_CHUNK * SUBLANES
    halo = jnp.zeros((CONV_LEFT, SUBLANES, RNN_CB), F32)
    xt_ref[0:CONV_LEFT] = halo
    xt_ref[seq + CONV_LEFT:seq + 2 * CONV_LEFT] = halo

    def load_chunk(c, carry):
        t0 = pl.multiple_of(c * RNN_CHUNK, RNN_CHUNK)
        x = xr_ref[:, pl.ds(t0, RNN_CHUNK), :].astype(F32)
        xt_ref[pl.ds(t0 + CONV_LEFT, RNN_CHUNK)] = jnp.swapaxes(x, 0, 1)
        return carry

    lax.fori_loop(0, n_chunks, load_chunk, 0)

    half_w = 0.5 * cw_ref[...]
    half_b = 0.5 * cb_ref[...]

    def half_conv(t0):
        acc = half_b.reshape(1, 1, RNN_CB)
        for tap in range(CONV_W):
            acc = acc + xt_ref[pl.ds(t0 + tap, RNN_CHUNK)] * half_w[tap:tap + 1, :].reshape(1, 1, RNN_CB)
        return acc

    for d in range(2):
        decay = (-0.5 * LRU_C * LOG2_E) * jax.nn.softplus(-lam_ref[d:d + 1, :])
        half_bias = b_ref[d:d + 1, :]

        def chunk(ci, h, d=d, decay=decay, half_bias=half_bias):
            c = ci if d == 0 else n_chunks - 1 - ci
            t0 = pl.multiple_of(c * RNN_CHUNK, RNN_CHUNK)
            hx = half_conv(t0).reshape(rows, RNN_CB)
            z = jnp.dot(hx.astype(BF16), w_ref[d], preferred_element_type=F32) + half_bias
            a = jnp.exp2(decay * jnp.tanh(z[:, :RNN_CB]) + decay)
            m = 1.0 - a * a
            mult = jnp.where(m == 0.0, 0.0, m * lax.rsqrt(m))
            u = mult * ((jnp.tanh(z[:, RNN_CB:]) + 1.0) * hx)
            a_ref[...] = a.reshape(RNN_CHUNK, SUBLANES, RNN_CB)
            u_ref[...] = u.reshape(RNN_CHUNK, SUBLANES, RNN_CB)

            def steps(gi, h):
                s0 = pl.multiple_of((gi if d == 0 else RNN_CHUNK // RNN_UNROLL - 1 - gi) * RNN_UNROLL, RNN_UNROLL)
                for j in (range(RNN_UNROLL) if d == 0 else reversed(range(RNN_UNROLL))):
                    h = a_ref[s0 + j] * h + u_ref[s0 + j]
                    if d == 0:
                        yt_ref[t0 + s0 + j] = h
                    else:
                        yt_ref[t0 + s0 + j] = yt_ref[t0 + s0 + j] + h
                return h

            return lax.fori_loop(0, RNN_CHUNK // RNN_UNROLL, steps, h)

        hl_ref[d] = lax.fori_loop(0, n_chunks, chunk, h0_ref[d])

    def store_chunk(c, carry):
        t0 = pl.multiple_of(c * RNN_CHUNK, RNN_CHUNK)
        y = jnp.swapaxes(yt_ref[pl.ds(t0, RNN_CHUNK)], 0, 1)
        y_ref[:, pl.ds(t0, RNN_CHUNK), :] = y.astype(BF16)
        return carry

    lax.fori_loop(0, n_chunks, store_chunk, 0)


def _rnn(stream, xr, conv_w, conv_b, w_blk, b_blk, lam, h0, layer):
    n_cb = D_RNN // RNN_CB
    seq = stream.seq
    n_groups = stream.batch // SUBLANES
    return pl.pallas_call(
        functools.partial(_rnn_kernel, seq),
        grid=(n_groups, n_cb),
        in_specs=[
            pl.BlockSpec((SUBLANES, seq, RNN_CB), lambda g, j: (g, 0, j)),
            pl.BlockSpec((None, CONV_W, RNN_CB), lambda g, j: (layer, 0, j)),
            pl.BlockSpec((None, 1, RNN_CB), lambda g, j: (layer, 0, j)),
            pl.BlockSpec((None, None, 2, RNN_CB, 2 * RNN_CB), lambda g, j: (layer, j, 0, 0, 0)),
            pl.BlockSpec((None, None, 2, 2 * RNN_CB), lambda g, j: (layer, j, 0, 0)),
            pl.BlockSpec((None, 2, RNN_CB), lambda g, j: (layer, 0, j)),
            pl.BlockSpec((None, 2, SUBLANES, RNN_CB), lambda g, j: (g, 0, 0, j)),
        ],
        out_specs=[
            pl.BlockSpec((SUBLANES, seq, RNN_CB), lambda g, j: (g, 0, j)),
            pl.BlockSpec((None, 2, SUBLANES, RNN_CB), lambda g, j: (g, 0, 0, j)),
        ],
        out_shape=[
            jax.ShapeDtypeStruct((n_groups * SUBLANES, seq, D_RNN), BF16),
            jax.ShapeDtypeStruct((n_groups, 2, SUBLANES, D_RNN), F32),
        ],
        scratch_shapes=[
            pltpu.VMEM((seq + 2 * CONV_LEFT, SUBLANES, RNN_CB), F32),
            pltpu.VMEM((seq, SUBLANES, RNN_CB), F32),
            pltpu.VMEM((RNN_CHUNK, SUBLANES, RNN_CB), F32),
            pltpu.VMEM((RNN_CHUNK, SUBLANES, RNN_CB), F32),
        ],
        compiler_params=_params("arbitrary", "arbitrary"),
        name=f"rnn_{stream.name}",
    )(xr.reshape(stream.batch, seq, D_RNN), conv_w, conv_b.reshape(DEPTH, 1, D_RNN), w_blk, b_blk, lam, h0)


def _lru_block_weights(lru_wa, lru_wx, lru_ba, lru_bx):
    n_cb = D_RNN // RNN_CB
    per = RNN_CB // LRU_BW

    def dense(w):
        w = w.reshape(DEPTH, 2, n_cb, per, LRU_BW, LRU_BW)
        eye = jnp.eye(per, dtype=w.dtype)
        full = jnp.einsum("ldcpkj,pq->ldcpkqj", w, eye)
        return full.reshape(DEPTH, 2, n_cb, RNN_CB, RNN_CB)

    w = jnp.concatenate([dense(lru_wa), dense(lru_wx)], axis=-1)
    w = jnp.transpose(w, (0, 2, 1, 3, 4)).astype(BF16)
    b = jnp.concatenate([lru_ba.reshape(DEPTH, 2, n_cb, RNN_CB),
                         lru_bx.reshape(DEPTH, 2, n_cb, RNN_CB)], axis=-1)
    return w, jnp.transpose(0.5 * b, (0, 2, 1, 3))


def _route(scores, sel):
    grp_score = []
    for g in range(N_GROUPS):
        a, b, c, d = sel[g * EXPERTS_PER_GROUP:(g + 1) * EXPERTS_PER_GROUP]
        hi1, lo1 = jnp.maximum(a, b), jnp.minimum(a, b)
        hi2, lo2 = jnp.maximum(c, d), jnp.minimum(c, d)
        grp_score.append(jnp.maximum(hi1, hi2) + jnp.maximum(jnp.minimum(hi1, hi2), jnp.maximum(lo1, lo2)))
    best = jnp.zeros_like(grp_score[0], dtype=jnp.int32)
    best_val = grp_score[0]
    for g in range(1, N_GROUPS):
        better = grp_score[g] > best_val
        best = jnp.where(better, g, best)
        best_val = jnp.where(better, grp_score[g], best_val)
    chosen = []
    for e in range(N_EXPERTS):
        g = e // EXPERTS_PER_GROUP
        rank = jnp.zeros_like(best)
        for o in range(g * EXPERTS_PER_GROUP, (g + 1) * EXPERTS_PER_GROUP):
            if o == e:
                continue
            ahead = (sel[o] >= sel[e]) if o < e else (sel[o] > sel[e])
            rank = rank + jnp.where(ahead, 1, 0)
        chosen.append(jnp.where(best == g, rank, 2) < 2)
    taken, gate = [], []
    for j in range(EXPERTS_PER_GROUP):
        t = jnp.zeros_like(best)
        s = jnp.zeros_like(scores[0])
        for g in range(N_GROUPS):
            e = g * EXPERTS_PER_GROUP + j
            t = t + jnp.where(chosen[e], 1, 0)
            s = s + jnp.where(chosen[e], scores[e], 0.0)
        taken.append(t > 0)
        gate.append(s)
    total = gate[0] + gate[1] + gate[2] + gate[3]
    pair = jnp.zeros_like(best)
    w_a = jnp.zeros_like(total)
    w_b = jnp.zeros_like(total)
    for order, (a, b) in enumerate(PAIR_SLOTS):
        both = taken[a] & taken[b]
        pair = jnp.where(both, order, pair)
        w_a = jnp.where(both, gate[a], w_a)
        w_b = jnp.where(both, gate[b], w_b)
    return best * PAIRS_PER_GROUP + pair, w_a / total, w_b / total


def _merge_kernel(x_ref, att_ref, rnn_ref, gr_ref, ga_ref, gb_ref,
                  g1_ref, sh2_ref, sc2_ref, n2_ref, wap_ref, wrp_ref, wo_ref, rw_ref, rb_ref,
                  xo_ref, h2_ref, bucket_ref):
    att = jnp.dot(att_ref[...], wap_ref[...], preferred_element_type=F32)
    gated = (jax.nn.gelu(gr_ref[...].astype(F32)) * rnn_ref[...].astype(F32)).astype(BF16)
    rnn = jnp.dot(gated, wrp_ref[...], preferred_element_type=F32)
    merged = _sigmoid(ga_ref[...].astype(F32)) * att + _sigmoid(gb_ref[...].astype(F32)) * rnn
    x = x_ref[...] + g1_ref[...] * jnp.dot(merged.astype(BF16), wo_ref[...], preferred_element_type=F32)
    xo_ref[...] = x
    y = x * lax.rsqrt(jnp.mean(x * x, axis=-1, keepdims=True) + EPS) * n2_ref[...]
    h2 = y * (1.0 + sc2_ref[...]) + sh2_ref[...]
    h2_ref[:, :D_MODEL] = h2
    h_hi = h2.astype(BF16)
    h_lo = (h2 - h_hi.astype(F32)).astype(BF16)
    rw = rw_ref[...]
    w_hi = rw.astype(BF16)
    w_lo = (rw - w_hi.astype(F32)).astype(BF16)
    logits = _qk(w_hi, h_hi) + (_qk(w_hi, h_lo) + _qk(w_lo, h_hi))
    score = _sigmoid(logits)
    sel = score + rb_ref[...]
    bucket, gate_a, gate_b = _route([score[e:e + 1, :] for e in range(N_EXPERTS)],
                                [sel[e:e + 1, :] for e in range(N_EXPERTS)])
    bucket_ref[...] = bucket
    pad = jnp.zeros((LANES - 2, MM_TILE), F32)
    h2_ref[:, D_MODEL:] = jnp.concatenate([gate_a, gate_b, pad], axis=0).T


def _merge(stream, x, att, rnn, gr, ga, gb, mods, norm_g, wap, wrp, wo, router_wt, router_b, layer):
    tok = lambda i: (i, 0)
    wide = pl.BlockSpec((MM_TILE, D_MODEL), tok)
    mod = lambda col: pl.BlockSpec((None, 1, D_MODEL),
                                   lambda i: (layer * COND_ROWS + stream.mod_row(i, MM_TILE), 0, col))
    mat = pl.BlockSpec((None, D_MODEL, D_MODEL), lambda i: (layer, 0, 0))
    return pl.pallas_call(
        _merge_kernel,
        grid=(stream.rows // MM_TILE,),
        in_specs=[
            wide, wide, wide, wide, wide, wide,
            mod(2), mod(3), mod(4),
            pl.BlockSpec((None, 1, D_MODEL), lambda i: (layer, 0, 0)),
            mat, mat, mat,
            pl.BlockSpec((N_EXPERTS, D_MODEL), lambda i: (0, 0)),
            pl.BlockSpec((N_EXPERTS, 1), lambda i: (0, 0)),
        ],
        out_specs=[wide, pl.BlockSpec((MM_TILE, ROW_W), tok),
                   pl.BlockSpec((None, 1, MM_TILE), lambda i: (i, 0, 0))],
        out_shape=[
            jax.ShapeDtypeStruct((stream.rows, D_MODEL), F32),
            jax.ShapeDtypeStruct((stream.rows, ROW_W), F32),
            jax.ShapeDtypeStruct((stream.rows // MM_TILE, 1, MM_TILE), jnp.int32),
        ],
        compiler_params=_params("arbitrary"),
        name=f"merge_{stream.name}",
    )(x, att, rnn.reshape(stream.rows, D_RNN), gr, ga, gb, mods, mods, mods,
      norm_g.reshape(DEPTH, 1, D_MODEL), wap, wrp, wo, router_wt, router_b.reshape(N_EXPERTS, 1))


def _plan_kernel(b_ref, pos_ref, tiles_ref):
    b = b_ref[...]
    r = lax.broadcasted_iota(jnp.int32, (TOK_TILE, TOK_TILE), 0)
    c = lax.broadcasted_iota(jnp.int32, (TOK_TILE, TOK_TILE), 1)
    before_in_tile = jnp.where(r < c, 1.0, 0.0).astype(BF16)
    br = lax.broadcasted_iota(jnp.int32, (N_TOK_TILES, N_TOK_TILES), 0)
    bc = lax.broadcasted_iota(jnp.int32, (N_TOK_TILES, N_TOK_TILES), 1)
    earlier_tiles = jnp.where(bc < br, 1.0, 0.0).astype(BF16)
    lane = lax.broadcasted_iota(jnp.int32, (1, LANES), 1)
    tile_start = (lane * MOE_TM).astype(F32)
    start = jnp.zeros((1, 1), F32)
    pos = jnp.zeros(b.shape, F32)
    tile_bucket = jnp.zeros((1, LANES), F32)
    for k in range(N_BUCKETS):
        mask = jnp.where(b == k, 1.0, 0.0)
        rank = jnp.dot(mask.astype(BF16), before_in_tile, preferred_element_type=F32)
        per_tile = jnp.sum(mask, axis=1, keepdims=True)
        tile_off = jnp.dot(earlier_tiles, jnp.broadcast_to(per_tile, (N_TOK_TILES, LANES)).astype(BF16),
                           preferred_element_type=F32)[:, :1]
        total = jnp.sum(per_tile, axis=0, keepdims=True)
        padded = jnp.floor((total + (MOE_TM - 0.5)) * (1.0 / MOE_TM)) * MOE_TM
        pos = pos + mask * (start + tile_off + rank)
        start = start + padded
        tile_bucket = tile_bucket + jnp.where(tile_start >= start, 1.0, 0.0)
    pos_ref[...] = pos.astype(jnp.int32)
    n_used = jnp.floor((start + 0.5) * (1.0 / MOE_TM))
    tiles_ref[...] = jnp.where(lane == LANES - 1, n_used, tile_bucket).astype(jnp.int32)


def _plan(bucket_lat, bucket_ctx):
    bucket = jnp.concatenate([bucket_lat.reshape(-1, TOK_TILE), bucket_ctx.reshape(-1, TOK_TILE)], axis=0)
    return pl.pallas_call(
        _plan_kernel,
        out_shape=[
            jax.ShapeDtypeStruct((N_TOK_TILES, TOK_TILE), jnp.int32),
            jax.ShapeDtypeStruct((1, LANES), jnp.int32),
        ],
        compiler_params=pltpu.CompilerParams(vmem_limit_bytes=VMEM_LIMIT),
        name="plan",
    )(bucket)


def _row_copy(src, dst, sem):
    return pltpu.make_async_copy(src, dst, sem)


def _dispatch_kernel(pos_ref, h_ref, xs_in_ref, xs_ref, sem):
    del xs_in_ref

    for r in range(DISPATCH_TILE):
        _row_copy(h_ref.at[pl.ds(r, 1), :], xs_ref.at[pl.ds(pos_ref[0, r], 1), :], sem).start(priority=r % 2)

    def wait(r, carry):
        _row_copy(h_ref.at[pl.ds(0, 1), :], xs_ref.at[pl.ds(0, 1), :], sem).wait()
        return carry

    lax.fori_loop(0, DISPATCH_TILE, wait, 0, unroll=8)


def _dispatch(stream, pos, h2, xs):
    first = stream.first_tile * TOK_TILE // DISPATCH_TILE
    return pl.pallas_call(
        _dispatch_kernel,
        grid=(stream.rows // DISPATCH_TILE,),
        in_specs=[
            pl.BlockSpec((None, 1, DISPATCH_TILE), lambda i: (first + i, 0, 0), memory_space=pltpu.SMEM),
            pl.BlockSpec((DISPATCH_TILE, ROW_W), lambda i: (i, 0)),
            pl.BlockSpec(memory_space=pl.ANY),
        ],
        out_specs=pl.BlockSpec(memory_space=pl.ANY),
        out_shape=jax.ShapeDtypeStruct((MOE_ROWS, ROW_W), F32),
        input_output_aliases={2: 0},
        scratch_shapes=[pltpu.SemaphoreType.DMA(())],
        compiler_params=_params("arbitrary"),
        name=f"dispatch_{stream.name}",
    )(pos.reshape(T_ALL // DISPATCH_TILE, 1, DISPATCH_TILE), h2, xs)


def _bucket_expert(k, which):
    p = k % PAIRS_PER_GROUP
    in_group = 0
    for order, slots in enumerate(PAIR_SLOTS):
        in_group = jnp.where(p == order, slots[which], in_group)
    return (k // PAIRS_PER_GROUP) * EXPERTS_PER_GROUP + in_group


def _weight_copies(layer, wgu_ref, wd_ref, wgu_f, wd_f, sem, bucket, which, landing):
    e = _bucket_expert(bucket, which)
    return [pltpu.make_async_copy(wgu_ref.at[layer, e], wgu_f.at[landing, which], sem.at[landing, 2 * which]),
            pltpu.make_async_copy(wd_ref.at[layer, e], wd_f.at[landing, which], sem.at[landing, 2 * which + 1])]


def _experts_kernel(layer, tiles_ref, xs_ref, wgu_ref, wd_ref, y_ref, wgu_f, wd_f, sem, home_ref):
    i = pl.program_id(0)
    n_used = tiles_ref[0, LANES - 1]
    in_use = i < n_used
    cur = tiles_ref[0, i]
    prev = tiles_ref[0, jnp.maximum(i - 1, 0)]

    def changed(bucket, before):
        return [_bucket_expert(bucket, which) != _bucket_expert(before, which) for which in range(2)]

    def copies(bucket, which, landing):
        return _weight_copies(layer, wgu_ref, wd_ref, wgu_f, wd_f, sem, bucket, which, landing)

    @pl.when(jnp.logical_and(in_use, jnp.logical_or(i == 0, cur != prev)))
    def _():
        @pl.when(i == 0)
        def _():
            for which in range(2):
                home_ref[which] = 1
                for c in copies(cur, which, 0):
                    c.start()

        needed = [jnp.logical_or(i == 0, c) for c in changed(cur, prev)]
        for which in range(2):
            @pl.when(needed[which])
            def _(which=which):
                landing = 1 - home_ref[which]
                for c in copies(cur, which, landing):
                    c.wait()
                home_ref[which] = landing

        nxt_i = lax.while_loop(lambda j: jnp.logical_and(j < n_used, tiles_ref[0, jnp.minimum(j, n_used - 1)] == cur),
                               lambda j: j + 1, i + 1)

        @pl.when(nxt_i < n_used)
        def _():
            nxt = tiles_ref[0, nxt_i]
            for which, differs in enumerate(changed(nxt, cur)):
                @pl.when(differs)
                def _(which=which):
                    for c in copies(nxt, which, 1 - home_ref[which]):
                        c.start()

    @pl.when(jnp.logical_not(in_use))
    def _():
        y_ref[...] = jnp.zeros_like(y_ref)

    @pl.when(in_use)
    def _():
        x = xs_ref[:, :D_MODEL].astype(BF16)

        def branch(which, gate):
            home = home_ref[which]
            gu = jnp.dot(x, wgu_f[home, which].astype(BF16), preferred_element_type=F32)
            act = jax.nn.silu(gu[:, :D_EXPERT]) * gu[:, D_EXPERT:] * gate
            return jnp.dot(act.astype(BF16), wd_f[home, which].astype(BF16), preferred_element_type=F32)

        y_ref[...] = (branch(0, xs_ref[:, D_MODEL:D_MODEL + 1])
                      + branch(1, xs_ref[:, D_MODEL + 1:D_MODEL + 2]))


def _experts(tiles, xs, wgu, wd, layer):
    return pl.pallas_call(
        functools.partial(_experts_kernel, layer),
        grid_spec=pltpu.PrefetchScalarGridSpec(
            num_scalar_prefetch=1,
            grid=(N_MOE_TILES,),
            in_specs=[
                pl.BlockSpec((MOE_TM, ROW_W), lambda i, t: (i, 0)),
                pl.BlockSpec(memory_space=pl.ANY),
                pl.BlockSpec(memory_space=pl.ANY),
            ],
            out_specs=pl.BlockSpec((MOE_TM, D_MODEL), lambda i, t: (i, 0)),
            scratch_shapes=[
                pltpu.VMEM((2, 2, D_MODEL, 2 * D_EXPERT), F32), pltpu.VMEM((2, 2, D_EXPERT, D_MODEL), F32),
                pltpu.SemaphoreType.DMA((2, 4)), pltpu.SMEM((2,), jnp.int32),
            ],
        ),
        out_shape=jax.ShapeDtypeStruct((MOE_ROWS, D_MODEL), F32),
        compiler_params=_params("arbitrary"),
        name="experts",
    )(tiles, xs, wgu, wd)


def _combine_kernel(n_tiles, final, pos_ref, next_pos_ref, x_ref, g2_ref, fg_ref, y_ref, o_ref, rows_ref, sem):
    i = pl.program_id(0)

    def gather(p_ref, into):
        for r in range(TOK_TILE):
            _row_copy(y_ref.at[pl.ds(p_ref[0, r], 1), :], rows_ref.at[into, pl.ds(r, 1), :],
                      sem.at[into]).start(priority=r % 2)

    def step(slot):
        if slot == 0:
            @pl.when(i == 0)
            def _():
                gather(pos_ref, 0)

        @pl.when(i + 1 < n_tiles)
        def _():
            gather(next_pos_ref, 1 - slot)

        def wait(r, carry):
            _row_copy(y_ref.at[pl.ds(0, 1), :], rows_ref.at[slot, pl.ds(0, 1), :], sem.at[slot]).wait()
            return carry

        lax.fori_loop(0, TOK_TILE, wait, 0, unroll=8)
        x = x_ref[...] + g2_ref[...] * rows_ref[slot]
        if final:
            x = x * lax.rsqrt(jnp.mean(x * x, axis=-1, keepdims=True) + EPS) * fg_ref[...]
        o_ref[...] = x

    for parity in range(2):
        pl.when(i % 2 == parity)(functools.partial(step, parity))


def _combine(stream, pos, x, mods, y, final_g, layer):
    first, last = stream.first_tile, stream.first_tile + stream.n_tiles - 1
    return pl.pallas_call(
        functools.partial(_combine_kernel, stream.n_tiles, layer == DEPTH - 1),
        grid=(stream.n_tiles,),
        in_specs=[
            pl.BlockSpec((None, 1, TOK_TILE), lambda i: (first + i, 0, 0), memory_space=pltpu.SMEM),
            pl.BlockSpec((None, 1, TOK_TILE), lambda i: (jnp.minimum(first + i + 1, last), 0, 0),
                         memory_space=pltpu.SMEM),
            pl.BlockSpec((TOK_TILE, D_MODEL), lambda i: (i, 0)),
            pl.BlockSpec((None, 1, D_MODEL), lambda i: (layer * COND_ROWS + stream.mod_row(i, TOK_TILE), 0, 5)),
            pl.BlockSpec((1, D_MODEL), lambda i: (0, 0)),
            pl.BlockSpec(memory_space=pl.ANY),
        ],
        out_specs=pl.BlockSpec((TOK_TILE, D_MODEL), lambda i: (i, 0)),
        out_shape=jax.ShapeDtypeStruct((stream.rows, D_MODEL), F32),
        scratch_shapes=[pltpu.VMEM((2, TOK_TILE, D_MODEL), F32), pltpu.SemaphoreType.DMA((2,))],
        compiler_params=_params("arbitrary"),
        name=f"combine_{stream.name}",
    )(pos, pos, x, mods, final_g.reshape(1, D_MODEL), y)


def _rope_tables():
    t = jnp.arange(DEC_SEQ, dtype=jnp.int32)
    row = (t // GRID_W).astype(F32)
    col = (t % GRID_W).astype(F32)
    inv = ROPE_BASE ** (-jnp.arange(ROPE_PAIRS_PER_AXIS, dtype=F32) / ROPE_PAIRS_PER_AXIS)
    ang = jnp.concatenate([row[:, None] * inv, col[:, None] * inv], axis=-1)
    cos, sin = jnp.cos(ang), jnp.sin(ang)
    return jnp.concatenate([cos, cos], axis=-1), jnp.concatenate([-sin, sin], axis=-1)


def kernel(x_prompt, x_sample, cache_k, cache_v, state_h, c, c_ctx, norm1_g, norm2_g, w_ada, b_ada, w_in, conv_w, conv_b, lru_wa, lru_ba, lru_wx, lru_bx, lru_lambda, attn_sink, w_attn_proj, w_rnn_proj, w_out, router_w, router_b, w_gate_up, w_down, final_norm_g):
    xs_tok = {LATENT: x_sample.reshape(T_LAT, D_MODEL), CONTEXT: x_prompt.reshape(T_CTX, D_MODEL)}
    cond = jnp.concatenate([c_ctx[None, :], c, jnp.zeros((COND_ROWS - 1 - DEC_BATCH, D_MODEL), F32)], axis=0)
    mods = _ada(cond, w_ada, b_ada).reshape(DEPTH * COND_ROWS, 1, N_ADA * D_MODEL)
    rope_cos, rope_sin = _rope_tables()

    w_in_b = w_in.astype(BF16)
    wap_b = w_attn_proj.astype(BF16)
    wrp_b = w_rnn_proj.astype(BF16)
    wo_b = w_out.astype(BF16)
    lru_w, lru_b = _lru_block_weights(lru_wa, lru_wx, lru_ba, lru_bx)
    router_wt = router_w.T

    h0 = {LATENT: jnp.transpose(state_h.astype(F32), (1, 2, 0, 3)).reshape(DEPTH, 1, 2, DEC_BATCH, D_RNN),
          CONTEXT: jnp.zeros((DEPTH, BATCH // SUBLANES, 2, SUBLANES, D_RNN), F32)}
    ck = jnp.transpose(cache_k, (1, 0, 2, 3, 4)).reshape(DEPTH, DEC_BATCH, PAST_LEN, KV_W)
    cv = jnp.transpose(cache_v, (1, 0, 2, 3, 4)).reshape(DEPTH, DEC_BATCH, PAST_LEN, KV_W)

    xs = jnp.zeros((MOE_ROWS, ROW_W), F32)
    new_k, new_v, new_h = [], [], []
    for l in range(DEPTH):
        merged = {}
        for stream in (LATENT, CONTEXT):
            q, k, v, xr, gr, ga, gb = _inproj(stream, xs_tok[stream], norm1_g, mods, rope_cos, rope_sin,
                                              w_in_b, l)
            if stream.rotary:
                att = _lat_attn(q, k, v, ck[l], cv[l], attn_sink[l])
            else:
                att = _ctx_attn(q, k, v, attn_sink[l])
                new_k.append(k.reshape(BATCH, SEQ, N_KV_HEADS, HEAD_DIM))
                new_v.append(v.reshape(BATCH, SEQ, N_KV_HEADS, HEAD_DIM))
            y, h_last = _rnn(stream, xr, conv_w, conv_b, lru_w, lru_b, lru_lambda, h0[stream][l], l)
            if not stream.rotary:
                new_h.append(jnp.transpose(h_last, (0, 2, 1, 3)).reshape(BATCH, 2, D_RNN))
            merged[stream] = _merge(stream, xs_tok[stream], att, y, gr, ga, gb, mods, norm2_g,
                                    wap_b, wrp_b, wo_b, router_wt, router_b, l)
        pos, tiles = _plan(merged[LATENT][2], merged[CONTEXT][2])
        pos = pos.reshape(N_TOK_TILES, 1, TOK_TILE)
        for stream in (LATENT, CONTEXT):
            xs = _dispatch(stream, pos, merged[stream][1], xs)
        y_sorted = _experts(tiles, xs, w_gate_up, w_down, l)
        for stream in (LATENT, CONTEXT):
            xs_tok[stream] = _combine(stream, pos, merged[stream][0], mods, y_sorted, final_norm_g, l)

    return (xs_tok[CONTEXT].reshape(BATCH, SEQ, D_MODEL), xs_tok[LATENT].reshape(DEC_BATCH, DEC_SEQ, D_MODEL),
            jnp.stack(new_k, axis=1), jnp.stack(new_v, axis=1),
            jnp.stack(new_h, axis=1).astype(x_prompt.dtype))
```

```python
import functools
from typing import NamedTuple

import jax
import jax.numpy as jnp
from jax import lax
from jax.experimental import pallas as pl
from jax.experimental.pallas import tpu as pltpu

F32 = jnp.float32
BF16 = jnp.bfloat16

D_MODEL = 1024
BATCH = 16
SEQ = 256
DEPTH = 2
DEC_BATCH = 8
DEC_SEQ = 1024
PAST_LEN = 512
GRID_W = 64
N_HEADS = 8
N_KV_HEADS = 2
GQA_GROUP = N_HEADS // N_KV_HEADS
HEAD_DIM = 128
ATTN_W = N_HEADS * HEAD_DIM
KV_W = N_KV_HEADS * HEAD_DIM
WINDOW = 128
Q_BLOCK = 128
ROPE_BASE = 10000.0
ROPE_PAIRS_PER_AXIS = HEAD_DIM // 4
D_RNN = 1024
LRU_BLOCKS = 16
LRU_BW = D_RNN // LRU_BLOCKS
LRU_C = 8.0
CONV_W = 4
CONV_LEFT = 2
N_EXPERTS = 16
N_GROUPS = 4
EXPERTS_PER_GROUP = N_EXPERTS // N_GROUPS
D_EXPERT = 512
N_ADA = 6
EPS = 1e-6
IN_COLS = ATTN_W + 2 * KV_W + 2 * D_RNN + 2 * D_MODEL

T_LAT = DEC_BATCH * DEC_SEQ
T_CTX = BATCH * SEQ
T_ALL = T_LAT + T_CTX

SUBLANES = 8
LANES = 128
VMEM_LIMIT = 56 * 1024 * 1024

TOK_TILE = 512
DISPATCH_TILE = 2048
MM_TILE = 512
N_TOK_TILES = T_ALL // TOK_TILE
COND_ROWS = 16


class _Stream(NamedTuple):
    name: str
    batch: int
    seq: int
    first_tile: int
    rotary: bool

    @property
    def rows(self):
        return self.batch * self.seq

    @property
    def n_tiles(self):
        return self.rows // TOK_TILE

    def mod_row(self, i, tile):
        return 1 + i // (self.seq // tile) if self.rotary else 0


LATENT = _Stream("lat", DEC_BATCH, DEC_SEQ, 0, True)
CONTEXT = _Stream("ctx", BATCH, SEQ, T_LAT // TOK_TILE, False)

RNN_CB = 256
RNN_CHUNK = 256
RNN_UNROLL = 8
LOG2_E = 1.4426950408889634
GELU_C0 = 0.7978845608028654
GELU_C1 = GELU_C0 * 0.044715
PAIR_SLOTS = ((0, 1), (0, 2), (0, 3), (1, 3), (1, 2), (3, 2))
PAIRS_PER_GROUP = len(PAIR_SLOTS)
N_BUCKETS = N_GROUPS * PAIRS_PER_GROUP
ROW_W = D_MODEL + LANES
MOE_TM = 384
N_MOE_TILES = (T_ALL + N_BUCKETS * (MOE_TM - 1) + MOE_TM - 1) // MOE_TM
MOE_ROWS = N_MOE_TILES * MOE_TM


def _params(*sem):
    return pltpu.CompilerParams(dimension_semantics=sem, vmem_limit_bytes=VMEM_LIMIT)


def _sigmoid(x):
    return 0.5 * jnp.tanh(0.5 * x) + 0.5


def _ada_kernel(cond_ref, w_ref, b_ref, o_ref):
    s = jax.nn.silu(cond_ref[...]).astype(BF16)
    o_ref[...] = jnp.dot(s, w_ref[...].astype(BF16), preferred_element_type=F32) + b_ref[...]


def _ada(cond, w_ada, b_ada):
    cols = N_ADA * D_MODEL
    tn = 1536
    return pl.pallas_call(
        _ada_kernel,
        grid=(DEPTH, cols // tn),
        in_specs=[
            pl.BlockSpec((COND_ROWS, D_MODEL), lambda l, j: (0, 0)),
            pl.BlockSpec((None, D_MODEL, tn), lambda l, j: (l, 0, j)),
            pl.BlockSpec((None, 1, tn), lambda l, j: (l, 0, j)),
        ],
        out_specs=pl.BlockSpec((None, COND_ROWS, tn), lambda l, j: (l, 0, j)),
        out_shape=jax.ShapeDtypeStruct((DEPTH, COND_ROWS, cols), F32),
        compiler_params=_params("arbitrary", "arbitrary"),
        name="ada",
    )(cond, w_ada, b_ada.reshape(DEPTH, 1, cols))


def _inproj_kernel(rotary, x_ref, g_ref, sh_ref, sc_ref, *refs):
    if rotary:
        cos_ref, sin_ref, w_ref, q_ref, k_ref, v_ref, xr_ref, gr_ref, ga_ref, gb_ref = refs
    else:
        w_ref, q_ref, k_ref, v_ref, xr_ref, gr_ref, ga_ref, gb_ref = refs
    x = x_ref[...]
    y = x * lax.rsqrt(jnp.mean(x * x, axis=-1, keepdims=True) + EPS) * g_ref[...]
    h = (y * (1.0 + sc_ref[...]) + sh_ref[...]).astype(BF16)

    def proj(lo, width):
        return jnp.dot(h, w_ref[:, lo:lo + width], preferred_element_type=F32)

    def rope(t):
        if not rotary:
            return t
        return t * cos_ref[...] + pltpu.roll(t, HEAD_DIM // 2, 1) * sin_ref[...]

    scale = HEAD_DIM ** -0.5 * LOG2_E
    qk = proj(0, ATTN_W + KV_W)
    for hd in range(N_HEADS):
        q = (rope(qk[:, hd * HEAD_DIM:(hd + 1) * HEAD_DIM]) * scale).astype(BF16)
        for blk in range(MM_TILE // Q_BLOCK):
            q_ref[blk, hd] = q[blk * Q_BLOCK:(blk + 1) * Q_BLOCK]
    for g in range(N_KV_HEADS):
        k = qk[:, ATTN_W + g * HEAD_DIM:ATTN_W + (g + 1) * HEAD_DIM]
        k_ref[:, g * HEAD_DIM:(g + 1) * HEAD_DIM] = rope(k)
    v_ref[...] = proj(ATTN_W + KV_W, KV_W)
    base = ATTN_W + 2 * KV_W
    xr_ref[...] = proj(base, D_RNN).astype(BF16)
    gr_ref[...] = proj(base + D_RNN, D_RNN).astype(BF16)
    ga_ref[...] = proj(base + 2 * D_RNN, D_MODEL).astype(BF16)
    gb_ref[...] = proj(base + 2 * D_RNN + D_MODEL, D_MODEL).astype(BF16)


def _inproj(stream, x, norm_g, mods, rope_cos, rope_sin, w_in, layer):
    row = lambda i: layer * COND_ROWS + stream.mod_row(i, MM_TILE)
    tok = lambda i: (i, 0)
    rope_blk = lambda i: (i % (stream.seq // MM_TILE), 0)
    wide = pl.BlockSpec((MM_TILE, D_MODEL), tok)
    kv = pl.BlockSpec((MM_TILE, KV_W), tok)
    rope_specs = [pl.BlockSpec((MM_TILE, HEAD_DIM), rope_blk)] * 2 if stream.rotary else []
    rope_args = (rope_cos, rope_sin) if stream.rotary else ()
    return pl.pallas_call(
        functools.partial(_inproj_kernel, stream.rotary),
        grid=(stream.rows // MM_TILE,),
        in_specs=[
            wide,
            pl.BlockSpec((None, 1, D_MODEL), lambda i: (layer, 0, 0)),
            pl.BlockSpec((None, 1, D_MODEL), lambda i: (row(i), 0, 0)),
            pl.BlockSpec((None, 1, D_MODEL), lambda i: (row(i), 0, 1)),
        ] + rope_specs + [
            pl.BlockSpec((None, D_MODEL, IN_COLS), lambda i: (layer, 0, 0)),
        ],
        out_specs=[pl.BlockSpec((MM_TILE // Q_BLOCK, N_HEADS, Q_BLOCK, HEAD_DIM), lambda i: (i, 0, 0, 0)),
                   kv, kv, wide, wide, wide, wide],
        out_shape=[
            jax.ShapeDtypeStruct((stream.rows // Q_BLOCK, N_HEADS, Q_BLOCK, HEAD_DIM), BF16),
            jax.ShapeDtypeStruct((stream.rows, KV_W), F32),
            jax.ShapeDtypeStruct((stream.rows, KV_W), F32),
            jax.ShapeDtypeStruct((stream.rows, D_RNN), BF16),
            jax.ShapeDtypeStruct((stream.rows, D_RNN), BF16),
            jax.ShapeDtypeStruct((stream.rows, D_MODEL), BF16),
            jax.ShapeDtypeStruct((stream.rows, D_MODEL), BF16),
        ],
        compiler_params=_params("arbitrary"),
        name=f"inproj_{stream.name}",
    )(x, norm_g.reshape(DEPTH, 1, D_MODEL), mods, mods, *rope_args, w_in)


def _qk(q, k):
    return lax.dot_general(q, k, (((1,), (1,)), ((), ())), preferred_element_type=F32)


def _store_heads(o_ref, g, o, rows):
    for r in range(GQA_GROUP):
        hd = g * GQA_GROUP + r
        o_ref[:, hd * HEAD_DIM:(hd + 1) * HEAD_DIM] = o[r * rows:(r + 1) * rows].astype(BF16)


def _softmax_rows(s_tiles, sink, p_ref, tail_ref, rows):
    top = s_tiles[0][1]
    for _, t in s_tiles[1:]:
        top = jnp.maximum(top, t)
    m = jnp.maximum(jnp.max(top, axis=-1, keepdims=True), sink)
    for c, t in s_tiles:
        p_ref[rows, c:c + LANES] = jnp.exp2(t - m).astype(BF16)
    tail_ref[rows, :] = jnp.broadcast_to(jnp.exp2(sink - m), top.shape)


def _values_and_ones(v):
    ones = jnp.where(lax.broadcasted_iota(jnp.int32, v.shape, 1) == 0, 1.0, 0.0).astype(v.dtype)
    return jnp.concatenate([v, ones], axis=1)


def _attend(p_ref, tail_ref, v):
    acc = jnp.dot(p_ref[...], _values_and_ones(v), preferred_element_type=F32)
    return acc[:, :HEAD_DIM] / (acc[:, HEAD_DIM:HEAD_DIM + 1] + tail_ref[:, :1])


def _ctx_attn_kernel(sink_ref, q_ref, k_ref, v_ref, o_ref, s_ref, p_ref, tail_ref):
    for g in range(N_KV_HEADS):
        heads = q_ref[:, g * GQA_GROUP:(g + 1) * GQA_GROUP]
        q = jnp.swapaxes(heads, 0, 1).reshape(GQA_GROUP * SEQ, HEAD_DIM)
        k = k_ref[:, g * HEAD_DIM:(g + 1) * HEAD_DIM].astype(BF16)
        v = v_ref[:, g * HEAD_DIM:(g + 1) * HEAD_DIM].astype(BF16)
        s_ref[...] = _qk(q, k)
        for r in range(GQA_GROUP):
            rows = slice(r * SEQ, (r + 1) * SEQ)
            tiles = [(c, s_ref[rows, c:c + LANES]) for c in range(0, SEQ, LANES)]
            _softmax_rows(tiles, sink_ref[g * GQA_GROUP + r] * LOG2_E, p_ref, tail_ref, rows)
        _store_heads(o_ref, g, _attend(p_ref, tail_ref, v), SEQ)


def _ctx_attn(q, k, v, sink):
    blk = lambda b: (b, 0)
    rows = GQA_GROUP * SEQ
    return pl.pallas_call(
        _ctx_attn_kernel,
        grid=(BATCH,),
        in_specs=[
            pl.BlockSpec(memory_space=pltpu.SMEM),
            pl.BlockSpec((SEQ // Q_BLOCK, N_HEADS, Q_BLOCK, HEAD_DIM), lambda b: (b, 0, 0, 0)),
            pl.BlockSpec((SEQ, KV_W), blk),
            pl.BlockSpec((SEQ, KV_W), blk),
        ],
        out_specs=pl.BlockSpec((SEQ, ATTN_W), lambda b: (b, 0)),
        out_shape=jax.ShapeDtypeStruct((T_CTX, ATTN_W), BF16),
        scratch_shapes=[pltpu.VMEM((rows, SEQ), F32), pltpu.VMEM((rows, SEQ), BF16),
                        pltpu.VMEM((rows, LANES), F32)],
        compiler_params=_params("arbitrary"),
        name="ctx_attn",
    )(sink, q, k, v)


def _lat_attn_kernel(sink_ref, q_ref, kp_ref, kc_ref, kn_ref, vp_ref, vc_ref, vn_ref,
                     ck_ref, cv_ref, o_ref, band_ref, mask_ref, s_ref, p_ref, tail_ref):
    j = pl.program_id(1)
    rows = GQA_GROUP * Q_BLOCK
    band = Q_BLOCK + 2 * WINDOW

    @pl.when(jnp.logical_and(pl.program_id(0) == 0, j == 0))
    def _():
        ahead = (lax.broadcasted_iota(jnp.int32, (rows, band), 1) - WINDOW
                 - lax.broadcasted_iota(jnp.int32, (rows, band), 0) % Q_BLOCK)
        band_ref[...] = jnp.where(jnp.abs(ahead) <= WINDOW, 0.0, -jnp.inf)

    kpos = j * Q_BLOCK - WINDOW + lax.broadcasted_iota(jnp.int32, (1, band), 1)
    mask_ref[...] = band_ref[...] + jnp.where((kpos >= 0) & (kpos < DEC_SEQ), 0.0, -jnp.inf)
    for g in range(N_KV_HEADS):
        cols = slice(g * HEAD_DIM, (g + 1) * HEAD_DIM)
        q = q_ref[g * GQA_GROUP:(g + 1) * GQA_GROUP].reshape(rows, HEAD_DIM)
        keys = jnp.concatenate([kp_ref[:, cols], kc_ref[:, cols], kn_ref[:, cols], ck_ref[:, cols]],
                               axis=0).astype(BF16)
        vals = jnp.concatenate([vp_ref[:, cols], vc_ref[:, cols], vn_ref[:, cols], cv_ref[:, cols]],
                               axis=0).astype(BF16)
        s_ref[...] = _qk(q, keys)
        for r in range(GQA_GROUP):
            rows_r = slice(r * Q_BLOCK, (r + 1) * Q_BLOCK)
            tiles = [(c, s_ref[rows_r, c:c + LANES] + mask_ref[rows_r, c:c + LANES]) for c in range(0, band, LANES)]
            tiles += [(c, s_ref[rows_r, c:c + LANES]) for c in range(band, band + PAST_LEN, LANES)]
            _softmax_rows(tiles, sink_ref[g * GQA_GROUP + r] * LOG2_E, p_ref, tail_ref, rows_r)
        _store_heads(o_ref, g, _attend(p_ref, tail_ref, vals), Q_BLOCK)


def _lat_attn(q, k, v, cache_k, cache_v, sink):
    nb = DEC_SEQ // Q_BLOCK
    cur = lambda b, j: (b * nb + j, 0)
    prev = lambda b, j: (b * nb + jnp.maximum(j - 1, 0), 0)
    nxt = lambda b, j: (b * nb + jnp.minimum(j + 1, nb - 1), 0)
    kvb = lambda im: pl.BlockSpec((Q_BLOCK, KV_W), im)
    cache = pl.BlockSpec((None, PAST_LEN, KV_W), lambda b, j: (b, 0, 0))
    return pl.pallas_call(
        _lat_attn_kernel,
        grid=(DEC_BATCH, nb),
        in_specs=[
            pl.BlockSpec(memory_space=pltpu.SMEM),
            pl.BlockSpec((None, N_HEADS, Q_BLOCK, HEAD_DIM), lambda b, j: (b * nb + j, 0, 0, 0)),
            kvb(prev), kvb(cur), kvb(nxt), kvb(prev), kvb(cur), kvb(nxt),
            cache, cache,
        ],
        out_specs=pl.BlockSpec((Q_BLOCK, ATTN_W), cur),
        out_shape=jax.ShapeDtypeStruct((T_LAT, ATTN_W), BF16),
        scratch_shapes=[
            pltpu.VMEM((GQA_GROUP * Q_BLOCK, Q_BLOCK + 2 * WINDOW), F32),
            pltpu.VMEM((GQA_GROUP * Q_BLOCK, Q_BLOCK + 2 * WINDOW), F32),
            pltpu.VMEM((GQA_GROUP * Q_BLOCK, Q_BLOCK + 2 * WINDOW + PAST_LEN), F32),
            pltpu.VMEM((GQA_GROUP * Q_BLOCK, Q_BLOCK + 2 * WINDOW + PAST_LEN), BF16),
            pltpu.VMEM((GQA_GROUP * Q_BLOCK, LANES), F32),
        ],
        compiler_params=_params("arbitrary", "arbitrary"),
        name="lat_attn",
    )(sink, q, k, k, k, v, v, v, cache_k, cache_v)


def _rnn_kernel(seq, xr_ref, cw_ref, cb_ref, w_ref, b_ref, lam_ref, h0_ref,
                y_ref, hl_ref, xt_ref, yt_ref, a_ref, u_ref):
    n_chunks = seq // RNN_CHUNK
    rows = RNN_CHUNK * SUBLANES
    halo = jnp.zeros((CONV_LEFT, SUBLANES, RNN_CB), F32)
    xt_ref[0:CONV_LEFT] = halo
    xt_ref[seq + CONV_LEFT:seq + 2 * CONV_LEFT] = halo

    def load_chunk(c, carry):
        t0 = pl.multiple_of(c * RNN_CHUNK, RNN_CHUNK)
        x = xr_ref[:, pl.ds(t0, RNN_CHUNK), :].astype(F32)
        xt_ref[pl.ds(t0 + CONV_LEFT, RNN_CHUNK)] = jnp.swapaxes(x, 0, 1)
        return carry

    lax.fori_loop(0, n_chunks, load_chunk, 0)

    half_w = 0.5 * cw_ref[...]
    half_b = 0.5 * cb_ref[...]

    def half_conv(t0):
        acc = half_b.reshape(1, 1, RNN_CB)
        for tap in range(CONV_W):
            acc = acc + xt_ref[pl.ds(t0 + tap, RNN_CHUNK)] * half_w[tap:tap + 1, :].reshape(1, 1, RNN_CB)
        return acc

    for d in range(2):
        decay = (-0.5 * LRU_C * LOG2_E) * jax.nn.softplus(-lam_ref[d:d + 1, :])
        half_bias = b_ref[d:d + 1, :]

        def chunk(ci, h, d=d, decay=decay, half_bias=half_bias):
            c = ci if d == 0 else n_chunks - 1 - ci
            t0 = pl.multiple_of(c * RNN_CHUNK, RNN_CHUNK)
            hx = half_conv(t0).reshape(rows, RNN_CB)
            z = jnp.dot(hx.astype(BF16), w_ref[d], preferred_element_type=F32) + half_bias
            a = jnp.exp2(decay * jnp.tanh(z[:, :RNN_CB]) + decay)
            m = 1.0 - a * a
            mult = jnp.where(m == 0.0, 0.0, m * lax.rsqrt(m))
            u = mult * ((jnp.tanh(z[:, RNN_CB:]) + 1.0) * hx)
            a_ref[...] = a.reshape(RNN_CHUNK, SUBLANES, RNN_CB)
            u_ref[...] = u.reshape(RNN_CHUNK, SUBLANES, RNN_CB)

            def steps(gi, h):
                s0 = pl.multiple_of((gi if d == 0 else RNN_CHUNK // RNN_UNROLL - 1 - gi) * RNN_UNROLL, RNN_UNROLL)
                for j in (range(RNN_UNROLL) if d == 0 else reversed(range(RNN_UNROLL))):
                    h = a_ref[s0 + j] * h + u_ref[s0 + j]
                    if d == 0:
                        yt_ref[t0 + s0 + j] = h
                    else:
                        yt_ref[t0 + s0 + j] = yt_ref[t0 + s0 + j] + h
                return h

            return lax.fori_loop(0, RNN_CHUNK // RNN_UNROLL, steps, h)

        hl_ref[d] = lax.fori_loop(0, n_chunks, chunk, h0_ref[d])

    def store_chunk(c, carry):
        t0 = pl.multiple_of(c * RNN_CHUNK, RNN_CHUNK)
        y = jnp.swapaxes(yt_ref[pl.ds(t0, RNN_CHUNK)], 0, 1)
        y_ref[:, pl.ds(t0, RNN_CHUNK), :] = y.astype(BF16)
        return carry

    lax.fori_loop(0, n_chunks, store_chunk, 0)


def _rnn(stream, xr, conv_w, conv_b, w_blk, b_blk, lam, h0, layer):
    n_cb = D_RNN // RNN_CB
    seq = stream.seq
    n_groups = stream.batch // SUBLANES
    return pl.pallas_call(
        functools.partial(_rnn_kernel, seq),
        grid=(n_groups, n_cb),
        in_specs=[
            pl.BlockSpec((SUBLANES, seq, RNN_CB), lambda g, j: (g, 0, j)),
            pl.BlockSpec((None, CONV_W, RNN_CB), lambda g, j: (layer, 0, j)),
            pl.BlockSpec((None, 1, RNN_CB), lambda g, j: (layer, 0, j)),
            pl.BlockSpec((None, None, 2, RNN_CB, 2 * RNN_CB), lambda g, j: (layer, j, 0, 0, 0)),
            pl.BlockSpec((None, None, 2, 2 * RNN_CB), lambda g, j: (layer, j, 0, 0)),
            pl.BlockSpec((None, 2, RNN_CB), lambda g, j: (layer, 0, j)),
            pl.BlockSpec((None, 2, SUBLANES, RNN_CB), lambda g, j: (g, 0, 0, j)),
        ],
        out_specs=[
            pl.BlockSpec((SUBLANES, seq, RNN_CB), lambda g, j: (g, 0, j)),
            pl.BlockSpec((None, 2, SUBLANES, RNN_CB), lambda g, j: (g, 0, 0, j)),
        ],
        out_shape=[
            jax.ShapeDtypeStruct((n_groups * SUBLANES, seq, D_RNN), BF16),
            jax.ShapeDtypeStruct((n_groups, 2, SUBLANES, D_RNN), F32),
        ],
        scratch_shapes=[
            pltpu.VMEM((seq + 2 * CONV_LEFT, SUBLANES, RNN_CB), F32),
            pltpu.VMEM((seq, SUBLANES, RNN_CB), F32),
            pltpu.VMEM((RNN_CHUNK, SUBLANES, RNN_CB), F32),
            pltpu.VMEM((RNN_CHUNK, SUBLANES, RNN_CB), F32),
        ],
        compiler_params=_params("arbitrary", "arbitrary"),
        name=f"rnn_{stream.name}",
    )(xr.reshape(stream.batch, seq, D_RNN), conv_w, conv_b.reshape(DEPTH, 1, D_RNN), w_blk, b_blk, lam, h0)


def _lru_block_weights(lru_wa, lru_wx, lru_ba, lru_bx):
    n_cb = D_RNN // RNN_CB
    per = RNN_CB // LRU_BW

    def dense(w):
        w = w.reshape(DEPTH, 2, n_cb, per, LRU_BW, LRU_BW)
        eye = jnp.eye(per, dtype=w.dtype)
        full = jnp.einsum("ldcpkj,pq->ldcpkqj", w, eye)
        return full.reshape(DEPTH, 2, n_cb, RNN_CB, RNN_CB)

    w = jnp.concatenate([dense(lru_wa), dense(lru_wx)], axis=-1)
    w = jnp.transpose(w, (0, 2, 1, 3, 4)).astype(BF16)
    b = jnp.concatenate([lru_ba.reshape(DEPTH, 2, n_cb, RNN_CB),
                         lru_bx.reshape(DEPTH, 2, n_cb, RNN_CB)], axis=-1)
    return w, jnp.transpose(0.5 * b, (0, 2, 1, 3))


def _route(scores, sel):
    grp_score = []
    for g in range(N_GROUPS):
        a, b, c, d = sel[g * EXPERTS_PER_GROUP:(g + 1) * EXPERTS_PER_GROUP]
        hi1, lo1 = jnp.maximum(a, b), jnp.minimum(a, b)
        hi2, lo2 = jnp.maximum(c, d), jnp.minimum(c, d)
        grp_score.append(jnp.maximum(hi1, hi2) + jnp.maximum(jnp.minimum(hi1, hi2), jnp.maximum(lo1, lo2)))
    best = jnp.zeros_like(grp_score[0], dtype=jnp.int32)
    best_val = grp_score[0]
    for g in range(1, N_GROUPS):
        better = grp_score[g] > best_val
        best = jnp.where(better, g, best)
        best_val = jnp.where(better, grp_score[g], best_val)
    chosen = []
    for e in range(N_EXPERTS):
        g = e // EXPERTS_PER_GROUP
        rank = jnp.zeros_like(best)
        for o in range(g * EXPERTS_PER_GROUP, (g + 1) * EXPERTS_PER_GROUP):
            if o == e:
                continue
            ahead = (sel[o] >= sel[e]) if o < e else (sel[o] > sel[e])
            rank = rank + jnp.where(ahead, 1, 0)
        chosen.append(jnp.where(best == g, rank, 2) < 2)
    taken, gate = [], []
    for j in range(EXPERTS_PER_GROUP):
        t = jnp.zeros_like(best)
        s = jnp.zeros_like(scores[0])
        for g in range(N_GROUPS):
            e = g * EXPERTS_PER_GROUP + j
            t = t + jnp.where(chosen[e], 1, 0)
            s = s + jnp.where(chosen[e], scores[e], 0.0)
        taken.append(t > 0)
        gate.append(s)
    total = gate[0] + gate[1] + gate[2] + gate[3]
    pair = jnp.zeros_like(best)
    w_a = jnp.zeros_like(total)
    w_b = jnp.zeros_like(total)
    for order, (a, b) in enumerate(PAIR_SLOTS):
        both = taken[a] & taken[b]
        pair = jnp.where(both, order, pair)
        w_a = jnp.where(both, gate[a], w_a)
        w_b = jnp.where(both, gate[b], w_b)
    return best * PAIRS_PER_GROUP + pair, w_a / total, w_b / total


def _merge_kernel(x_ref, att_ref, rnn_ref, gr_ref, ga_ref, gb_ref,
                  g1_ref, sh2_ref, sc2_ref, n2_ref, wap_ref, wrp_ref, wo_ref, rw_ref, rb_ref,
                  xo_ref, h2_ref, bucket_ref):
    att = jnp.dot(att_ref[...], wap_ref[...], preferred_element_type=F32)
    gr = gr_ref[...].astype(F32)
    twice_gelu = gr * (1.0 + jnp.tanh(gr * (GELU_C0 + GELU_C1 * (gr * gr))))
    rnn = jnp.dot((twice_gelu * rnn_ref[...].astype(F32)).astype(BF16), wrp_ref[...], preferred_element_type=F32)
    merged = ((jnp.tanh(ga_ref[...].astype(F32)) + 1.0) * att
              + (jnp.tanh(gb_ref[...].astype(F32)) + 1.0) * rnn)
    x = x_ref[...] + g1_ref[...] * jnp.dot(merged.astype(BF16), wo_ref[...], preferred_element_type=F32)
    xo_ref[...] = x
    y = x * lax.rsqrt(jnp.mean(x * x, axis=-1, keepdims=True) + EPS) * n2_ref[...]
    h2 = y * (1.0 + sc2_ref[...]) + sh2_ref[...]
    h2_ref[:, :D_MODEL] = h2
    h_hi = h2.astype(BF16)
    h_lo = (h2 - h_hi.astype(F32)).astype(BF16)
    rw = rw_ref[...]
    w_hi = rw.astype(BF16)
    w_lo = (rw - w_hi.astype(F32)).astype(BF16)
    logits = _qk(w_hi, h_hi) + (_qk(w_hi, h_lo) + _qk(w_lo, h_hi))
    score = _sigmoid(logits)
    sel = score + rb_ref[...]
    bucket, gate_a, gate_b = _route([score[e:e + 1, :] for e in range(N_EXPERTS)],
                                [sel[e:e + 1, :] for e in range(N_EXPERTS)])
    bucket_ref[...] = bucket
    pad = jnp.zeros((LANES - 2, MM_TILE), F32)
    h2_ref[:, D_MODEL:] = jnp.concatenate([gate_a, gate_b, pad], axis=0).T


def _merge(stream, x, att, rnn, gr, ga, gb, mods, norm_g, wap, wrp, wo, router_wt, router_b, layer):
    tok = lambda i: (i, 0)
    wide = pl.BlockSpec((MM_TILE, D_MODEL), tok)
    mod = lambda col: pl.BlockSpec((None, 1, D_MODEL),
                                   lambda i: (layer * COND_ROWS + stream.mod_row(i, MM_TILE), 0, col))
    mat = pl.BlockSpec((None, D_MODEL, D_MODEL), lambda i: (layer, 0, 0))
    return pl.pallas_call(
        _merge_kernel,
        grid=(stream.rows // MM_TILE,),
        in_specs=[
            wide, wide, wide, wide, wide, wide,
            mod(2), mod(3), mod(4),
            pl.BlockSpec((None, 1, D_MODEL), lambda i: (layer, 0, 0)),
            mat, mat, mat,
            pl.BlockSpec((N_EXPERTS, D_MODEL), lambda i: (0, 0)),
            pl.BlockSpec((N_EXPERTS, 1), lambda i: (0, 0)),
        ],
        out_specs=[wide, pl.BlockSpec((MM_TILE, ROW_W), tok),
                   pl.BlockSpec((None, 1, MM_TILE), lambda i: (i, 0, 0))],
        out_shape=[
            jax.ShapeDtypeStruct((stream.rows, D_MODEL), F32),
            jax.ShapeDtypeStruct((stream.rows, ROW_W), F32),
            jax.ShapeDtypeStruct((stream.rows // MM_TILE, 1, MM_TILE), jnp.int32),
        ],
        compiler_params=_params("arbitrary"),
        name=f"merge_{stream.name}",
    )(x, att, rnn.reshape(stream.rows, D_RNN), gr, ga, gb, mods, mods, mods,
      norm_g.reshape(DEPTH, 1, D_MODEL), wap, wrp, wo, router_wt, router_b.reshape(N_EXPERTS, 1))


def _plan_kernel(b_ref, pos_ref, tiles_ref):
    b = b_ref[...]
    r = lax.broadcasted_iota(jnp.int32, (TOK_TILE, TOK_TILE), 0)
    c = lax.broadcasted_iota(jnp.int32, (TOK_TILE, TOK_TILE), 1)
    before_in_tile = jnp.where(r < c, 1.0, 0.0).astype(BF16)
    br = lax.broadcasted_iota(jnp.int32, (N_TOK_TILES, N_TOK_TILES), 0)
    bc = lax.broadcasted_iota(jnp.int32, (N_TOK_TILES, N_TOK_TILES), 1)
    earlier_tiles = jnp.where(bc < br, 1.0, 0.0).astype(BF16)
    lane = lax.broadcasted_iota(jnp.int32, (1, LANES), 1)
    tile_start = (lane * MOE_TM).astype(F32)
    start = jnp.zeros((1, 1), F32)
    pos = jnp.zeros(b.shape, F32)
    tile_bucket = jnp.zeros((1, LANES), F32)
    for k in range(N_BUCKETS):
        mask = jnp.where(b == k, 1.0, 0.0)
        rank = jnp.dot(mask.astype(BF16), before_in_tile, preferred_element_type=F32)
        per_tile = jnp.sum(mask, axis=1, keepdims=True)
        tile_off = jnp.dot(earlier_tiles, jnp.broadcast_to(per_tile, (N_TOK_TILES, LANES)).astype(BF16),
                           preferred_element_type=F32)[:, :1]
        total = jnp.sum(per_tile, axis=0, keepdims=True)
        padded = jnp.floor((total + (MOE_TM - 0.5)) * (1.0 / MOE_TM)) * MOE_TM
        pos = pos + mask * (start + tile_off + rank)
        start = start + padded
        tile_bucket = tile_bucket + jnp.where(tile_start >= start, 1.0, 0.0)
    pos_ref[...] = pos.astype(jnp.int32)
    n_used = jnp.floor((start + 0.5) * (1.0 / MOE_TM))
    tiles_ref[...] = jnp.where(lane == LANES - 1, n_used, tile_bucket).astype(jnp.int32)


def _plan(bucket_lat, bucket_ctx):
    bucket = jnp.concatenate([bucket_lat.reshape(-1, TOK_TILE), bucket_ctx.reshape(-1, TOK_TILE)], axis=0)
    return pl.pallas_call(
        _plan_kernel,
        out_shape=[
            jax.ShapeDtypeStruct((N_TOK_TILES, TOK_TILE), jnp.int32),
            jax.ShapeDtypeStruct((1, LANES), jnp.int32),
        ],
        compiler_params=pltpu.CompilerParams(vmem_limit_bytes=VMEM_LIMIT),
        name="plan",
    )(bucket)


def _row_copy(src, dst, sem):
    return pltpu.make_async_copy(src, dst, sem)


def _dispatch_kernel(pos_ref, h_ref, xs_in_ref, xs_ref, sem):
    del xs_in_ref

    for r in range(DISPATCH_TILE):
        _row_copy(h_ref.at[pl.ds(r, 1), :], xs_ref.at[pl.ds(pos_ref[0, r], 1), :], sem).start(priority=r % 2)

    def wait(r, carry):
        _row_copy(h_ref.at[pl.ds(0, 1), :], xs_ref.at[pl.ds(0, 1), :], sem).wait()
        return carry

    lax.fori_loop(0, DISPATCH_TILE, wait, 0, unroll=8)


def _dispatch(stream, pos, h2, xs):
    first = stream.first_tile * TOK_TILE // DISPATCH_TILE
    return pl.pallas_call(
        _dispatch_kernel,
        grid=(stream.rows // DISPATCH_TILE,),
        in_specs=[
            pl.BlockSpec((None, 1, DISPATCH_TILE), lambda i: (first + i, 0, 0), memory_space=pltpu.SMEM),
            pl.BlockSpec((DISPATCH_TILE, ROW_W), lambda i: (i, 0)),
            pl.BlockSpec(memory_space=pl.ANY),
        ],
        out_specs=pl.BlockSpec(memory_space=pl.ANY),
        out_shape=jax.ShapeDtypeStruct((MOE_ROWS, ROW_W), F32),
        input_output_aliases={2: 0},
        scratch_shapes=[pltpu.SemaphoreType.DMA(())],
        compiler_params=_params("arbitrary"),
        name=f"dispatch_{stream.name}",
    )(pos.reshape(T_ALL // DISPATCH_TILE, 1, DISPATCH_TILE), h2, xs)


def _bucket_expert(k, which):
    p = k % PAIRS_PER_GROUP
    in_group = 0
    for order, slots in enumerate(PAIR_SLOTS):
        in_group = jnp.where(p == order, slots[which], in_group)
    return (k // PAIRS_PER_GROUP) * EXPERTS_PER_GROUP + in_group


def _weight_copies(layer, wgu_ref, wd_ref, wgu_f, wd_f, sem, bucket, which, landing):
    e = _bucket_expert(bucket, which)
    return [pltpu.make_async_copy(wgu_ref.at[layer, e], wgu_f.at[landing, which], sem.at[landing, 2 * which]),
            pltpu.make_async_copy(wd_ref.at[layer, e], wd_f.at[landing, which], sem.at[landing, 2 * which + 1])]


def _experts_kernel(layer, tiles_ref, xs_ref, wgu_ref, wd_ref, y_ref, wgu_f, wd_f, sem, home_ref):
    i = pl.program_id(0)
    n_used = tiles_ref[0, LANES - 1]
    in_use = i < n_used
    cur = tiles_ref[0, i]
    prev = tiles_ref[0, jnp.maximum(i - 1, 0)]

    def changed(bucket, before):
        return [_bucket_expert(bucket, which) != _bucket_expert(before, which) for which in range(2)]

    def copies(bucket, which, landing):
        return _weight_copies(layer, wgu_ref, wd_ref, wgu_f, wd_f, sem, bucket, which, landing)

    @pl.when(jnp.logical_and(in_use, jnp.logical_or(i == 0, cur != prev)))
    def _():
        @pl.when(i == 0)
        def _():
            for which in range(2):
                home_ref[which] = 1
                for c in copies(cur, which, 0):
                    c.start()

        needed = [jnp.logical_or(i == 0, c) for c in changed(cur, prev)]
        for which in range(2):
            @pl.when(needed[which])
            def _(which=which):
                landing = 1 - home_ref[which]
                for c in copies(cur, which, landing):
                    c.wait()
                home_ref[which] = landing

        nxt_i = lax.while_loop(lambda j: jnp.logical_and(j < n_used, tiles_ref[0, jnp.minimum(j, n_used - 1)] == cur),
                               lambda j: j + 1, i + 1)

        @pl.when(nxt_i < n_used)
        def _():
            nxt = tiles_ref[0, nxt_i]
            for which, differs in enumerate(changed(nxt, cur)):
                @pl.when(differs)
                def _(which=which):
                    for c in copies(nxt, which, 1 - home_ref[which]):
                        c.start()

    @pl.when(jnp.logical_not(in_use))
    def _():
        y_ref[...] = jnp.zeros_like(y_ref)

    @pl.when(in_use)
    def _():
        x = xs_ref[:, :D_MODEL].astype(BF16)

        def branch(which, gate):
            home = home_ref[which]
            gu = jnp.dot(x, wgu_f[home, which].astype(BF16), preferred_element_type=F32)
            act = jax.nn.silu(gu[:, :D_EXPERT]) * gu[:, D_EXPERT:] * gate
            return jnp.dot(act.astype(BF16), wd_f[home, which].astype(BF16), preferred_element_type=F32)

        y_ref[...] = (branch(0, xs_ref[:, D_MODEL:D_MODEL + 1])
                      + branch(1, xs_ref[:, D_MODEL + 1:D_MODEL + 2]))


def _experts(tiles, xs, wgu, wd, layer):
    return pl.pallas_call(
        functools.partial(_experts_kernel, layer),
        grid_spec=pltpu.PrefetchScalarGridSpec(
            num_scalar_prefetch=1,
            grid=(N_MOE_TILES,),
            in_specs=[
                pl.BlockSpec((MOE_TM, ROW_W), lambda i, t: (i, 0)),
                pl.BlockSpec(memory_space=pl.ANY),
                pl.BlockSpec(memory_space=pl.ANY),
            ],
            out_specs=pl.BlockSpec((MOE_TM, D_MODEL), lambda i, t: (i, 0)),
            scratch_shapes=[
                pltpu.VMEM((2, 2, D_MODEL, 2 * D_EXPERT), F32), pltpu.VMEM((2, 2, D_EXPERT, D_MODEL), F32),
                pltpu.SemaphoreType.DMA((2, 4)), pltpu.SMEM((2,), jnp.int32),
            ],
        ),
        out_shape=jax.ShapeDtypeStruct((MOE_ROWS, D_MODEL), F32),
        compiler_params=_params("arbitrary"),
        name="experts",
    )(tiles, xs, wgu, wd)


def _combine_kernel(n_tiles, final, pos_ref, next_pos_ref, x_ref, g2_ref, fg_ref, y_ref, o_ref, rows_ref, sem):
    i = pl.program_id(0)

    def gather(p_ref, into):
        for r in range(TOK_TILE):
            _row_copy(y_ref.at[pl.ds(p_ref[0, r], 1), :], rows_ref.at[into, pl.ds(r, 1), :],
                      sem.at[into]).start(priority=r % 2)

    def step(slot):
        if slot == 0:
            @pl.when(i == 0)
            def _():
                gather(pos_ref, 0)

        @pl.when(i + 1 < n_tiles)
        def _():
            gather(next_pos_ref, 1 - slot)

        def wait(r, carry):
            _row_copy(y_ref.at[pl.ds(0, 1), :], rows_ref.at[slot, pl.ds(0, 1), :], sem.at[slot]).wait()
            return carry

        lax.fori_loop(0, TOK_TILE, wait, 0, unroll=8)
        x = x_ref[...] + g2_ref[...] * rows_ref[slot]
        if final:
            x = x * lax.rsqrt(jnp.mean(x * x, axis=-1, keepdims=True) + EPS) * fg_ref[...]
        o_ref[...] = x

    for parity in range(2):
        pl.when(i % 2 == parity)(functools.partial(step, parity))


def _combine(stream, pos, x, mods, y, final_g, layer):
    first, last = stream.first_tile, stream.first_tile + stream.n_tiles - 1
    return pl.pallas_call(
        functools.partial(_combine_kernel, stream.n_tiles, layer == DEPTH - 1),
        grid=(stream.n_tiles,),
        in_specs=[
            pl.BlockSpec((None, 1, TOK_TILE), lambda i: (first + i, 0, 0), memory_space=pltpu.SMEM),
            pl.BlockSpec((None, 1, TOK_TILE), lambda i: (jnp.minimum(first + i + 1, last), 0, 0),
                         memory_space=pltpu.SMEM),
            pl.BlockSpec((TOK_TILE, D_MODEL), lambda i: (i, 0)),
            pl.BlockSpec((None, 1, D_MODEL), lambda i: (layer * COND_ROWS + stream.mod_row(i, TOK_TILE), 0, 5)),
            pl.BlockSpec((1, D_MODEL), lambda i: (0, 0)),
            pl.BlockSpec(memory_space=pl.ANY),
        ],
        out_specs=pl.BlockSpec((TOK_TILE, D_MODEL), lambda i: (i, 0)),
        out_shape=jax.ShapeDtypeStruct((stream.rows, D_MODEL), F32),
        scratch_shapes=[pltpu.VMEM((2, TOK_TILE, D_MODEL), F32), pltpu.SemaphoreType.DMA((2,))],
        compiler_params=_params("arbitrary"),
        name=f"combine_{stream.name}",
    )(pos, pos, x, mods, final_g.reshape(1, D_MODEL), y)


def _rope_tables():
    t = jnp.arange(DEC_SEQ, dtype=jnp.int32)
    row = (t // GRID_W).astype(F32)
    col = (t % GRID_W).astype(F32)
    inv = ROPE_BASE ** (-jnp.arange(ROPE_PAIRS_PER_AXIS, dtype=F32) / ROPE_PAIRS_PER_AXIS)
    ang = jnp.concatenate([row[:, None] * inv, col[:, None] * inv], axis=-1)
    cos, sin = jnp.cos(ang), jnp.sin(ang)
    return jnp.concatenate([cos, cos], axis=-1), jnp.concatenate([-sin, sin], axis=-1)


def kernel(x_prompt, x_sample, cache_k, cache_v, state_h, c, c_ctx, norm1_g, norm2_g, w_ada, b_ada, w_in, conv_w, conv_b, lru_wa, lru_ba, lru_wx, lru_bx, lru_lambda, attn_sink, w_attn_proj, w_rnn_proj, w_out, router_w, router_b, w_gate_up, w_down, final_norm_g):
    xs_tok = {LATENT: x_sample.reshape(T_LAT, D_MODEL), CONTEXT: x_prompt.reshape(T_CTX, D_MODEL)}
    cond = jnp.concatenate([c_ctx[None, :], c, jnp.zeros((COND_ROWS - 1 - DEC_BATCH, D_MODEL), F32)], axis=0)
    mods = _ada(cond, w_ada, b_ada).reshape(DEPTH * COND_ROWS, 1, N_ADA * D_MODEL)
    rope_cos, rope_sin = _rope_tables()

    gate_cols = jnp.arange(IN_COLS) >= IN_COLS - 2 * D_MODEL
    w_in_b = jnp.where(gate_cols, 0.5 * w_in, w_in).astype(BF16)
    wap_b = w_attn_proj.astype(BF16)
    wrp_b = (0.5 * w_rnn_proj).astype(BF16)
    wo_b = (0.5 * w_out).astype(BF16)
    lru_w, lru_b = _lru_block_weights(lru_wa, lru_wx, lru_ba, lru_bx)
    router_wt = router_w.T

    h0 = {LATENT: jnp.transpose(state_h.astype(F32), (1, 2, 0, 3)).reshape(DEPTH, 1, 2, DEC_BATCH, D_RNN),
          CONTEXT: jnp.zeros((DEPTH, BATCH // SUBLANES, 2, SUBLANES, D_RNN), F32)}
    ck = jnp.transpose(cache_k, (1, 0, 2, 3, 4)).reshape(DEPTH, DEC_BATCH, PAST_LEN, KV_W)
    cv = jnp.transpose(cache_v, (1, 0, 2, 3, 4)).reshape(DEPTH, DEC_BATCH, PAST_LEN, KV_W)

    xs = jnp.zeros((MOE_ROWS, ROW_W), F32)
    new_k, new_v, new_h = [], [], []
    for l in range(DEPTH):
        merged = {}
        for stream in (LATENT, CONTEXT):
            q, k, v, xr, gr, ga, gb = _inproj(stream, xs_tok[stream], norm1_g, mods, rope_cos, rope_sin,
                                              w_in_b, l)
            if stream.rotary:
                att = _lat_attn(q, k, v, ck[l], cv[l], attn_sink[l])
            else:
                att = _ctx_attn(q, k, v, attn_sink[l])
                new_k.append(k.reshape(BATCH, SEQ, N_KV_HEADS, HEAD_DIM))
                new_v.append(v.reshape(BATCH, SEQ, N_KV_HEADS, HEAD_DIM))
            y, h_last = _rnn(stream, xr, conv_w, conv_b, lru_w, lru_b, lru_lambda, h0[stream][l], l)
            if not stream.rotary:
                new_h.append(jnp.transpose(h_last, (0, 2, 1, 3)).reshape(BATCH, 2, D_RNN))
            merged[stream] = _merge(stream, xs_tok[stream], att, y, gr, ga, gb, mods, norm2_g,
                                    wap_b, wrp_b, wo_b, router_wt, router_b, l)
        pos, tiles = _plan(merged[LATENT][2], merged[CONTEXT][2])
        pos = pos.reshape(N_TOK_TILES, 1, TOK_TILE)
        for stream in (LATENT, CONTEXT):
            xs = _dispatch(stream, pos, merged[stream][1], xs)
        y_sorted = _experts(tiles, xs, w_gate_up, w_down, l)
        for stream in (LATENT, CONTEXT):
            xs_tok[stream] = _combine(stream, pos, merged[stream][0], mods, y_sorted, final_norm_g, l)

    return (xs_tok[CONTEXT].reshape(BATCH, SEQ, D_MODEL), xs_tok[LATENT].reshape(DEC_BATCH, DEC_SEQ, D_MODEL),
            jnp.stack(new_k, axis=1), jnp.stack(new_v, axis=1),
            jnp.stack(new_h, axis=1).astype(x_prompt.dtype))
```

```python
import functools
from typing import NamedTuple

import jax
import jax.numpy as jnp
from jax import lax
from jax.experimental import pallas as pl
from jax.experimental.pallas import tpu as pltpu

F32 = jnp.float32
BF16 = jnp.bfloat16

D_MODEL = 1024
BATCH = 16
SEQ = 256
DEPTH = 2
DEC_BATCH = 8
DEC_SEQ = 1024
PAST_LEN = 512
GRID_W = 64
N_HEADS = 8
N_KV_HEADS = 2
GQA_GROUP = N_HEADS // N_KV_HEADS
HEAD_DIM = 128
ATTN_W = N_HEADS * HEAD_DIM
KV_W = N_KV_HEADS * HEAD_DIM
WINDOW = 128
Q_BLOCK = 128
ROPE_BASE = 10000.0
ROPE_PAIRS_PER_AXIS = HEAD_DIM // 4
D_RNN = 1024
LRU_BLOCKS = 16
LRU_BW = D_RNN // LRU_BLOCKS
LRU_C = 8.0
CONV_W = 4
CONV_LEFT = 2
N_EXPERTS = 16
N_GROUPS = 4
EXPERTS_PER_GROUP = N_EXPERTS // N_GROUPS
D_EXPERT = 512
N_ADA = 6
EPS = 1e-6
IN_COLS = ATTN_W + 2 * KV_W + 2 * D_RNN + 2 * D_MODEL

T_LAT = DEC_BATCH * DEC_SEQ
T_CTX = BATCH * SEQ
T_ALL = T_LAT + T_CTX

SUBLANES = 8
LANES = 128
VMEM_LIMIT = 56 * 1024 * 1024

TOK_TILE = 512
DISPATCH_TILE = 1024
MM_TILE = 512
N_TOK_TILES = T_ALL // TOK_TILE
COND_ROWS = 16


class _Stream(NamedTuple):
    name: str
    batch: int
    seq: int
    first_tile: int
    rotary: bool

    @property
    def rows(self):
        return self.batch * self.seq

    @property
    def n_tiles(self):
        return self.rows // TOK_TILE

    def mod_row(self, i, tile):
        return 1 + i // (self.seq // tile) if self.rotary else 0


LATENT = _Stream("lat", DEC_BATCH, DEC_SEQ, 0, True)
CONTEXT = _Stream("ctx", BATCH, SEQ, T_LAT // TOK_TILE, False)

RNN_CB = 256
RNN_CHUNK = 256
RNN_UNROLL = 8
LOG2_E = 1.4426950408889634
GELU_C0 = 0.7978845608028654
GELU_C1 = GELU_C0 * 0.044715
PAIR_SLOTS = ((0, 1), (0, 2), (0, 3), (1, 3), (1, 2), (3, 2))
PAIRS_PER_GROUP = len(PAIR_SLOTS)
N_BUCKETS = N_GROUPS * PAIRS_PER_GROUP
ROW_W = D_MODEL + LANES
MOE_TM = 256
N_MOE_TILES = (T_ALL + N_BUCKETS * (MOE_TM - 1) + MOE_TM - 1) // MOE_TM
MOE_ROWS = N_MOE_TILES * MOE_TM


def _params(*sem):
    return pltpu.CompilerParams(dimension_semantics=sem, vmem_limit_bytes=VMEM_LIMIT)


def _sigmoid(x):
    return 0.5 * jnp.tanh(0.5 * x) + 0.5


def _ada_kernel(cond_ref, w_ref, b_ref, o_ref):
    s = jax.nn.silu(cond_ref[...]).astype(BF16)
    o_ref[...] = jnp.dot(s, w_ref[...].astype(BF16), preferred_element_type=F32) + b_ref[...]


def _ada(cond, w_ada, b_ada):
    cols = N_ADA * D_MODEL
    tn = 1536
    return pl.pallas_call(
        _ada_kernel,
        grid=(DEPTH, cols // tn),
        in_specs=[
            pl.BlockSpec((COND_ROWS, D_MODEL), lambda l, j: (0, 0)),
            pl.BlockSpec((None, D_MODEL, tn), lambda l, j: (l, 0, j)),
            pl.BlockSpec((None, 1, tn), lambda l, j: (l, 0, j)),
        ],
        out_specs=pl.BlockSpec((None, COND_ROWS, tn), lambda l, j: (l, 0, j)),
        out_shape=jax.ShapeDtypeStruct((DEPTH, COND_ROWS, cols), F32),
        compiler_params=_params("arbitrary", "arbitrary"),
        name="ada",
    )(cond, w_ada, b_ada.reshape(DEPTH, 1, cols))


def _inproj_kernel(rotary, x_ref, g_ref, sh_ref, sc_ref, *refs):
    if rotary:
        cos_ref, sin_ref, w_ref, q_ref, k_ref, v_ref, xr_ref, gr_ref, ga_ref, gb_ref = refs
    else:
        w_ref, q_ref, k_ref, v_ref, xr_ref, gr_ref, ga_ref, gb_ref = refs
    x = x_ref[...]
    y = x * lax.rsqrt(jnp.mean(x * x, axis=-1, keepdims=True) + EPS) * g_ref[...]
    h = (y * (1.0 + sc_ref[...]) + sh_ref[...]).astype(BF16)

    def proj(lo, width):
        return jnp.dot(h, w_ref[:, lo:lo + width], preferred_element_type=F32)

    def rope(t):
        if not rotary:
            return t
        return t * cos_ref[...] + pltpu.roll(t, HEAD_DIM // 2, 1) * sin_ref[...]

    scale = HEAD_DIM ** -0.5 * LOG2_E
    qk = proj(0, ATTN_W + KV_W)
    for hd in range(N_HEADS):
        q = (rope(qk[:, hd * HEAD_DIM:(hd + 1) * HEAD_DIM]) * scale).astype(BF16)
        for blk in range(MM_TILE // Q_BLOCK):
            q_ref[blk, hd] = q[blk * Q_BLOCK:(blk + 1) * Q_BLOCK]
    for g in range(N_KV_HEADS):
        k = qk[:, ATTN_W + g * HEAD_DIM:ATTN_W + (g + 1) * HEAD_DIM]
        k_ref[:, g * HEAD_DIM:(g + 1) * HEAD_DIM] = rope(k)
    v_ref[...] = proj(ATTN_W + KV_W, KV_W)
    base = ATTN_W + 2 * KV_W
    xr_ref[...] = proj(base, D_RNN).astype(BF16)
    gr_ref[...] = proj(base + D_RNN, D_RNN).astype(BF16)
    ga_ref[...] = proj(base + 2 * D_RNN, D_MODEL).astype(BF16)
    gb_ref[...] = proj(base + 2 * D_RNN + D_MODEL, D_MODEL).astype(BF16)


def _inproj(stream, x, norm_g, mods, rope_cos, rope_sin, w_in, layer):
    row = lambda i: layer * COND_ROWS + stream.mod_row(i, MM_TILE)
    tok = lambda i: (i, 0)
    rope_blk = lambda i: (i % (stream.seq // MM_TILE), 0)
    wide = pl.BlockSpec((MM_TILE, D_MODEL), tok)
    kv = pl.BlockSpec((MM_TILE, KV_W), tok)
    rope_specs = [pl.BlockSpec((MM_TILE, HEAD_DIM), rope_blk)] * 2 if stream.rotary else []
    rope_args = (rope_cos, rope_sin) if stream.rotary else ()
    return pl.pallas_call(
        functools.partial(_inproj_kernel, stream.rotary),
        grid=(stream.rows // MM_TILE,),
        in_specs=[
            wide,
            pl.BlockSpec((None, 1, D_MODEL), lambda i: (layer, 0, 0)),
            pl.BlockSpec((None, 1, D_MODEL), lambda i: (row(i), 0, 0)),
            pl.BlockSpec((None, 1, D_MODEL), lambda i: (row(i), 0, 1)),
        ] + rope_specs + [
            pl.BlockSpec((None, D_MODEL, IN_COLS), lambda i: (layer, 0, 0)),
        ],
        out_specs=[pl.BlockSpec((MM_TILE // Q_BLOCK, N_HEADS, Q_BLOCK, HEAD_DIM), lambda i: (i, 0, 0, 0)),
                   kv, kv, wide, wide, wide, wide],
        out_shape=[
            jax.ShapeDtypeStruct((stream.rows // Q_BLOCK, N_HEADS, Q_BLOCK, HEAD_DIM), BF16),
            jax.ShapeDtypeStruct((stream.rows, KV_W), F32),
            jax.ShapeDtypeStruct((stream.rows, KV_W), F32),
            jax.ShapeDtypeStruct((stream.rows, D_RNN), BF16),
            jax.ShapeDtypeStruct((stream.rows, D_RNN), BF16),
            jax.ShapeDtypeStruct((stream.rows, D_MODEL), BF16),
            jax.ShapeDtypeStruct((stream.rows, D_MODEL), BF16),
        ],
        compiler_params=_params("arbitrary"),
        name=f"inproj_{stream.name}",
    )(x, norm_g.reshape(DEPTH, 1, D_MODEL), mods, mods, *rope_args, w_in)


def _qk(q, k):
    return lax.dot_general(q, k, (((1,), (1,)), ((), ())), preferred_element_type=F32)


def _store_heads(o_ref, g, o, rows):
    for r in range(GQA_GROUP):
        hd = g * GQA_GROUP + r
        o_ref[:, hd * HEAD_DIM:(hd + 1) * HEAD_DIM] = o[r * rows:(r + 1) * rows].astype(BF16)


def _softmax_rows(s_tiles, sink, p_ref, tail_ref, rows):
    top = s_tiles[0][1]
    for _, t in s_tiles[1:]:
        top = jnp.maximum(top, t)
    m = jnp.maximum(jnp.max(top, axis=-1, keepdims=True), sink)
    for c, t in s_tiles:
        p_ref[rows, c:c + LANES] = jnp.exp2(t - m).astype(BF16)
    tail_ref[rows, :] = jnp.broadcast_to(jnp.exp2(sink - m), top.shape)


def _values_and_ones(v):
    ones = jnp.where(lax.broadcasted_iota(jnp.int32, v.shape, 1) == 0, 1.0, 0.0).astype(v.dtype)
    return jnp.concatenate([v, ones], axis=1)


def _attend(p_ref, tail_ref, v):
    acc = jnp.dot(p_ref[...], _values_and_ones(v), preferred_element_type=F32)
    return acc[:, :HEAD_DIM] / (acc[:, HEAD_DIM:HEAD_DIM + 1] + tail_ref[:, :1])


def _ctx_attn_kernel(sink_ref, q_ref, k_ref, v_ref, o_ref, s_ref, p_ref, tail_ref):
    for g in range(N_KV_HEADS):
        heads = q_ref[:, g * GQA_GROUP:(g + 1) * GQA_GROUP]
        q = jnp.swapaxes(heads, 0, 1).reshape(GQA_GROUP * SEQ, HEAD_DIM)
        k = k_ref[:, g * HEAD_DIM:(g + 1) * HEAD_DIM].astype(BF16)
        v = v_ref[:, g * HEAD_DIM:(g + 1) * HEAD_DIM].astype(BF16)
        s_ref[...] = _qk(q, k)
        for r in range(GQA_GROUP):
            rows = slice(r * SEQ, (r + 1) * SEQ)
            tiles = [(c, s_ref[rows, c:c + LANES]) for c in range(0, SEQ, LANES)]
            _softmax_rows(tiles, sink_ref[g * GQA_GROUP + r] * LOG2_E, p_ref, tail_ref, rows)
        _store_heads(o_ref, g, _attend(p_ref, tail_ref, v), SEQ)


def _ctx_attn(q, k, v, sink):
    blk = lambda b: (b, 0)
    rows = GQA_GROUP * SEQ
    return pl.pallas_call(
        _ctx_attn_kernel,
        grid=(BATCH,),
        in_specs=[
            pl.BlockSpec(memory_space=pltpu.SMEM),
            pl.BlockSpec((SEQ // Q_BLOCK, N_HEADS, Q_BLOCK, HEAD_DIM), lambda b: (b, 0, 0, 0)),
            pl.BlockSpec((SEQ, KV_W), blk),
            pl.BlockSpec((SEQ, KV_W), blk),
        ],
        out_specs=pl.BlockSpec((SEQ, ATTN_W), lambda b: (b, 0)),
        out_shape=jax.ShapeDtypeStruct((T_CTX, ATTN_W), BF16),
        scratch_shapes=[pltpu.VMEM((rows, SEQ), F32), pltpu.VMEM((rows, SEQ), BF16),
                        pltpu.VMEM((rows, LANES), F32)],
        compiler_params=_params("arbitrary"),
        name="ctx_attn",
    )(sink, q, k, v)


def _lat_attn_kernel(sink_ref, q_ref, kp_ref, kc_ref, kn_ref, vp_ref, vc_ref, vn_ref,
                     ck_ref, cv_ref, o_ref, band_ref, mask_ref, s_ref, p_ref, tail_ref):
    j = pl.program_id(1)
    rows = GQA_GROUP * Q_BLOCK
    band = Q_BLOCK + 2 * WINDOW

    @pl.when(jnp.logical_and(pl.program_id(0) == 0, j == 0))
    def _():
        ahead = (lax.broadcasted_iota(jnp.int32, (rows, band), 1) - WINDOW
                 - lax.broadcasted_iota(jnp.int32, (rows, band), 0) % Q_BLOCK)
        band_ref[...] = jnp.where(jnp.abs(ahead) <= WINDOW, 0.0, -jnp.inf)

    kpos = j * Q_BLOCK - WINDOW + lax.broadcasted_iota(jnp.int32, (1, band), 1)
    mask_ref[...] = band_ref[...] + jnp.where((kpos >= 0) & (kpos < DEC_SEQ), 0.0, -jnp.inf)
    for g in range(N_KV_HEADS):
        cols = slice(g * HEAD_DIM, (g + 1) * HEAD_DIM)
        q = q_ref[g * GQA_GROUP:(g + 1) * GQA_GROUP].reshape(rows, HEAD_DIM)
        keys = jnp.concatenate([kp_ref[:, cols], kc_ref[:, cols], kn_ref[:, cols], ck_ref[:, cols]],
                               axis=0).astype(BF16)
        vals = jnp.concatenate([vp_ref[:, cols], vc_ref[:, cols], vn_ref[:, cols], cv_ref[:, cols]],
                               axis=0).astype(BF16)
        s_ref[...] = _qk(q, keys)
        for r in range(GQA_GROUP):
            rows_r = slice(r * Q_BLOCK, (r + 1) * Q_BLOCK)
            tiles = [(c, s_ref[rows_r, c:c + LANES] + mask_ref[rows_r, c:c + LANES]) for c in range(0, band, LANES)]
            tiles += [(c, s_ref[rows_r, c:c + LANES]) for c in range(band, band + PAST_LEN, LANES)]
            _softmax_rows(tiles, sink_ref[g * GQA_GROUP + r] * LOG2_E, p_ref, tail_ref, rows_r)
        _store_heads(o_ref, g, _attend(p_ref, tail_ref, vals), Q_BLOCK)


def _lat_attn(q, k, v, cache_k, cache_v, sink):
    nb = DEC_SEQ // Q_BLOCK
    cur = lambda b, j: (b * nb + j, 0)
    prev = lambda b, j: (b * nb + jnp.maximum(j - 1, 0), 0)
    nxt = lambda b, j: (b * nb + jnp.minimum(j + 1, nb - 1), 0)
    kvb = lambda im: pl.BlockSpec((Q_BLOCK, KV_W), im)
    cache = pl.BlockSpec((None, PAST_LEN, KV_W), lambda b, j: (b, 0, 0))
    return pl.pallas_call(
        _lat_attn_kernel,
        grid=(DEC_BATCH, nb),
        in_specs=[
            pl.BlockSpec(memory_space=pltpu.SMEM),
            pl.BlockSpec((None, N_HEADS, Q_BLOCK, HEAD_DIM), lambda b, j: (b * nb + j, 0, 0, 0)),
            kvb(prev), kvb(cur), kvb(nxt), kvb(prev), kvb(cur), kvb(nxt),
            cache, cache,
        ],
        out_specs=pl.BlockSpec((Q_BLOCK, ATTN_W), cur),
        out_shape=jax.ShapeDtypeStruct((T_LAT, ATTN_W), BF16),
        scratch_shapes=[
            pltpu.VMEM((GQA_GROUP * Q_BLOCK, Q_BLOCK + 2 * WINDOW), F32),
            pltpu.VMEM((GQA_GROUP * Q_BLOCK, Q_BLOCK + 2 * WINDOW), F32),
            pltpu.VMEM((GQA_GROUP * Q_BLOCK, Q_BLOCK + 2 * WINDOW + PAST_LEN), F32),
            pltpu.VMEM((GQA_GROUP * Q_BLOCK, Q_BLOCK + 2 * WINDOW + PAST_LEN), BF16),
            pltpu.VMEM((GQA_GROUP * Q_BLOCK, LANES), F32),
        ],
        compiler_params=_params("arbitrary", "arbitrary"),
        name="lat_attn",
    )(sink, q, k, k, k, v, v, v, cache_k, cache_v)


def _rnn_kernel(seq, xr_ref, cw_ref, cb_ref, w_ref, b_ref, lam_ref, h0_ref,
                y_ref, hl_ref, xt_ref, yt_ref, a_ref, u_ref):
    n_chunks = seq // RNN_CHUNK
    rows = RNN_CHUNK * SUBLANES
    halo = jnp.zeros((CONV_LEFT, SUBLANES, RNN_CB), F32)
    xt_ref[0:CONV_LEFT] = halo
    xt_ref[seq + CONV_LEFT:seq + 2 * CONV_LEFT] = halo

    def load_chunk(c, carry):
        t0 = pl.multiple_of(c * RNN_CHUNK, RNN_CHUNK)
        x = xr_ref[:, pl.ds(t0, RNN_CHUNK), :].astype(F32)
        xt_ref[pl.ds(t0 + CONV_LEFT, RNN_CHUNK)] = jnp.swapaxes(x, 0, 1)
        return carry

    lax.fori_loop(0, n_chunks, load_chunk, 0)

    half_w = 0.5 * cw_ref[...]
    half_b = 0.5 * cb_ref[...]

    def half_conv(t0):
        acc = half_b.reshape(1, 1, RNN_CB)
        for tap in range(CONV_W):
            acc = acc + xt_ref[pl.ds(t0 + tap, RNN_CHUNK)] * half_w[tap:tap + 1, :].reshape(1, 1, RNN_CB)
        return acc

    for d in range(2):
        decay = (-0.5 * LRU_C * LOG2_E) * jax.nn.softplus(-lam_ref[d:d + 1, :])
        half_bias = b_ref[d:d + 1, :]

        def chunk(ci, h, d=d, decay=decay, half_bias=half_bias):
            c = ci if d == 0 else n_chunks - 1 - ci
            t0 = pl.multiple_of(c * RNN_CHUNK, RNN_CHUNK)
            hx = half_conv(t0).reshape(rows, RNN_CB)
            z = jnp.dot(hx.astype(BF16), w_ref[d], preferred_element_type=F32) + half_bias
            a = jnp.exp2(decay * jnp.tanh(z[:, :RNN_CB]) + decay)
            m = 1.0 - a * a
            mult = jnp.where(m == 0.0, 0.0, m * lax.rsqrt(m))
            u = mult * ((jnp.tanh(z[:, RNN_CB:]) + 1.0) * hx)
            a_ref[...] = a.reshape(RNN_CHUNK, SUBLANES, RNN_CB)
            u_ref[...] = u.reshape(RNN_CHUNK, SUBLANES, RNN_CB)

            def steps(gi, h):
                s0 = pl.multiple_of((gi if d == 0 else RNN_CHUNK // RNN_UNROLL - 1 - gi) * RNN_UNROLL, RNN_UNROLL)
                for j in (range(RNN_UNROLL) if d == 0 else reversed(range(RNN_UNROLL))):
                    h = a_ref[s0 + j] * h + u_ref[s0 + j]
                    if d == 0:
                        yt_ref[t0 + s0 + j] = h
                    else:
                        yt_ref[t0 + s0 + j] = yt_ref[t0 + s0 + j] + h
                return h

            return lax.fori_loop(0, RNN_CHUNK // RNN_UNROLL, steps, h)

        hl_ref[d] = lax.fori_loop(0, n_chunks, chunk, h0_ref[d])

    def store_chunk(c, carry):
        t0 = pl.multiple_of(c * RNN_CHUNK, RNN_CHUNK)
        y = jnp.swapaxes(yt_ref[pl.ds(t0, RNN_CHUNK)], 0, 1)
        y_ref[:, pl.ds(t0, RNN_CHUNK), :] = y.astype(BF16)
        return carry

    lax.fori_loop(0, n_chunks, store_chunk, 0)


def _rnn(stream, xr, conv_w, conv_b, w_blk, b_blk, lam, h0, layer):
    n_cb = D_RNN // RNN_CB
    seq = stream.seq
    n_groups = stream.batch // SUBLANES
    return pl.pallas_call(
        functools.partial(_rnn_kernel, seq),
        grid=(n_groups, n_cb),
        in_specs=[
            pl.BlockSpec((SUBLANES, seq, RNN_CB), lambda g, j: (g, 0, j)),
            pl.BlockSpec((None, CONV_W, RNN_CB), lambda g, j: (layer, 0, j)),
            pl.BlockSpec((None, 1, RNN_CB), lambda g, j: (layer, 0, j)),
            pl.BlockSpec((None, None, 2, RNN_CB, 2 * RNN_CB), lambda g, j: (layer, j, 0, 0, 0)),
            pl.BlockSpec((None, None, 2, 2 * RNN_CB), lambda g, j: (layer, j, 0, 0)),
            pl.BlockSpec((None, 2, RNN_CB), lambda g, j: (layer, 0, j)),
            pl.BlockSpec((None, 2, SUBLANES, RNN_CB), lambda g, j: (g, 0, 0, j)),
        ],
        out_specs=[
            pl.BlockSpec((SUBLANES, seq, RNN_CB), lambda g, j: (g, 0, j)),
            pl.BlockSpec((None, 2, SUBLANES, RNN_CB), lambda g, j: (g, 0, 0, j)),
        ],
        out_shape=[
            jax.ShapeDtypeStruct((n_groups * SUBLANES, seq, D_RNN), BF16),
            jax.ShapeDtypeStruct((n_groups, 2, SUBLANES, D_RNN), F32),
        ],
        scratch_shapes=[
            pltpu.VMEM((seq + 2 * CONV_LEFT, SUBLANES, RNN_CB), F32),
            pltpu.VMEM((seq, SUBLANES, RNN_CB), F32),
            pltpu.VMEM((RNN_CHUNK, SUBLANES, RNN_CB), F32),
            pltpu.VMEM((RNN_CHUNK, SUBLANES, RNN_CB), F32),
        ],
        compiler_params=_params("arbitrary", "arbitrary"),
        name=f"rnn_{stream.name}",
    )(xr.reshape(stream.batch, seq, D_RNN), conv_w, conv_b.reshape(DEPTH, 1, D_RNN), w_blk, b_blk, lam, h0)


def _lru_block_weights(lru_wa, lru_wx, lru_ba, lru_bx):
    n_cb = D_RNN // RNN_CB
    per = RNN_CB // LRU_BW

    def dense(w):
        w = w.reshape(DEPTH, 2, n_cb, per, LRU_BW, LRU_BW)
        eye = jnp.eye(per, dtype=w.dtype)
        full = jnp.einsum("ldcpkj,pq->ldcpkqj", w, eye)
        return full.reshape(DEPTH, 2, n_cb, RNN_CB, RNN_CB)

    w = jnp.concatenate([dense(lru_wa), dense(lru_wx)], axis=-1)
    w = jnp.transpose(w, (0, 2, 1, 3, 4)).astype(BF16)
    b = jnp.concatenate([lru_ba.reshape(DEPTH, 2, n_cb, RNN_CB),
                         lru_bx.reshape(DEPTH, 2, n_cb, RNN_CB)], axis=-1)
    return w, jnp.transpose(0.5 * b, (0, 2, 1, 3))


def _route(scores, sel):
    grp_score = []
    for g in range(N_GROUPS):
        a, b, c, d = sel[g * EXPERTS_PER_GROUP:(g + 1) * EXPERTS_PER_GROUP]
        hi1, lo1 = jnp.maximum(a, b), jnp.minimum(a, b)
        hi2, lo2 = jnp.maximum(c, d), jnp.minimum(c, d)
        grp_score.append(jnp.maximum(hi1, hi2) + jnp.maximum(jnp.minimum(hi1, hi2), jnp.maximum(lo1, lo2)))
    best = jnp.zeros_like(grp_score[0], dtype=jnp.int32)
    best_val = grp_score[0]
    for g in range(1, N_GROUPS):
        better = grp_score[g] > best_val
        best = jnp.where(better, g, best)
        best_val = jnp.where(better, grp_score[g], best_val)
    chosen = []
    for e in range(N_EXPERTS):
        g = e // EXPERTS_PER_GROUP
        rank = jnp.zeros_like(best)
        for o in range(g * EXPERTS_PER_GROUP, (g + 1) * EXPERTS_PER_GROUP):
            if o == e:
                continue
            ahead = (sel[o] >= sel[e]) if o < e else (sel[o] > sel[e])
            rank = rank + jnp.where(ahead, 1, 0)
        chosen.append(jnp.where(best == g, rank, 2) < 2)
    taken, gate = [], []
    for j in range(EXPERTS_PER_GROUP):
        t = jnp.zeros_like(best)
        s = jnp.zeros_like(scores[0])
        for g in range(N_GROUPS):
            e = g * EXPERTS_PER_GROUP + j
            t = t + jnp.where(chosen[e], 1, 0)
            s = s + jnp.where(chosen[e], scores[e], 0.0)
        taken.append(t > 0)
        gate.append(s)
    total = gate[0] + gate[1] + gate[2] + gate[3]
    pair = jnp.zeros_like(best)
    w_a = jnp.zeros_like(total)
    w_b = jnp.zeros_like(total)
    for order, (a, b) in enumerate(PAIR_SLOTS):
        both = taken[a] & taken[b]
        pair = jnp.where(both, order, pair)
        w_a = jnp.where(both, gate[a], w_a)
        w_b = jnp.where(both, gate[b], w_b)
    return best * PAIRS_PER_GROUP + pair, w_a / total, w_b / total


def _merge_kernel(x_ref, att_ref, rnn_ref, gr_ref, ga_ref, gb_ref,
                  g1_ref, sh2_ref, sc2_ref, n2_ref, wap_ref, wrp_ref, wo_ref, rw_ref, rb_ref,
                  xo_ref, h2_ref, bucket_ref):
    att = jnp.dot(att_ref[...], wap_ref[...], preferred_element_type=F32)
    gr = gr_ref[...].astype(F32)
    twice_gelu = gr * (1.0 + jnp.tanh(gr * (GELU_C0 + GELU_C1 * (gr * gr))))
    rnn = jnp.dot((twice_gelu * rnn_ref[...].astype(F32)).astype(BF16), wrp_ref[...], preferred_element_type=F32)
    merged = ((jnp.tanh(ga_ref[...].astype(F32)) + 1.0) * att
              + (jnp.tanh(gb_ref[...].astype(F32)) + 1.0) * rnn)
    x = x_ref[...] + g1_ref[...] * jnp.dot(merged.astype(BF16), wo_ref[...], preferred_element_type=F32)
    xo_ref[...] = x
    y = x * lax.rsqrt(jnp.mean(x * x, axis=-1, keepdims=True) + EPS) * n2_ref[...]
    h2 = y * (1.0 + sc2_ref[...]) + sh2_ref[...]
    h2_ref[:, :D_MODEL] = h2
    h_hi = h2.astype(BF16)
    h_lo = (h2 - h_hi.astype(F32)).astype(BF16)
    rw = rw_ref[...]
    w_hi = rw.astype(BF16)
    w_lo = (rw - w_hi.astype(F32)).astype(BF16)
    logits = _qk(w_hi, h_hi) + (_qk(w_hi, h_lo) + _qk(w_lo, h_hi))
    score = _sigmoid(logits)
    sel = score + rb_ref[...]
    bucket, gate_a, gate_b = _route([score[e:e + 1, :] for e in range(N_EXPERTS)],
                                [sel[e:e + 1, :] for e in range(N_EXPERTS)])
    bucket_ref[...] = bucket
    pad = jnp.zeros((LANES - 2, MM_TILE), F32)
    h2_ref[:, D_MODEL:] = jnp.concatenate([gate_a, gate_b, pad], axis=0).T


def _merge(stream, x, att, rnn, gr, ga, gb, mods, norm_g, wap, wrp, wo, router_wt, router_b, layer):
    tok = lambda i: (i, 0)
    wide = pl.BlockSpec((MM_TILE, D_MODEL), tok)
    mod = lambda col: pl.BlockSpec((None, 1, D_MODEL),
                                   lambda i: (layer * COND_ROWS + stream.mod_row(i, MM_TILE), 0, col))
    mat = pl.BlockSpec((None, D_MODEL, D_MODEL), lambda i: (layer, 0, 0))
    return pl.pallas_call(
        _merge_kernel,
        grid=(stream.rows // MM_TILE,),
        in_specs=[
            wide, wide, wide, wide, wide, wide,
            mod(2), mod(3), mod(4),
            pl.BlockSpec((None, 1, D_MODEL), lambda i: (layer, 0, 0)),
            mat, mat, mat,
            pl.BlockSpec((N_EXPERTS, D_MODEL), lambda i: (0, 0)),
            pl.BlockSpec((N_EXPERTS, 1), lambda i: (0, 0)),
        ],
        out_specs=[wide, pl.BlockSpec((MM_TILE, ROW_W), tok),
                   pl.BlockSpec((None, 1, MM_TILE), lambda i: (i, 0, 0))],
        out_shape=[
            jax.ShapeDtypeStruct((stream.rows, D_MODEL), F32),
            jax.ShapeDtypeStruct((stream.rows, ROW_W), F32),
            jax.ShapeDtypeStruct((stream.rows // MM_TILE, 1, MM_TILE), jnp.int32),
        ],
        compiler_params=_params("arbitrary"),
        name=f"merge_{stream.name}",
    )(x, att, rnn.reshape(stream.rows, D_RNN), gr, ga, gb, mods, mods, mods,
      norm_g.reshape(DEPTH, 1, D_MODEL), wap, wrp, wo, router_wt, router_b.reshape(N_EXPERTS, 1))


def _plan_kernel(b_ref, pos_ref, tiles_ref):
    b = b_ref[...]
    r = lax.broadcasted_iota(jnp.int32, (TOK_TILE, TOK_TILE), 0)
    c = lax.broadcasted_iota(jnp.int32, (TOK_TILE, TOK_TILE), 1)
    before_in_tile = jnp.where(r < c, 1.0, 0.0).astype(BF16)
    br = lax.broadcasted_iota(jnp.int32, (N_TOK_TILES, N_TOK_TILES), 0)
    bc = lax.broadcasted_iota(jnp.int32, (N_TOK_TILES, N_TOK_TILES), 1)
    earlier_tiles = jnp.where(bc < br, 1.0, 0.0).astype(BF16)
    lane = lax.broadcasted_iota(jnp.int32, (1, LANES), 1)
    tile_start = (lane * MOE_TM).astype(F32)
    start = jnp.zeros((1, 1), F32)
    pos = jnp.zeros(b.shape, F32)
    tile_bucket = jnp.zeros((1, LANES), F32)
    for k in range(N_BUCKETS):
        mask = jnp.where(b == k, 1.0, 0.0)
        rank = jnp.dot(mask.astype(BF16), before_in_tile, preferred_element_type=F32)
        per_tile = jnp.sum(mask, axis=1, keepdims=True)
        tile_off = jnp.dot(earlier_tiles, jnp.broadcast_to(per_tile, (N_TOK_TILES, LANES)).astype(BF16),
                           preferred_element_type=F32)[:, :1]
        total = jnp.sum(per_tile, axis=0, keepdims=True)
        padded = jnp.floor((total + (MOE_TM - 0.5)) * (1.0 / MOE_TM)) * MOE_TM
        pos = pos + mask * (start + tile_off + rank)
        start = start + padded
        tile_bucket = tile_bucket + jnp.where(tile_start >= start, 1.0, 0.0)
    pos_ref[...] = pos.astype(jnp.int32)
    n_used = jnp.floor((start + 0.5) * (1.0 / MOE_TM))
    tiles_ref[...] = jnp.where(lane == LANES - 1, n_used, tile_bucket).astype(jnp.int32)


def _plan(bucket_lat, bucket_ctx):
    bucket = jnp.concatenate([bucket_lat.reshape(-1, TOK_TILE), bucket_ctx.reshape(-1, TOK_TILE)], axis=0)
    return pl.pallas_call(
        _plan_kernel,
        out_shape=[
            jax.ShapeDtypeStruct((N_TOK_TILES, TOK_TILE), jnp.int32),
            jax.ShapeDtypeStruct((1, LANES), jnp.int32),
        ],
        compiler_params=pltpu.CompilerParams(vmem_limit_bytes=VMEM_LIMIT),
        name="plan",
    )(bucket)


def _row_copy(src, dst, sem):
    return pltpu.make_async_copy(src, dst, sem)


def _dispatch_kernel(pos_ref, h_ref, xs_in_ref, xs_ref, sem):
    del xs_in_ref

    for r in range(DISPATCH_TILE):
        _row_copy(h_ref.at[pl.ds(r, 1), :], xs_ref.at[pl.ds(pos_ref[0, r], 1), :], sem).start(priority=r % 2)

    def wait(r, carry):
        _row_copy(h_ref.at[pl.ds(0, 1), :], xs_ref.at[pl.ds(0, 1), :], sem).wait()
        return carry

    lax.fori_loop(0, DISPATCH_TILE, wait, 0, unroll=8)


def _dispatch(stream, pos, h2, xs):
    first = stream.first_tile * TOK_TILE // DISPATCH_TILE
    return pl.pallas_call(
        _dispatch_kernel,
        grid=(stream.rows // DISPATCH_TILE,),
        in_specs=[
            pl.BlockSpec((None, 1, DISPATCH_TILE), lambda i: (first + i, 0, 0), memory_space=pltpu.SMEM),
            pl.BlockSpec((DISPATCH_TILE, ROW_W), lambda i: (i, 0)),
            pl.BlockSpec(memory_space=pl.ANY),
        ],
        out_specs=pl.BlockSpec(memory_space=pl.ANY),
        out_shape=jax.ShapeDtypeStruct((MOE_ROWS, ROW_W), F32),
        input_output_aliases={2: 0},
        scratch_shapes=[pltpu.SemaphoreType.DMA(())],
        compiler_params=_params("arbitrary"),
        name=f"dispatch_{stream.name}",
    )(pos.reshape(T_ALL // DISPATCH_TILE, 1, DISPATCH_TILE), h2, xs)


def _bucket_expert(k, which):
    p = k % PAIRS_PER_GROUP
    in_group = 0
    for order, slots in enumerate(PAIR_SLOTS):
        in_group = jnp.where(p == order, slots[which], in_group)
    return (k // PAIRS_PER_GROUP) * EXPERTS_PER_GROUP + in_group


def _weight_copies(layer, wgu_ref, wd_ref, wgu_f, wd_f, sem, bucket, which, landing):
    e = _bucket_expert(bucket, which)
    return [pltpu.make_async_copy(wgu_ref.at[layer, e], wgu_f.at[landing, which], sem.at[landing, 2 * which]),
            pltpu.make_async_copy(wd_ref.at[layer, e], wd_f.at[landing, which], sem.at[landing, 2 * which + 1])]


def _experts_kernel(layer, tiles_ref, xs_ref, wgu_ref, wd_ref, y_ref, wgu_f, wd_f, sem, home_ref):
    i = pl.program_id(0)
    n_used = tiles_ref[0, LANES - 1]
    in_use = i < n_used
    cur = tiles_ref[0, i]
    prev = tiles_ref[0, jnp.maximum(i - 1, 0)]

    def changed(bucket, before):
        return [_bucket_expert(bucket, which) != _bucket_expert(before, which) for which in range(2)]

    def copies(bucket, which, landing):
        return _weight_copies(layer, wgu_ref, wd_ref, wgu_f, wd_f, sem, bucket, which, landing)

    @pl.when(jnp.logical_and(in_use, jnp.logical_or(i == 0, cur != prev)))
    def _():
        @pl.when(i == 0)
        def _():
            for which in range(2):
                home_ref[which] = 1
                for c in copies(cur, which, 0):
                    c.start()

        needed = [jnp.logical_or(i == 0, c) for c in changed(cur, prev)]
        for which in range(2):
            @pl.when(needed[which])
            def _(which=which):
                landing = 1 - home_ref[which]
                for c in copies(cur, which, landing):
                    c.wait()
                home_ref[which] = landing

        nxt_i = lax.while_loop(lambda j: jnp.logical_and(j < n_used, tiles_ref[0, jnp.minimum(j, n_used - 1)] == cur),
                               lambda j: j + 1, i + 1)

        @pl.when(nxt_i < n_used)
        def _():
            nxt = tiles_ref[0, nxt_i]
            for which, differs in enumerate(changed(nxt, cur)):
                @pl.when(differs)
                def _(which=which):
                    for c in copies(nxt, which, 1 - home_ref[which]):
                        c.start()

    @pl.when(jnp.logical_not(in_use))
    def _():
        y_ref[...] = jnp.zeros_like(y_ref)

    @pl.when(in_use)
    def _():
        x = xs_ref[:, :D_MODEL].astype(BF16)

        def branch(which, gate):
            home = home_ref[which]
            gu = jnp.dot(x, wgu_f[home, which].astype(BF16), preferred_element_type=F32)
            act = jax.nn.silu(gu[:, :D_EXPERT]) * gu[:, D_EXPERT:] * gate
            return jnp.dot(act.astype(BF16), wd_f[home, which].astype(BF16), preferred_element_type=F32)

        y_ref[...] = (branch(0, xs_ref[:, D_MODEL:D_MODEL + 1])
                      + branch(1, xs_ref[:, D_MODEL + 1:D_MODEL + 2]))


def _experts(tiles, xs, wgu, wd, layer):
    return pl.pallas_call(
        functools.partial(_experts_kernel, layer),
        grid_spec=pltpu.PrefetchScalarGridSpec(
            num_scalar_prefetch=1,
            grid=(N_MOE_TILES,),
            in_specs=[
                pl.BlockSpec((MOE_TM, ROW_W), lambda i, t: (i, 0)),
                pl.BlockSpec(memory_space=pl.ANY),
                pl.BlockSpec(memory_space=pl.ANY),
            ],
            out_specs=pl.BlockSpec((MOE_TM, D_MODEL), lambda i, t: (i, 0)),
            scratch_shapes=[
                pltpu.VMEM((2, 2, D_MODEL, 2 * D_EXPERT), F32), pltpu.VMEM((2, 2, D_EXPERT, D_MODEL), F32),
                pltpu.SemaphoreType.DMA((2, 4)), pltpu.SMEM((2,), jnp.int32),
            ],
        ),
        out_shape=jax.ShapeDtypeStruct((MOE_ROWS, D_MODEL), F32),
        compiler_params=_params("arbitrary"),
        name="experts",
    )(tiles, xs, wgu, wd)


def _combine_kernel(n_tiles, final, pos_ref, next_pos_ref, x_ref, g2_ref, fg_ref, y_ref, o_ref, rows_ref, sem):
    i = pl.program_id(0)

    def gather(p_ref, into):
        for r in range(TOK_TILE):
            _row_copy(y_ref.at[pl.ds(p_ref[0, r], 1), :], rows_ref.at[into, pl.ds(r, 1), :],
                      sem.at[into]).start(priority=r % 2)

    def step(slot):
        if slot == 0:
            @pl.when(i == 0)
            def _():
                gather(pos_ref, 0)

        @pl.when(i + 1 < n_tiles)
        def _():
            gather(next_pos_ref, 1 - slot)

        def wait(r, carry):
            _row_copy(y_ref.at[pl.ds(0, 1), :], rows_ref.at[slot, pl.ds(0, 1), :], sem.at[slot]).wait()
            return carry

        lax.fori_loop(0, TOK_TILE, wait, 0, unroll=8)
        x = x_ref[...] + g2_ref[...] * rows_ref[slot]
        if final:
            x = x * lax.rsqrt(jnp.mean(x * x, axis=-1, keepdims=True) + EPS) * fg_ref[...]
        o_ref[...] = x

    for parity in range(2):
        pl.when(i % 2 == parity)(functools.partial(step, parity))


def _combine(stream, pos, x, mods, y, final_g, layer):
    first, last = stream.first_tile, stream.first_tile + stream.n_tiles - 1
    return pl.pallas_call(
        functools.partial(_combine_kernel, stream.n_tiles, layer == DEPTH - 1),
        grid=(stream.n_tiles,),
        in_specs=[
            pl.BlockSpec((None, 1, TOK_TILE), lambda i: (first + i, 0, 0), memory_space=pltpu.SMEM),
            pl.BlockSpec((None, 1, TOK_TILE), lambda i: (jnp.minimum(first + i + 1, last), 0, 0),
                         memory_space=pltpu.SMEM),
            pl.BlockSpec((TOK_TILE, D_MODEL), lambda i: (i, 0)),
            pl.BlockSpec((None, 1, D_MODEL), lambda i: (layer * COND_ROWS + stream.mod_row(i, TOK_TILE), 0, 5)),
            pl.BlockSpec((1, D_MODEL), lambda i: (0, 0)),
            pl.BlockSpec(memory_space=pl.ANY),
        ],
        out_specs=pl.BlockSpec((TOK_TILE, D_MODEL), lambda i: (i, 0)),
        out_shape=jax.ShapeDtypeStruct((stream.rows, D_MODEL), F32),
        scratch_shapes=[pltpu.VMEM((2, TOK_TILE, D_MODEL), F32), pltpu.SemaphoreType.DMA((2,))],
        compiler_params=_params("arbitrary"),
        name=f"combine_{stream.name}",
    )(pos, pos, x, mods, final_g.reshape(1, D_MODEL), y)


def _rope_tables():
    t = jnp.arange(DEC_SEQ, dtype=jnp.int32)
    row = (t // GRID_W).astype(F32)
    col = (t % GRID_W).astype(F32)
    inv = ROPE_BASE ** (-jnp.arange(ROPE_PAIRS_PER_AXIS, dtype=F32) / ROPE_PAIRS_PER_AXIS)
    ang = jnp.concatenate([row[:, None] * inv, col[:, None] * inv], axis=-1)
    cos, sin = jnp.cos(ang), jnp.sin(ang)
    return jnp.concatenate([cos, cos], axis=-1), jnp.concatenate([-sin, sin], axis=-1)


def kernel(x_prompt, x_sample, cache_k, cache_v, state_h, c, c_ctx, norm1_g, norm2_g, w_ada, b_ada, w_in, conv_w, conv_b, lru_wa, lru_ba, lru_wx, lru_bx, lru_lambda, attn_sink, w_attn_proj, w_rnn_proj, w_out, router_w, router_b, w_gate_up, w_down, final_norm_g):
    xs_tok = {LATENT: x_sample.reshape(T_LAT, D_MODEL), CONTEXT: x_prompt.reshape(T_CTX, D_MODEL)}
    cond = jnp.concatenate([c_ctx[None, :], c, jnp.zeros((COND_ROWS - 1 - DEC_BATCH, D_MODEL), F32)], axis=0)
    mods = _ada(cond, w_ada, b_ada).reshape(DEPTH * COND_ROWS, 1, N_ADA * D_MODEL)
    rope_cos, rope_sin = _rope_tables()

    gate_cols = jnp.arange(IN_COLS) >= IN_COLS - 2 * D_MODEL
    w_in_b = jnp.where(gate_cols, 0.5 * w_in, w_in).astype(BF16)
    wap_b = w_attn_proj.astype(BF16)
    wrp_b = (0.5 * w_rnn_proj).astype(BF16)
    wo_b = (0.5 * w_out).astype(BF16)
    lru_w, lru_b = _lru_block_weights(lru_wa, lru_wx, lru_ba, lru_bx)
    router_wt = router_w.T

    h0 = {LATENT: jnp.transpose(state_h.astype(F32), (1, 2, 0, 3)).reshape(DEPTH, 1, 2, DEC_BATCH, D_RNN),
          CONTEXT: jnp.zeros((DEPTH, BATCH // SUBLANES, 2, SUBLANES, D_RNN), F32)}
    ck = jnp.transpose(cache_k, (1, 0, 2, 3, 4)).reshape(DEPTH, DEC_BATCH, PAST_LEN, KV_W)
    cv = jnp.transpose(cache_v, (1, 0, 2, 3, 4)).reshape(DEPTH, DEC_BATCH, PAST_LEN, KV_W)

    xs = jnp.zeros((MOE_ROWS, ROW_W), F32)
    new_k, new_v, new_h = [], [], []
    for l in range(DEPTH):
        merged = {}
        for stream in (LATENT, CONTEXT):
            q, k, v, xr, gr, ga, gb = _inproj(stream, xs_tok[stream], norm1_g, mods, rope_cos, rope_sin,
                                              w_in_b, l)
            if stream.rotary:
                att = _lat_attn(q, k, v, ck[l], cv[l], attn_sink[l])
            else:
                att = _ctx_attn(q, k, v, attn_sink[l])
                new_k.append(k.reshape(BATCH, SEQ, N_KV_HEADS, HEAD_DIM))
                new_v.append(v.reshape(BATCH, SEQ, N_KV_HEADS, HEAD_DIM))
            y, h_last = _rnn(stream, xr, conv_w, conv_b, lru_w, lru_b, lru_lambda, h0[stream][l], l)
            if not stream.rotary:
                new_h.append(jnp.transpose(h_last, (0, 2, 1, 3)).reshape(BATCH, 2, D_RNN))
            merged[stream] = _merge(stream, xs_tok[stream], att, y, gr, ga, gb, mods, norm2_g,
                                    wap_b, wrp_b, wo_b, router_wt, router_b, l)
        pos, tiles = _plan(merged[LATENT][2], merged[CONTEXT][2])
        pos = pos.reshape(N_TOK_TILES, 1, TOK_TILE)
        for stream in (LATENT, CONTEXT):
            xs = _dispatch(stream, pos, merged[stream][1], xs)
        y_sorted = _experts(tiles, xs, w_gate_up, w_down, l)
        for stream in (LATENT, CONTEXT):
            xs_tok[stream] = _combine(stream, pos, merged[stream][0], mods, y_sorted, final_norm_g, l)

    return (xs_tok[CONTEXT].reshape(BATCH, SEQ, D_MODEL), xs_tok[LATENT].reshape(DEC_BATCH, DEC_SEQ, D_MODEL),
            jnp.stack(new_k, axis=1), jnp.stack(new_v, axis=1),
            jnp.stack(new_h, axis=1).astype(x_prompt.dtype))
```
